```python
import math
import jax, jax.numpy as jnp
from jax import lax
import numpy as np

D_MODEL = 2048
BATCH = 4
SEQ = 4096
DEPTH = 1

SSD_HEADS = 32
SSD_HEAD_DIM = 64
SSD_DIM = SSD_HEADS * SSD_HEAD_DIM
SSD_GROUPS = 4
SSD_HEADS_PER_GROUP = SSD_HEADS // SSD_GROUPS
SSD_STATE = 128
SSD_CHUNK = 128
CONV_WIDTH = 4
CONV_DIM = SSD_DIM + 2 * SSD_GROUPS * SSD_STATE
FOX_HEADS = 16
FOX_HEAD_DIM = 128
FOX_DIM = FOX_HEADS * FOX_HEAD_DIM
Q_BLOCK = 128
N_BRANCH = 2
IN_PROJ_DIM = SSD_DIM + CONV_DIM + SSD_HEADS + 3 * FOX_DIM + FOX_HEADS + N_BRANCH * D_MODEL
N_GROUPS = 8
EXPERTS_PER_GROUP = 8
N_EXPERTS = N_GROUPS * EXPERTS_PER_GROUP
TOP_K_IN_GROUP = 2
D_EXPERT = 512
ROW_BLOCK = 128
EPS = 1e-6

kernel_name = "hybrid_ssd_fox_hiermoe_block"


def rmsnorm(x, w):
    xf = x.astype(jnp.float32)
    xf = xf * lax.rsqrt(jnp.mean(xf * xf, axis=-1, keepdims=True) + EPS)
    return (xf * w.astype(jnp.float32)).astype(x.dtype)


def causal_depthwise_conv(x, w, b):
    c = x.shape[-1]
    y = lax.conv_general_dilated(x, w[:, None, :].astype(x.dtype), window_strides=(1,),
                                 padding=[(CONV_WIDTH - 1, 0)],
                                 dimension_numbers=("NWC", "WIO", "NWC"),
                                 feature_group_count=c)
    return y + b.astype(x.dtype)


def ssd_mixer(z, xbc, dt_raw, conv_w, conv_b, dt_bias, a_log, d_skip, norm_w):
    out_dtype = z.dtype
    bsz, s, _ = xbc.shape
    nc = s // SSD_CHUNK
    g, e, p, n, l = SSD_GROUPS, SSD_HEADS_PER_GROUP, SSD_HEAD_DIM, SSD_STATE, SSD_CHUNK
    xbc = jax.nn.silu(causal_depthwise_conv(xbc, conv_w, conv_b)).astype(jnp.float32)
    xs = xbc[..., :SSD_DIM]
    bm = xbc[..., SSD_DIM:SSD_DIM + g * n].reshape(bsz, nc, l, g, n)
    cm = xbc[..., SSD_DIM + g * n:].reshape(bsz, nc, l, g, n)
    dt = jax.nn.softplus(dt_raw.astype(jnp.float32) + dt_bias.astype(jnp.float32))
    a = -jnp.exp(a_log.astype(jnp.float32))
    a_cs = jnp.cumsum((dt * a).reshape(bsz, nc, l, SSD_HEADS), axis=2)
    x_dt = (xs.reshape(bsz, s, SSD_HEADS, p) * dt[..., None]).reshape(bsz, nc, l, g, e, p)
    seg = a_cs[:, :, :, None, :] - a_cs[:, :, None, :, :]
    causal = jnp.tril(jnp.ones((l, l), dtype=bool))[None, None, :, :, None]
    lmat = jnp.exp(jnp.where(causal, seg, -jnp.inf)).reshape(bsz, nc, l, l, g, e)
    cb = jnp.einsum('bclgn,bcsgn->bclsg', cm, bm)
    y_diag = jnp.einsum('bclsge,bcsgep->bclgep', cb[..., None] * lmat, x_dt)
    decay_states = jnp.exp(a_cs[:, :, -1:, :] - a_cs).reshape(bsz, nc, l, g, e)
    states = jnp.einsum('bclgn,bclgep->bcgepn', bm, x_dt * decay_states[..., None])
    chunk_decay = jnp.exp(a_cs[:, :, -1, :]).reshape(bsz, nc, g, e)

    def step(carry, inp):
        st, dec = inp
        return carry * dec[..., None, None] + st, carry

    init = jnp.zeros((bsz, g, e, p, n), jnp.float32)
    _, prev = lax.scan(step, init, (jnp.moveaxis(states, 1, 0), jnp.moveaxis(chunk_decay, 1, 0)))
    prev = jnp.moveaxis(prev, 0, 1)
    y_off = jnp.einsum('bclgn,bcgepn->bclgep', cm, prev) * jnp.exp(a_cs).reshape(bsz, nc, l, g, e)[..., None]
    y = (y_diag + y_off).reshape(bsz, s, SSD_HEADS, p)
    y = y + d_skip.astype(jnp.float32)[:, None] * xs.reshape(bsz, s, SSD_HEADS, p)
    y = y.reshape(bsz, s, SSD_DIM) * jax.nn.silu(z.astype(jnp.float32))
    return rmsnorm(y, norm_w).astype(out_dtype)


def forgetting_attention(q, k, v, f_logits, f_bias):
    bsz, s, _ = q.shape
    h, d = FOX_HEADS, FOX_HEAD_DIM
    q = q.reshape(bsz, s, h, d).transpose(0, 2, 1, 3)
    k = k.reshape(bsz, s, h, d).transpose(0, 2, 1, 3)
    v = v.reshape(bsz, s, h, d).transpose(0, 2, 1, 3)
    log_f = jax.nn.log_sigmoid(f_logits.astype(jnp.float32) + f_bias.astype(jnp.float32))
    cum = jnp.cumsum(log_f, axis=1).transpose(0, 2, 1)
    scale = 1.0 / math.sqrt(d)
    key_pos = jnp.arange(s)

    def block(i):
        start = i * Q_BLOCK
        qb = lax.dynamic_slice_in_dim(q, start, Q_BLOCK, axis=2)
        cq = lax.dynamic_slice_in_dim(cum, start, Q_BLOCK, axis=2)
        logits = jnp.einsum('bhqd,bhkd->bhqk', qb, k).astype(jnp.float32) * scale
        logits = logits + cq[..., :, None] - cum[:, :, None, :]
        mask = key_pos[None, :] <= (start + jnp.arange(Q_BLOCK))[:, None]
        probs = jax.nn.softmax(jnp.where(mask, logits, -jnp.inf), axis=-1)
        return jnp.einsum('bhqk,bhkd->bhqd', probs.astype(v.dtype), v)

    out = lax.map(block, jnp.arange(s // Q_BLOCK))
    return out.transpose(1, 0, 3, 2, 4).reshape(bsz, s, FOX_DIM)


def hierarchical_moe(u, w_rg, b_rg, w_re, b_re, w_gate, w_up, w_down):
    bsz, s, d = u.shape
    t = bsz * s
    uf = u.reshape(t, d)
    g_prob = jax.nn.softmax((uf @ w_rg).astype(jnp.float32) + b_rg.astype(jnp.float32), axis=-1)
    g_p, g_idx = lax.top_k(g_prob, 1)
    e_logits = ((uf @ w_re).astype(jnp.float32) + b_re.astype(jnp.float32)).reshape(t, N_GROUPS, EXPERTS_PER_GROUP)
    e_in = jnp.take_along_axis(e_logits, g_idx[:, :, None], axis=1)[:, 0]
    e_p, e_loc = lax.top_k(jax.nn.softmax(e_in, axis=-1), TOP_K_IN_GROUP)
    weights = g_p * (e_p / jnp.sum(e_p, axis=-1, keepdims=True))
    expert_idx = g_idx * EXPERTS_PER_GROUP + e_loc
    tk = t * TOP_K_IN_GROUP
    n_blocks = (tk + ROW_BLOCK - 1) // ROW_BLOCK + N_EXPERTS
    n_rows = n_blocks * ROW_BLOCK
    e_flat = expert_idx.reshape(tk)
    w_flat = weights.reshape(tk)
    tok_flat = jnp.repeat(jnp.arange(t, dtype=jnp.int32), TOP_K_IN_GROUP)
    order = jnp.argsort(e_flat)
    e_sorted = e_flat[order]
    counts = jnp.bincount(e_flat, length=N_EXPERTS)
    start = jnp.cumsum(counts) - counts
    padded = ((counts + ROW_BLOCK - 1) // ROW_BLOCK) * ROW_BLOCK
    pend = jnp.cumsum(padded)
    pstart = pend - padded
    dest = pstart[e_sorted] + (jnp.arange(tk) - start[e_sorted])
    row_tok = jnp.zeros((n_rows,), jnp.int32).at[dest].set(tok_flat[order])
    row_w = jnp.zeros((n_rows,), jnp.float32).at[dest].set(w_flat[order])
    block_expert = jnp.clip(jnp.searchsorted(pend, jnp.arange(n_blocks) * ROW_BLOCK, side='right'),
                            0, N_EXPERTS - 1).astype(jnp.int32)
    xs = uf[row_tok].reshape(n_blocks, ROW_BLOCK, d)

    def expert_block(args):
        xb, ex = args
        hdn = jax.nn.silu(xb @ jnp.take(w_gate, ex, axis=0)) * (xb @ jnp.take(w_up, ex, axis=0))
        return hdn @ jnp.take(w_down, ex, axis=0)

    y_rows = lax.map(expert_block, (xs, block_expert)).reshape(n_rows, d)
    y_rows = y_rows * row_w[:, None].astype(y_rows.dtype)
    out = jax.ops.segment_sum(y_rows, row_tok, num_segments=t)
    return out.reshape(bsz, s, d)


def setup_inputs(seed: int = 0) -> dict:
    key = jax.random.key(seed)
    ks = jax.random.split(key, 24)
    f32 = jnp.float32
    nrm = lambda k, shape, sc: jax.random.normal(k, shape, f32) * sc
    gain = lambda k, shape: 1.0 + 0.02 * jax.random.normal(k, shape, f32)
    dt0 = jnp.exp(jax.random.uniform(ks[5], (DEPTH, SSD_HEADS), f32, math.log(1e-3), math.log(1e-1)))
    return {
        "x": jax.random.normal(ks[0], (BATCH, SEQ, D_MODEL), f32),
        "norm_mix_w": gain(ks[1], (DEPTH, D_MODEL)),
        "w_in": nrm(ks[2], (DEPTH, D_MODEL, IN_PROJ_DIM), D_MODEL ** -0.5),
        "conv_w": nrm(ks[3], (DEPTH, CONV_WIDTH, CONV_DIM), CONV_WIDTH ** -0.5),
        "conv_b": nrm(ks[4], (DEPTH, CONV_DIM), 0.02),
        "dt_bias": dt0 + jnp.log(-jnp.expm1(-dt0)),
        "a_log": jnp.log(jax.random.uniform(ks[6], (DEPTH, SSD_HEADS), f32, 1.0, 16.0)),
        "d_skip": gain(ks[7], (DEPTH, SSD_HEADS)),
        "ssd_norm_w": gain(ks[8], (DEPTH, SSD_DIM)),
        "fox_f_bias": jax.random.uniform(ks[9], (DEPTH, FOX_HEADS), f32, 2.0, 5.0),
        "w_proj_ssd": nrm(ks[10], (DEPTH, SSD_DIM, D_MODEL), SSD_DIM ** -0.5),
        "w_proj_fox": nrm(ks[11], (DEPTH, FOX_DIM, D_MODEL), FOX_DIM ** -0.5),
        "w_out": nrm(ks[12], (DEPTH, D_MODEL, D_MODEL), D_MODEL ** -0.5),
        "norm_moe_w": gain(ks[13], (DEPTH, D_MODEL)),
        "w_router_group": nrm(ks[14], (DEPTH, D_MODEL, N_GROUPS), D_MODEL ** -0.5),
        "b_router_group": nrm(ks[15], (DEPTH, N_GROUPS), 0.01),
        "w_router_expert": nrm(ks[16], (DEPTH, D_MODEL, N_EXPERTS), D_MODEL ** -0.5),
        "b_router_expert": nrm(ks[17], (DEPTH, N_EXPERTS), 0.01),
        "w_gate_exp": nrm(ks[18], (DEPTH, N_EXPERTS, D_MODEL, D_EXPERT), D_MODEL ** -0.5),
        "w_up_exp": nrm(ks[19], (DEPTH, N_EXPERTS, D_MODEL, D_EXPERT), D_MODEL ** -0.5),
        "w_down_exp": nrm(ks[20], (DEPTH, N_EXPERTS, D_EXPERT, D_MODEL), D_EXPERT ** -0.5),
        "norm_final_w": gain(ks[21], (D_MODEL,)),
    }


def reference(x, norm_mix_w, w_in, conv_w, conv_b, dt_bias, a_log, d_skip, ssd_norm_w, fox_f_bias,
              w_proj_ssd, w_proj_fox, w_out, norm_moe_w, w_router_group, b_router_group,
              w_router_expert, b_router_expert, w_gate_exp, w_up_exp, w_down_exp, norm_final_w):
    sizes = (SSD_DIM, CONV_DIM, SSD_HEADS, FOX_DIM, FOX_DIM, FOX_DIM, FOX_HEADS, N_BRANCH * D_MODEL)
    offsets = [int(o) for o in np.cumsum(sizes)[:-1]]
    h = x
    for layer in range(DEPTH):
        u = rmsnorm(h, norm_mix_w[layer])
        proj = u @ w_in[layer]
        z, xbc, dt_raw, q, k, v, f_logits, gate_logits = jnp.split(proj, offsets, axis=-1)
        y_ssd = ssd_mixer(z, xbc, dt_raw, conv_w[layer], conv_b[layer], dt_bias[layer],
                          a_log[layer], d_skip[layer], ssd_norm_w[layer])
        y_fox = forgetting_attention(q, k, v, f_logits, fox_f_bias[layer])
        gates = jax.nn.sigmoid(gate_logits.astype(jnp.float32)).astype(h.dtype)
        g_ssd, g_fox = jnp.split(gates, N_BRANCH, axis=-1)
        mixed = g_ssd * (y_ssd @ w_proj_ssd[layer]) + g_fox * (y_fox @ w_proj_fox[layer])
        h = h + mixed @ w_out[layer]
        u2 = rmsnorm(h, norm_moe_w[layer])
        h = h + hierarchical_moe(u2, w_router_group[layer], b_router_group[layer],
                                 w_router_expert[layer], b_router_expert[layer],
                                 w_gate_exp[layer], w_up_exp[layer], w_down_exp[layer])
    return rmsnorm(h, norm_final_w)
```

```python
import functools
import math

import jax
import jax.numpy as jnp
from jax import lax
from jax.experimental import pallas as pl
from jax.experimental.pallas import tpu as pltpu

F32 = jnp.float32
BF16 = jnp.bfloat16
I32 = jnp.int32
U32 = jnp.uint32

D_MODEL = 2048
SSD_HEADS = 32
SSD_HEAD_DIM = 64
SSD_DIM = SSD_HEADS * SSD_HEAD_DIM
SSD_GROUPS = 4
SSD_STATE = 128
SSD_CHUNK = 128
CONV_WIDTH = 4
CONV_DIM = SSD_DIM + 2 * SSD_GROUPS * SSD_STATE
FOX_HEADS = 16
FOX_HEAD_DIM = 128
FOX_DIM = FOX_HEADS * FOX_HEAD_DIM
N_GROUPS = 8
EXPERTS_PER_GROUP = 8
N_EXPERTS = N_GROUPS * EXPERTS_PER_GROUP
TOP_K = 2
D_EXPERT = 512
EPS = 1e-6

LANES = 128
SUBLANES = 8
VMEM_LIMIT = 52 * 1024 * 1024

COL_Z = 0
COL_Q = COL_Z + SSD_DIM
COL_K = COL_Q + FOX_DIM
COL_V = COL_K + FOX_DIM
COL_GA = COL_V + FOX_DIM
COL_GB = COL_GA + D_MODEL
COL_XBC = COL_GB + D_MODEL
PROJ_COLS = COL_XBC + CONV_DIM
SMALL_DT = 0
SMALL_F = SSD_HEADS

ROW_BLOCK = 128
HALF = D_MODEL // 2


def _cparams(sem, vmem=VMEM_LIMIT):
    return pltpu.CompilerParams(dimension_semantics=sem, vmem_limit_bytes=vmem)


def _silu(x):
    return x * (1.0 / (1.0 + jnp.exp(-x)))


def _softplus(x):
    return jnp.maximum(x, 0.0) + jnp.log(1.0 + jnp.exp(-jnp.abs(x)))


def _log_sigmoid(x):
    return -_softplus(-x)


def _pack_bf16_pair(x):
    n = x.shape[1] // 2
    lo = pltpu.bitcast(x[:, :n].astype(BF16).astype(F32), U32)
    hi = pltpu.bitcast(x[:, n:].astype(BF16).astype(F32), U32)
    return (hi & jnp.uint32(0xFFFF0000)) | (lo >> 16)


def _unpack_bf16_pair(p):
    lo = pltpu.bitcast(p << 16, F32)
    hi = pltpu.bitcast(p & jnp.uint32(0xFFFF0000), F32)
    return lo, hi


def _norm_small_kernel(x_ref, nw_ref, ws_ref, u_ref, s_ref):
    x = x_ref[...]
    ms = jnp.mean(x * x, axis=-1, keepdims=True)
    u = (x * lax.rsqrt(ms + EPS) * nw_ref[...]).astype(BF16)
    u_ref[...] = u
    s_ref[...] = jnp.dot(u, ws_ref[...], preferred_element_type=F32)


def _norm_small(x2, nw, w_small, tm=512):
    t = x2.shape[0]
    return pl.pallas_call(
        _norm_small_kernel,
        name="norm_small",
        grid=(t // tm,),
        in_specs=[pl.BlockSpec((tm, D_MODEL), lambda i: (i, 0)),
                  pl.BlockSpec((1, D_MODEL), lambda i: (0, 0)),
                  pl.BlockSpec((D_MODEL, LANES), lambda i: (0, 0))],
        out_specs=[pl.BlockSpec((tm, D_MODEL), lambda i: (i, 0)),
                   pl.BlockSpec((tm, LANES), lambda i: (i, 0))],
        out_shape=[jax.ShapeDtypeStruct((t, D_MODEL), BF16),
                   jax.ShapeDtypeStruct((t, LANES), F32)],
        compiler_params=_cparams(("parallel",)),
    )(x2, nw, w_small)


def _matmul_kernel(a_ref, b_ref, o_ref):
    o_ref[...] = jnp.dot(a_ref[...], b_ref[...], preferred_element_type=F32).astype(o_ref.dtype)


def _matmul(a, b, tm=1024, tn=1024):
    m, k = a.shape
    n = b.shape[1]
    return pl.pallas_call(
        _matmul_kernel,
        name="in_proj",
        grid=(m // tm, n // tn),
        in_specs=[pl.BlockSpec((tm, k), lambda i, j: (i, 0)),
                  pl.BlockSpec((k, tn), lambda i, j: (0, j))],
        out_specs=pl.BlockSpec((tm, tn), lambda i, j: (i, j)),
        out_shape=jax.ShapeDtypeStruct((m, n), BF16),
        compiler_params=_cparams(("parallel", "parallel")),
    )(a, b)


HALO = SUBLANES


def _ssd_kernel(z_ref, xbc_ref, halo_ref, small_ref, cw_ref, cb_ref, dtb_ref, aneg_ref, dexp_ref,
                nw_ref, expand_ref, y_ref, ext_ref, state_ref, ydiag_ref):
    c = pl.program_id(1)
    l = SSD_CHUNK
    n = SSD_STATE

    @pl.when(c == 0)
    def _():
        state_ref[...] = jnp.zeros_like(state_ref)

    halo = halo_ref[...].astype(F32)
    halo = jnp.where(c == 0, jnp.zeros_like(halo), halo)
    ext_ref[0:HALO, :] = halo
    ext_ref[HALO:HALO + l, :] = xbc_ref[...].astype(F32)
    conv = cb_ref[...] + cw_ref[CONV_WIDTH - 1:CONV_WIDTH, :] * ext_ref[HALO:HALO + l, :]
    for j in range(CONV_WIDTH - 1):
        off = HALO - (CONV_WIDTH - 1) + j
        conv = conv + cw_ref[j:j + 1, :] * ext_ref[off:off + l, :]
    xbc = _silu(conv)
    xs = xbc[:, :SSD_DIM]
    bm = xbc[:, SSD_DIM:SSD_DIM + SSD_GROUPS * n]
    cm = xbc[:, SSD_DIM + SSD_GROUPS * n:]

    dt = _softplus(small_ref[:, SMALL_DT:SMALL_DT + SSD_HEADS] + dtb_ref[...])
    adt = dt * aneg_ref[...]
    row = lax.broadcasted_iota(I32, (l, l), 0)
    col = lax.broadcasted_iota(I32, (l, l), 1)
    causal = col <= row
    tril = jnp.where(causal, 1.0, 0.0).astype(F32)
    a_cs = jnp.dot(tril, adt, preferred_element_type=F32, precision=lax.Precision.HIGHEST)
    a_cs_t = jnp.transpose(a_cs)
    a_last = a_cs[l - 1:l, :]

    expand = expand_ref[...]
    hp = lax.Precision.HIGHEST
    dt_x = jnp.dot(dt, expand, preferred_element_type=F32, precision=hp)
    acs_x = jnp.dot(a_cs, expand, preferred_element_type=F32, precision=hp)
    alast_x = jnp.dot(a_last, expand, preferred_element_type=F32, precision=hp)
    x_dt = xs * dt_x
    decay_in = jnp.exp(acs_x)
    decay_out = jnp.exp(alast_x - acs_x)
    chunk_decay = jnp.exp(alast_x)
    x_dt_b = x_dt.astype(BF16)
    xd_b = (x_dt * decay_out).astype(BF16)

    lane = lax.broadcasted_iota(I32, (l, LANES), 1)
    lo_mask = lane < SSD_HEAD_DIM
    heads_per_group = SSD_HEADS // SSD_GROUPS
    gw = heads_per_group * SSD_HEAD_DIM

    for g in range(SSD_GROUPS):
        bg = bm[:, g * n:(g + 1) * n].astype(BF16)
        cg = cm[:, g * n:(g + 1) * n].astype(BF16)
        cb = lax.dot_general(cg, bg, (((1,), (1,)), ((), ())), preferred_element_type=F32)
        for pair in range(heads_per_group // 2):
            h0 = g * heads_per_group + 2 * pair
            lane0 = h0 * SSD_HEAD_DIM
            ms = []
            for h in (h0, h0 + 1):
                seg = a_cs[:, h:h + 1] - a_cs_t[h:h + 1, :]
                lmat = jnp.exp(jnp.where(causal, seg, -jnp.inf))
                ms.append((cb * lmat).astype(BF16))
            lhs = jnp.concatenate(ms, axis=1)
            xp = x_dt_b[:, lane0:lane0 + LANES]
            zero = jnp.zeros_like(xp)
            rhs = jnp.concatenate([jnp.where(lo_mask, xp, zero), jnp.where(lo_mask, zero, xp)], axis=0)
            ydiag_ref[:, lane0:lane0 + LANES] = jnp.dot(lhs, rhs, preferred_element_type=F32)
        st = state_ref[:, g * gw:(g + 1) * gw]
        y_off = jnp.dot(cg, st.astype(BF16), preferred_element_type=F32)
        ydiag_ref[:, g * gw:(g + 1) * gw] += y_off * decay_in[:, g * gw:(g + 1) * gw]
        new = lax.dot_general(bg, xd_b[:, g * gw:(g + 1) * gw], (((0,), (0,)), ((), ())),
                              preferred_element_type=F32)
        state_ref[:, g * gw:(g + 1) * gw] = st * chunk_decay[:, g * gw:(g + 1) * gw] + new

    y = ydiag_ref[...] + dexp_ref[...] * xs
    y = y * _silu(z_ref[...].astype(F32))
    ms = jnp.mean(y * y, axis=-1, keepdims=True)
    y_ref[...] = (y * lax.rsqrt(ms + EPS) * nw_ref[...]).astype(y_ref.dtype)


def _ssd(proj, small, conv_w, conv_b, dt_bias, a_log, d_skip, norm_w, bsz, seq):
    l = SSD_CHUNK
    nc = seq // l
    t = bsz * seq
    aneg = -jnp.exp(a_log.astype(F32)).reshape(1, SSD_HEADS)
    dexp = jnp.repeat(d_skip.astype(F32), SSD_HEAD_DIM).reshape(1, SSD_DIM)
    expand = jnp.repeat(jnp.eye(SSD_HEADS, dtype=F32), SSD_HEAD_DIM, axis=1)
    xbc_blk = COL_XBC // CONV_DIM
    halo_per_chunk = l // HALO

    def row_map(b, c):
        return b * nc + c

    return pl.pallas_call(
        _ssd_kernel,
        name="ssd",
        grid=(bsz, nc),
        in_specs=[
            pl.BlockSpec((l, SSD_DIM), lambda b, c: (row_map(b, c), COL_Z // SSD_DIM)),
            pl.BlockSpec((l, CONV_DIM), lambda b, c: (row_map(b, c), xbc_blk)),
            pl.BlockSpec((HALO, CONV_DIM),
                         lambda b, c: (jnp.maximum(row_map(b, c) * halo_per_chunk - 1, 0), xbc_blk)),
            pl.BlockSpec((l, LANES), lambda b, c: (row_map(b, c), 0)),
            pl.BlockSpec((CONV_WIDTH, CONV_DIM), lambda b, c: (0, 0)),
            pl.BlockSpec((1, CONV_DIM), lambda b, c: (0, 0)),
            pl.BlockSpec((1, SSD_HEADS), lambda b, c: (0, 0)),
            pl.BlockSpec((1, SSD_HEADS), lambda b, c: (0, 0)),
            pl.BlockSpec((1, SSD_DIM), lambda b, c: (0, 0)),
            pl.BlockSpec((1, SSD_DIM), lambda b, c: (0, 0)),
            pl.BlockSpec((SSD_HEADS, SSD_DIM), lambda b, c: (0, 0)),
        ],
        out_specs=pl.BlockSpec((l, SSD_DIM), lambda b, c: (row_map(b, c), 0)),
        out_shape=jax.ShapeDtypeStruct((t, SSD_DIM), BF16),
        scratch_shapes=[pltpu.VMEM((HALO + l, CONV_DIM), F32),
                        pltpu.VMEM((SSD_STATE, SSD_DIM), F32),
                        pltpu.VMEM((l, SSD_DIM), F32)],
        compiler_params=_cparams(("parallel", "arbitrary")),
    )(proj, proj, proj, small, conv_w.astype(F32), conv_b.astype(F32).reshape(1, CONV_DIM),
      dt_bias.astype(F32).reshape(1, SSD_HEADS), aneg, dexp, norm_w.astype(F32).reshape(1, SSD_DIM), expand)


CUM_ROWS = 256


def _cum_kernel(small_ref, fb_ref, cum_ref, carry_ref):
    j = pl.program_id(1)

    @pl.when(j == 0)
    def _():
        carry_ref[...] = jnp.zeros_like(carry_ref)

    lf = _log_sigmoid(small_ref[...] + fb_ref[...])
    row = lax.broadcasted_iota(I32, (CUM_ROWS, CUM_ROWS), 0)
    col = lax.broadcasted_iota(I32, (CUM_ROWS, CUM_ROWS), 1)
    tril = jnp.where(col <= row, 1.0, 0.0).astype(F32)
    cs = jnp.dot(tril, lf, preferred_element_type=F32, precision=lax.Precision.HIGHEST) + carry_ref[...]
    cum_ref[...] = cs
    carry_ref[...] = cs[CUM_ROWS - 1:CUM_ROWS, :]


def _fox_cum(small, fox_f_bias, bsz, seq):
    fb = jnp.zeros((1, LANES), F32).at[0, SMALL_F:SMALL_F + FOX_HEADS].set(fox_f_bias.astype(F32))
    nj = seq // CUM_ROWS
    return pl.pallas_call(
        _cum_kernel,
        name="fox_cum",
        grid=(bsz, nj),
        in_specs=[pl.BlockSpec((CUM_ROWS, LANES), lambda b, j: (b * nj + j, 0)),
                  pl.BlockSpec((1, LANES), lambda b, j: (0, 0))],
        out_specs=pl.BlockSpec((CUM_ROWS, LANES), lambda b, j: (b * nj + j, 0)),
        out_shape=jax.ShapeDtypeStruct((bsz * seq, LANES), F32),
        scratch_shapes=[pltpu.VMEM((1, LANES), F32)],
        compiler_params=_cparams(("parallel", "arbitrary")),
    )(small, fb)


def _fox_kernel(q_ref, k_ref, v_ref, cq_ref, ck_ref, o_ref, m_ref, l_ref, acc_ref, *, tq, tk):
    h = pl.program_id(1)
    qi = pl.program_id(2)
    ki = pl.program_id(3)
    scale = 1.0 / math.sqrt(FOX_HEAD_DIM)

    @pl.when(ki == 0)
    def _():
        m_ref[...] = jnp.full_like(m_ref, -jnp.inf)
        l_ref[...] = jnp.zeros_like(l_ref)
        acc_ref[...] = jnp.zeros_like(acc_ref)

    def step(masked):
        s = lax.dot_general(q_ref[...], k_ref[...], (((1,), (1,)), ((), ())),
                            preferred_element_type=F32) * scale
        lane = lax.broadcasted_iota(I32, (tq, LANES), 1)
        cq = jnp.sum(jnp.where(lane == h + SMALL_F, cq_ref[...], 0.0), axis=-1, keepdims=True)
        s = s + cq - ck_ref[...]
        if masked:
            row = lax.broadcasted_iota(I32, (tq, tk), 0)
            col = lax.broadcasted_iota(I32, (tq, tk), 1)
            s = jnp.where(col <= row, s, -jnp.inf)
        m_prev = m_ref[...]
        m_new = jnp.maximum(m_prev, jnp.max(s, axis=-1, keepdims=True))
        alpha = jnp.exp(m_prev - m_new)
        p = jnp.exp(s - m_new)
        l_ref[...] = alpha * l_ref[...] + jnp.sum(p, axis=-1, keepdims=True)
        acc_ref[...] = alpha * acc_ref[...] + jnp.dot(p.astype(BF16), v_ref[...], preferred_element_type=F32)
        m_ref[...] = m_new

    @pl.when(ki < qi)
    def _():
        step(False)

    @pl.when(ki == qi)
    def _():
        step(True)
        o_ref[...] = (acc_ref[...] / l_ref[...]).astype(o_ref.dtype)


def _fox(proj, cum, cum_row, bsz, seq, tq=512):
    tk = tq
    nq = seq // tq
    t = bsz * seq
    d = FOX_HEAD_DIM
    kern = functools.partial(_fox_kernel, tq=tq, tk=tk)

    def kv_row(b, qi, ki):
        return b * nq + jnp.minimum(ki, qi)

    return pl.pallas_call(
        kern,
        name="fox",
        grid=(bsz, FOX_HEADS, nq, nq),
        in_specs=[
            pl.BlockSpec((tq, d), lambda b, h, qi, ki: (b * nq + qi, COL_Q // d + h)),
            pl.BlockSpec((tk, d), lambda b, h, qi, ki: (kv_row(b, qi, ki), COL_K // d + h)),
            pl.BlockSpec((tk, d), lambda b, h, qi, ki: (kv_row(b, qi, ki), COL_V // d + h)),
            pl.BlockSpec((tq, LANES), lambda b, h, qi, ki: (b * nq + qi, 0)),
            pl.BlockSpec((None, 1, tk), lambda b, h, qi, ki: (b * FOX_HEADS + h, 0, jnp.minimum(ki, qi))),
        ],
        out_specs=pl.BlockSpec((tq, d), lambda b, h, qi, ki: (b * nq + qi, h)),
        out_shape=jax.ShapeDtypeStruct((t, FOX_DIM), BF16),
        scratch_shapes=[pltpu.VMEM((tq, 1), F32), pltpu.VMEM((tq, 1), F32), pltpu.VMEM((tq, d), F32)],
        compiler_params=_cparams(("parallel", "parallel", "parallel", "arbitrary")),
    )(proj, proj, proj, cum, cum_row)


def _mix_kernel(ya_ref, yb_ref, ga_ref, gb_ref, wa_ref, wb_ref, o_ref):
    pa = jnp.dot(ya_ref[...], wa_ref[...], preferred_element_type=F32)
    pb = jnp.dot(yb_ref[...], wb_ref[...], preferred_element_type=F32)
    ga = 1.0 / (1.0 + jnp.exp(-ga_ref[...].astype(F32)))
    gb = 1.0 / (1.0 + jnp.exp(-gb_ref[...].astype(F32)))
    o_ref[...] = (ga * pa + gb * pb).astype(o_ref.dtype)


def _mix(y_a, y_b, proj, w_a, w_b, tm=512, tn=1024):
    t = y_a.shape[0]
    return pl.pallas_call(
        _mix_kernel,
        name="mix",
        grid=(D_MODEL // tn, t // tm),
        in_specs=[
            pl.BlockSpec((tm, SSD_DIM), lambda j, i: (i, 0)),
            pl.BlockSpec((tm, FOX_DIM), lambda j, i: (i, 0)),
            pl.BlockSpec((tm, tn), lambda j, i: (i, COL_GA // tn + j)),
            pl.BlockSpec((tm, tn), lambda j, i: (i, COL_GB // tn + j)),
            pl.BlockSpec((SSD_DIM, tn), lambda j, i: (0, j)),
            pl.BlockSpec((FOX_DIM, tn), lambda j, i: (0, j)),
        ],
        out_specs=pl.BlockSpec((tm, tn), lambda j, i: (i, j)),
        out_shape=jax.ShapeDtypeStruct((t, D_MODEL), BF16),
        compiler_params=_cparams(("parallel", "parallel")),
    )(y_a, y_b, proj, proj, w_a, w_b)


def _outproj_kernel(m_ref, x_ref, wo_ref, nw_ref, wr_ref, br_ref, h_ref, u_ref, eidx_ref, wts_ref):
    h1 = x_ref[...] + jnp.dot(m_ref[...], wo_ref[...], preferred_element_type=F32)
    h_ref[...] = h1
    ms = jnp.mean(h1 * h1, axis=-1, keepdims=True)
    u2 = h1 * lax.rsqrt(ms + EPS) * nw_ref[...]
    u_ref[...] = _pack_bf16_pair(u2)

    logits = jnp.dot(u2, wr_ref[...], preferred_element_type=F32, precision=lax.Precision.HIGHEST) + br_ref[...]
    tm = logits.shape[0]
    lane = lax.broadcasted_iota(I32, (tm, LANES), 1)
    neg = -jnp.inf
    big = jnp.int32(2 * LANES)
    gl = jnp.where(lane < N_GROUPS, logits, neg)
    gmax = jnp.max(gl, axis=-1, keepdims=True)
    gsum = jnp.sum(jnp.exp(gl - gmax), axis=-1, keepdims=True)
    g_p = 1.0 / gsum
    g_idx = jnp.min(jnp.where(gl == gmax, lane, big), axis=-1, keepdims=True)
    e_of_lane = lane - N_GROUPS
    in_grp = (e_of_lane >= g_idx * EXPERTS_PER_GROUP) & (e_of_lane < (g_idx + 1) * EXPERTS_PER_GROUP)
    el = jnp.where(in_grp, logits, neg)
    m1 = jnp.max(el, axis=-1, keepdims=True)
    i1 = jnp.min(jnp.where(el == m1, lane, big), axis=-1, keepdims=True)
    el2 = jnp.where(lane == i1, neg, el)
    m2 = jnp.max(el2, axis=-1, keepdims=True)
    i2 = jnp.min(jnp.where(el2 == m2, lane, big), axis=-1, keepdims=True)
    esum = jnp.sum(jnp.exp(el - m1), axis=-1, keepdims=True)
    p1 = 1.0 / esum
    p2 = jnp.exp(m2 - m1) / esum
    w1 = g_p * (p1 / (p1 + p2))
    w2 = g_p * (p2 / (p1 + p2))
    eidx_ref[...] = jnp.where(lane == 0, i1 - N_GROUPS, jnp.where(lane == 1, i2 - N_GROUPS, 0))
    wts_ref[...] = jnp.where(lane == 0, w1, jnp.where(lane == 1, w2, 0.0))


def _outproj(mixed, x2, w_o, nw, w_router, b_router, tm=256):
    t = mixed.shape[0]
    return pl.pallas_call(
        _outproj_kernel,
        name="outproj",
        grid=(t // tm,),
        in_specs=[
            pl.BlockSpec((tm, D_MODEL), lambda i: (i, 0)),
            pl.BlockSpec((tm, D_MODEL), lambda i: (i, 0)),
            pl.BlockSpec((D_MODEL, D_MODEL), lambda i: (0, 0)),
            pl.BlockSpec((1, D_MODEL), lambda i: (0, 0)),
            pl.BlockSpec((D_MODEL, LANES), lambda i: (0, 0)),
            pl.BlockSpec((1, LANES), lambda i: (0, 0)),
        ],
        out_specs=[
            pl.BlockSpec((tm, D_MODEL), lambda i: (i, 0)),
            pl.BlockSpec((tm, HALF), lambda i: (i, 0)),
            pl.BlockSpec((tm, LANES), lambda i: (i, 0)),
            pl.BlockSpec((tm, LANES), lambda i: (i, 0)),
        ],
        out_shape=[
            jax.ShapeDtypeStruct((t, D_MODEL), F32),
            jax.ShapeDtypeStruct((t, HALF), U32),
            jax.ShapeDtypeStruct((t, LANES), I32),
            jax.ShapeDtypeStruct((t, LANES), F32),
        ],
        compiler_params=_cparams(("parallel",)),
    )(mixed, x2, w_o, nw, w_router, b_router)


RANK_BLOCK = 512


def _rank_kernel(e_ref, rank_ref, cnt_ref, carry_ref):
    i = pl.program_id(0)
    r = RANK_BLOCK

    @pl.when(i == 0)
    def _():
        carry_ref[...] = jnp.zeros_like(carry_ref)

    e = e_ref[0]
    expert = lax.broadcasted_iota(I32, (N_EXPERTS, r), 0)
    onehot = jnp.where(expert == e, 1.0, 0.0).astype(F32)
    jrow = lax.broadcasted_iota(I32, (r, r), 0)
    jcol = lax.broadcasted_iota(I32, (r, r), 1)
    before = jnp.where(jrow < jcol, 1.0, 0.0).astype(BF16)
    cum = jnp.dot(onehot.astype(BF16), before, preferred_element_type=F32)
    carry = carry_ref[...]
    rank = jnp.sum(onehot * (cum + carry[:, 0:1]), axis=0, keepdims=True)
    rank_ref[0] = rank.astype(I32)
    carry = carry + jnp.sum(onehot, axis=1, keepdims=True)
    carry_ref[...] = carry
    cnt_ref[...] = carry


def _rank(e_blocks):
    nb = e_blocks.shape[0]
    return pl.pallas_call(
        _rank_kernel,
        name="rank",
        grid=(nb,),
        in_specs=[pl.BlockSpec((1, 1, RANK_BLOCK), lambda i: (i, 0, 0))],
        out_specs=[pl.BlockSpec((1, 1, RANK_BLOCK), lambda i: (i, 0, 0)),
                   pl.BlockSpec((N_EXPERTS, LANES), lambda i: (0, 0))],
        out_shape=[jax.ShapeDtypeStruct((nb, 1, RANK_BLOCK), I32),
                   jax.ShapeDtypeStruct((N_EXPERTS, LANES), F32)],
        scratch_shapes=[pltpu.VMEM((N_EXPERTS, LANES), F32)],
        compiler_params=_cparams(("arbitrary",)),
    )(e_blocks)


def _dest_kernel(e_ref, rank_ref, pstart_ref, dest_ref):
    e = e_ref[0]
    expert = lax.broadcasted_iota(I32, (N_EXPERTS, RANK_BLOCK), 0)
    start = jnp.sum(jnp.where(expert == e, pstart_ref[:, 0:1], 0), axis=0, keepdims=True)
    dest_ref[0] = rank_ref[0] + start


def _dest(e_blocks, rank, pstart):
    nb = e_blocks.shape[0]
    return pl.pallas_call(
        _dest_kernel,
        name="dest",
        grid=(nb,),
        in_specs=[pl.BlockSpec((1, 1, RANK_BLOCK), lambda i: (i, 0, 0)),
                  pl.BlockSpec((1, 1, RANK_BLOCK), lambda i: (i, 0, 0)),
                  pl.BlockSpec((N_EXPERTS, LANES), lambda i: (0, 0))],
        out_specs=pl.BlockSpec((1, 1, RANK_BLOCK), lambda i: (i, 0, 0)),
        out_shape=jax.ShapeDtypeStruct((nb, 1, RANK_BLOCK), I32),
        compiler_params=_cparams(("parallel",)),
    )(e_blocks, rank, pstart)


def _row_copy(src_ref, src_row, dst_ref, dst_row, sem):
    return pltpu.make_async_copy(src_ref.at[pl.ds(src_row, 1)], dst_ref.at[pl.ds(dst_row, 1)], sem)


def _dispatch_kernel(dest_ref, u_ref, zeros_ref, xs_ref, sem):
    del zeros_ref
    i = pl.program_id(0)
    tok0 = i * (RANK_BLOCK // TOP_K)

    def issue(c, carry):
        _row_copy(u_ref, tok0 + c // TOP_K, xs_ref, dest_ref[0, 0, c], sem).start()
        return carry

    lax.fori_loop(0, RANK_BLOCK, issue, 0)

    def drain(c, carry):
        _row_copy(u_ref, 0, xs_ref, 0, sem).wait()
        return carry

    lax.fori_loop(0, RANK_BLOCK, drain, 0)


def _dispatch(dest, u2p, n_rows):
    nb = dest.shape[0]
    zeros = jnp.zeros((n_rows, HALF), U32)
    return pl.pallas_call(
        _dispatch_kernel,
        name="dispatch",
        grid=(nb,),
        in_specs=[pl.BlockSpec((1, 1, RANK_BLOCK), lambda i: (i, 0, 0), memory_space=pltpu.SMEM),
                  pl.BlockSpec(memory_space=pl.ANY),
                  pl.BlockSpec(memory_space=pl.ANY)],
        out_specs=pl.BlockSpec(memory_space=pl.ANY),
        out_shape=jax.ShapeDtypeStruct((n_rows, HALF), U32),
        scratch_shapes=[pltpu.SemaphoreType.DMA(())],
        input_output_aliases={2: 0},
        compiler_params=_cparams(("arbitrary",)),
    )(dest, u2p, zeros)


def _experts_kernel(be_ref, nused_ref, x_ref, wg_ref, wu_ref, wd_ref, y_ref):
    i = pl.program_id(0)

    @pl.when(i < nused_ref[0])
    def _():
        lo, hi = _unpack_bf16_pair(x_ref[...])
        lo = lo.astype(BF16)
        hi = hi.astype(BF16)
        gate = (jnp.dot(lo, wg_ref[0, :HALF, :], preferred_element_type=F32)
                + jnp.dot(hi, wg_ref[0, HALF:, :], preferred_element_type=F32))
        up = (jnp.dot(lo, wu_ref[0, :HALF, :], preferred_element_type=F32)
              + jnp.dot(hi, wu_ref[0, HALF:, :], preferred_element_type=F32))
        hdn = (_silu(gate) * up).astype(BF16)
        y = jnp.dot(hdn, wd_ref[0], preferred_element_type=F32)
        y_ref[...] = _pack_bf16_pair(y)

    @pl.when(i >= nused_ref[0])
    def _():
        y_ref[...] = jnp.zeros_like(y_ref)


def _experts(block_expert, n_used, xs, w_gate, w_up, w_down):
    n_rows = xs.shape[0]
    n_blocks = n_rows // ROW_BLOCK

    def xmap(i, be, nu):
        return (jnp.minimum(i, jnp.maximum(nu[0] - 1, 0)), 0)

    def wmap(i, be, nu):
        return (be[i], 0, 0)

    grid_spec = pltpu.PrefetchScalarGridSpec(
        num_scalar_prefetch=2,
        grid=(n_blocks,),
        in_specs=[pl.BlockSpec((ROW_BLOCK, HALF), xmap),
                  pl.BlockSpec((1, D_MODEL, D_EXPERT), wmap),
                  pl.BlockSpec((1, D_MODEL, D_EXPERT), wmap),
                  pl.BlockSpec((1, D_EXPERT, D_MODEL), wmap)],
        out_specs=pl.BlockSpec((ROW_BLOCK, HALF), lambda i, be, nu: (i, 0)),
    )
    return pl.pallas_call(
        _experts_kernel,
        name="experts",
        grid_spec=grid_spec,
        out_shape=jax.ShapeDtypeStruct((n_rows, HALF), U32),
        compiler_params=_cparams(("arbitrary",)),
    )(block_expert, n_used, xs, w_gate, w_up, w_down)


COMBINE_TOKENS = RANK_BLOCK // TOP_K


def _combine_kernel(dest_ref, h_ref, wts_ref, nw_ref, y_ref, o_ref, buf_ref, sem):
    ts = COMBINE_TOKENS

    def issue(c, carry):
        _row_copy(y_ref, dest_ref[0, 0, c], buf_ref, (c % TOP_K) * ts + c // TOP_K, sem).start()
        return carry

    lax.fori_loop(0, RANK_BLOCK, issue, 0)

    def drain(c, carry):
        _row_copy(y_ref, 0, buf_ref, 0, sem).wait()
        return carry

    lax.fori_loop(0, RANK_BLOCK, drain, 0)

    w = wts_ref[...]
    w0 = w[:, 0:1]
    w1 = w[:, 1:2]
    lo0, hi0 = _unpack_bf16_pair(buf_ref[0:ts, :])
    lo1, hi1 = _unpack_bf16_pair(buf_ref[ts:2 * ts, :])
    h = h_ref[...]
    out_lo = h[:, :HALF] + w0 * lo0 + w1 * lo1
    out_hi = h[:, HALF:] + w0 * hi0 + w1 * hi1
    ms = (jnp.sum(out_lo * out_lo, axis=-1, keepdims=True)
          + jnp.sum(out_hi * out_hi, axis=-1, keepdims=True)) * (1.0 / D_MODEL)
    inv = lax.rsqrt(ms + EPS)
    o_ref[:, :HALF] = out_lo * inv * nw_ref[:, :HALF]
    o_ref[:, HALF:] = out_hi * inv * nw_ref[:, HALF:]


def _combine(dest, h1, wts, nw, y):
    t = h1.shape[0]
    ts = COMBINE_TOKENS
    return pl.pallas_call(
        _combine_kernel,
        name="combine",
        grid=(t // ts,),
        in_specs=[pl.BlockSpec((1, 1, RANK_BLOCK), lambda i: (i, 0, 0), memory_space=pltpu.SMEM),
                  pl.BlockSpec((ts, D_MODEL), lambda i: (i, 0)),
                  pl.BlockSpec((ts, LANES), lambda i: (i, 0)),
                  pl.BlockSpec((1, D_MODEL), lambda i: (0, 0)),
                  pl.BlockSpec(memory_space=pl.ANY)],
        out_specs=pl.BlockSpec((ts, D_MODEL), lambda i: (i, 0)),
        out_shape=jax.ShapeDtypeStruct((t, D_MODEL), F32),
        scratch_shapes=[pltpu.VMEM((RANK_BLOCK, HALF), U32), pltpu.SemaphoreType.DMA(())],
        compiler_params=_cparams(("arbitrary",)),
    )(dest, h1, wts, nw, y)


def _permute_w_in(w_in):
    sizes = (SSD_DIM, CONV_DIM, SSD_HEADS, FOX_DIM, FOX_DIM, FOX_DIM, FOX_HEADS, 2 * D_MODEL)
    offs = [0]
    for s in sizes:
        offs.append(offs[-1] + s)
    z, xbc, dt, q, k, v, f, gates = (w_in[:, offs[i]:offs[i + 1]] for i in range(len(sizes)))
    wide = jnp.concatenate([z, q, k, v, gates, xbc], axis=1).astype(BF16)
    pad = jnp.zeros((w_in.shape[0], LANES - SSD_HEADS - FOX_HEADS), w_in.dtype)
    narrow = jnp.concatenate([dt, f, pad], axis=1).astype(BF16)
    return wide, narrow


def _layer(h, p, bsz, seq):
    t = bsz * seq
    wide, narrow = _permute_w_in(p["w_in"])
    u, small = _norm_small(h, p["norm_mix_w"].astype(F32).reshape(1, D_MODEL), narrow)
    proj = _matmul(u, wide)
    y_ssd = _ssd(proj, small, p["conv_w"], p["conv_b"], p["dt_bias"], p["a_log"], p["d_skip"],
                 p["ssd_norm_w"], bsz, seq)
    cum = _fox_cum(small, p["fox_f_bias"], bsz, seq)
    cum_row = cum[:, SMALL_F:SMALL_F + FOX_HEADS].reshape(bsz, seq, FOX_HEADS).transpose(0, 2, 1)
    cum_row = cum_row.reshape(bsz * FOX_HEADS, 1, seq)
    y_fox = _fox(proj, cum, cum_row, bsz, seq)
    mixed = _mix(y_ssd, y_fox, proj, p["w_proj_ssd"].astype(BF16), p["w_proj_fox"].astype(BF16))

    w_router = jnp.concatenate(
        [p["w_router_group"], p["w_router_expert"],
         jnp.zeros((D_MODEL, LANES - N_GROUPS - N_EXPERTS), F32)], axis=1).astype(F32)
    b_router = jnp.concatenate(
        [p["b_router_group"], p["b_router_expert"],
         jnp.zeros((LANES - N_GROUPS - N_EXPERTS,), F32)]).astype(F32).reshape(1, LANES)
    h1, u2p, eidx, wts = _outproj(mixed, h, p["w_out"].astype(BF16),
                                  p["norm_moe_w"].astype(F32).reshape(1, D_MODEL), w_router, b_router)

    tk = t * TOP_K
    e_blocks = eidx[:, :TOP_K].reshape(tk // RANK_BLOCK, 1, RANK_BLOCK)
    rank, counts = _rank(e_blocks)
    counts = counts[:, 0].astype(I32)
    padded = ((counts + ROW_BLOCK - 1) // ROW_BLOCK) * ROW_BLOCK
    pend = jnp.cumsum(padded)
    pstart = pend - padded
    n_blocks = tk // ROW_BLOCK + N_EXPERTS
    block_expert = jnp.clip(jnp.searchsorted(pend, jnp.arange(n_blocks, dtype=I32) * ROW_BLOCK, side="right"),
                            0, N_EXPERTS - 1).astype(I32)
    n_used = (pend[-1:] // ROW_BLOCK).astype(I32)
    dest = _dest(e_blocks, rank, jnp.broadcast_to(pstart[:, None], (N_EXPERTS, LANES)).astype(I32))
    xs = _dispatch(dest, u2p, n_blocks * ROW_BLOCK)
    y = _experts(block_expert, n_used, xs, p["w_gate_exp"].astype(BF16), p["w_up_exp"].astype(BF16),
                 p["w_down_exp"].astype(BF16))
    return dest, h1, wts, y


def kernel(x, norm_mix_w, w_in, conv_w, conv_b, dt_bias, a_log, d_skip, ssd_norm_w, fox_f_bias, w_proj_ssd,
           w_proj_fox, w_out, norm_moe_w, w_router_group, b_router_group, w_router_expert, b_router_expert,
           w_gate_exp, w_up_exp, w_down_exp, norm_final_w):
    bsz, seq, _ = x.shape
    depth = w_in.shape[0]
    assert depth == 1, "the fused final norm assumes a single layer"
    stacked = dict(norm_mix_w=norm_mix_w, w_in=w_in, conv_w=conv_w, conv_b=conv_b, dt_bias=dt_bias, a_log=a_log,
                   d_skip=d_skip, ssd_norm_w=ssd_norm_w, fox_f_bias=fox_f_bias, w_proj_ssd=w_proj_ssd,
                   w_proj_fox=w_proj_fox, w_out=w_out, norm_moe_w=norm_moe_w, w_router_group=w_router_group,
                   b_router_group=b_router_group, w_router_expert=w_router_expert,
                   b_router_expert=b_router_expert, w_gate_exp=w_gate_exp, w_up_exp=w_up_exp,
                   w_down_exp=w_down_exp)
    p = {name: v[0] for name, v in stacked.items()}
    h = x.reshape(bsz * seq, D_MODEL)
    dest, h1, wts, y = _layer(h, p, bsz, seq)
    out = _combine(dest, h1, wts, norm_final_w.astype(F32).reshape(1, D_MODEL), y)
    return out.reshape(bsz, seq, D_MODEL)
```

```python
import functools
import math

import jax
import jax.numpy as jnp
from jax import lax
from jax.experimental import pallas as pl
from jax.experimental.pallas import tpu as pltpu

F32 = jnp.float32
BF16 = jnp.bfloat16
I32 = jnp.int32
U32 = jnp.uint32

D_MODEL = 2048
SSD_HEADS = 32
SSD_HEAD_DIM = 64
SSD_DIM = SSD_HEADS * SSD_HEAD_DIM
SSD_GROUPS = 4
SSD_STATE = 128
SSD_CHUNK = 128
CONV_WIDTH = 4
CONV_DIM = SSD_DIM + 2 * SSD_GROUPS * SSD_STATE
FOX_HEADS = 16
FOX_HEAD_DIM = 128
FOX_DIM = FOX_HEADS * FOX_HEAD_DIM
N_GROUPS = 8
EXPERTS_PER_GROUP = 8
N_EXPERTS = N_GROUPS * EXPERTS_PER_GROUP
TOP_K = 2
D_EXPERT = 512
EPS = 1e-6

LANES = 128
SUBLANES = 8
VMEM_LIMIT = 52 * 1024 * 1024

COL_Z = 0
COL_Q = COL_Z + SSD_DIM
COL_K = COL_Q + FOX_DIM
COL_V = COL_K + FOX_DIM
COL_GA = COL_V + FOX_DIM
COL_GB = COL_GA + D_MODEL
COL_XBC = COL_GB + D_MODEL
PROJ_COLS = COL_XBC + CONV_DIM
SMALL_DT = 0
SMALL_F = SSD_HEADS

ROW_BLOCK = 128
HALF = D_MODEL // 2


def _cparams(sem, vmem=VMEM_LIMIT):
    return pltpu.CompilerParams(dimension_semantics=sem, vmem_limit_bytes=vmem)


def _silu(x):
    return x * (1.0 / (1.0 + jnp.exp(-x)))


def _softplus(x):
    return jnp.maximum(x, 0.0) + jnp.log(1.0 + jnp.exp(-jnp.abs(x)))


def _log_sigmoid(x):
    return -_softplus(-x)


def _pack_bf16_pair(x):
    n = x.shape[1] // 2
    lo = pltpu.bitcast(x[:, :n].astype(BF16).astype(F32), U32)
    hi = pltpu.bitcast(x[:, n:].astype(BF16).astype(F32), U32)
    return (hi & jnp.uint32(0xFFFF0000)) | (lo >> 16)


def _unpack_bf16_pair(p):
    lo = pltpu.bitcast(p << 16, F32)
    hi = pltpu.bitcast(p & jnp.uint32(0xFFFF0000), F32)
    return lo, hi


def _norm_small_kernel(x_ref, nw_ref, ws_ref, u_ref, s_ref):
    x = x_ref[...]
    ms = jnp.mean(x * x, axis=-1, keepdims=True)
    u = (x * lax.rsqrt(ms + EPS) * nw_ref[...]).astype(BF16)
    u_ref[...] = u
    s_ref[...] = jnp.dot(u, ws_ref[...], preferred_element_type=F32)


def _norm_small(x2, nw, w_small, tm=512):
    t = x2.shape[0]
    return pl.pallas_call(
        _norm_small_kernel,
        name="norm_small",
        grid=(t // tm,),
        in_specs=[pl.BlockSpec((tm, D_MODEL), lambda i: (i, 0)),
                  pl.BlockSpec((1, D_MODEL), lambda i: (0, 0)),
                  pl.BlockSpec((D_MODEL, LANES), lambda i: (0, 0))],
        out_specs=[pl.BlockSpec((tm, D_MODEL), lambda i: (i, 0)),
                   pl.BlockSpec((tm, LANES), lambda i: (i, 0))],
        out_shape=[jax.ShapeDtypeStruct((t, D_MODEL), BF16),
                   jax.ShapeDtypeStruct((t, LANES), F32)],
        compiler_params=_cparams(("parallel",)),
    )(x2, nw, w_small)


def _matmul_kernel(a_ref, b_ref, o_ref):
    o_ref[...] = jnp.dot(a_ref[...], b_ref[...], preferred_element_type=F32).astype(o_ref.dtype)


def _matmul(a, b, tm=1024, tn=1024):
    m, k = a.shape
    n = b.shape[1]
    return pl.pallas_call(
        _matmul_kernel,
        name="in_proj",
        grid=(m // tm, n // tn),
        in_specs=[pl.BlockSpec((tm, k), lambda i, j: (i, 0)),
                  pl.BlockSpec((k, tn), lambda i, j: (0, j))],
        out_specs=pl.BlockSpec((tm, tn), lambda i, j: (i, j)),
        out_shape=jax.ShapeDtypeStruct((m, n), BF16),
        compiler_params=_cparams(("parallel", "parallel")),
    )(a, b)


HALO = SUBLANES


def _ssd_kernel(z_ref, xbc_ref, halo_ref, small_ref, cw_ref, cb_ref, dtb_ref, aneg_ref, dexp_ref,
                nw_ref, expand_ref, y_ref, ext_ref, state_ref, ydiag_ref):
    c = pl.program_id(1)
    l = SSD_CHUNK
    n = SSD_STATE

    @pl.when(c == 0)
    def _():
        state_ref[...] = jnp.zeros_like(state_ref)

    halo = halo_ref[...].astype(F32)
    halo = jnp.where(c == 0, jnp.zeros_like(halo), halo)
    ext_ref[0:HALO, :] = halo
    ext_ref[HALO:HALO + l, :] = xbc_ref[...].astype(F32)
    conv = cb_ref[...] + cw_ref[CONV_WIDTH - 1:CONV_WIDTH, :] * ext_ref[HALO:HALO + l, :]
    for j in range(CONV_WIDTH - 1):
        off = HALO - (CONV_WIDTH - 1) + j
        conv = conv + cw_ref[j:j + 1, :] * ext_ref[off:off + l, :]
    xbc = _silu(conv)
    xs = xbc[:, :SSD_DIM]
    bm = xbc[:, SSD_DIM:SSD_DIM + SSD_GROUPS * n]
    cm = xbc[:, SSD_DIM + SSD_GROUPS * n:]

    dt = _softplus(small_ref[:, SMALL_DT:SMALL_DT + SSD_HEADS] + dtb_ref[...])
    adt = dt * aneg_ref[...]
    row = lax.broadcasted_iota(I32, (l, l), 0)
    col = lax.broadcasted_iota(I32, (l, l), 1)
    causal = col <= row
    tril = jnp.where(causal, 1.0, 0.0).astype(F32)
    a_cs = jnp.dot(tril, adt, preferred_element_type=F32, precision=lax.Precision.HIGHEST)
    a_cs_t = jnp.transpose(a_cs)
    a_last = a_cs[l - 1:l, :]

    expand = expand_ref[...]
    hp = lax.Precision.HIGHEST
    dt_x = jnp.dot(dt, expand, preferred_element_type=F32, precision=hp)
    acs_x = jnp.dot(a_cs, expand, preferred_element_type=F32, precision=hp)
    alast_x = jnp.dot(a_last, expand, preferred_element_type=F32, precision=hp)
    x_dt = xs * dt_x
    decay_in = jnp.exp(acs_x)
    decay_out = jnp.exp(alast_x - acs_x)
    chunk_decay = jnp.exp(alast_x)
    x_dt_b = x_dt.astype(BF16)
    xd_b = (x_dt * decay_out).astype(BF16)

    lane = lax.broadcasted_iota(I32, (l, LANES), 1)
    lo_mask = lane < SSD_HEAD_DIM
    heads_per_group = SSD_HEADS // SSD_GROUPS
    gw = heads_per_group * SSD_HEAD_DIM

    for g in range(SSD_GROUPS):
        bg = bm[:, g * n:(g + 1) * n].astype(BF16)
        cg = cm[:, g * n:(g + 1) * n].astype(BF16)
        cb = lax.dot_general(cg, bg, (((1,), (1,)), ((), ())), preferred_element_type=F32)
        for pair in range(heads_per_group // 2):
            h0 = g * heads_per_group + 2 * pair
            lane0 = h0 * SSD_HEAD_DIM
            ms = []
            for h in (h0, h0 + 1):
                seg = a_cs[:, h:h + 1] - a_cs_t[h:h + 1, :]
                lmat = jnp.exp(jnp.where(causal, seg, -jnp.inf))
                ms.append((cb * lmat).astype(BF16))
            lhs = jnp.concatenate(ms, axis=1)
            xp = x_dt_b[:, lane0:lane0 + LANES]
            zero = jnp.zeros_like(xp)
            rhs = jnp.concatenate([jnp.where(lo_mask, xp, zero), jnp.where(lo_mask, zero, xp)], axis=0)
            ydiag_ref[:, lane0:lane0 + LANES] = jnp.dot(lhs, rhs, preferred_element_type=F32)
        st = state_ref[:, g * gw:(g + 1) * gw]
        y_off = jnp.dot(cg, st.astype(BF16), preferred_element_type=F32)
        ydiag_ref[:, g * gw:(g + 1) * gw] += y_off * decay_in[:, g * gw:(g + 1) * gw]
        new = lax.dot_general(bg, xd_b[:, g * gw:(g + 1) * gw], (((0,), (0,)), ((), ())),
                              preferred_element_type=F32)
        state_ref[:, g * gw:(g + 1) * gw] = st * chunk_decay[:, g * gw:(g + 1) * gw] + new

    y = ydiag_ref[...] + dexp_ref[...] * xs
    y = y * _silu(z_ref[...].astype(F32))
    ms = jnp.mean(y * y, axis=-1, keepdims=True)
    y_ref[...] = (y * lax.rsqrt(ms + EPS) * nw_ref[...]).astype(y_ref.dtype)


def _ssd(proj, small, conv_w, conv_b, dt_bias, a_log, d_skip, norm_w, bsz, seq):
    l = SSD_CHUNK
    nc = seq // l
    t = bsz * seq
    aneg = -jnp.exp(a_log.astype(F32)).reshape(1, SSD_HEADS)
    dexp = jnp.repeat(d_skip.astype(F32), SSD_HEAD_DIM).reshape(1, SSD_DIM)
    expand = jnp.repeat(jnp.eye(SSD_HEADS, dtype=F32), SSD_HEAD_DIM, axis=1)
    xbc_blk = COL_XBC // CONV_DIM
    halo_per_chunk = l // HALO

    def row_map(b, c):
        return b * nc + c

    return pl.pallas_call(
        _ssd_kernel,
        name="ssd",
        grid=(bsz, nc),
        in_specs=[
            pl.BlockSpec((l, SSD_DIM), lambda b, c: (row_map(b, c), COL_Z // SSD_DIM)),
            pl.BlockSpec((l, CONV_DIM), lambda b, c: (row_map(b, c), xbc_blk)),
            pl.BlockSpec((HALO, CONV_DIM),
                         lambda b, c: (jnp.maximum(row_map(b, c) * halo_per_chunk - 1, 0), xbc_blk)),
            pl.BlockSpec((l, LANES), lambda b, c: (row_map(b, c), 0)),
            pl.BlockSpec((CONV_WIDTH, CONV_DIM), lambda b, c: (0, 0)),
            pl.BlockSpec((1, CONV_DIM), lambda b, c: (0, 0)),
            pl.BlockSpec((1, SSD_HEADS), lambda b, c: (0, 0)),
            pl.BlockSpec((1, SSD_HEADS), lambda b, c: (0, 0)),
            pl.BlockSpec((1, SSD_DIM), lambda b, c: (0, 0)),
            pl.BlockSpec((1, SSD_DIM), lambda b, c: (0, 0)),
            pl.BlockSpec((SSD_HEADS, SSD_DIM), lambda b, c: (0, 0)),
        ],
        out_specs=pl.BlockSpec((l, SSD_DIM), lambda b, c: (row_map(b, c), 0)),
        out_shape=jax.ShapeDtypeStruct((t, SSD_DIM), BF16),
        scratch_shapes=[pltpu.VMEM((HALO + l, CONV_DIM), F32),
                        pltpu.VMEM((SSD_STATE, SSD_DIM), F32),
                        pltpu.VMEM((l, SSD_DIM), F32)],
        compiler_params=_cparams(("parallel", "arbitrary")),
    )(proj, proj, proj, small, conv_w.astype(F32), conv_b.astype(F32).reshape(1, CONV_DIM),
      dt_bias.astype(F32).reshape(1, SSD_HEADS), aneg, dexp, norm_w.astype(F32).reshape(1, SSD_DIM), expand)


CUM_ROWS = 256


def _cum_kernel(small_ref, fb_ref, cum_ref, carry_ref):
    j = pl.program_id(1)

    @pl.when(j == 0)
    def _():
        carry_ref[...] = jnp.zeros_like(carry_ref)

    lf = _log_sigmoid(small_ref[...] + fb_ref[...])
    row = lax.broadcasted_iota(I32, (CUM_ROWS, CUM_ROWS), 0)
    col = lax.broadcasted_iota(I32, (CUM_ROWS, CUM_ROWS), 1)
    tril = jnp.where(col <= row, 1.0, 0.0).astype(F32)
    cs = jnp.dot(tril, lf, preferred_element_type=F32, precision=lax.Precision.HIGHEST) + carry_ref[...]
    cum_ref[...] = cs * LOG2E
    carry_ref[...] = cs[CUM_ROWS - 1:CUM_ROWS, :]


def _fox_cum(small, fox_f_bias, bsz, seq):
    fb = jnp.zeros((1, LANES), F32).at[0, SMALL_F:SMALL_F + FOX_HEADS].set(fox_f_bias.astype(F32))
    nj = seq // CUM_ROWS
    return pl.pallas_call(
        _cum_kernel,
        name="fox_cum",
        grid=(bsz, nj),
        in_specs=[pl.BlockSpec((CUM_ROWS, LANES), lambda b, j: (b * nj + j, 0)),
                  pl.BlockSpec((1, LANES), lambda b, j: (0, 0))],
        out_specs=pl.BlockSpec((CUM_ROWS, LANES), lambda b, j: (b * nj + j, 0)),
        out_shape=jax.ShapeDtypeStruct((bsz * seq, LANES), F32),
        scratch_shapes=[pltpu.VMEM((1, LANES), F32)],
        compiler_params=_cparams(("parallel", "arbitrary")),
    )(small, fb)


FOX_HEADS_PER_STEP = 2
LOG2E = 1.4426950408889634


def _fox_kernel(q_ref, k_ref, v_ref, cq_ref, ck_ref, o_ref, *, tq, tk):
    hp = pl.program_id(1)
    qi = pl.program_id(2)
    d = FOX_HEAD_DIM
    c2 = LOG2E / math.sqrt(d)
    lane = lax.broadcasted_iota(I32, (tq, LANES), 1)
    cq_all = cq_ref[...]

    qs, cqs = [], []
    for hh in range(FOX_HEADS_PER_STEP):
        qs.append((q_ref[:, hh * d:(hh + 1) * d].astype(F32) * c2).astype(BF16))
        head_lane = SMALL_F + hp * FOX_HEADS_PER_STEP + hh
        cqs.append(jnp.sum(jnp.where(lane == head_lane, cq_all, 0.0), axis=-1, keepdims=True))

    def block(ki, carry, masked):
        out = []
        row0 = pl.multiple_of(ki * tk, tk)
        for hh in range(FOX_HEADS_PER_STEP):
            m_prev, l_prev, acc_prev = carry[hh]
            k = k_ref[pl.ds(row0, tk), hh * d:(hh + 1) * d]
            v = v_ref[pl.ds(row0, tk), hh * d:(hh + 1) * d]
            s = lax.dot_general(qs[hh], k, (((1,), (1,)), ((), ())), preferred_element_type=F32)
            t = s - ck_ref[hh, pl.ds(ki, 1), :]
            if masked:
                row = lax.broadcasted_iota(I32, (tq, tk), 0)
                col = lax.broadcasted_iota(I32, (tq, tk), 1)
                t = jnp.where(col <= row, t, -jnp.inf)
            m_new = jnp.maximum(m_prev, jnp.max(t, axis=-1, keepdims=True) + cqs[hh])
            r = m_new - cqs[hh]
            p = jnp.exp2(t - r)
            alpha = jnp.exp2(m_prev - m_new)
            l_new = alpha * l_prev + jnp.sum(p, axis=-1, keepdims=True)
            acc_new = alpha * acc_prev + jnp.dot(p.astype(BF16), v, preferred_element_type=F32)
            out.append((m_new, l_new, acc_new))
        return tuple(out)

    init = tuple((jnp.full((tq, 1), -jnp.inf, F32), jnp.zeros((tq, 1), F32), jnp.zeros((tq, d), F32))
                 for _ in range(FOX_HEADS_PER_STEP))
    carry = lax.fori_loop(0, qi, lambda ki, c: block(ki, c, False), init)
    carry = block(qi, carry, True)
    for hh in range(FOX_HEADS_PER_STEP):
        _, l_fin, acc_fin = carry[hh]
        o_ref[:, hh * d:(hh + 1) * d] = (acc_fin / l_fin).astype(o_ref.dtype)


def _fox(proj, cum, bsz, seq, tq=512):
    tk = tq
    nq = seq // tq
    cum_row = cum[:, SMALL_F:SMALL_F + FOX_HEADS].reshape(bsz, seq, FOX_HEADS).transpose(0, 2, 1)
    cum_row = cum_row.reshape(bsz, FOX_HEADS, nq, tk)
    t = bsz * seq
    w = FOX_HEADS_PER_STEP * FOX_HEAD_DIM
    n_hp = FOX_HEADS // FOX_HEADS_PER_STEP
    kern = functools.partial(_fox_kernel, tq=tq, tk=tk)
    return pl.pallas_call(
        kern,
        name="fox",
        grid=(bsz, n_hp, nq),
        in_specs=[
            pl.BlockSpec((tq, w), lambda b, hp, qi: (b * nq + qi, COL_Q // w + hp)),
            pl.BlockSpec((seq, w), lambda b, hp, qi: (b, COL_K // w + hp)),
            pl.BlockSpec((seq, w), lambda b, hp, qi: (b, COL_V // w + hp)),
            pl.BlockSpec((tq, LANES), lambda b, hp, qi: (b * nq + qi, 0)),
            pl.BlockSpec((None, FOX_HEADS_PER_STEP, nq, tk), lambda b, hp, qi: (b, hp, 0, 0)),
        ],
        out_specs=pl.BlockSpec((tq, w), lambda b, hp, qi: (b * nq + qi, hp)),
        out_shape=jax.ShapeDtypeStruct((t, FOX_DIM), BF16),
        compiler_params=_cparams(("parallel", "parallel", "arbitrary")),
    )(proj, proj, proj, cum, cum_row)


def _mix_kernel(ya_ref, yb_ref, ga_ref, gb_ref, wa_ref, wb_ref, o_ref):
    pa = jnp.dot(ya_ref[...], wa_ref[...], preferred_element_type=F32)
    pb = jnp.dot(yb_ref[...], wb_ref[...], preferred_element_type=F32)
    ga = 1.0 / (1.0 + jnp.exp(-ga_ref[...].astype(F32)))
    gb = 1.0 / (1.0 + jnp.exp(-gb_ref[...].astype(F32)))
    o_ref[...] = (ga * pa + gb * pb).astype(o_ref.dtype)


def _mix(y_a, y_b, proj, w_a, w_b, tm=512, tn=1024):
    t = y_a.shape[0]
    return pl.pallas_call(
        _mix_kernel,
        name="mix",
        grid=(D_MODEL // tn, t // tm),
        in_specs=[
            pl.BlockSpec((tm, SSD_DIM), lambda j, i: (i, 0)),
            pl.BlockSpec((tm, FOX_DIM), lambda j, i: (i, 0)),
            pl.BlockSpec((tm, tn), lambda j, i: (i, COL_GA // tn + j)),
            pl.BlockSpec((tm, tn), lambda j, i: (i, COL_GB // tn + j)),
            pl.BlockSpec((SSD_DIM, tn), lambda j, i: (0, j)),
            pl.BlockSpec((FOX_DIM, tn), lambda j, i: (0, j)),
        ],
        out_specs=pl.BlockSpec((tm, tn), lambda j, i: (i, j)),
        out_shape=jax.ShapeDtypeStruct((t, D_MODEL), BF16),
        compiler_params=_cparams(("parallel", "parallel")),
    )(y_a, y_b, proj, proj, w_a, w_b)


def _outproj_kernel(m_ref, x_ref, wo_ref, nw_ref, wr_ref, br_ref, h_ref, u_ref, eidx_ref, wts_ref):
    h1 = x_ref[...] + jnp.dot(m_ref[...], wo_ref[...], preferred_element_type=F32)
    h_ref[...] = h1
    ms = jnp.mean(h1 * h1, axis=-1, keepdims=True)
    u2 = h1 * lax.rsqrt(ms + EPS) * nw_ref[...]
    u_ref[...] = _pack_bf16_pair(u2)

    u_hi = u2.astype(BF16)
    u_lo = (u2 - u_hi.astype(F32)).astype(BF16)
    hh_hl = jnp.dot(u_hi, wr_ref[...], preferred_element_type=F32)
    lh = jnp.dot(u_lo, wr_ref[:, :LANES], preferred_element_type=F32)
    logits = hh_hl[:, :LANES] + (hh_hl[:, LANES:] + lh) + br_ref[...]
    tm = logits.shape[0]
    lane = lax.broadcasted_iota(I32, (tm, LANES), 1)
    neg = -jnp.inf
    big = jnp.int32(2 * LANES)
    gl = jnp.where(lane < N_GROUPS, logits, neg)
    gmax = jnp.max(gl, axis=-1, keepdims=True)
    gsum = jnp.sum(jnp.exp(gl - gmax), axis=-1, keepdims=True)
    g_p = 1.0 / gsum
    g_idx = jnp.min(jnp.where(gl == gmax, lane, big), axis=-1, keepdims=True)
    e_of_lane = lane - N_GROUPS
    in_grp = (e_of_lane >= g_idx * EXPERTS_PER_GROUP) & (e_of_lane < (g_idx + 1) * EXPERTS_PER_GROUP)
    el = jnp.where(in_grp, logits, neg)
    m1 = jnp.max(el, axis=-1, keepdims=True)
    i1 = jnp.min(jnp.where(el == m1, lane, big), axis=-1, keepdims=True)
    el2 = jnp.where(lane == i1, neg, el)
    m2 = jnp.max(el2, axis=-1, keepdims=True)
    i2 = jnp.min(jnp.where(el2 == m2, lane, big), axis=-1, keepdims=True)
    esum = jnp.sum(jnp.exp(el - m1), axis=-1, keepdims=True)
    p1 = 1.0 / esum
    p2 = jnp.exp(m2 - m1) / esum
    w1 = g_p * (p1 / (p1 + p2))
    w2 = g_p * (p2 / (p1 + p2))
    eidx_ref[...] = jnp.where(lane == 0, i1 - N_GROUPS, jnp.where(lane == 1, i2 - N_GROUPS, 0))
    wts_ref[...] = jnp.where(lane == 0, w1, jnp.where(lane == 1, w2, 0.0))


def _outproj(mixed, x2, w_o, nw, w_router, b_router, tm=256):
    t = mixed.shape[0]
    return pl.pallas_call(
        _outproj_kernel,
        name="outproj",
        grid=(t // tm,),
        in_specs=[
            pl.BlockSpec((tm, D_MODEL), lambda i: (i, 0)),
            pl.BlockSpec((tm, D_MODEL), lambda i: (i, 0)),
            pl.BlockSpec((D_MODEL, D_MODEL), lambda i: (0, 0)),
            pl.BlockSpec((1, D_MODEL), lambda i: (0, 0)),
            pl.BlockSpec((D_MODEL, 2 * LANES), lambda i: (0, 0)),
            pl.BlockSpec((1, LANES), lambda i: (0, 0)),
        ],
        out_specs=[
            pl.BlockSpec((tm, D_MODEL), lambda i: (i, 0)),
            pl.BlockSpec((tm, HALF), lambda i: (i, 0)),
            pl.BlockSpec((tm, LANES), lambda i: (i, 0)),
            pl.BlockSpec((tm, LANES), lambda i: (i, 0)),
        ],
        out_shape=[
            jax.ShapeDtypeStruct((t, D_MODEL), F32),
            jax.ShapeDtypeStruct((t, HALF), U32),
            jax.ShapeDtypeStruct((t, LANES), I32),
            jax.ShapeDtypeStruct((t, LANES), F32),
        ],
        compiler_params=_cparams(("parallel",)),
    )(mixed, x2, w_o, nw, w_router, b_router)


RANK_BLOCK = 512


def _rank_kernel(e_ref, rank_ref, cnt_ref, carry_ref):
    i = pl.program_id(0)
    r = RANK_BLOCK

    @pl.when(i == 0)
    def _():
        carry_ref[...] = jnp.zeros_like(carry_ref)

    e = e_ref[0]
    expert = lax.broadcasted_iota(I32, (N_EXPERTS, r), 0)
    onehot = jnp.where(expert == e, 1.0, 0.0).astype(F32)
    jrow = lax.broadcasted_iota(I32, (r, r), 0)
    jcol = lax.broadcasted_iota(I32, (r, r), 1)
    before = jnp.where(jrow < jcol, 1.0, 0.0).astype(BF16)
    cum = jnp.dot(onehot.astype(BF16), before, preferred_element_type=F32)
    carry = carry_ref[...]
    rank = jnp.sum(onehot * (cum + carry[:, 0:1]), axis=0, keepdims=True)
    rank_ref[0] = rank.astype(I32)
    carry = carry + jnp.sum(onehot, axis=1, keepdims=True)
    carry_ref[...] = carry
    cnt_ref[...] = carry


def _rank(e_blocks):
    nb = e_blocks.shape[0]
    return pl.pallas_call(
        _rank_kernel,
        name="rank",
        grid=(nb,),
        in_specs=[pl.BlockSpec((1, 1, RANK_BLOCK), lambda i: (i, 0, 0))],
        out_specs=[pl.BlockSpec((1, 1, RANK_BLOCK), lambda i: (i, 0, 0)),
                   pl.BlockSpec((N_EXPERTS, LANES), lambda i: (0, 0))],
        out_shape=[jax.ShapeDtypeStruct((nb, 1, RANK_BLOCK), I32),
                   jax.ShapeDtypeStruct((N_EXPERTS, LANES), F32)],
        scratch_shapes=[pltpu.VMEM((N_EXPERTS, LANES), F32)],
        compiler_params=_cparams(("arbitrary",)),
    )(e_blocks)


def _dest_kernel(e_ref, rank_ref, pstart_ref, dest_ref):
    e = e_ref[0]
    expert = lax.broadcasted_iota(I32, (N_EXPERTS, RANK_BLOCK), 0)
    start = jnp.sum(jnp.where(expert == e, pstart_ref[:, 0:1], 0), axis=0, keepdims=True)
    dest_ref[0] = rank_ref[0] + start


def _dest(e_blocks, rank, pstart):
    nb = e_blocks.shape[0]
    return pl.pallas_call(
        _dest_kernel,
        name="dest",
        grid=(nb,),
        in_specs=[pl.BlockSpec((1, 1, RANK_BLOCK), lambda i: (i, 0, 0)),
                  pl.BlockSpec((1, 1, RANK_BLOCK), lambda i: (i, 0, 0)),
                  pl.BlockSpec((N_EXPERTS, LANES), lambda i: (0, 0))],
        out_specs=pl.BlockSpec((1, 1, RANK_BLOCK), lambda i: (i, 0, 0)),
        out_shape=jax.ShapeDtypeStruct((nb, 1, RANK_BLOCK), I32),
        compiler_params=_cparams(("parallel",)),
    )(e_blocks, rank, pstart)


def _row_copy(src_ref, src_row, dst_ref, dst_row, sem):
    return pltpu.make_async_copy(src_ref.at[pl.ds(src_row, 1)], dst_ref.at[pl.ds(dst_row, 1)], sem)


DMA_UNROLL = 16


def _dispatch_kernel(dest_ref, u_ref, zeros_ref, xs_ref, sem):
    del zeros_ref

    def issue(g, carry):
        tok = g * (DMA_UNROLL // TOP_K)
        c0 = g * DMA_UNROLL
        for j in range(DMA_UNROLL):
            _row_copy(u_ref, tok + j // TOP_K, xs_ref, dest_ref[0, 0, c0 + j], sem).start()
        return carry

    lax.fori_loop(0, RANK_BLOCK // DMA_UNROLL, issue, 0)
    pltpu.make_async_copy(xs_ref.at[pl.ds(0, RANK_BLOCK)], xs_ref.at[pl.ds(0, RANK_BLOCK)], sem).wait()


def _dispatch(dest, u2p, n_rows):
    nb = dest.shape[0]
    zeros = jnp.zeros((n_rows, HALF), U32)
    return pl.pallas_call(
        _dispatch_kernel,
        name="dispatch",
        grid=(nb,),
        in_specs=[pl.BlockSpec((1, 1, RANK_BLOCK), lambda i: (i, 0, 0), memory_space=pltpu.SMEM),
                  pl.BlockSpec((RANK_BLOCK // TOP_K, HALF), lambda i: (i, 0)),
                  pl.BlockSpec(memory_space=pl.ANY)],
        out_specs=pl.BlockSpec(memory_space=pl.ANY),
        out_shape=jax.ShapeDtypeStruct((n_rows, HALF), U32),
        scratch_shapes=[pltpu.SemaphoreType.DMA(())],
        input_output_aliases={2: 0},
        compiler_params=_cparams(("arbitrary",)),
    )(dest, u2p, zeros)


def _experts_kernel(be_ref, first_ref, nused_ref, x_ref, wg_ref, wu_ref, wd_ref, y_ref, wgb_ref, wub_ref, wdb_ref):
    i = pl.program_id(0)
    used = i < nused_ref[0]

    @pl.when(used & (first_ref[i] == 1))
    def _():
        wgb_ref[...] = wg_ref[0].astype(BF16)
        wub_ref[...] = wu_ref[0].astype(BF16)
        wdb_ref[...] = wd_ref[0].astype(BF16)

    @pl.when(used)
    def _():
        lo, hi = _unpack_bf16_pair(x_ref[...])
        lo = lo.astype(BF16)
        hi = hi.astype(BF16)
        gate = (jnp.dot(lo, wgb_ref[:HALF, :], preferred_element_type=F32)
                + jnp.dot(hi, wgb_ref[HALF:, :], preferred_element_type=F32))
        up = (jnp.dot(lo, wub_ref[:HALF, :], preferred_element_type=F32)
              + jnp.dot(hi, wub_ref[HALF:, :], preferred_element_type=F32))
        hdn = (_silu(gate) * up).astype(BF16)
        y = jnp.dot(hdn, wdb_ref[...], preferred_element_type=F32)
        y_ref[...] = _pack_bf16_pair(y)

    @pl.when(jnp.logical_not(used))
    def _():
        y_ref[...] = jnp.zeros_like(y_ref)


def _experts(block_expert, n_used, xs, w_gate, w_up, w_down):
    n_rows = xs.shape[0]
    n_blocks = n_rows // ROW_BLOCK
    first = jnp.concatenate([jnp.ones((1,), I32),
                             (block_expert[1:] != block_expert[:-1]).astype(I32)])

    def xmap(i, be, fi, nu):
        return (jnp.minimum(i, jnp.maximum(nu[0] - 1, 0)), 0)

    def wmap(i, be, fi, nu):
        return (be[i], 0, 0)

    grid_spec = pltpu.PrefetchScalarGridSpec(
        num_scalar_prefetch=3,
        grid=(n_blocks,),
        in_specs=[pl.BlockSpec((ROW_BLOCK, HALF), xmap),
                  pl.BlockSpec((1, D_MODEL, D_EXPERT), wmap),
                  pl.BlockSpec((1, D_MODEL, D_EXPERT), wmap),
                  pl.BlockSpec((1, D_EXPERT, D_MODEL), wmap)],
        out_specs=pl.BlockSpec((ROW_BLOCK, HALF), lambda i, be, fi, nu: (i, 0)),
        scratch_shapes=[pltpu.VMEM((D_MODEL, D_EXPERT), BF16),
                        pltpu.VMEM((D_MODEL, D_EXPERT), BF16),
                        pltpu.VMEM((D_EXPERT, D_MODEL), BF16)],
    )
    return pl.pallas_call(
        _experts_kernel,
        name="experts",
        grid_spec=grid_spec,
        out_shape=jax.ShapeDtypeStruct((n_rows, HALF), U32),
        compiler_params=_cparams(("arbitrary",)),
    )(block_expert, first, n_used, xs, w_gate, w_up, w_down)


COMBINE_TOKENS = RANK_BLOCK // TOP_K


def _combine_kernel(dest_ref, h_ref, wts_ref, nw_ref, y_ref, o_ref, buf_ref, sem):
    ts = COMBINE_TOKENS

    def issue(g, carry):
        tok = g * (DMA_UNROLL // TOP_K)
        c0 = g * DMA_UNROLL
        for j in range(DMA_UNROLL):
            _row_copy(y_ref, dest_ref[0, 0, c0 + j], buf_ref, (j % TOP_K) * ts + tok + j // TOP_K, sem).start()
        return carry

    lax.fori_loop(0, RANK_BLOCK // DMA_UNROLL, issue, 0)
    pltpu.make_async_copy(y_ref.at[pl.ds(0, RANK_BLOCK)], buf_ref, sem).wait()

    w = wts_ref[...]
    w0 = w[:, 0:1]
    w1 = w[:, 1:2]
    lo0, hi0 = _unpack_bf16_pair(buf_ref[0:ts, :])
    lo1, hi1 = _unpack_bf16_pair(buf_ref[ts:2 * ts, :])
    h = h_ref[...]
    out_lo = h[:, :HALF] + w0 * lo0 + w1 * lo1
    out_hi = h[:, HALF:] + w0 * hi0 + w1 * hi1
    ms = (jnp.sum(out_lo * out_lo, axis=-1, keepdims=True)
          + jnp.sum(out_hi * out_hi, axis=-1, keepdims=True)) * (1.0 / D_MODEL)
    inv = lax.rsqrt(ms + EPS)
    o_ref[:, :HALF] = out_lo * inv * nw_ref[:, :HALF]
    o_ref[:, HALF:] = out_hi * inv * nw_ref[:, HALF:]


def _combine(dest, h1, wts, nw, y):
    t = h1.shape[0]
    ts = COMBINE_TOKENS
    return pl.pallas_call(
        _combine_kernel,
        name="combine",
        grid=(t // ts,),
        in_specs=[pl.BlockSpec((1, 1, RANK_BLOCK), lambda i: (i, 0, 0), memory_space=pltpu.SMEM),
                  pl.BlockSpec((ts, D_MODEL), lambda i: (i, 0)),
                  pl.BlockSpec((ts, LANES), lambda i: (i, 0)),
                  pl.BlockSpec((1, D_MODEL), lambda i: (0, 0)),
                  pl.BlockSpec(memory_space=pl.ANY)],
        out_specs=pl.BlockSpec((ts, D_MODEL), lambda i: (i, 0)),
        out_shape=jax.ShapeDtypeStruct((t, D_MODEL), F32),
        scratch_shapes=[pltpu.VMEM((RANK_BLOCK, HALF), U32), pltpu.SemaphoreType.DMA(())],
        compiler_params=_cparams(("arbitrary",)),
    )(dest, h1, wts, nw, y)


def _permute_w_in(w_in):
    sizes = (SSD_DIM, CONV_DIM, SSD_HEADS, FOX_DIM, FOX_DIM, FOX_DIM, FOX_HEADS, 2 * D_MODEL)
    offs = [0]
    for s in sizes:
        offs.append(offs[-1] + s)
    z, xbc, dt, q, k, v, f, gates = (w_in[:, offs[i]:offs[i + 1]] for i in range(len(sizes)))
    wide = jnp.concatenate([z, q, k, v, gates, xbc], axis=1).astype(BF16)
    pad = jnp.zeros((w_in.shape[0], LANES - SSD_HEADS - FOX_HEADS), w_in.dtype)
    narrow = jnp.concatenate([dt, f, pad], axis=1).astype(BF16)
    return wide, narrow


def _layer(h, p, bsz, seq):
    t = bsz * seq
    wide, narrow = _permute_w_in(p["w_in"])
    u, small = _norm_small(h, p["norm_mix_w"].astype(F32).reshape(1, D_MODEL), narrow)
    proj = _matmul(u, wide)
    y_ssd = _ssd(proj, small, p["conv_w"], p["conv_b"], p["dt_bias"], p["a_log"], p["d_skip"],
                 p["ssd_norm_w"], bsz, seq)
    cum = _fox_cum(small, p["fox_f_bias"], bsz, seq)
    y_fox = _fox(proj, cum, bsz, seq)
    mixed = _mix(y_ssd, y_fox, proj, p["w_proj_ssd"].astype(BF16), p["w_proj_fox"].astype(BF16))

    w_router = jnp.concatenate(
        [p["w_router_group"], p["w_router_expert"],
         jnp.zeros((D_MODEL, LANES - N_GROUPS - N_EXPERTS), F32)], axis=1).astype(F32)
    w_router_hi = w_router.astype(BF16)
    w_router_lo = (w_router - w_router_hi.astype(F32)).astype(BF16)
    w_router = jnp.concatenate([w_router_hi, w_router_lo], axis=1)
    b_router = jnp.concatenate(
        [p["b_router_group"], p["b_router_expert"],
         jnp.zeros((LANES - N_GROUPS - N_EXPERTS,), F32)]).astype(F32).reshape(1, LANES)
    h1, u2p, eidx, wts = _outproj(mixed, h, p["w_out"].astype(BF16),
                                  p["norm_moe_w"].astype(F32).reshape(1, D_MODEL), w_router, b_router)

    tk = t * TOP_K
    e_blocks = eidx[:, :TOP_K].reshape(tk // RANK_BLOCK, 1, RANK_BLOCK)
    rank, counts = _rank(e_blocks)
    counts = counts[:, 0].astype(I32)
    padded = ((counts + ROW_BLOCK - 1) // ROW_BLOCK) * ROW_BLOCK
    pend = jnp.cumsum(padded)
    pstart = pend - padded
    n_blocks = tk // ROW_BLOCK + N_EXPERTS
    block_row0 = jnp.arange(n_blocks, dtype=I32) * ROW_BLOCK
    block_expert = jnp.minimum(jnp.sum((pend[None, :] <= block_row0[:, None]).astype(I32), axis=1), N_EXPERTS - 1)
    n_used = (pend[-1:] // ROW_BLOCK).astype(I32)
    dest = _dest(e_blocks, rank, jnp.broadcast_to(pstart[:, None], (N_EXPERTS, LANES)).astype(I32))
    xs = _dispatch(dest, u2p, n_blocks * ROW_BLOCK)
    y = _experts(block_expert, n_used, xs, p["w_gate_exp"], p["w_up_exp"], p["w_down_exp"])
    return dest, h1, wts, y


def kernel(x, norm_mix_w, w_in, conv_w, conv_b, dt_bias, a_log, d_skip, ssd_norm_w, fox_f_bias, w_proj_ssd,
           w_proj_fox, w_out, norm_moe_w, w_router_group, b_router_group, w_router_expert, b_router_expert,
           w_gate_exp, w_up_exp, w_down_exp, norm_final_w):
    bsz, seq, _ = x.shape
    depth = w_in.shape[0]
    assert depth == 1, "the fused final norm assumes a single layer"
    stacked = dict(norm_mix_w=norm_mix_w, w_in=w_in, conv_w=conv_w, conv_b=conv_b, dt_bias=dt_bias, a_log=a_log,
                   d_skip=d_skip, ssd_norm_w=ssd_norm_w, fox_f_bias=fox_f_bias, w_proj_ssd=w_proj_ssd,
                   w_proj_fox=w_proj_fox, w_out=w_out, norm_moe_w=norm_moe_w, w_router_group=w_router_group,
                   b_router_group=b_router_group, w_router_expert=w_router_expert,
                   b_router_expert=b_router_expert, w_gate_exp=w_gate_exp, w_up_exp=w_up_exp,
                   w_down_exp=w_down_exp)
    p = {name: v[0] for name, v in stacked.items()}
    h = x.reshape(bsz * seq, D_MODEL)
    dest, h1, wts, y = _layer(h, p, bsz, seq)
    out = _combine(dest, h1, wts, norm_final_w.astype(F32).reshape(1, D_MODEL), y)
    return out.reshape(bsz, seq, D_MODEL)
```

```python
import functools
import math

import jax
import jax.numpy as jnp
from jax import lax
from jax.experimental import pallas as pl
from jax.experimental.pallas import tpu as pltpu

F32 = jnp.float32
BF16 = jnp.bfloat16
I32 = jnp.int32
U32 = jnp.uint32

D_MODEL = 2048
SSD_HEADS = 32
SSD_HEAD_DIM = 64
SSD_DIM = SSD_HEADS * SSD_HEAD_DIM
SSD_GROUPS = 4
SSD_STATE = 128
SSD_CHUNK = 128
CONV_WIDTH = 4
CONV_DIM = SSD_DIM + 2 * SSD_GROUPS * SSD_STATE
FOX_HEADS = 16
FOX_HEAD_DIM = 128
FOX_DIM = FOX_HEADS * FOX_HEAD_DIM
N_GROUPS = 8
EXPERTS_PER_GROUP = 8
N_EXPERTS = N_GROUPS * EXPERTS_PER_GROUP
TOP_K = 2
D_EXPERT = 512
EPS = 1e-6

LANES = 128
SUBLANES = 8
VMEM_LIMIT = 52 * 1024 * 1024

COL_Z = 0
COL_Q = COL_Z + SSD_DIM
COL_K = COL_Q + FOX_DIM
COL_V = COL_K + FOX_DIM
COL_GA = COL_V + FOX_DIM
COL_GB = COL_GA + D_MODEL
COL_XBC = COL_GB + D_MODEL
PROJ_COLS = COL_XBC + CONV_DIM
SMALL_DT = 0
SMALL_DT_COPIES = 3
SMALL_F = SMALL_DT_COPIES * SSD_HEADS

ROW_BLOCK = 128
HALF = D_MODEL // 2


def _cparams(sem, vmem=VMEM_LIMIT):
    return pltpu.CompilerParams(dimension_semantics=sem, vmem_limit_bytes=vmem)


def _silu(x):
    return x * (1.0 / (1.0 + jnp.exp(-x)))


def _softplus(x):
    return jnp.maximum(x, 0.0) + jnp.log(1.0 + jnp.exp(-jnp.abs(x)))


def _log_sigmoid(x):
    return -_softplus(-x)


def _split3(x):
    hi = x.astype(BF16)
    rest = x - hi.astype(F32)
    mid = rest.astype(BF16)
    lo = (rest - mid.astype(F32)).astype(BF16)
    return hi, mid, lo


def _pack_bf16_pair(x):
    n = x.shape[1] // 2
    lo = pltpu.bitcast(x[:, :n].astype(BF16).astype(F32), U32)
    hi = pltpu.bitcast(x[:, n:].astype(BF16).astype(F32), U32)
    return (hi & jnp.uint32(0xFFFF0000)) | (lo >> 16)


def _unpack_bf16_pair(p):
    lo = pltpu.bitcast(p << 16, F32)
    hi = pltpu.bitcast(p & jnp.uint32(0xFFFF0000), F32)
    return lo, hi


def _norm_small_kernel(x_ref, nw_ref, ws_ref, u_ref, s_ref):
    x = x_ref[...]
    ms = jnp.mean(x * x, axis=-1, keepdims=True)
    u = (x * lax.rsqrt(ms + EPS) * nw_ref[...]).astype(BF16)
    u_ref[...] = u
    s_ref[...] = jnp.dot(u, ws_ref[...], preferred_element_type=F32)


def _norm_small(x2, nw, w_small, tm=512):
    t = x2.shape[0]
    return pl.pallas_call(
        _norm_small_kernel,
        name="norm_small",
        grid=(t // tm,),
        in_specs=[pl.BlockSpec((tm, D_MODEL), lambda i: (i, 0)),
                  pl.BlockSpec((1, D_MODEL), lambda i: (0, 0)),
                  pl.BlockSpec((D_MODEL, LANES), lambda i: (0, 0))],
        out_specs=[pl.BlockSpec((tm, D_MODEL), lambda i: (i, 0)),
                   pl.BlockSpec((tm, LANES), lambda i: (i, 0))],
        out_shape=[jax.ShapeDtypeStruct((t, D_MODEL), BF16),
                   jax.ShapeDtypeStruct((t, LANES), F32)],
        compiler_params=_cparams(("parallel",)),
    )(x2, nw, w_small)


def _matmul_kernel(a_ref, b_ref, o_ref):
    o_ref[...] = jnp.dot(a_ref[...], b_ref[...], preferred_element_type=F32).astype(o_ref.dtype)


def _matmul(a, b, tm=1024, tn=1024):
    m, k = a.shape
    n = b.shape[1]
    return pl.pallas_call(
        _matmul_kernel,
        name="in_proj",
        grid=(m // tm, n // tn),
        in_specs=[pl.BlockSpec((tm, k), lambda i, j: (i, 0)),
                  pl.BlockSpec((k, tn), lambda i, j: (0, j))],
        out_specs=pl.BlockSpec((tm, tn), lambda i, j: (i, j)),
        out_shape=jax.ShapeDtypeStruct((m, n), BF16),
        compiler_params=_cparams(("parallel", "parallel")),
    )(a, b)


HALO = 2 * SUBLANES


def _ssd_kernel(z_ref, xbc_ref, halo_ref, small_ref, cw_ref, cb_ref, dtb_ref, aneg_ref, dexp_ref,
                nw_ref, expand_ref, shift_ref, y_ref, state_ref, ydiag_ref):
    c = pl.program_id(1)
    l = SSD_CHUNK
    n = SSD_STATE

    @pl.when(c == 0)
    def _():
        state_ref[...] = jnp.zeros_like(state_ref)

    cur = xbc_ref[...]
    halo = halo_ref[...]
    halo = jnp.where(c == 0, jnp.zeros_like(halo), halo)
    ext = jnp.concatenate([halo, cur], axis=0)
    shifted = jnp.dot(shift_ref[...], ext, preferred_element_type=F32)
    conv = cb_ref[...] + cw_ref[CONV_WIDTH - 1:CONV_WIDTH, :] * cur.astype(F32)
    for j in range(CONV_WIDTH - 1):
        conv = conv + cw_ref[j:j + 1, :] * shifted[j * l:(j + 1) * l]
    xbc = _silu(conv)
    xs = xbc[:, :SSD_DIM]
    bm = xbc[:, SSD_DIM:SSD_DIM + SSD_GROUPS * n]
    cm = xbc[:, SSD_DIM + SSD_GROUPS * n:]

    h3 = SMALL_DT_COPIES * SSD_HEADS
    dt3 = _softplus(small_ref[:, SMALL_DT:SMALL_DT + h3] + dtb_ref[...])
    adt3 = dt3 * aneg_ref[...]
    row = lax.broadcasted_iota(I32, (l, l), 0)
    col = lax.broadcasted_iota(I32, (l, l), 1)
    causal = col <= row
    tril = jnp.where(causal, 1.0, 0.0).astype(BF16)
    a_cs3 = sum(jnp.dot(tril, piece, preferred_element_type=F32) for piece in _split3(adt3))
    a_cs = a_cs3[:, :SSD_HEADS]
    a_cs_t = jnp.transpose(a_cs)
    a_last3 = a_cs3[l - 1:l, :]

    lane3 = lax.broadcasted_iota(I32, (l, h3), 1)

    def pieces_by_lane_group(x3):
        hi, mid, lo = _split3(x3)
        return jnp.where(lane3 < SSD_HEADS, hi, jnp.where(lane3 < 2 * SSD_HEADS, mid, lo))

    lhs3 = jnp.concatenate([pieces_by_lane_group(dt3),
                            pieces_by_lane_group(jnp.exp(a_cs3)),
                            pieces_by_lane_group(jnp.exp(a_last3 - a_cs3))], axis=0)
    expanded = jnp.dot(lhs3, expand_ref[...], preferred_element_type=F32)
    dt_x = expanded[0:l]
    decay_in = expanded[l:2 * l]
    decay_out = expanded[2 * l:3 * l]
    chunk_decay = decay_in[l - 1:l, :]
    x_dt = xs * dt_x
    x_dt_b = x_dt.astype(BF16)
    xd_b = (x_dt * decay_out).astype(BF16)

    lane = lax.broadcasted_iota(I32, (l, LANES), 1)
    lo_mask = lane < SSD_HEAD_DIM
    heads_per_group = SSD_HEADS // SSD_GROUPS
    gw = heads_per_group * SSD_HEAD_DIM

    for g in range(SSD_GROUPS):
        bg = bm[:, g * n:(g + 1) * n].astype(BF16)
        cg = cm[:, g * n:(g + 1) * n].astype(BF16)
        cb = lax.dot_general(cg, bg, (((1,), (1,)), ((), ())), preferred_element_type=F32)
        for pair in range(heads_per_group // 2):
            h0 = g * heads_per_group + 2 * pair
            lane0 = h0 * SSD_HEAD_DIM
            ms = []
            for h in (h0, h0 + 1):
                seg = a_cs[:, h:h + 1] - a_cs_t[h:h + 1, :]
                lmat = jnp.exp(jnp.where(causal, seg, -jnp.inf))
                ms.append((cb * lmat).astype(BF16))
            lhs = jnp.concatenate(ms, axis=1)
            xp = x_dt_b[:, lane0:lane0 + LANES]
            zero = jnp.zeros_like(xp)
            rhs = jnp.concatenate([jnp.where(lo_mask, xp, zero), jnp.where(lo_mask, zero, xp)], axis=0)
            ydiag_ref[:, lane0:lane0 + LANES] = jnp.dot(lhs, rhs, preferred_element_type=F32)
        st = state_ref[:, g * gw:(g + 1) * gw]
        y_off = jnp.dot(cg, st.astype(BF16), preferred_element_type=F32)
        ydiag_ref[:, g * gw:(g + 1) * gw] += y_off * decay_in[:, g * gw:(g + 1) * gw]
        new = lax.dot_general(bg, xd_b[:, g * gw:(g + 1) * gw], (((0,), (0,)), ((), ())),
                              preferred_element_type=F32)
        state_ref[:, g * gw:(g + 1) * gw] = st * chunk_decay[:, g * gw:(g + 1) * gw] + new

    y = ydiag_ref[...] + dexp_ref[...] * xs
    y = y * _silu(z_ref[...].astype(F32))
    ms = jnp.mean(y * y, axis=-1, keepdims=True)
    y_ref[...] = (y * lax.rsqrt(ms + EPS) * nw_ref[...]).astype(y_ref.dtype)


def _ssd(proj, small, conv_w, conv_b, dt_bias, a_log, d_skip, norm_w, bsz, seq):
    l = SSD_CHUNK
    nc = seq // l
    t = bsz * seq
    h3 = SMALL_DT_COPIES * SSD_HEADS
    aneg = jnp.tile(-jnp.exp(a_log.astype(F32)), SMALL_DT_COPIES).reshape(1, h3)
    dtb = jnp.tile(dt_bias.astype(F32), SMALL_DT_COPIES).reshape(1, h3)
    dexp = jnp.repeat(d_skip.astype(F32), SSD_HEAD_DIM).reshape(1, SSD_DIM)
    expand = jnp.tile(jnp.repeat(jnp.eye(SSD_HEADS, dtype=BF16), SSD_HEAD_DIM, axis=1),
                      (SMALL_DT_COPIES, 1))
    out_row = jnp.arange((CONV_WIDTH - 1) * l)
    src_row = HALO + out_row % l - (CONV_WIDTH - 1) + out_row // l
    shift = (jnp.arange(HALO + l)[None, :] == src_row[:, None]).astype(BF16)
    xbc_blk = COL_XBC // CONV_DIM
    halo_per_chunk = l // HALO

    def row_map(b, c):
        return b * nc + c

    return pl.pallas_call(
        _ssd_kernel,
        name="ssd",
        grid=(bsz, nc),
        in_specs=[
            pl.BlockSpec((l, SSD_DIM), lambda b, c: (row_map(b, c), COL_Z // SSD_DIM)),
            pl.BlockSpec((l, CONV_DIM), lambda b, c: (row_map(b, c), xbc_blk)),
            pl.BlockSpec((HALO, CONV_DIM),
                         lambda b, c: (jnp.maximum(row_map(b, c) * halo_per_chunk - 1, 0), xbc_blk)),
            pl.BlockSpec((l, LANES), lambda b, c: (row_map(b, c), 0)),
            pl.BlockSpec((CONV_WIDTH, CONV_DIM), lambda b, c: (0, 0)),
            pl.BlockSpec((1, CONV_DIM), lambda b, c: (0, 0)),
            pl.BlockSpec((1, h3), lambda b, c: (0, 0)),
            pl.BlockSpec((1, h3), lambda b, c: (0, 0)),
            pl.BlockSpec((1, SSD_DIM), lambda b, c: (0, 0)),
            pl.BlockSpec((1, SSD_DIM), lambda b, c: (0, 0)),
            pl.BlockSpec((h3, SSD_DIM), lambda b, c: (0, 0)),
            pl.BlockSpec(((CONV_WIDTH - 1) * l, HALO + l), lambda b, c: (0, 0)),
        ],
        out_specs=pl.BlockSpec((l, SSD_DIM), lambda b, c: (row_map(b, c), 0)),
        out_shape=jax.ShapeDtypeStruct((t, SSD_DIM), BF16),
        scratch_shapes=[pltpu.VMEM((SSD_STATE, SSD_DIM), F32),
                        pltpu.VMEM((l, SSD_DIM), F32)],
        compiler_params=_cparams(("parallel", "arbitrary")),
    )(proj, proj, proj, small, conv_w.astype(F32), conv_b.astype(F32).reshape(1, CONV_DIM),
      dtb, aneg, dexp, norm_w.astype(F32).reshape(1, SSD_DIM), expand, shift)


CUM_ROWS = 256


def _cum_kernel(small_ref, fb_ref, cum_ref, carry_ref):
    j = pl.program_id(1)

    @pl.when(j == 0)
    def _():
        carry_ref[...] = jnp.zeros_like(carry_ref)

    lf = _log_sigmoid(small_ref[...] + fb_ref[...])
    row = lax.broadcasted_iota(I32, (CUM_ROWS, CUM_ROWS), 0)
    col = lax.broadcasted_iota(I32, (CUM_ROWS, CUM_ROWS), 1)
    tril = jnp.where(col <= row, 1.0, 0.0).astype(F32)
    cs = jnp.dot(tril, lf, preferred_element_type=F32, precision=lax.Precision.HIGHEST) + carry_ref[...]
    cum_ref[...] = cs * LOG2E
    carry_ref[...] = cs[CUM_ROWS - 1:CUM_ROWS, :]


def _fox_cum(small, fox_f_bias, bsz, seq):
    fb = jnp.zeros((1, LANES), F32).at[0, SMALL_F:SMALL_F + FOX_HEADS].set(fox_f_bias.astype(F32))
    nj = seq // CUM_ROWS
    return pl.pallas_call(
        _cum_kernel,
        name="fox_cum",
        grid=(bsz, nj),
        in_specs=[pl.BlockSpec((CUM_ROWS, LANES), lambda b, j: (b * nj + j, 0)),
                  pl.BlockSpec((1, LANES), lambda b, j: (0, 0))],
        out_specs=pl.BlockSpec((CUM_ROWS, LANES), lambda b, j: (b * nj + j, 0)),
        out_shape=jax.ShapeDtypeStruct((bsz * seq, LANES), F32),
        scratch_shapes=[pltpu.VMEM((1, LANES), F32)],
        compiler_params=_cparams(("parallel", "arbitrary")),
    )(small, fb)


FOX_HEADS_PER_STEP = 2
LOG2E = 1.4426950408889634


FOX_SLAB = 128


def _fox_kernel(q_ref, k_ref, v_ref, cq_ref, ck_ref, o_ref, sa_ref, sb_ref, p_ref, m_ref, l_ref, alpha_ref,
                cqrep_ref, acc_ref, *, tq, tk):
    hp = pl.program_id(1)
    qi = pl.program_id(2)
    d = FOX_HEAD_DIM
    c2 = LOG2E / math.sqrt(d)
    lane = lax.broadcasted_iota(I32, (tq, LANES), 1)
    cq_all = cq_ref[...]

    qs, cqs = [], []
    for hh in range(FOX_HEADS_PER_STEP):
        qs.append((q_ref[:, hh * d:(hh + 1) * d].astype(F32) * c2).astype(BF16))
        head_lane = SMALL_F + hp * FOX_HEADS_PER_STEP + hh
        cq_col = jnp.sum(jnp.where(lane == head_lane, cq_all, 0.0), axis=-1, keepdims=True)
        cqs.append(jnp.broadcast_to(cq_col, (tq, LANES)))

    s_slots = (sa_ref, sb_ref)

    def scores(ki, slot):
        row0 = pl.multiple_of(ki * tk, tk)
        for hh in range(FOX_HEADS_PER_STEP):
            k = k_ref[pl.ds(row0, tk), hh * d:(hh + 1) * d]
            s = lax.dot_general(qs[hh], k, (((1,), (1,)), ((), ())), preferred_element_type=F32)
            s_slots[slot][hh] = s - ck_ref[hh, pl.ds(ki, 1), :]

    def update(ki, slot, masked):
        row0 = pl.multiple_of(ki * tk, tk)
        s_ref = s_slots[slot]
        n_ct = tk // LANES
        for hh in range(FOX_HEADS_PER_STEP):
            for rc in range(tq // FOX_SLAB):
                rows = slice(rc * FOX_SLAB, (rc + 1) * FOX_SLAB)

                def slab(ct):
                    x = s_ref[hh, rows, ct * LANES:(ct + 1) * LANES]
                    if masked:
                        row = lax.broadcasted_iota(I32, (FOX_SLAB, LANES), 0) + (qi * tq + rc * FOX_SLAB)
                        col = lax.broadcasted_iota(I32, (FOX_SLAB, LANES), 1) + (ki * tk + ct * LANES)
                        x = jnp.where(col <= row, x, -jnp.inf)
                    return x

                tmax = slab(0)
                for ct in range(1, n_ct):
                    tmax = jnp.maximum(tmax, slab(ct))
                row_max = jnp.max(tmax, axis=-1, keepdims=True)
                cq = cqrep_ref[hh, rows, :]
                m_prev = m_ref[hh, rows, :]
                m_new = jnp.maximum(m_prev, jnp.broadcast_to(row_max, (FOX_SLAB, LANES)) + cq)
                r = m_new - cq
                alpha = jnp.exp2(m_prev - m_new)
                psum = None
                for ct in range(n_ct):
                    p = jnp.exp2(slab(ct) - r)
                    psum = p if psum is None else psum + p
                    p_ref[hh, rows, ct * LANES:(ct + 1) * LANES] = p.astype(BF16)
                l_ref[hh, rows, :] = alpha * l_ref[hh, rows, :] + psum
                m_ref[hh, rows, :] = m_new
                alpha_ref[hh, rows, :] = alpha
            v = v_ref[pl.ds(row0, tk), hh * d:(hh + 1) * d]
            acc_ref[hh] = alpha_ref[hh] * acc_ref[hh] + jnp.dot(p_ref[hh], v, preferred_element_type=F32)

    for hh in range(FOX_HEADS_PER_STEP):
        cqrep_ref[hh] = cqs[hh]

    m_ref[...] = jnp.full_like(m_ref, -jnp.inf)
    l_ref[...] = jnp.zeros_like(l_ref)
    acc_ref[...] = jnp.zeros_like(acc_ref)

    n_full = qi
    scores(0, 0)

    def pair(j, carry):
        b0 = 2 * j
        scores(b0 + 1, 1)
        update(b0, 0, False)
        scores(b0 + 2, 0)
        update(b0 + 1, 1, False)
        return carry

    lax.fori_loop(0, n_full // 2, pair, 0)

    @pl.when(n_full % 2 == 0)
    def _():
        update(n_full, 0, True)

    @pl.when(n_full % 2 == 1)
    def _():
        scores(n_full, 1)
        update(n_full - 1, 0, False)
        update(n_full, 1, True)

    for hh in range(FOX_HEADS_PER_STEP):
        l_fin = jnp.sum(l_ref[hh], axis=-1, keepdims=True)
        o_ref[:, hh * d:(hh + 1) * d] = (acc_ref[hh] / l_fin).astype(o_ref.dtype)


def _fox(proj, cum, bsz, seq, tq=512):
    tk = tq
    nq = seq // tq
    nk = seq // tk
    hps = FOX_HEADS_PER_STEP
    cum_row = cum[:, SMALL_F:SMALL_F + FOX_HEADS].reshape(bsz, seq, FOX_HEADS).transpose(0, 2, 1)
    cum_row = cum_row.reshape(bsz, FOX_HEADS, nk, tk)
    t = bsz * seq
    w = FOX_HEADS_PER_STEP * FOX_HEAD_DIM
    n_hp = FOX_HEADS // FOX_HEADS_PER_STEP
    kern = functools.partial(_fox_kernel, tq=tq, tk=tk)
    return pl.pallas_call(
        kern,
        name="fox",
        grid=(bsz, n_hp, nq),
        in_specs=[
            pl.BlockSpec((tq, w), lambda b, hp, qi: (b * nq + qi, COL_Q // w + hp)),
            pl.BlockSpec((seq, w), lambda b, hp, qi: (b, COL_K // w + hp)),
            pl.BlockSpec((seq, w), lambda b, hp, qi: (b, COL_V // w + hp)),
            pl.BlockSpec((tq, LANES), lambda b, hp, qi: (b * nq + qi, 0)),
            pl.BlockSpec((None, FOX_HEADS_PER_STEP, nk, tk), lambda b, hp, qi: (b, hp, 0, 0)),
        ],
        out_specs=pl.BlockSpec((tq, w), lambda b, hp, qi: (b * nq + qi, hp)),
        out_shape=jax.ShapeDtypeStruct((t, FOX_DIM), BF16),
        scratch_shapes=[pltpu.VMEM((hps, tq, tk), F32), pltpu.VMEM((hps, tq, tk), F32),
                        pltpu.VMEM((hps, tq, tk), BF16),
                        pltpu.VMEM((hps, tq, LANES), F32), pltpu.VMEM((hps, tq, LANES), F32),
                        pltpu.VMEM((hps, tq, LANES), F32), pltpu.VMEM((hps, tq, LANES), F32),
                        pltpu.VMEM((hps, tq, FOX_HEAD_DIM), F32)],
        compiler_params=_cparams(("parallel", "parallel", "arbitrary")),
    )(proj, proj, proj, cum, cum_row)


def _mix_kernel(ya_ref, yb_ref, ga_ref, gb_ref, wa_ref, wb_ref, o_ref):
    pa = jnp.dot(ya_ref[...], wa_ref[...], preferred_element_type=F32)
    pb = jnp.dot(yb_ref[...], wb_ref[...], preferred_element_type=F32)
    ga = 1.0 / (1.0 + jnp.exp(-ga_ref[...].astype(F32)))
    gb = 1.0 / (1.0 + jnp.exp(-gb_ref[...].astype(F32)))
    o_ref[...] = (ga * pa + gb * pb).astype(o_ref.dtype)


def _mix(y_a, y_b, proj, w_a, w_b, tm=512, tn=1024):
    t = y_a.shape[0]
    return pl.pallas_call(
        _mix_kernel,
        name="mix",
        grid=(D_MODEL // tn, t // tm),
        in_specs=[
            pl.BlockSpec((tm, SSD_DIM), lambda j, i: (i, 0)),
            pl.BlockSpec((tm, FOX_DIM), lambda j, i: (i, 0)),
            pl.BlockSpec((tm, tn), lambda j, i: (i, COL_GA // tn + j)),
            pl.BlockSpec((tm, tn), lambda j, i: (i, COL_GB // tn + j)),
            pl.BlockSpec((SSD_DIM, tn), lambda j, i: (0, j)),
            pl.BlockSpec((FOX_DIM, tn), lambda j, i: (0, j)),
        ],
        out_specs=pl.BlockSpec((tm, tn), lambda j, i: (i, j)),
        out_shape=jax.ShapeDtypeStruct((t, D_MODEL), BF16),
        compiler_params=_cparams(("parallel", "parallel")),
    )(y_a, y_b, proj, proj, w_a, w_b)


def _outproj_kernel(m_ref, x_ref, wo_ref, nw_ref, wr_ref, br_ref, h_ref, u_ref, eidx_ref, wts_ref):
    h1 = x_ref[...] + jnp.dot(m_ref[...], wo_ref[...], preferred_element_type=F32)
    h_ref[...] = h1
    ms = jnp.mean(h1 * h1, axis=-1, keepdims=True)
    u2 = h1 * lax.rsqrt(ms + EPS) * nw_ref[...]
    u_ref[...] = _pack_bf16_pair(u2)

    u_hi = u2.astype(BF16)
    u_lo = (u2 - u_hi.astype(F32)).astype(BF16)
    hh_hl = jnp.dot(u_hi, wr_ref[...], preferred_element_type=F32)
    lh = jnp.dot(u_lo, wr_ref[:, :LANES], preferred_element_type=F32)
    logits = hh_hl[:, :LANES] + (hh_hl[:, LANES:] + lh) + br_ref[...]
    tm = logits.shape[0]
    lane = lax.broadcasted_iota(I32, (tm, LANES), 1)
    neg = -jnp.inf
    big = jnp.int32(2 * LANES)
    gl = jnp.where(lane < N_GROUPS, logits, neg)
    gmax = jnp.max(gl, axis=-1, keepdims=True)
    gsum = jnp.sum(jnp.exp(gl - gmax), axis=-1, keepdims=True)
    g_p = 1.0 / gsum
    g_idx = jnp.min(jnp.where(gl == gmax, lane, big), axis=-1, keepdims=True)
    e_of_lane = lane - N_GROUPS
    in_grp = (e_of_lane >= g_idx * EXPERTS_PER_GROUP) & (e_of_lane < (g_idx + 1) * EXPERTS_PER_GROUP)
    el = jnp.where(in_grp, logits, neg)
    m1 = jnp.max(el, axis=-1, keepdims=True)
    i1 = jnp.min(jnp.where(el == m1, lane, big), axis=-1, keepdims=True)
    el2 = jnp.where(lane == i1, neg, el)
    m2 = jnp.max(el2, axis=-1, keepdims=True)
    i2 = jnp.min(jnp.where(el2 == m2, lane, big), axis=-1, keepdims=True)
    esum = jnp.sum(jnp.exp(el - m1), axis=-1, keepdims=True)
    p1 = 1.0 / esum
    p2 = jnp.exp(m2 - m1) / esum
    w1 = g_p * (p1 / (p1 + p2))
    w2 = g_p * (p2 / (p1 + p2))
    eidx_ref[...] = jnp.where(lane == 0, i1 - N_GROUPS, jnp.where(lane == 1, i2 - N_GROUPS, 0))
    wts_ref[...] = jnp.where(lane == 0, w1, jnp.where(lane == 1, w2, 0.0))


def _outproj(mixed, x2, w_o, nw, w_router, b_router, tm=256):
    t = mixed.shape[0]
    return pl.pallas_call(
        _outproj_kernel,
        name="outproj",
        grid=(t // tm,),
        in_specs=[
            pl.BlockSpec((tm, D_MODEL), lambda i: (i, 0)),
            pl.BlockSpec((tm, D_MODEL), lambda i: (i, 0)),
            pl.BlockSpec((D_MODEL, D_MODEL), lambda i: (0, 0)),
            pl.BlockSpec((1, D_MODEL), lambda i: (0, 0)),
            pl.BlockSpec((D_MODEL, 2 * LANES), lambda i: (0, 0)),
            pl.BlockSpec((1, LANES), lambda i: (0, 0)),
        ],
        out_specs=[
            pl.BlockSpec((tm, D_MODEL), lambda i: (i, 0)),
            pl.BlockSpec((tm, HALF), lambda i: (i, 0)),
            pl.BlockSpec((tm, LANES), lambda i: (i, 0)),
            pl.BlockSpec((tm, LANES), lambda i: (i, 0)),
        ],
        out_shape=[
            jax.ShapeDtypeStruct((t, D_MODEL), F32),
            jax.ShapeDtypeStruct((t, HALF), U32),
            jax.ShapeDtypeStruct((t, LANES), I32),
            jax.ShapeDtypeStruct((t, LANES), F32),
        ],
        compiler_params=_cparams(("parallel",)),
    )(mixed, x2, w_o, nw, w_router, b_router)


RANK_BLOCK = 512


def _rank_kernel(e_ref, rank_ref, cnt_ref, carry_ref):
    i = pl.program_id(0)
    r = RANK_BLOCK

    @pl.when(i == 0)
    def _():
        carry_ref[...] = jnp.zeros_like(carry_ref)

    e = e_ref[0]
    expert = lax.broadcasted_iota(I32, (N_EXPERTS, r), 0)
    onehot = jnp.where(expert == e, 1.0, 0.0).astype(F32)
    jrow = lax.broadcasted_iota(I32, (r, r), 0)
    jcol = lax.broadcasted_iota(I32, (r, r), 1)
    before = jnp.where(jrow < jcol, 1.0, 0.0).astype(BF16)
    cum = jnp.dot(onehot.astype(BF16), before, preferred_element_type=F32)
    carry = carry_ref[...]
    rank = jnp.sum(onehot * (cum + carry[:, 0:1]), axis=0, keepdims=True)
    rank_ref[0] = rank.astype(I32)
    carry = carry + jnp.sum(onehot, axis=1, keepdims=True)
    carry_ref[...] = carry
    cnt_ref[...] = carry


def _rank(e_blocks):
    nb = e_blocks.shape[0]
    return pl.pallas_call(
        _rank_kernel,
        name="rank",
        grid=(nb,),
        in_specs=[pl.BlockSpec((1, 1, RANK_BLOCK), lambda i: (i, 0, 0))],
        out_specs=[pl.BlockSpec((1, 1, RANK_BLOCK), lambda i: (i, 0, 0)),
                   pl.BlockSpec((N_EXPERTS, LANES), lambda i: (0, 0))],
        out_shape=[jax.ShapeDtypeStruct((nb, 1, RANK_BLOCK), I32),
                   jax.ShapeDtypeStruct((N_EXPERTS, LANES), F32)],
        scratch_shapes=[pltpu.VMEM((N_EXPERTS, LANES), F32)],
        compiler_params=_cparams(("arbitrary",)),
    )(e_blocks)


def _dest_kernel(e_ref, rank_ref, pstart_ref, dest_ref):
    e = e_ref[0]
    expert = lax.broadcasted_iota(I32, (N_EXPERTS, RANK_BLOCK), 0)
    start = jnp.sum(jnp.where(expert == e, pstart_ref[:, 0:1], 0), axis=0, keepdims=True)
    dest_ref[0] = rank_ref[0] + start


def _dest(e_blocks, rank, pstart):
    nb = e_blocks.shape[0]
    return pl.pallas_call(
        _dest_kernel,
        name="dest",
        grid=(nb,),
        in_specs=[pl.BlockSpec((1, 1, RANK_BLOCK), lambda i: (i, 0, 0)),
                  pl.BlockSpec((1, 1, RANK_BLOCK), lambda i: (i, 0, 0)),
                  pl.BlockSpec((N_EXPERTS, LANES), lambda i: (0, 0))],
        out_specs=pl.BlockSpec((1, 1, RANK_BLOCK), lambda i: (i, 0, 0)),
        out_shape=jax.ShapeDtypeStruct((nb, 1, RANK_BLOCK), I32),
        compiler_params=_cparams(("parallel",)),
    )(e_blocks, rank, pstart)


def _row_copy(src_ref, src_row, dst_ref, dst_row, sem):
    return pltpu.make_async_copy(src_ref.at[pl.ds(src_row, 1)], dst_ref.at[pl.ds(dst_row, 1)], sem)


DMA_UNROLL = 16


def _dispatch_kernel(dest_ref, u_ref, zeros_ref, xs_ref, sem):
    del zeros_ref

    def issue(g, carry):
        tok = g * (DMA_UNROLL // TOP_K)
        c0 = g * DMA_UNROLL
        for j in range(DMA_UNROLL):
            _row_copy(u_ref, tok + j // TOP_K, xs_ref, dest_ref[0, 0, c0 + j], sem).start()
        return carry

    lax.fori_loop(0, RANK_BLOCK // DMA_UNROLL, issue, 0)
    pltpu.make_async_copy(xs_ref.at[pl.ds(0, RANK_BLOCK)], xs_ref.at[pl.ds(0, RANK_BLOCK)], sem).wait()


def _dispatch(dest, u2p, n_rows):
    nb = dest.shape[0]
    zeros = jnp.zeros((n_rows, HALF), U32)
    return pl.pallas_call(
        _dispatch_kernel,
        name="dispatch",
        grid=(nb,),
        in_specs=[pl.BlockSpec((1, 1, RANK_BLOCK), lambda i: (i, 0, 0), memory_space=pltpu.SMEM),
                  pl.BlockSpec((RANK_BLOCK // TOP_K, HALF), lambda i: (i, 0)),
                  pl.BlockSpec(memory_space=pl.ANY)],
        out_specs=pl.BlockSpec(memory_space=pl.ANY),
        out_shape=jax.ShapeDtypeStruct((n_rows, HALF), U32),
        scratch_shapes=[pltpu.SemaphoreType.DMA(())],
        input_output_aliases={2: 0},
        compiler_params=_cparams(("arbitrary",)),
    )(dest, u2p, zeros)


def _experts_kernel(be_ref, first_ref, nused_ref, x_ref, wg_ref, wu_ref, wd_ref, y_ref, wgb_ref, wub_ref, wdb_ref):
    i = pl.program_id(0)
    used = i < nused_ref[0]

    @pl.when(used & (first_ref[i] == 1))
    def _():
        wgb_ref[...] = wg_ref[0].astype(BF16)
        wub_ref[...] = wu_ref[0].astype(BF16)
        wdb_ref[...] = wd_ref[0].astype(BF16)

    @pl.when(used)
    def _():
        lo, hi = _unpack_bf16_pair(x_ref[...])
        lo = lo.astype(BF16)
        hi = hi.astype(BF16)
        gate = (jnp.dot(lo, wgb_ref[:HALF, :], preferred_element_type=F32)
                + jnp.dot(hi, wgb_ref[HALF:, :], preferred_element_type=F32))
        up = (jnp.dot(lo, wub_ref[:HALF, :], preferred_element_type=F32)
              + jnp.dot(hi, wub_ref[HALF:, :], preferred_element_type=F32))
        hdn = (_silu(gate) * up).astype(BF16)
        y = jnp.dot(hdn, wdb_ref[...], preferred_element_type=F32)
        y_ref[...] = _pack_bf16_pair(y)

    @pl.when(jnp.logical_not(used))
    def _():
        y_ref[...] = jnp.zeros_like(y_ref)


def _experts(block_expert, n_used, xs, w_gate, w_up, w_down):
    n_rows = xs.shape[0]
    n_blocks = n_rows // ROW_BLOCK
    first = jnp.concatenate([jnp.ones((1,), I32),
                             (block_expert[1:] != block_expert[:-1]).astype(I32)])

    def xmap(i, be, fi, nu):
        return (jnp.minimum(i, jnp.maximum(nu[0] - 1, 0)), 0)

    def wmap(i, be, fi, nu):
        return (be[i], 0, 0)

    grid_spec = pltpu.PrefetchScalarGridSpec(
        num_scalar_prefetch=3,
        grid=(n_blocks,),
        in_specs=[pl.BlockSpec((ROW_BLOCK, HALF), xmap),
                  pl.BlockSpec((1, D_MODEL, D_EXPERT), wmap),
                  pl.BlockSpec((1, D_MODEL, D_EXPERT), wmap),
                  pl.BlockSpec((1, D_EXPERT, D_MODEL), wmap)],
        out_specs=pl.BlockSpec((ROW_BLOCK, HALF), lambda i, be, fi, nu: (i, 0)),
        scratch_shapes=[pltpu.VMEM((D_MODEL, D_EXPERT), BF16),
                        pltpu.VMEM((D_MODEL, D_EXPERT), BF16),
                        pltpu.VMEM((D_EXPERT, D_MODEL), BF16)],
    )
    return pl.pallas_call(
        _experts_kernel,
        name="experts",
        grid_spec=grid_spec,
        out_shape=jax.ShapeDtypeStruct((n_rows, HALF), U32),
        compiler_params=_cparams(("arbitrary",)),
    )(block_expert, first, n_used, xs, w_gate, w_up, w_down)


COMBINE_TOKENS = RANK_BLOCK // TOP_K


def _combine_kernel(dest_ref, h_ref, wts_ref, nw_ref, y_ref, o_ref, buf_ref, sem):
    ts = COMBINE_TOKENS

    def issue(g, carry):
        tok = g * (DMA_UNROLL // TOP_K)
        c0 = g * DMA_UNROLL
        for j in range(DMA_UNROLL):
            _row_copy(y_ref, dest_ref[0, 0, c0 + j], buf_ref, (j % TOP_K) * ts + tok + j // TOP_K, sem).start()
        return carry

    lax.fori_loop(0, RANK_BLOCK // DMA_UNROLL, issue, 0)
    pltpu.make_async_copy(y_ref.at[pl.ds(0, RANK_BLOCK)], buf_ref, sem).wait()

    w = wts_ref[...]
    w0 = w[:, 0:1]
    w1 = w[:, 1:2]
    lo0, hi0 = _unpack_bf16_pair(buf_ref[0:ts, :])
    lo1, hi1 = _unpack_bf16_pair(buf_ref[ts:2 * ts, :])
    h = h_ref[...]
    out_lo = h[:, :HALF] + w0 * lo0 + w1 * lo1
    out_hi = h[:, HALF:] + w0 * hi0 + w1 * hi1
    ms = (jnp.sum(out_lo * out_lo, axis=-1, keepdims=True)
          + jnp.sum(out_hi * out_hi, axis=-1, keepdims=True)) * (1.0 / D_MODEL)
    inv = lax.rsqrt(ms + EPS)
    o_ref[:, :HALF] = out_lo * inv * nw_ref[:, :HALF]
    o_ref[:, HALF:] = out_hi * inv * nw_ref[:, HALF:]


def _combine(dest, h1, wts, nw, y):
    t = h1.shape[0]
    ts = COMBINE_TOKENS
    return pl.pallas_call(
        _combine_kernel,
        name="combine",
        grid=(t // ts,),
        in_specs=[pl.BlockSpec((1, 1, RANK_BLOCK), lambda i: (i, 0, 0), memory_space=pltpu.SMEM),
                  pl.BlockSpec((ts, D_MODEL), lambda i: (i, 0)),
                  pl.BlockSpec((ts, LANES), lambda i: (i, 0)),
                  pl.BlockSpec((1, D_MODEL), lambda i: (0, 0)),
                  pl.BlockSpec(memory_space=pl.ANY)],
        out_specs=pl.BlockSpec((ts, D_MODEL), lambda i: (i, 0)),
        out_shape=jax.ShapeDtypeStruct((t, D_MODEL), F32),
        scratch_shapes=[pltpu.VMEM((RANK_BLOCK, HALF), U32), pltpu.SemaphoreType.DMA(())],
        compiler_params=_cparams(("arbitrary",)),
    )(dest, h1, wts, nw, y)


def _permute_w_in(w_in):
    sizes = (SSD_DIM, CONV_DIM, SSD_HEADS, FOX_DIM, FOX_DIM, FOX_DIM, FOX_HEADS, 2 * D_MODEL)
    offs = [0]
    for s in sizes:
        offs.append(offs[-1] + s)
    z, xbc, dt, q, k, v, f, gates = (w_in[:, offs[i]:offs[i + 1]] for i in range(len(sizes)))
    wide = jnp.concatenate([z, q, k, v, gates, xbc], axis=1).astype(BF16)
    pad = jnp.zeros((w_in.shape[0], LANES - SMALL_F - FOX_HEADS), w_in.dtype)
    narrow = jnp.concatenate([dt] * SMALL_DT_COPIES + [f, pad], axis=1).astype(BF16)
    return wide, narrow


def _layer(h, p, bsz, seq):
    t = bsz * seq
    wide, narrow = _permute_w_in(p["w_in"])
    u, small = _norm_small(h, p["norm_mix_w"].astype(F32).reshape(1, D_MODEL), narrow)
    proj = _matmul(u, wide)
    y_ssd = _ssd(proj, small, p["conv_w"], p["conv_b"], p["dt_bias"], p["a_log"], p["d_skip"],
                 p["ssd_norm_w"], bsz, seq)
    cum = _fox_cum(small, p["fox_f_bias"], bsz, seq)
    y_fox = _fox(proj, cum, bsz, seq)
    mixed = _mix(y_ssd, y_fox, proj, p["w_proj_ssd"].astype(BF16), p["w_proj_fox"].astype(BF16))

    w_router = jnp.concatenate(
        [p["w_router_group"], p["w_router_expert"],
         jnp.zeros((D_MODEL, LANES - N_GROUPS - N_EXPERTS), F32)], axis=1).astype(F32)
    w_router_hi = w_router.astype(BF16)
    w_router_lo = (w_router - w_router_hi.astype(F32)).astype(BF16)
    w_router = jnp.concatenate([w_router_hi, w_router_lo], axis=1)
    b_router = jnp.concatenate(
        [p["b_router_group"], p["b_router_expert"],
         jnp.zeros((LANES - N_GROUPS - N_EXPERTS,), F32)]).astype(F32).reshape(1, LANES)
    h1, u2p, eidx, wts = _outproj(mixed, h, p["w_out"].astype(BF16),
                                  p["norm_moe_w"].astype(F32).reshape(1, D_MODEL), w_router, b_router)

    tk = t * TOP_K
    e_blocks = eidx[:, :TOP_K].reshape(tk // RANK_BLOCK, 1, RANK_BLOCK)
    rank, counts = _rank(e_blocks)
    counts = counts[:, 0].astype(I32)
    padded = ((counts + ROW_BLOCK - 1) // ROW_BLOCK) * ROW_BLOCK
    pend = jnp.cumsum(padded)
    pstart = pend - padded
    n_blocks = tk // ROW_BLOCK + N_EXPERTS
    block_row0 = jnp.arange(n_blocks, dtype=I32) * ROW_BLOCK
    block_expert = jnp.minimum(jnp.sum((pend[None, :] <= block_row0[:, None]).astype(I32), axis=1), N_EXPERTS - 1)
    n_used = (pend[-1:] // ROW_BLOCK).astype(I32)
    dest = _dest(e_blocks, rank, jnp.broadcast_to(pstart[:, None], (N_EXPERTS, LANES)).astype(I32))
    xs = _dispatch(dest, u2p, n_blocks * ROW_BLOCK)
    y = _experts(block_expert, n_used, xs, p["w_gate_exp"], p["w_up_exp"], p["w_down_exp"])
    return dest, h1, wts, y


def kernel(x, norm_mix_w, w_in, conv_w, conv_b, dt_bias, a_log, d_skip, ssd_norm_w, fox_f_bias, w_proj_ssd,
           w_proj_fox, w_out, norm_moe_w, w_router_group, b_router_group, w_router_expert, b_router_expert,
           w_gate_exp, w_up_exp, w_down_exp, norm_final_w):
    bsz, seq, _ = x.shape
    depth = w_in.shape[0]
    assert depth == 1, "the fused final norm assumes a single layer"
    stacked = dict(norm_mix_w=norm_mix_w, w_in=w_in, conv_w=conv_w, conv_b=conv_b, dt_bias=dt_bias, a_log=a_log,
                   d_skip=d_skip, ssd_norm_w=ssd_norm_w, fox_f_bias=fox_f_bias, w_proj_ssd=w_proj_ssd,
                   w_proj_fox=w_proj_fox, w_out=w_out, norm_moe_w=norm_moe_w, w_router_group=w_router_group,
                   b_router_group=b_router_group, w_router_expert=w_router_expert,
                   b_router_expert=b_router_expert, w_gate_exp=w_gate_exp, w_up_exp=w_up_exp,
                   w_down_exp=w_down_exp)
    p = {name: v[0] for name, v in stacked.items()}
    h = x.reshape(bsz * seq, D_MODEL)
    dest, h1, wts, y = _layer(h, p, bsz, seq)
    out = _combine(dest, h1, wts, norm_final_w.astype(F32).reshape(1, D_MODEL), y)
    return out.reshape(bsz, seq, D_MODEL)
```

```python
import functools
import math

import jax
import jax.numpy as jnp
from jax import lax
from jax.experimental import pallas as pl
from jax.experimental.pallas import tpu as pltpu

F32 = jnp.float32
BF16 = jnp.bfloat16
I32 = jnp.int32
U32 = jnp.uint32

D_MODEL = 2048
SSD_HEADS = 32
SSD_HEAD_DIM = 64
SSD_DIM = SSD_HEADS * SSD_HEAD_DIM
SSD_GROUPS = 4
SSD_STATE = 128
SSD_CHUNK = 128
CONV_WIDTH = 4
CONV_DIM = SSD_DIM + 2 * SSD_GROUPS * SSD_STATE
FOX_HEADS = 16
FOX_HEAD_DIM = 128
FOX_DIM = FOX_HEADS * FOX_HEAD_DIM
N_GROUPS = 8
EXPERTS_PER_GROUP = 8
N_EXPERTS = N_GROUPS * EXPERTS_PER_GROUP
TOP_K = 2
D_EXPERT = 512
EPS = 1e-6

LANES = 128
SUBLANES = 8
VMEM_LIMIT = 52 * 1024 * 1024

COL_Z = 0
COL_Q = COL_Z + SSD_DIM
COL_K = COL_Q + FOX_DIM
COL_V = COL_K + FOX_DIM
COL_GA = COL_V + FOX_DIM
COL_GB = COL_GA + D_MODEL
COL_XBC = COL_GB + D_MODEL
PROJ_COLS = COL_XBC + CONV_DIM
SMALL_DT = 0
SMALL_DT_COPIES = 3
SMALL_F = SMALL_DT_COPIES * SSD_HEADS

ROW_BLOCK = 128
HALF = D_MODEL // 2


def _cparams(sem, vmem=VMEM_LIMIT):
    return pltpu.CompilerParams(dimension_semantics=sem, vmem_limit_bytes=vmem)


def _silu(x):
    return x * (1.0 / (1.0 + jnp.exp(-x)))


def _softplus(x):
    return jnp.maximum(x, 0.0) + jnp.log(1.0 + jnp.exp(-jnp.abs(x)))


def _log_sigmoid(x):
    return -_softplus(-x)


def _split3(x):
    hi = x.astype(BF16)
    rest = x - hi.astype(F32)
    mid = rest.astype(BF16)
    lo = (rest - mid.astype(F32)).astype(BF16)
    return hi, mid, lo


def _pack_bf16_pair(x):
    n = x.shape[1] // 2
    lo = pltpu.bitcast(x[:, :n].astype(BF16).astype(F32), U32)
    hi = pltpu.bitcast(x[:, n:].astype(BF16).astype(F32), U32)
    return (hi & jnp.uint32(0xFFFF0000)) | (lo >> 16)


def _unpack_bf16_pair(p):
    lo = pltpu.bitcast(p << 16, F32)
    hi = pltpu.bitcast(p & jnp.uint32(0xFFFF0000), F32)
    return lo, hi


def _norm_small_kernel(x_ref, nw_ref, ws_ref, u_ref, s_ref):
    x = x_ref[...]
    ms = jnp.mean(x * x, axis=-1, keepdims=True)
    u = (x * lax.rsqrt(ms + EPS) * nw_ref[...]).astype(BF16)
    u_ref[...] = u
    s_ref[...] = jnp.dot(u, ws_ref[...], preferred_element_type=F32)


def _norm_small(x2, nw, w_small, tm=512):
    t = x2.shape[0]
    return pl.pallas_call(
        _norm_small_kernel,
        name="norm_small",
        grid=(t // tm,),
        in_specs=[pl.BlockSpec((tm, D_MODEL), lambda i: (i, 0)),
                  pl.BlockSpec((1, D_MODEL), lambda i: (0, 0)),
                  pl.BlockSpec((D_MODEL, LANES), lambda i: (0, 0))],
        out_specs=[pl.BlockSpec((tm, D_MODEL), lambda i: (i, 0)),
                   pl.BlockSpec((tm, LANES), lambda i: (i, 0))],
        out_shape=[jax.ShapeDtypeStruct((t, D_MODEL), BF16),
                   jax.ShapeDtypeStruct((t, LANES), F32)],
        compiler_params=_cparams(("parallel",)),
    )(x2, nw, w_small)


def _matmul_kernel(a_ref, b_ref, o_ref):
    o_ref[...] = jnp.dot(a_ref[...], b_ref[...], preferred_element_type=F32).astype(o_ref.dtype)


def _matmul(a, b, tm=1024, tn=1024):
    m, k = a.shape
    n = b.shape[1]
    return pl.pallas_call(
        _matmul_kernel,
        name="in_proj",
        grid=(m // tm, n // tn),
        in_specs=[pl.BlockSpec((tm, k), lambda i, j: (i, 0)),
                  pl.BlockSpec((k, tn), lambda i, j: (0, j))],
        out_specs=pl.BlockSpec((tm, tn), lambda i, j: (i, j)),
        out_shape=jax.ShapeDtypeStruct((m, n), BF16),
        compiler_params=_cparams(("parallel", "parallel")),
    )(a, b)


HALO = 2 * SUBLANES


def _ssd_kernel(z_ref, xbc_ref, halo_ref, small_ref, cw_ref, cb_ref, dtb_ref, aneg_ref, dexp_ref,
                nw_ref, expand_ref, shift_ref, y_ref, state_ref, ydiag_ref):
    c = pl.program_id(1)
    l = SSD_CHUNK
    n = SSD_STATE

    @pl.when(c == 0)
    def _():
        state_ref[...] = jnp.zeros_like(state_ref)

    cur = xbc_ref[...]
    halo = halo_ref[...]
    halo = jnp.where(c == 0, jnp.zeros_like(halo), halo)
    ext = jnp.concatenate([halo, cur], axis=0)
    shifted = jnp.dot(shift_ref[...], ext, preferred_element_type=F32)
    conv = cb_ref[...] + cw_ref[CONV_WIDTH - 1:CONV_WIDTH, :] * cur.astype(F32)
    for j in range(CONV_WIDTH - 1):
        conv = conv + cw_ref[j:j + 1, :] * shifted[j * l:(j + 1) * l]
    xbc = _silu(conv)
    xs = xbc[:, :SSD_DIM]
    bm = xbc[:, SSD_DIM:SSD_DIM + SSD_GROUPS * n]
    cm = xbc[:, SSD_DIM + SSD_GROUPS * n:]

    h3 = SMALL_DT_COPIES * SSD_HEADS
    dt3 = _softplus(small_ref[:, SMALL_DT:SMALL_DT + h3] + dtb_ref[...])
    adt3 = dt3 * aneg_ref[...]
    row = lax.broadcasted_iota(I32, (l, l), 0)
    col = lax.broadcasted_iota(I32, (l, l), 1)
    causal = col <= row
    tril = jnp.where(causal, 1.0, 0.0).astype(BF16)
    a_cs3 = sum(jnp.dot(tril, piece, preferred_element_type=F32) for piece in _split3(adt3))
    a_cs = a_cs3[:, :SSD_HEADS]
    a_cs_t = jnp.transpose(a_cs)
    a_last3 = a_cs3[l - 1:l, :]

    lane3 = lax.broadcasted_iota(I32, (l, h3), 1)

    def pieces_by_lane_group(x3):
        hi, mid, lo = _split3(x3)
        return jnp.where(lane3 < SSD_HEADS, hi, jnp.where(lane3 < 2 * SSD_HEADS, mid, lo))

    lhs3 = jnp.concatenate([pieces_by_lane_group(dt3),
                            pieces_by_lane_group(jnp.exp(a_cs3)),
                            pieces_by_lane_group(jnp.exp(a_last3 - a_cs3))], axis=0)
    expanded = jnp.dot(lhs3, expand_ref[...], preferred_element_type=F32)
    dt_x = expanded[0:l]
    decay_in = expanded[l:2 * l]
    decay_out = expanded[2 * l:3 * l]
    chunk_decay = decay_in[l - 1:l, :]
    x_dt = xs * dt_x
    x_dt_b = x_dt.astype(BF16)
    xd_b = (x_dt * decay_out).astype(BF16)

    lane = lax.broadcasted_iota(I32, (l, LANES), 1)
    lo_mask = lane < SSD_HEAD_DIM
    heads_per_group = SSD_HEADS // SSD_GROUPS
    gw = heads_per_group * SSD_HEAD_DIM

    for g in range(SSD_GROUPS):
        bg = bm[:, g * n:(g + 1) * n].astype(BF16)
        cg = cm[:, g * n:(g + 1) * n].astype(BF16)
        cb = lax.dot_general(cg, bg, (((1,), (1,)), ((), ())), preferred_element_type=F32)
        for pair in range(heads_per_group // 2):
            h0 = g * heads_per_group + 2 * pair
            lane0 = h0 * SSD_HEAD_DIM
            ms = []
            for h in (h0, h0 + 1):
                seg = a_cs[:, h:h + 1] - a_cs_t[h:h + 1, :]
                lmat = jnp.exp(jnp.where(causal, seg, -jnp.inf))
                ms.append((cb * lmat).astype(BF16))
            lhs = jnp.concatenate(ms, axis=1)
            xp = x_dt_b[:, lane0:lane0 + LANES]
            zero = jnp.zeros_like(xp)
            rhs = jnp.concatenate([jnp.where(lo_mask, xp, zero), jnp.where(lo_mask, zero, xp)], axis=0)
            ydiag_ref[:, lane0:lane0 + LANES] = jnp.dot(lhs, rhs, preferred_element_type=F32)
        st = state_ref[:, g * gw:(g + 1) * gw]
        y_off = jnp.dot(cg, st.astype(BF16), preferred_element_type=F32)
        ydiag_ref[:, g * gw:(g + 1) * gw] += y_off * decay_in[:, g * gw:(g + 1) * gw]
        new = lax.dot_general(bg, xd_b[:, g * gw:(g + 1) * gw], (((0,), (0,)), ((), ())),
                              preferred_element_type=F32)
        state_ref[:, g * gw:(g + 1) * gw] = st * chunk_decay[:, g * gw:(g + 1) * gw] + new

    y = ydiag_ref[...] + dexp_ref[...] * xs
    y = y * _silu(z_ref[...].astype(F32))
    ms = jnp.mean(y * y, axis=-1, keepdims=True)
    y_ref[...] = (y * lax.rsqrt(ms + EPS) * nw_ref[...]).astype(y_ref.dtype)


def _ssd(proj, small, conv_w, conv_b, dt_bias, a_log, d_skip, norm_w, bsz, seq):
    l = SSD_CHUNK
    nc = seq // l
    t = bsz * seq
    h3 = SMALL_DT_COPIES * SSD_HEADS
    aneg = jnp.tile(-jnp.exp(a_log.astype(F32)), SMALL_DT_COPIES).reshape(1, h3)
    dtb = jnp.tile(dt_bias.astype(F32), SMALL_DT_COPIES).reshape(1, h3)
    dexp = jnp.repeat(d_skip.astype(F32), SSD_HEAD_DIM).reshape(1, SSD_DIM)
    expand = jnp.tile(jnp.repeat(jnp.eye(SSD_HEADS, dtype=BF16), SSD_HEAD_DIM, axis=1),
                      (SMALL_DT_COPIES, 1))
    out_row = jnp.arange((CONV_WIDTH - 1) * l)
    src_row = HALO + out_row % l - (CONV_WIDTH - 1) + out_row // l
    shift = (jnp.arange(HALO + l)[None, :] == src_row[:, None]).astype(BF16)
    xbc_blk = COL_XBC // CONV_DIM
    halo_per_chunk = l // HALO

    def row_map(b, c):
        return b * nc + c

    return pl.pallas_call(
        _ssd_kernel,
        name="ssd",
        grid=(bsz, nc),
        in_specs=[
            pl.BlockSpec((l, SSD_DIM), lambda b, c: (row_map(b, c), COL_Z // SSD_DIM)),
            pl.BlockSpec((l, CONV_DIM), lambda b, c: (row_map(b, c), xbc_blk)),
            pl.BlockSpec((HALO, CONV_DIM),
                         lambda b, c: (jnp.maximum(row_map(b, c) * halo_per_chunk - 1, 0), xbc_blk)),
            pl.BlockSpec((l, LANES), lambda b, c: (row_map(b, c), 0)),
            pl.BlockSpec((CONV_WIDTH, CONV_DIM), lambda b, c: (0, 0)),
            pl.BlockSpec((1, CONV_DIM), lambda b, c: (0, 0)),
            pl.BlockSpec((1, h3), lambda b, c: (0, 0)),
            pl.BlockSpec((1, h3), lambda b, c: (0, 0)),
            pl.BlockSpec((1, SSD_DIM), lambda b, c: (0, 0)),
            pl.BlockSpec((1, SSD_DIM), lambda b, c: (0, 0)),
            pl.BlockSpec((h3, SSD_DIM), lambda b, c: (0, 0)),
            pl.BlockSpec(((CONV_WIDTH - 1) * l, HALO + l), lambda b, c: (0, 0)),
        ],
        out_specs=pl.BlockSpec((l, SSD_DIM), lambda b, c: (row_map(b, c), 0)),
        out_shape=jax.ShapeDtypeStruct((t, SSD_DIM), BF16),
        scratch_shapes=[pltpu.VMEM((SSD_STATE, SSD_DIM), F32),
                        pltpu.VMEM((l, SSD_DIM), F32)],
        compiler_params=_cparams(("parallel", "arbitrary")),
    )(proj, proj, proj, small, conv_w.astype(F32), conv_b.astype(F32).reshape(1, CONV_DIM),
      dtb, aneg, dexp, norm_w.astype(F32).reshape(1, SSD_DIM), expand, shift)


CUM_ROWS = 256


def _cum_kernel(small_ref, fb_ref, cum_ref, carry_ref):
    j = pl.program_id(1)

    @pl.when(j == 0)
    def _():
        carry_ref[...] = jnp.zeros_like(carry_ref)

    lf = _log_sigmoid(small_ref[...] + fb_ref[...])
    row = lax.broadcasted_iota(I32, (CUM_ROWS, CUM_ROWS), 0)
    col = lax.broadcasted_iota(I32, (CUM_ROWS, CUM_ROWS), 1)
    tril = jnp.where(col <= row, 1.0, 0.0).astype(F32)
    cs = jnp.dot(tril, lf, preferred_element_type=F32, precision=lax.Precision.HIGHEST) + carry_ref[...]
    cum_ref[...] = cs * LOG2E
    carry_ref[...] = cs[CUM_ROWS - 1:CUM_ROWS, :]


def _fox_cum(small, fox_f_bias, bsz, seq):
    fb = jnp.zeros((1, LANES), F32).at[0, SMALL_F:SMALL_F + FOX_HEADS].set(fox_f_bias.astype(F32))
    nj = seq // CUM_ROWS
    return pl.pallas_call(
        _cum_kernel,
        name="fox_cum",
        grid=(bsz, nj),
        in_specs=[pl.BlockSpec((CUM_ROWS, LANES), lambda b, j: (b * nj + j, 0)),
                  pl.BlockSpec((1, LANES), lambda b, j: (0, 0))],
        out_specs=pl.BlockSpec((CUM_ROWS, LANES), lambda b, j: (b * nj + j, 0)),
        out_shape=jax.ShapeDtypeStruct((bsz * seq, LANES), F32),
        scratch_shapes=[pltpu.VMEM((1, LANES), F32)],
        compiler_params=_cparams(("parallel", "arbitrary")),
    )(small, fb)


FOX_HEADS_PER_STEP = 2
LOG2E = 1.4426950408889634


FOX_SLAB = 128


def _fox_kernel(q_ref, k_ref, v_ref, cq_ref, ck_ref, o_ref, sa_ref, sb_ref, p_ref, m_ref, l_ref, alpha_ref,
                cqrep_ref, acc_ref, *, tq, tk):
    hp = pl.program_id(1)
    qi = pl.program_id(2)
    d = FOX_HEAD_DIM
    c2 = LOG2E / math.sqrt(d)
    lane = lax.broadcasted_iota(I32, (tq, LANES), 1)
    cq_all = cq_ref[...]

    qs, cqs = [], []
    for hh in range(FOX_HEADS_PER_STEP):
        qs.append((q_ref[:, hh * d:(hh + 1) * d].astype(F32) * c2).astype(BF16))
        head_lane = SMALL_F + hp * FOX_HEADS_PER_STEP + hh
        cq_col = jnp.sum(jnp.where(lane == head_lane, cq_all, 0.0), axis=-1, keepdims=True)
        cqs.append(jnp.broadcast_to(cq_col, (tq, LANES)))

    s_slots = (sa_ref, sb_ref)

    def scores(ki, slot):
        row0 = pl.multiple_of(ki * tk, tk)
        for hh in range(FOX_HEADS_PER_STEP):
            k = k_ref[pl.ds(row0, tk), hh * d:(hh + 1) * d]
            s = lax.dot_general(qs[hh], k, (((1,), (1,)), ((), ())), preferred_element_type=F32)
            s_slots[slot][hh] = s - ck_ref[hh, pl.ds(ki, 1), :]

    def update(ki, slot, masked):
        row0 = pl.multiple_of(ki * tk, tk)
        s_ref = s_slots[slot]
        n_ct = tk // LANES
        for hh in range(FOX_HEADS_PER_STEP):
            for rc in range(tq // FOX_SLAB):
                rows = slice(rc * FOX_SLAB, (rc + 1) * FOX_SLAB)

                def slab(ct):
                    x = s_ref[hh, rows, ct * LANES:(ct + 1) * LANES]
                    if masked:
                        row = lax.broadcasted_iota(I32, (FOX_SLAB, LANES), 0) + (qi * tq + rc * FOX_SLAB)
                        col = lax.broadcasted_iota(I32, (FOX_SLAB, LANES), 1) + (ki * tk + ct * LANES)
                        x = jnp.where(col <= row, x, -jnp.inf)
                    return x

                tmax = slab(0)
                for ct in range(1, n_ct):
                    tmax = jnp.maximum(tmax, slab(ct))
                row_max = jnp.max(tmax, axis=-1, keepdims=True)
                cq = cqrep_ref[hh, rows, :]
                m_prev = m_ref[hh, rows, :]
                m_new = jnp.maximum(m_prev, jnp.broadcast_to(row_max, (FOX_SLAB, LANES)) + cq)
                r = m_new - cq
                alpha = jnp.exp2(m_prev - m_new)
                psum = None
                for ct in range(n_ct):
                    p = jnp.exp2(slab(ct) - r)
                    psum = p if psum is None else psum + p
                    p_ref[hh, rows, ct * LANES:(ct + 1) * LANES] = p.astype(BF16)
                l_ref[hh, rows, :] = alpha * l_ref[hh, rows, :] + psum
                m_ref[hh, rows, :] = m_new
                alpha_ref[hh, rows, :] = alpha
            v = v_ref[pl.ds(row0, tk), hh * d:(hh + 1) * d]
            acc_ref[hh] = alpha_ref[hh] * acc_ref[hh] + jnp.dot(p_ref[hh], v, preferred_element_type=F32)

    for hh in range(FOX_HEADS_PER_STEP):
        cqrep_ref[hh] = cqs[hh]

    m_ref[...] = jnp.full_like(m_ref, -jnp.inf)
    l_ref[...] = jnp.zeros_like(l_ref)
    acc_ref[...] = jnp.zeros_like(acc_ref)

    n_full = qi
    scores(0, 0)

    def pair(j, carry):
        b0 = 2 * j
        scores(b0 + 1, 1)
        update(b0, 0, False)
        scores(b0 + 2, 0)
        update(b0 + 1, 1, False)
        return carry

    lax.fori_loop(0, n_full // 2, pair, 0)

    @pl.when(n_full % 2 == 0)
    def _():
        update(n_full, 0, True)

    @pl.when(n_full % 2 == 1)
    def _():
        scores(n_full, 1)
        update(n_full - 1, 0, False)
        update(n_full, 1, True)

    for hh in range(FOX_HEADS_PER_STEP):
        l_fin = jnp.sum(l_ref[hh], axis=-1, keepdims=True)
        o_ref[:, hh * d:(hh + 1) * d] = (acc_ref[hh] / l_fin).astype(o_ref.dtype)


def _fox(proj, cum, bsz, seq, tq=512):
    tk = tq
    nq = seq // tq
    nk = seq // tk
    hps = FOX_HEADS_PER_STEP
    cum_row = cum[:, SMALL_F:SMALL_F + FOX_HEADS].reshape(bsz, seq, FOX_HEADS).transpose(0, 2, 1)
    cum_row = cum_row.reshape(bsz, FOX_HEADS, nk, tk)
    t = bsz * seq
    w = FOX_HEADS_PER_STEP * FOX_HEAD_DIM
    n_hp = FOX_HEADS // FOX_HEADS_PER_STEP
    kern = functools.partial(_fox_kernel, tq=tq, tk=tk)
    return pl.pallas_call(
        kern,
        name="fox",
        grid=(bsz, n_hp, nq),
        in_specs=[
            pl.BlockSpec((tq, w), lambda b, hp, qi: (b * nq + qi, COL_Q // w + hp)),
            pl.BlockSpec((seq, w), lambda b, hp, qi: (b, COL_K // w + hp)),
            pl.BlockSpec((seq, w), lambda b, hp, qi: (b, COL_V // w + hp)),
            pl.BlockSpec((tq, LANES), lambda b, hp, qi: (b * nq + qi, 0)),
            pl.BlockSpec((None, FOX_HEADS_PER_STEP, nk, tk), lambda b, hp, qi: (b, hp, 0, 0)),
        ],
        out_specs=pl.BlockSpec((tq, w), lambda b, hp, qi: (b * nq + qi, hp)),
        out_shape=jax.ShapeDtypeStruct((t, FOX_DIM), BF16),
        scratch_shapes=[pltpu.VMEM((hps, tq, tk), F32), pltpu.VMEM((hps, tq, tk), F32),
                        pltpu.VMEM((hps, tq, tk), BF16),
                        pltpu.VMEM((hps, tq, LANES), F32), pltpu.VMEM((hps, tq, LANES), F32),
                        pltpu.VMEM((hps, tq, LANES), F32), pltpu.VMEM((hps, tq, LANES), F32),
                        pltpu.VMEM((hps, tq, FOX_HEAD_DIM), F32)],
        compiler_params=_cparams(("parallel", "parallel", "arbitrary")),
    )(proj, proj, proj, cum, cum_row)


def _mix_kernel(ya_ref, yb_ref, ga_ref, gb_ref, wa_ref, wb_ref, o_ref):
    pa = jnp.dot(ya_ref[...], wa_ref[...], preferred_element_type=F32)
    pb = jnp.dot(yb_ref[...], wb_ref[...], preferred_element_type=F32)
    ga = 1.0 / (1.0 + jnp.exp(-ga_ref[...].astype(F32)))
    gb = 1.0 / (1.0 + jnp.exp(-gb_ref[...].astype(F32)))
    o_ref[...] = (ga * pa + gb * pb).astype(o_ref.dtype)


def _mix(y_a, y_b, proj, w_a, w_b, tm=512, tn=1024):
    t = y_a.shape[0]
    return pl.pallas_call(
        _mix_kernel,
        name="mix",
        grid=(D_MODEL // tn, t // tm),
        in_specs=[
            pl.BlockSpec((tm, SSD_DIM), lambda j, i: (i, 0)),
            pl.BlockSpec((tm, FOX_DIM), lambda j, i: (i, 0)),
            pl.BlockSpec((tm, tn), lambda j, i: (i, COL_GA // tn + j)),
            pl.BlockSpec((tm, tn), lambda j, i: (i, COL_GB // tn + j)),
            pl.BlockSpec((SSD_DIM, tn), lambda j, i: (0, j)),
            pl.BlockSpec((FOX_DIM, tn), lambda j, i: (0, j)),
        ],
        out_specs=pl.BlockSpec((tm, tn), lambda j, i: (i, j)),
        out_shape=jax.ShapeDtypeStruct((t, D_MODEL), BF16),
        compiler_params=_cparams(("parallel", "parallel")),
    )(y_a, y_b, proj, proj, w_a, w_b)


def _outproj_kernel(m_ref, x_ref, wo_ref, nw_ref, wr_ref, br_ref, h_ref, u_ref, eidx_ref, wts_ref):
    h1 = x_ref[...] + jnp.dot(m_ref[...], wo_ref[...], preferred_element_type=F32)
    h_ref[...] = h1
    ms = jnp.mean(h1 * h1, axis=-1, keepdims=True)
    u2 = h1 * lax.rsqrt(ms + EPS) * nw_ref[...]
    u_ref[...] = _pack_bf16_pair(u2)

    u_hi = u2.astype(BF16)
    u_lo = (u2 - u_hi.astype(F32)).astype(BF16)
    hh_hl = jnp.dot(u_hi, wr_ref[...], preferred_element_type=F32)
    lh = jnp.dot(u_lo, wr_ref[:, :LANES], preferred_element_type=F32)
    logits = hh_hl[:, :LANES] + (hh_hl[:, LANES:] + lh) + br_ref[...]
    tm = logits.shape[0]
    lane = lax.broadcasted_iota(I32, (tm, LANES), 1)
    neg = -jnp.inf
    big = jnp.int32(2 * LANES)
    gl = jnp.where(lane < N_GROUPS, logits, neg)
    gmax = jnp.max(gl, axis=-1, keepdims=True)
    gsum = jnp.sum(jnp.exp(gl - gmax), axis=-1, keepdims=True)
    g_p = 1.0 / gsum
    g_idx = jnp.min(jnp.where(gl == gmax, lane, big), axis=-1, keepdims=True)
    e_of_lane = lane - N_GROUPS
    in_grp = (e_of_lane >= g_idx * EXPERTS_PER_GROUP) & (e_of_lane < (g_idx + 1) * EXPERTS_PER_GROUP)
    el = jnp.where(in_grp, logits, neg)
    m1 = jnp.max(el, axis=-1, keepdims=True)
    i1 = jnp.min(jnp.where(el == m1, lane, big), axis=-1, keepdims=True)
    el2 = jnp.where(lane == i1, neg, el)
    m2 = jnp.max(el2, axis=-1, keepdims=True)
    i2 = jnp.min(jnp.where(el2 == m2, lane, big), axis=-1, keepdims=True)
    esum = jnp.sum(jnp.exp(el - m1), axis=-1, keepdims=True)
    p1 = 1.0 / esum
    p2 = jnp.exp(m2 - m1) / esum
    w1 = g_p * (p1 / (p1 + p2))
    w2 = g_p * (p2 / (p1 + p2))
    eidx_ref[...] = jnp.where(lane == 0, i1 - N_GROUPS, jnp.where(lane == 1, i2 - N_GROUPS, 0))
    wts_ref[...] = jnp.where(lane == 0, w1, jnp.where(lane == 1, w2, 0.0))


def _outproj(mixed, x2, w_o, nw, w_router, b_router, tm=256):
    t = mixed.shape[0]
    return pl.pallas_call(
        _outproj_kernel,
        name="outproj",
        grid=(t // tm,),
        in_specs=[
            pl.BlockSpec((tm, D_MODEL), lambda i: (i, 0)),
            pl.BlockSpec((tm, D_MODEL), lambda i: (i, 0)),
            pl.BlockSpec((D_MODEL, D_MODEL), lambda i: (0, 0)),
            pl.BlockSpec((1, D_MODEL), lambda i: (0, 0)),
            pl.BlockSpec((D_MODEL, 2 * LANES), lambda i: (0, 0)),
            pl.BlockSpec((1, LANES), lambda i: (0, 0)),
        ],
        out_specs=[
            pl.BlockSpec((tm, D_MODEL), lambda i: (i, 0)),
            pl.BlockSpec((tm, HALF), lambda i: (i, 0)),
            pl.BlockSpec((tm, LANES), lambda i: (i, 0)),
            pl.BlockSpec((tm, LANES), lambda i: (i, 0)),
        ],
        out_shape=[
            jax.ShapeDtypeStruct((t, D_MODEL), F32),
            jax.ShapeDtypeStruct((t, HALF), U32),
            jax.ShapeDtypeStruct((t, LANES), I32),
            jax.ShapeDtypeStruct((t, LANES), F32),
        ],
        compiler_params=_cparams(("parallel",)),
    )(mixed, x2, w_o, nw, w_router, b_router)


RANK_BLOCK = 512


def _rank_kernel(e_ref, rank_ref, cnt_ref, carry_ref):
    i = pl.program_id(0)
    r = RANK_BLOCK

    @pl.when(i == 0)
    def _():
        carry_ref[...] = jnp.zeros_like(carry_ref)

    e = e_ref[0]
    expert = lax.broadcasted_iota(I32, (N_EXPERTS, r), 0)
    onehot = jnp.where(expert == e, 1.0, 0.0).astype(F32)
    jrow = lax.broadcasted_iota(I32, (r, r), 0)
    jcol = lax.broadcasted_iota(I32, (r, r), 1)
    before = jnp.where(jrow < jcol, 1.0, 0.0).astype(BF16)
    cum = jnp.dot(onehot.astype(BF16), before, preferred_element_type=F32)
    carry = carry_ref[...]
    rank = jnp.sum(onehot * (cum + carry[:, 0:1]), axis=0, keepdims=True)
    rank_ref[0] = rank.astype(I32)
    carry = carry + jnp.sum(onehot, axis=1, keepdims=True)
    carry_ref[...] = carry
    cnt_ref[...] = carry


def _rank(e_blocks):
    nb = e_blocks.shape[0]
    return pl.pallas_call(
        _rank_kernel,
        name="rank",
        grid=(nb,),
        in_specs=[pl.BlockSpec((1, 1, RANK_BLOCK), lambda i: (i, 0, 0))],
        out_specs=[pl.BlockSpec((1, 1, RANK_BLOCK), lambda i: (i, 0, 0)),
                   pl.BlockSpec((N_EXPERTS, LANES), lambda i: (0, 0))],
        out_shape=[jax.ShapeDtypeStruct((nb, 1, RANK_BLOCK), I32),
                   jax.ShapeDtypeStruct((N_EXPERTS, LANES), F32)],
        scratch_shapes=[pltpu.VMEM((N_EXPERTS, LANES), F32)],
        compiler_params=_cparams(("arbitrary",)),
    )(e_blocks)


def _dest_kernel(e_ref, rank_ref, pstart_ref, dest_ref):
    e = e_ref[0]
    expert = lax.broadcasted_iota(I32, (N_EXPERTS, RANK_BLOCK), 0)
    start = jnp.sum(jnp.where(expert == e, pstart_ref[:, 0:1], 0), axis=0, keepdims=True)
    dest_ref[0] = rank_ref[0] + start


def _dest(e_blocks, rank, pstart):
    nb = e_blocks.shape[0]
    return pl.pallas_call(
        _dest_kernel,
        name="dest",
        grid=(nb,),
        in_specs=[pl.BlockSpec((1, 1, RANK_BLOCK), lambda i: (i, 0, 0)),
                  pl.BlockSpec((1, 1, RANK_BLOCK), lambda i: (i, 0, 0)),
                  pl.BlockSpec((N_EXPERTS, LANES), lambda i: (0, 0))],
        out_specs=pl.BlockSpec((1, 1, RANK_BLOCK), lambda i: (i, 0, 0)),
        out_shape=jax.ShapeDtypeStruct((nb, 1, RANK_BLOCK), I32),
        compiler_params=_cparams(("parallel",)),
    )(e_blocks, rank, pstart)


def _row_copy(src_ref, src_row, dst_ref, dst_row, sem):
    return pltpu.make_async_copy(src_ref.at[pl.ds(src_row, 1)], dst_ref.at[pl.ds(dst_row, 1)], sem)


DMA_UNROLL = 16


def _dispatch_kernel(dest_ref, u_ref, zeros_ref, xs_ref, sem):
    del zeros_ref

    def issue(g, carry):
        tok = g * (DMA_UNROLL // TOP_K)
        c0 = g * DMA_UNROLL
        for j in range(DMA_UNROLL):
            _row_copy(u_ref, tok + j // TOP_K, xs_ref, dest_ref[0, 0, c0 + j], sem).start()
        return carry

    lax.fori_loop(0, RANK_BLOCK // DMA_UNROLL, issue, 0)
    pltpu.make_async_copy(xs_ref.at[pl.ds(0, RANK_BLOCK)], xs_ref.at[pl.ds(0, RANK_BLOCK)], sem).wait()


def _dispatch(dest, u2p, n_rows):
    nb = dest.shape[0]
    zeros = jnp.zeros((n_rows, HALF), U32)
    return pl.pallas_call(
        _dispatch_kernel,
        name="dispatch",
        grid=(nb,),
        in_specs=[pl.BlockSpec((1, 1, RANK_BLOCK), lambda i: (i, 0, 0), memory_space=pltpu.SMEM),
                  pl.BlockSpec((RANK_BLOCK // TOP_K, HALF), lambda i: (i, 0)),
                  pl.BlockSpec(memory_space=pl.ANY)],
        out_specs=pl.BlockSpec(memory_space=pl.ANY),
        out_shape=jax.ShapeDtypeStruct((n_rows, HALF), U32),
        scratch_shapes=[pltpu.SemaphoreType.DMA(())],
        input_output_aliases={2: 0},
        compiler_params=_cparams(("arbitrary",)),
    )(dest, u2p, zeros)


def _experts_kernel(bstart_ref, nblk_ref, xs_ref, wg_ref, wu_ref, wd_ref, y_ref,
                    wgb_ref, wub_ref, wdb_ref, xbuf_ref, ybuf_ref, xsem, ysem):
    e = pl.program_id(0)
    n_e = pl.num_programs(0)
    g0 = bstart_ref[e]
    nb = nblk_ref[e]
    n_used = bstart_ref[n_e - 1] + nblk_ref[n_e - 1]
    n_total = y_ref.shape[0] // ROW_BLOCK

    def x_copy(g, slot):
        return pltpu.make_async_copy(xs_ref.at[pl.ds(g * ROW_BLOCK, ROW_BLOCK)], xbuf_ref.at[slot], xsem.at[slot])

    def y_copy(g, slot):
        return pltpu.make_async_copy(ybuf_ref.at[slot], y_ref.at[pl.ds(g * ROW_BLOCK, ROW_BLOCK)], ysem.at[slot])

    @pl.when((e == 0) & (n_used > 0))
    def _():
        x_copy(0, 0).start()

    @pl.when(nb > 0)
    def _():
        wgb_ref[...] = wg_ref[0].astype(BF16)
        wub_ref[...] = wu_ref[0].astype(BF16)
        wdb_ref[...] = wd_ref[0].astype(BF16)

    def block(j, carry):
        g = g0 + j
        slot = g % 2
        x_copy(g, slot).wait()

        @pl.when(g + 1 < n_used)
        def _():
            x_copy(g + 1, 1 - slot).start()

        @pl.when(g >= 2)
        def _():
            y_copy(g - 2, slot).wait()

        lo, hi = _unpack_bf16_pair(xbuf_ref[slot])
        lo = lo.astype(BF16)
        hi = hi.astype(BF16)
        gate = (jnp.dot(lo, wgb_ref[:HALF, :], preferred_element_type=F32)
                + jnp.dot(hi, wgb_ref[HALF:, :], preferred_element_type=F32))
        up = (jnp.dot(lo, wub_ref[:HALF, :], preferred_element_type=F32)
              + jnp.dot(hi, wub_ref[HALF:, :], preferred_element_type=F32))
        hdn = (_silu(gate) * up).astype(BF16)
        y = jnp.dot(hdn, wdb_ref[...], preferred_element_type=F32)
        ybuf_ref[slot] = _pack_bf16_pair(y)
        y_copy(g, slot).start()
        return carry

    lax.fori_loop(0, nb, block, 0)

    @pl.when(e == n_e - 1)
    def _():
        @pl.when(n_used >= 2)
        def _():
            y_copy(n_used - 2, n_used % 2).wait()

        @pl.when(n_used >= 1)
        def _():
            y_copy(n_used - 1, (n_used - 1) % 2).wait()

        ybuf_ref[0] = jnp.zeros((ROW_BLOCK, HALF), U32)

        def fill(g, carry):
            y_copy(g, 0).start()
            y_copy(g, 0).wait()
            return carry

        lax.fori_loop(n_used, n_total, fill, 0)


def _experts(block_start, block_count, xs, w_gate, w_up, w_down):
    n_rows = xs.shape[0]

    def wmap(e, bs, bc):
        return (e, 0, 0)

    grid_spec = pltpu.PrefetchScalarGridSpec(
        num_scalar_prefetch=2,
        grid=(N_EXPERTS,),
        in_specs=[pl.BlockSpec(memory_space=pl.ANY),
                  pl.BlockSpec((1, D_MODEL, D_EXPERT), wmap),
                  pl.BlockSpec((1, D_MODEL, D_EXPERT), wmap),
                  pl.BlockSpec((1, D_EXPERT, D_MODEL), wmap)],
        out_specs=pl.BlockSpec(memory_space=pl.ANY),
        scratch_shapes=[pltpu.VMEM((D_MODEL, D_EXPERT), BF16),
                        pltpu.VMEM((D_MODEL, D_EXPERT), BF16),
                        pltpu.VMEM((D_EXPERT, D_MODEL), BF16),
                        pltpu.VMEM((2, ROW_BLOCK, HALF), U32),
                        pltpu.VMEM((2, ROW_BLOCK, HALF), U32),
                        pltpu.SemaphoreType.DMA((2,)),
                        pltpu.SemaphoreType.DMA((2,))],
    )
    return pl.pallas_call(
        _experts_kernel,
        name="experts",
        grid_spec=grid_spec,
        out_shape=jax.ShapeDtypeStruct((n_rows, HALF), U32),
        compiler_params=_cparams(("arbitrary",)),
    )(block_start, block_count, xs, w_gate, w_up, w_down)


COMBINE_TOKENS = RANK_BLOCK // TOP_K


def _combine_kernel(dest_ref, dest_next_ref, h_ref, wts_ref, nw_ref, y_ref, o_ref, buf_ref, sem):
    ts = COMBINE_TOKENS
    i = pl.program_id(0)
    slot = i % 2

    def gather(idx_ref, dst_slot):
        def issue(g, carry):
            tok = g * (DMA_UNROLL // TOP_K)
            c0 = g * DMA_UNROLL
            for j in range(DMA_UNROLL):
                _row_copy(y_ref, idx_ref[0, 0, c0 + j], buf_ref.at[dst_slot],
                          (j % TOP_K) * ts + tok + j // TOP_K, sem.at[dst_slot]).start()
            return carry

        lax.fori_loop(0, RANK_BLOCK // DMA_UNROLL, issue, 0)

    @pl.when(i == 0)
    def _():
        gather(dest_ref, 0)

    @pl.when(i + 1 < pl.num_programs(0))
    def _():
        gather(dest_next_ref, 1 - slot)

    pltpu.make_async_copy(y_ref.at[pl.ds(0, RANK_BLOCK)], buf_ref.at[slot], sem.at[slot]).wait()

    w = wts_ref[...]
    w0 = w[:, 0:1]
    w1 = w[:, 1:2]
    lo0, hi0 = _unpack_bf16_pair(buf_ref[slot, 0:ts, :])
    lo1, hi1 = _unpack_bf16_pair(buf_ref[slot, ts:2 * ts, :])
    h = h_ref[...]
    out_lo = h[:, :HALF] + w0 * lo0 + w1 * lo1
    out_hi = h[:, HALF:] + w0 * hi0 + w1 * hi1
    ms = (jnp.sum(out_lo * out_lo, axis=-1, keepdims=True)
          + jnp.sum(out_hi * out_hi, axis=-1, keepdims=True)) * (1.0 / D_MODEL)
    inv = lax.rsqrt(ms + EPS)
    o_ref[:, :HALF] = out_lo * inv * nw_ref[:, :HALF]
    o_ref[:, HALF:] = out_hi * inv * nw_ref[:, HALF:]


def _combine(dest, h1, wts, nw, y):
    t = h1.shape[0]
    ts = COMBINE_TOKENS
    n_steps = t // ts
    return pl.pallas_call(
        _combine_kernel,
        name="combine",
        grid=(n_steps,),
        in_specs=[pl.BlockSpec((1, 1, RANK_BLOCK), lambda i: (i, 0, 0), memory_space=pltpu.SMEM),
                  pl.BlockSpec((1, 1, RANK_BLOCK), lambda i: (jnp.minimum(i + 1, n_steps - 1), 0, 0),
                               memory_space=pltpu.SMEM),
                  pl.BlockSpec((ts, D_MODEL), lambda i: (i, 0)),
                  pl.BlockSpec((ts, LANES), lambda i: (i, 0)),
                  pl.BlockSpec((1, D_MODEL), lambda i: (0, 0)),
                  pl.BlockSpec(memory_space=pl.ANY)],
        out_specs=pl.BlockSpec((ts, D_MODEL), lambda i: (i, 0)),
        out_shape=jax.ShapeDtypeStruct((t, D_MODEL), F32),
        scratch_shapes=[pltpu.VMEM((2, RANK_BLOCK, HALF), U32), pltpu.SemaphoreType.DMA((2,))],
        compiler_params=_cparams(("arbitrary",)),
    )(dest, dest, h1, wts, nw, y)


def _permute_w_in(w_in):
    sizes = (SSD_DIM, CONV_DIM, SSD_HEADS, FOX_DIM, FOX_DIM, FOX_DIM, FOX_HEADS, 2 * D_MODEL)
    offs = [0]
    for s in sizes:
        offs.append(offs[-1] + s)
    z, xbc, dt, q, k, v, f, gates = (w_in[:, offs[i]:offs[i + 1]] for i in range(len(sizes)))
    wide = jnp.concatenate([z, q, k, v, gates, xbc], axis=1).astype(BF16)
    pad = jnp.zeros((w_in.shape[0], LANES - SMALL_F - FOX_HEADS), w_in.dtype)
    narrow = jnp.concatenate([dt] * SMALL_DT_COPIES + [f, pad], axis=1).astype(BF16)
    return wide, narrow


def _layer(h, p, bsz, seq):
    t = bsz * seq
    wide, narrow = _permute_w_in(p["w_in"])
    u, small = _norm_small(h, p["norm_mix_w"].astype(F32).reshape(1, D_MODEL), narrow)
    proj = _matmul(u, wide)
    y_ssd = _ssd(proj, small, p["conv_w"], p["conv_b"], p["dt_bias"], p["a_log"], p["d_skip"],
                 p["ssd_norm_w"], bsz, seq)
    cum = _fox_cum(small, p["fox_f_bias"], bsz, seq)
    y_fox = _fox(proj, cum, bsz, seq)
    mixed = _mix(y_ssd, y_fox, proj, p["w_proj_ssd"].astype(BF16), p["w_proj_fox"].astype(BF16))

    w_router = jnp.concatenate(
        [p["w_router_group"], p["w_router_expert"],
         jnp.zeros((D_MODEL, LANES - N_GROUPS - N_EXPERTS), F32)], axis=1).astype(F32)
    w_router_hi = w_router.astype(BF16)
    w_router_lo = (w_router - w_router_hi.astype(F32)).astype(BF16)
    w_router = jnp.concatenate([w_router_hi, w_router_lo], axis=1)
    b_router = jnp.concatenate(
        [p["b_router_group"], p["b_router_expert"],
         jnp.zeros((LANES - N_GROUPS - N_EXPERTS,), F32)]).astype(F32).reshape(1, LANES)
    h1, u2p, eidx, wts = _outproj(mixed, h, p["w_out"].astype(BF16),
                                  p["norm_moe_w"].astype(F32).reshape(1, D_MODEL), w_router, b_router)

    tk = t * TOP_K
    e_blocks = eidx[:, :TOP_K].reshape(tk // RANK_BLOCK, 1, RANK_BLOCK)
    rank, counts = _rank(e_blocks)
    counts = counts[:, 0].astype(I32)
    padded = ((counts + ROW_BLOCK - 1) // ROW_BLOCK) * ROW_BLOCK
    pend = jnp.cumsum(padded)
    pstart = pend - padded
    n_blocks = tk // ROW_BLOCK + N_EXPERTS
    dest = _dest(e_blocks, rank, jnp.broadcast_to(pstart[:, None], (N_EXPERTS, LANES)).astype(I32))
    xs = _dispatch(dest, u2p, n_blocks * ROW_BLOCK)
    y = _experts((pstart // ROW_BLOCK).astype(I32), (padded // ROW_BLOCK).astype(I32), xs,
                 p["w_gate_exp"], p["w_up_exp"], p["w_down_exp"])
    return dest, h1, wts, y


def kernel(x, norm_mix_w, w_in, conv_w, conv_b, dt_bias, a_log, d_skip, ssd_norm_w, fox_f_bias, w_proj_ssd,
           w_proj_fox, w_out, norm_moe_w, w_router_group, b_router_group, w_router_expert, b_router_expert,
           w_gate_exp, w_up_exp, w_down_exp, norm_final_w):
    bsz, seq, _ = x.shape
    depth = w_in.shape[0]
    assert depth == 1, "the fused final norm assumes a single layer"
    stacked = dict(norm_mix_w=norm_mix_w, w_in=w_in, conv_w=conv_w, conv_b=conv_b, dt_bias=dt_bias, a_log=a_log,
                   d_skip=d_skip, ssd_norm_w=ssd_norm_w, fox_f_bias=fox_f_bias, w_proj_ssd=w_proj_ssd,
                   w_proj_fox=w_proj_fox, w_out=w_out, norm_moe_w=norm_moe_w, w_router_group=w_router_group,
                   b_router_group=b_router_group, w_router_expert=w_router_expert,
                   b_router_expert=b_router_expert, w_gate_exp=w_gate_exp, w_up_exp=w_up_exp,
                   w_down_exp=w_down_exp)
    p = {name: v[0] for name, v in stacked.items()}
    h = x.reshape(bsz * seq, D_MODEL)
    dest, h1, wts, y = _layer(h, p, bsz, seq)
    out = _combine(dest, h1, wts, norm_final_w.astype(F32).reshape(1, D_MODEL), y)
    return out.reshape(bsz, seq, D_MODEL)
```

```python
import functools
import math

import jax
import jax.numpy as jnp
from jax import lax
from jax.experimental import pallas as pl
from jax.experimental.pallas import tpu as pltpu

F32 = jnp.float32
BF16 = jnp.bfloat16
I32 = jnp.int32
U32 = jnp.uint32

D_MODEL = 2048
SSD_HEADS = 32
SSD_HEAD_DIM = 64
SSD_DIM = SSD_HEADS * SSD_HEAD_DIM
SSD_GROUPS = 4
SSD_STATE = 128
SSD_CHUNK = 128
CONV_WIDTH = 4
CONV_DIM = SSD_DIM + 2 * SSD_GROUPS * SSD_STATE
FOX_HEADS = 16
FOX_HEAD_DIM = 128
FOX_DIM = FOX_HEADS * FOX_HEAD_DIM
N_GROUPS = 8
EXPERTS_PER_GROUP = 8
N_EXPERTS = N_GROUPS * EXPERTS_PER_GROUP
TOP_K = 2
D_EXPERT = 512
EPS = 1e-6

LANES = 128
SUBLANES = 8
VMEM_LIMIT = 52 * 1024 * 1024

COL_Z = 0
COL_Q = COL_Z + SSD_DIM
COL_K = COL_Q + FOX_DIM
COL_V = COL_K + FOX_DIM
COL_GA = COL_V + FOX_DIM
COL_GB = COL_GA + D_MODEL
COL_XBC = COL_GB + D_MODEL
PROJ_COLS = COL_XBC + CONV_DIM
SMALL_DT = 0
SMALL_DT_COPIES = 3
SMALL_F = SMALL_DT_COPIES * SSD_HEADS

ROW_BLOCK = 128
HALF = D_MODEL // 2


def _cparams(sem, vmem=VMEM_LIMIT):
    return pltpu.CompilerParams(dimension_semantics=sem, vmem_limit_bytes=vmem)


def _silu(x):
    return x * (1.0 / (1.0 + jnp.exp(-x)))


def _softplus(x):
    return jnp.maximum(x, 0.0) + jnp.log(1.0 + jnp.exp(-jnp.abs(x)))


def _log_sigmoid(x):
    return -_softplus(-x)


def _split3(x):
    hi = x.astype(BF16)
    rest = x - hi.astype(F32)
    mid = rest.astype(BF16)
    lo = (rest - mid.astype(F32)).astype(BF16)
    return hi, mid, lo


def _pack_bf16_pair(x):
    n = x.shape[1] // 2
    lo = pltpu.bitcast(x[:, :n].astype(BF16).astype(F32), U32)
    hi = pltpu.bitcast(x[:, n:].astype(BF16).astype(F32), U32)
    return (hi & jnp.uint32(0xFFFF0000)) | (lo >> 16)


def _unpack_bf16_pair(p):
    lo = pltpu.bitcast(p << 16, F32)
    hi = pltpu.bitcast(p & jnp.uint32(0xFFFF0000), F32)
    return lo, hi


def _norm_small_kernel(x_ref, nw_ref, ws_ref, u_ref, s_ref):
    x = x_ref[...]
    ms = jnp.mean(x * x, axis=-1, keepdims=True)
    u = (x * lax.rsqrt(ms + EPS) * nw_ref[...]).astype(BF16)
    u_ref[...] = u
    s_ref[...] = jnp.dot(u, ws_ref[...], preferred_element_type=F32)


def _norm_small(x2, nw, w_small, tm=512):
    t = x2.shape[0]
    return pl.pallas_call(
        _norm_small_kernel,
        name="norm_small",
        grid=(t // tm,),
        in_specs=[pl.BlockSpec((tm, D_MODEL), lambda i: (i, 0)),
                  pl.BlockSpec((1, D_MODEL), lambda i: (0, 0)),
                  pl.BlockSpec((D_MODEL, LANES), lambda i: (0, 0))],
        out_specs=[pl.BlockSpec((tm, D_MODEL), lambda i: (i, 0)),
                   pl.BlockSpec((tm, LANES), lambda i: (i, 0))],
        out_shape=[jax.ShapeDtypeStruct((t, D_MODEL), BF16),
                   jax.ShapeDtypeStruct((t, LANES), F32)],
        compiler_params=_cparams(("parallel",)),
    )(x2, nw, w_small)


IN_TN = 1024
IN_SRC_TILES = IN_TN // LANES + 1
_IN_SEGMENTS = ((COL_Z, 0, SSD_DIM),
                (COL_Q, SSD_DIM + CONV_DIM + SSD_HEADS, 3 * FOX_DIM),
                (COL_GA, SSD_DIM + CONV_DIM + SSD_HEADS + 3 * FOX_DIM + FOX_HEADS, 2 * D_MODEL),
                (COL_XBC, SSD_DIM, CONV_DIM))
IN_LANE_OFFSETS = tuple(sorted({src % LANES for _, src, _ in _IN_SEGMENTS}))


def _in_proj_source_table():
    src_col = [0] * (PROJ_COLS // IN_TN)
    for out0, src0, width in _IN_SEGMENTS:
        for c in range(0, width, IN_TN):
            src_col[(out0 + c) // IN_TN] = src0 + c
    return ([c // LANES for c in src_col], [c % LANES for c in src_col])


def _in_proj_kernel(tile_ref, off_ref, u_ref, *refs):
    w_refs = refs[:IN_SRC_TILES]
    o_ref, wbf_ref = refs[IN_SRC_TILES:]
    j = pl.program_id(0)
    i = pl.program_id(1)

    for off in IN_LANE_OFFSETS:
        @pl.when((i == 0) & (off_ref[j] == off))
        def _(off=off):
            for k in range(IN_TN // LANES):
                a = w_refs[k][...]
                if off:
                    a = jnp.concatenate([a[:, off:], w_refs[k + 1][:, :off]], axis=1)
                wbf_ref[:, k * LANES:(k + 1) * LANES] = a.astype(BF16)

    o_ref[...] = jnp.dot(u_ref[...], wbf_ref[...], preferred_element_type=F32).astype(o_ref.dtype)


def _in_proj(u, w_in, tm=1024):
    m, k = u.shape
    tiles, offs = _in_proj_source_table()
    last_tile = (w_in.shape[1] - 1) // LANES

    def wspec(kk):
        return pl.BlockSpec((k, LANES), lambda j, i, tile, off: (0, jnp.minimum(tile[j] + kk, last_tile)))

    grid_spec = pltpu.PrefetchScalarGridSpec(
        num_scalar_prefetch=2,
        grid=(PROJ_COLS // IN_TN, m // tm),
        in_specs=[pl.BlockSpec((tm, k), lambda j, i, tile, off: (i, 0))] + [wspec(kk) for kk in range(IN_SRC_TILES)],
        out_specs=pl.BlockSpec((tm, IN_TN), lambda j, i, tile, off: (i, j)),
        scratch_shapes=[pltpu.VMEM((k, IN_TN), BF16)],
    )
    return pl.pallas_call(
        _in_proj_kernel,
        name="in_proj",
        grid_spec=grid_spec,
        out_shape=jax.ShapeDtypeStruct((m, PROJ_COLS), BF16),
        compiler_params=_cparams(("arbitrary", "arbitrary")),
    )(jnp.asarray(tiles, I32), jnp.asarray(offs, I32), u, *([w_in] * IN_SRC_TILES))


HALO = 2 * SUBLANES


def _ssd_kernel(z_ref, xbc_ref, halo_ref, small_ref, cw_ref, cb_ref, dtb_ref, aneg_ref, dexp_ref,
                nw_ref, expand_ref, shift_ref, y_ref, state_ref, ydiag_ref):
    c = pl.program_id(1)
    l = SSD_CHUNK
    n = SSD_STATE

    @pl.when(c == 0)
    def _():
        state_ref[...] = jnp.zeros_like(state_ref)

    cur = xbc_ref[...]
    halo = halo_ref[...]
    halo = jnp.where(c == 0, jnp.zeros_like(halo), halo)
    ext = jnp.concatenate([halo, cur], axis=0)
    shifted = jnp.dot(shift_ref[...], ext, preferred_element_type=F32)
    conv = cb_ref[...] + cw_ref[CONV_WIDTH - 1:CONV_WIDTH, :] * cur.astype(F32)
    for j in range(CONV_WIDTH - 1):
        conv = conv + cw_ref[j:j + 1, :] * shifted[j * l:(j + 1) * l]
    xbc = _silu(conv)
    xs = xbc[:, :SSD_DIM]
    bm = xbc[:, SSD_DIM:SSD_DIM + SSD_GROUPS * n]
    cm = xbc[:, SSD_DIM + SSD_GROUPS * n:]

    h3 = SMALL_DT_COPIES * SSD_HEADS
    dt3 = _softplus(small_ref[:, SMALL_DT:SMALL_DT + h3] + dtb_ref[...])
    adt3 = dt3 * aneg_ref[...]
    row = lax.broadcasted_iota(I32, (l, l), 0)
    col = lax.broadcasted_iota(I32, (l, l), 1)
    causal = col <= row
    tril = jnp.where(causal, 1.0, 0.0).astype(BF16)
    a_cs3 = sum(jnp.dot(tril, piece, preferred_element_type=F32) for piece in _split3(adt3))
    a_cs = a_cs3[:, :SSD_HEADS]
    a_cs_t = jnp.transpose(a_cs)
    a_last3 = a_cs3[l - 1:l, :]

    lane3 = lax.broadcasted_iota(I32, (l, h3), 1)

    def pieces_by_lane_group(x3):
        hi, mid, lo = _split3(x3)
        return jnp.where(lane3 < SSD_HEADS, hi, jnp.where(lane3 < 2 * SSD_HEADS, mid, lo))

    lhs3 = jnp.concatenate([pieces_by_lane_group(dt3),
                            pieces_by_lane_group(jnp.exp(a_cs3)),
                            pieces_by_lane_group(jnp.exp(a_last3 - a_cs3))], axis=0)
    expanded = jnp.dot(lhs3, expand_ref[...], preferred_element_type=F32)
    dt_x = expanded[0:l]
    decay_in = expanded[l:2 * l]
    decay_out = expanded[2 * l:3 * l]
    chunk_decay = decay_in[l - 1:l, :]
    x_dt = xs * dt_x
    x_dt_b = x_dt.astype(BF16)
    xd_b = (x_dt * decay_out).astype(BF16)

    lane = lax.broadcasted_iota(I32, (l, LANES), 1)
    lo_mask = lane < SSD_HEAD_DIM
    heads_per_group = SSD_HEADS // SSD_GROUPS
    gw = heads_per_group * SSD_HEAD_DIM

    for g in range(SSD_GROUPS):
        bg = bm[:, g * n:(g + 1) * n].astype(BF16)
        cg = cm[:, g * n:(g + 1) * n].astype(BF16)
        cb = lax.dot_general(cg, bg, (((1,), (1,)), ((), ())), preferred_element_type=F32)
        for pair in range(heads_per_group // 2):
            h0 = g * heads_per_group + 2 * pair
            lane0 = h0 * SSD_HEAD_DIM
            ms = []
            for h in (h0, h0 + 1):
                seg = a_cs[:, h:h + 1] - a_cs_t[h:h + 1, :]
                lmat = jnp.exp(jnp.where(causal, seg, -jnp.inf))
                ms.append((cb * lmat).astype(BF16))
            lhs = jnp.concatenate(ms, axis=1)
            xp = x_dt_b[:, lane0:lane0 + LANES]
            zero = jnp.zeros_like(xp)
            rhs = jnp.concatenate([jnp.where(lo_mask, xp, zero), jnp.where(lo_mask, zero, xp)], axis=0)
            ydiag_ref[:, lane0:lane0 + LANES] = jnp.dot(lhs, rhs, preferred_element_type=F32)
        st = state_ref[:, g * gw:(g + 1) * gw]
        y_off = jnp.dot(cg, st.astype(BF16), preferred_element_type=F32)
        ydiag_ref[:, g * gw:(g + 1) * gw] += y_off * decay_in[:, g * gw:(g + 1) * gw]
        new = lax.dot_general(bg, xd_b[:, g * gw:(g + 1) * gw], (((0,), (0,)), ((), ())),
                              preferred_element_type=F32)
        state_ref[:, g * gw:(g + 1) * gw] = st * chunk_decay[:, g * gw:(g + 1) * gw] + new

    y = ydiag_ref[...] + dexp_ref[...] * xs
    y = y * _silu(z_ref[...].astype(F32))
    ms = jnp.mean(y * y, axis=-1, keepdims=True)
    y_ref[...] = (y * lax.rsqrt(ms + EPS) * nw_ref[...]).astype(y_ref.dtype)


def _ssd(proj, small, conv_w, conv_b, dt_bias, a_log, d_skip, norm_w, bsz, seq):
    l = SSD_CHUNK
    nc = seq // l
    t = bsz * seq
    h3 = SMALL_DT_COPIES * SSD_HEADS
    aneg = jnp.tile(-jnp.exp(a_log.astype(F32)), SMALL_DT_COPIES).reshape(1, h3)
    dtb = jnp.tile(dt_bias.astype(F32), SMALL_DT_COPIES).reshape(1, h3)
    dexp = jnp.repeat(d_skip.astype(F32), SSD_HEAD_DIM).reshape(1, SSD_DIM)
    expand = jnp.tile(jnp.repeat(jnp.eye(SSD_HEADS, dtype=BF16), SSD_HEAD_DIM, axis=1),
                      (SMALL_DT_COPIES, 1))
    out_row = jnp.arange((CONV_WIDTH - 1) * l)
    src_row = HALO + out_row % l - (CONV_WIDTH - 1) + out_row // l
    shift = (jnp.arange(HALO + l)[None, :] == src_row[:, None]).astype(BF16)
    xbc_blk = COL_XBC // CONV_DIM
    halo_per_chunk = l // HALO

    def row_map(b, c):
        return b * nc + c

    return pl.pallas_call(
        _ssd_kernel,
        name="ssd",
        grid=(bsz, nc),
        in_specs=[
            pl.BlockSpec((l, SSD_DIM), lambda b, c: (row_map(b, c), COL_Z // SSD_DIM)),
            pl.BlockSpec((l, CONV_DIM), lambda b, c: (row_map(b, c), xbc_blk)),
            pl.BlockSpec((HALO, CONV_DIM),
                         lambda b, c: (jnp.maximum(row_map(b, c) * halo_per_chunk - 1, 0), xbc_blk)),
            pl.BlockSpec((l, LANES), lambda b, c: (row_map(b, c), 0)),
            pl.BlockSpec((CONV_WIDTH, CONV_DIM), lambda b, c: (0, 0)),
            pl.BlockSpec((1, CONV_DIM), lambda b, c: (0, 0)),
            pl.BlockSpec((1, h3), lambda b, c: (0, 0)),
            pl.BlockSpec((1, h3), lambda b, c: (0, 0)),
            pl.BlockSpec((1, SSD_DIM), lambda b, c: (0, 0)),
            pl.BlockSpec((1, SSD_DIM), lambda b, c: (0, 0)),
            pl.BlockSpec((h3, SSD_DIM), lambda b, c: (0, 0)),
            pl.BlockSpec(((CONV_WIDTH - 1) * l, HALO + l), lambda b, c: (0, 0)),
        ],
        out_specs=pl.BlockSpec((l, SSD_DIM), lambda b, c: (row_map(b, c), 0)),
        out_shape=jax.ShapeDtypeStruct((t, SSD_DIM), BF16),
        scratch_shapes=[pltpu.VMEM((SSD_STATE, SSD_DIM), F32),
                        pltpu.VMEM((l, SSD_DIM), F32)],
        compiler_params=_cparams(("parallel", "arbitrary")),
    )(proj, proj, proj, small, conv_w.astype(F32), conv_b.astype(F32).reshape(1, CONV_DIM),
      dtb, aneg, dexp, norm_w.astype(F32).reshape(1, SSD_DIM), expand, shift)


CUM_ROWS = 256


def _cum_kernel(small_ref, fb_ref, cum_ref, carry_ref):
    j = pl.program_id(1)

    @pl.when(j == 0)
    def _():
        carry_ref[...] = jnp.zeros_like(carry_ref)

    lf = _log_sigmoid(small_ref[...] + fb_ref[...])
    row = lax.broadcasted_iota(I32, (CUM_ROWS, CUM_ROWS), 0)
    col = lax.broadcasted_iota(I32, (CUM_ROWS, CUM_ROWS), 1)
    tril = jnp.where(col <= row, 1.0, 0.0).astype(F32)
    cs = jnp.dot(tril, lf, preferred_element_type=F32, precision=lax.Precision.HIGHEST) + carry_ref[...]
    cum_ref[...] = cs * LOG2E
    carry_ref[...] = cs[CUM_ROWS - 1:CUM_ROWS, :]


def _fox_cum(small, fox_f_bias, bsz, seq):
    fb = jnp.zeros((1, LANES), F32).at[0, SMALL_F:SMALL_F + FOX_HEADS].set(fox_f_bias.astype(F32))
    nj = seq // CUM_ROWS
    return pl.pallas_call(
        _cum_kernel,
        name="fox_cum",
        grid=(bsz, nj),
        in_specs=[pl.BlockSpec((CUM_ROWS, LANES), lambda b, j: (b * nj + j, 0)),
                  pl.BlockSpec((1, LANES), lambda b, j: (0, 0))],
        out_specs=pl.BlockSpec((CUM_ROWS, LANES), lambda b, j: (b * nj + j, 0)),
        out_shape=jax.ShapeDtypeStruct((bsz * seq, LANES), F32),
        scratch_shapes=[pltpu.VMEM((1, LANES), F32)],
        compiler_params=_cparams(("parallel", "arbitrary")),
    )(small, fb)


FOX_HEADS_PER_STEP = 2
LOG2E = 1.4426950408889634


FOX_SLAB = 128


def _fox_kernel(q_ref, k_ref, v_ref, cq_ref, ck_ref, o_ref, sa_ref, sb_ref, p_ref, m_ref, l_ref, alpha_ref,
                cqrep_ref, acc_ref, *, tq, tk):
    hp = pl.program_id(1)
    qi = pl.program_id(2)
    d = FOX_HEAD_DIM
    c2 = LOG2E / math.sqrt(d)
    lane = lax.broadcasted_iota(I32, (tq, LANES), 1)
    cq_all = cq_ref[...]

    qs, cqs = [], []
    for hh in range(FOX_HEADS_PER_STEP):
        qs.append((q_ref[:, hh * d:(hh + 1) * d].astype(F32) * c2).astype(BF16))
        head_lane = SMALL_F + hp * FOX_HEADS_PER_STEP + hh
        cq_col = jnp.sum(jnp.where(lane == head_lane, cq_all, 0.0), axis=-1, keepdims=True)
        cqs.append(jnp.broadcast_to(cq_col, (tq, LANES)))

    s_slots = (sa_ref, sb_ref)

    def scores(ki, slot):
        row0 = pl.multiple_of(ki * tk, tk)
        for hh in range(FOX_HEADS_PER_STEP):
            k = k_ref[pl.ds(row0, tk), hh * d:(hh + 1) * d]
            s = lax.dot_general(qs[hh], k, (((1,), (1,)), ((), ())), preferred_element_type=F32)
            s_slots[slot][hh] = s - ck_ref[hh, pl.ds(ki, 1), :]

    def update(ki, slot, masked):
        row0 = pl.multiple_of(ki * tk, tk)
        s_ref = s_slots[slot]
        n_ct = tk // LANES
        for hh in range(FOX_HEADS_PER_STEP):
            for rc in range(tq // FOX_SLAB):
                rows = slice(rc * FOX_SLAB, (rc + 1) * FOX_SLAB)

                def slab(ct):
                    x = s_ref[hh, rows, ct * LANES:(ct + 1) * LANES]
                    if masked:
                        row = lax.broadcasted_iota(I32, (FOX_SLAB, LANES), 0) + (qi * tq + rc * FOX_SLAB)
                        col = lax.broadcasted_iota(I32, (FOX_SLAB, LANES), 1) + (ki * tk + ct * LANES)
                        x = jnp.where(col <= row, x, -jnp.inf)
                    return x

                tmax = slab(0)
                for ct in range(1, n_ct):
                    tmax = jnp.maximum(tmax, slab(ct))
                row_max = jnp.max(tmax, axis=-1, keepdims=True)
                cq = cqrep_ref[hh, rows, :]
                m_prev = m_ref[hh, rows, :]
                m_new = jnp.maximum(m_prev, jnp.broadcast_to(row_max, (FOX_SLAB, LANES)) + cq)
                r = m_new - cq
                alpha = jnp.exp2(m_prev - m_new)
                psum = None
                for ct in range(n_ct):
                    p = jnp.exp2(slab(ct) - r)
                    psum = p if psum is None else psum + p
                    p_ref[hh, rows, ct * LANES:(ct + 1) * LANES] = p.astype(BF16)
                l_ref[hh, rows, :] = alpha * l_ref[hh, rows, :] + psum
                m_ref[hh, rows, :] = m_new
                alpha_ref[hh, rows, :] = alpha
            v = v_ref[pl.ds(row0, tk), hh * d:(hh + 1) * d]
            acc_ref[hh] = alpha_ref[hh] * acc_ref[hh] + jnp.dot(p_ref[hh], v, preferred_element_type=F32)

    for hh in range(FOX_HEADS_PER_STEP):
        cqrep_ref[hh] = cqs[hh]

    m_ref[...] = jnp.full_like(m_ref, -jnp.inf)
    l_ref[...] = jnp.zeros_like(l_ref)
    acc_ref[...] = jnp.zeros_like(acc_ref)

    n_full = qi
    scores(0, 0)

    def pair(j, carry):
        b0 = 2 * j
        scores(b0 + 1, 1)
        update(b0, 0, False)
        scores(b0 + 2, 0)
        update(b0 + 1, 1, False)
        return carry

    lax.fori_loop(0, n_full // 2, pair, 0)

    @pl.when(n_full % 2 == 0)
    def _():
        update(n_full, 0, True)

    @pl.when(n_full % 2 == 1)
    def _():
        scores(n_full, 1)
        update(n_full - 1, 0, False)
        update(n_full, 1, True)

    for hh in range(FOX_HEADS_PER_STEP):
        l_fin = jnp.sum(l_ref[hh], axis=-1, keepdims=True)
        o_ref[:, hh * d:(hh + 1) * d] = (acc_ref[hh] / l_fin).astype(o_ref.dtype)


def _fox(proj, cum, bsz, seq, tq=512):
    tk = tq
    nq = seq // tq
    nk = seq // tk
    hps = FOX_HEADS_PER_STEP
    cum_row = cum[:, SMALL_F:SMALL_F + FOX_HEADS].reshape(bsz, seq, FOX_HEADS).transpose(0, 2, 1)
    cum_row = cum_row.reshape(bsz, FOX_HEADS, nk, tk)
    t = bsz * seq
    w = FOX_HEADS_PER_STEP * FOX_HEAD_DIM
    n_hp = FOX_HEADS // FOX_HEADS_PER_STEP
    kern = functools.partial(_fox_kernel, tq=tq, tk=tk)
    return pl.pallas_call(
        kern,
        name="fox",
        grid=(bsz, n_hp, nq),
        in_specs=[
            pl.BlockSpec((tq, w), lambda b, hp, qi: (b * nq + qi, COL_Q // w + hp)),
            pl.BlockSpec((seq, w), lambda b, hp, qi: (b, COL_K // w + hp)),
            pl.BlockSpec((seq, w), lambda b, hp, qi: (b, COL_V // w + hp)),
            pl.BlockSpec((tq, LANES), lambda b, hp, qi: (b * nq + qi, 0)),
            pl.BlockSpec((None, FOX_HEADS_PER_STEP, nk, tk), lambda b, hp, qi: (b, hp, 0, 0)),
        ],
        out_specs=pl.BlockSpec((tq, w), lambda b, hp, qi: (b * nq + qi, hp)),
        out_shape=jax.ShapeDtypeStruct((t, FOX_DIM), BF16),
        scratch_shapes=[pltpu.VMEM((hps, tq, tk), F32), pltpu.VMEM((hps, tq, tk), F32),
                        pltpu.VMEM((hps, tq, tk), BF16),
                        pltpu.VMEM((hps, tq, LANES), F32), pltpu.VMEM((hps, tq, LANES), F32),
                        pltpu.VMEM((hps, tq, LANES), F32), pltpu.VMEM((hps, tq, LANES), F32),
                        pltpu.VMEM((hps, tq, FOX_HEAD_DIM), F32)],
        compiler_params=_cparams(("parallel", "parallel", "arbitrary")),
    )(proj, proj, proj, cum, cum_row)


def _mix_kernel(ya_ref, yb_ref, ga_ref, gb_ref, wa_ref, wb_ref, o_ref):
    pa = jnp.dot(ya_ref[...], wa_ref[...], preferred_element_type=F32)
    pb = jnp.dot(yb_ref[...], wb_ref[...], preferred_element_type=F32)
    ga = 1.0 / (1.0 + jnp.exp(-ga_ref[...].astype(F32)))
    gb = 1.0 / (1.0 + jnp.exp(-gb_ref[...].astype(F32)))
    o_ref[...] = (ga * pa + gb * pb).astype(o_ref.dtype)


def _mix(y_a, y_b, proj, w_a, w_b, tm=512, tn=1024):
    t = y_a.shape[0]
    return pl.pallas_call(
        _mix_kernel,
        name="mix",
        grid=(D_MODEL // tn, t // tm),
        in_specs=[
            pl.BlockSpec((tm, SSD_DIM), lambda j, i: (i, 0)),
            pl.BlockSpec((tm, FOX_DIM), lambda j, i: (i, 0)),
            pl.BlockSpec((tm, tn), lambda j, i: (i, COL_GA // tn + j)),
            pl.BlockSpec((tm, tn), lambda j, i: (i, COL_GB // tn + j)),
            pl.BlockSpec((SSD_DIM, tn), lambda j, i: (0, j)),
            pl.BlockSpec((FOX_DIM, tn), lambda j, i: (0, j)),
        ],
        out_specs=pl.BlockSpec((tm, tn), lambda j, i: (i, j)),
        out_shape=jax.ShapeDtypeStruct((t, D_MODEL), BF16),
        compiler_params=_cparams(("parallel", "parallel")),
    )(y_a, y_b, proj, proj, w_a, w_b)


def _outproj_kernel(m_ref, x_ref, wo_ref, nw_ref, wr_ref, br_ref, h_ref, u_ref, eidx_ref, wts_ref):
    h1 = x_ref[...] + jnp.dot(m_ref[...], wo_ref[...], preferred_element_type=F32)
    h_ref[...] = h1
    ms = jnp.mean(h1 * h1, axis=-1, keepdims=True)
    u2 = h1 * lax.rsqrt(ms + EPS) * nw_ref[...]
    u_ref[...] = _pack_bf16_pair(u2)

    u_hi = u2.astype(BF16)
    u_lo = (u2 - u_hi.astype(F32)).astype(BF16)
    hh_hl = jnp.dot(u_hi, wr_ref[...], preferred_element_type=F32)
    lh = jnp.dot(u_lo, wr_ref[:, :LANES], preferred_element_type=F32)
    logits = hh_hl[:, :LANES] + (hh_hl[:, LANES:] + lh) + br_ref[...]
    tm = logits.shape[0]
    lane = lax.broadcasted_iota(I32, (tm, LANES), 1)
    neg = -jnp.inf
    big = jnp.int32(2 * LANES)
    gl = jnp.where(lane < N_GROUPS, logits, neg)
    gmax = jnp.max(gl, axis=-1, keepdims=True)
    gsum = jnp.sum(jnp.exp(gl - gmax), axis=-1, keepdims=True)
    g_p = 1.0 / gsum
    g_idx = jnp.min(jnp.where(gl == gmax, lane, big), axis=-1, keepdims=True)
    e_of_lane = lane - N_GROUPS
    in_grp = (e_of_lane >= g_idx * EXPERTS_PER_GROUP) & (e_of_lane < (g_idx + 1) * EXPERTS_PER_GROUP)
    el = jnp.where(in_grp, logits, neg)
    m1 = jnp.max(el, axis=-1, keepdims=True)
    i1 = jnp.min(jnp.where(el == m1, lane, big), axis=-1, keepdims=True)
    el2 = jnp.where(lane == i1, neg, el)
    m2 = jnp.max(el2, axis=-1, keepdims=True)
    i2 = jnp.min(jnp.where(el2 == m2, lane, big), axis=-1, keepdims=True)
    esum = jnp.sum(jnp.exp(el - m1), axis=-1, keepdims=True)
    p1 = 1.0 / esum
    p2 = jnp.exp(m2 - m1) / esum
    w1 = g_p * (p1 / (p1 + p2))
    w2 = g_p * (p2 / (p1 + p2))
    eidx_ref[...] = jnp.where(lane == 0, i1 - N_GROUPS, jnp.where(lane == 1, i2 - N_GROUPS, 0))
    wts_ref[...] = jnp.where(lane == 0, w1, jnp.where(lane == 1, w2, 0.0))


def _outproj(mixed, x2, w_o, nw, w_router, b_router, tm=512):
    t = mixed.shape[0]
    resident = pl.Buffered(1)
    return pl.pallas_call(
        _outproj_kernel,
        name="outproj",
        grid=(t // tm,),
        in_specs=[
            pl.BlockSpec((tm, D_MODEL), lambda i: (i, 0)),
            pl.BlockSpec((tm, D_MODEL), lambda i: (i, 0)),
            pl.BlockSpec((D_MODEL, D_MODEL), lambda i: (0, 0), pipeline_mode=resident),
            pl.BlockSpec((1, D_MODEL), lambda i: (0, 0)),
            pl.BlockSpec((D_MODEL, 2 * LANES), lambda i: (0, 0), pipeline_mode=resident),
            pl.BlockSpec((1, LANES), lambda i: (0, 0)),
        ],
        out_specs=[
            pl.BlockSpec((tm, D_MODEL), lambda i: (i, 0)),
            pl.BlockSpec((tm, HALF), lambda i: (i, 0)),
            pl.BlockSpec((tm, LANES), lambda i: (i, 0)),
            pl.BlockSpec((tm, LANES), lambda i: (i, 0)),
        ],
        out_shape=[
            jax.ShapeDtypeStruct((t, D_MODEL), F32),
            jax.ShapeDtypeStruct((t, HALF), U32),
            jax.ShapeDtypeStruct((t, LANES), I32),
            jax.ShapeDtypeStruct((t, LANES), F32),
        ],
        compiler_params=_cparams(("parallel",)),
    )(mixed, x2, w_o, nw, w_router, b_router)


RANK_BLOCK = 512


def _rank_kernel(e_ref, rank_ref, cnt_ref, carry_ref):
    i = pl.program_id(0)
    r = RANK_BLOCK

    @pl.when(i == 0)
    def _():
        carry_ref[...] = jnp.zeros_like(carry_ref)

    e = e_ref[0]
    expert = lax.broadcasted_iota(I32, (N_EXPERTS, r), 0)
    onehot = jnp.where(expert == e, 1.0, 0.0).astype(F32)
    jrow = lax.broadcasted_iota(I32, (r, r), 0)
    jcol = lax.broadcasted_iota(I32, (r, r), 1)
    before = jnp.where(jrow < jcol, 1.0, 0.0).astype(BF16)
    cum = jnp.dot(onehot.astype(BF16), before, preferred_element_type=F32)
    carry = carry_ref[...]
    rank = jnp.sum(onehot * (cum + carry[:, 0:1]), axis=0, keepdims=True)
    rank_ref[0] = rank.astype(I32)
    carry = carry + jnp.sum(onehot, axis=1, keepdims=True)
    carry_ref[...] = carry
    cnt_ref[...] = carry


def _rank(e_blocks):
    nb = e_blocks.shape[0]
    return pl.pallas_call(
        _rank_kernel,
        name="rank",
        grid=(nb,),
        in_specs=[pl.BlockSpec((1, 1, RANK_BLOCK), lambda i: (i, 0, 0))],
        out_specs=[pl.BlockSpec((1, 1, RANK_BLOCK), lambda i: (i, 0, 0)),
                   pl.BlockSpec((N_EXPERTS, LANES), lambda i: (0, 0))],
        out_shape=[jax.ShapeDtypeStruct((nb, 1, RANK_BLOCK), I32),
                   jax.ShapeDtypeStruct((N_EXPERTS, LANES), F32)],
        scratch_shapes=[pltpu.VMEM((N_EXPERTS, LANES), F32)],
        compiler_params=_cparams(("arbitrary",)),
    )(e_blocks)


def _dest_kernel(e_ref, rank_ref, pstart_ref, dest_ref):
    e = e_ref[0]
    expert = lax.broadcasted_iota(I32, (N_EXPERTS, RANK_BLOCK), 0)
    start = jnp.sum(jnp.where(expert == e, pstart_ref[:, 0:1], 0), axis=0, keepdims=True)
    dest_ref[0] = rank_ref[0] + start


def _dest(e_blocks, rank, pstart):
    nb = e_blocks.shape[0]
    return pl.pallas_call(
        _dest_kernel,
        name="dest",
        grid=(nb,),
        in_specs=[pl.BlockSpec((1, 1, RANK_BLOCK), lambda i: (i, 0, 0)),
                  pl.BlockSpec((1, 1, RANK_BLOCK), lambda i: (i, 0, 0)),
                  pl.BlockSpec((N_EXPERTS, LANES), lambda i: (0, 0))],
        out_specs=pl.BlockSpec((1, 1, RANK_BLOCK), lambda i: (i, 0, 0)),
        out_shape=jax.ShapeDtypeStruct((nb, 1, RANK_BLOCK), I32),
        compiler_params=_cparams(("parallel",)),
    )(e_blocks, rank, pstart)


def _row_copy(src_ref, src_row, dst_ref, dst_row, sem):
    return pltpu.make_async_copy(src_ref.at[pl.ds(src_row, 1)], dst_ref.at[pl.ds(dst_row, 1)], sem)


DMA_UNROLL = 16


def _dispatch_kernel(dest_ref, u_ref, zeros_ref, xs_ref, sem):
    del zeros_ref

    def issue(g, carry):
        tok = g * (DMA_UNROLL // TOP_K)
        c0 = g * DMA_UNROLL
        for j in range(DMA_UNROLL):
            _row_copy(u_ref, tok + j // TOP_K, xs_ref, dest_ref[0, 0, c0 + j], sem).start()
        return carry

    lax.fori_loop(0, RANK_BLOCK // DMA_UNROLL, issue, 0)
    pltpu.make_async_copy(xs_ref.at[pl.ds(0, RANK_BLOCK)], xs_ref.at[pl.ds(0, RANK_BLOCK)], sem).wait()


def _dispatch(dest, u2p, n_rows):
    nb = dest.shape[0]
    zeros = jnp.zeros((n_rows, HALF), U32)
    return pl.pallas_call(
        _dispatch_kernel,
        name="dispatch",
        grid=(nb,),
        in_specs=[pl.BlockSpec((1, 1, RANK_BLOCK), lambda i: (i, 0, 0), memory_space=pltpu.SMEM),
                  pl.BlockSpec((RANK_BLOCK // TOP_K, HALF), lambda i: (i, 0)),
                  pl.BlockSpec(memory_space=pl.ANY)],
        out_specs=pl.BlockSpec(memory_space=pl.ANY),
        out_shape=jax.ShapeDtypeStruct((n_rows, HALF), U32),
        scratch_shapes=[pltpu.SemaphoreType.DMA(())],
        input_output_aliases={2: 0},
        compiler_params=_cparams(("arbitrary",)),
    )(dest, u2p, zeros)


ROW_DMA_PRIORITY = 1


def _experts_kernel(bstart_ref, nblk_ref, xs_ref, wg_ref, wu_ref, wd_ref, y_ref,
                    wgb_ref, wub_ref, wdb_ref, xbuf_ref, ybuf_ref, xsem, ysem):
    e = pl.program_id(0)
    n_e = pl.num_programs(0)
    g0 = bstart_ref[e]
    nb = nblk_ref[e]
    n_used = bstart_ref[n_e - 1] + nblk_ref[n_e - 1]
    n_total = y_ref.shape[0] // ROW_BLOCK

    def x_copy(g, slot):
        return pltpu.make_async_copy(xs_ref.at[pl.ds(g * ROW_BLOCK, ROW_BLOCK)], xbuf_ref.at[slot], xsem.at[slot])

    def y_copy(g, slot):
        return pltpu.make_async_copy(ybuf_ref.at[slot], y_ref.at[pl.ds(g * ROW_BLOCK, ROW_BLOCK)], ysem.at[slot])

    @pl.when((e == 0) & (n_used > 0))
    def _():
        x_copy(0, 0).start(priority=ROW_DMA_PRIORITY)

    @pl.when(nb > 0)
    def _():
        wgb_ref[...] = wg_ref[0].astype(BF16)
        wub_ref[...] = wu_ref[0].astype(BF16)
        wdb_ref[...] = wd_ref[0].astype(BF16)

    def block(j, carry):
        g = g0 + j
        slot = g % 2
        x_copy(g, slot).wait()

        @pl.when(g + 1 < n_used)
        def _():
            x_copy(g + 1, 1 - slot).start(priority=ROW_DMA_PRIORITY)

        @pl.when(g >= 2)
        def _():
            y_copy(g - 2, slot).wait()

        lo, hi = _unpack_bf16_pair(xbuf_ref[slot])
        lo = lo.astype(BF16)
        hi = hi.astype(BF16)
        gate = (jnp.dot(lo, wgb_ref[:HALF, :], preferred_element_type=F32)
                + jnp.dot(hi, wgb_ref[HALF:, :], preferred_element_type=F32))
        up = (jnp.dot(lo, wub_ref[:HALF, :], preferred_element_type=F32)
              + jnp.dot(hi, wub_ref[HALF:, :], preferred_element_type=F32))
        hdn = (_silu(gate) * up).astype(BF16)
        y = jnp.dot(hdn, wdb_ref[...], preferred_element_type=F32)
        ybuf_ref[slot] = _pack_bf16_pair(y)
        y_copy(g, slot).start(priority=ROW_DMA_PRIORITY)
        return carry

    lax.fori_loop(0, nb, block, 0)

    @pl.when(e == n_e - 1)
    def _():
        @pl.when(n_used >= 2)
        def _():
            y_copy(n_used - 2, n_used % 2).wait()

        @pl.when(n_used >= 1)
        def _():
            y_copy(n_used - 1, (n_used - 1) % 2).wait()

        ybuf_ref[0] = jnp.zeros((ROW_BLOCK, HALF), U32)

        def fill(g, carry):
            y_copy(g, 0).start()
            y_copy(g, 0).wait()
            return carry

        lax.fori_loop(n_used, n_total, fill, 0)


def _experts(block_start, block_count, xs, w_gate, w_up, w_down):
    n_rows = xs.shape[0]

    def wmap(e, bs, bc):
        return (e, 0, 0)

    grid_spec = pltpu.PrefetchScalarGridSpec(
        num_scalar_prefetch=2,
        grid=(N_EXPERTS,),
        in_specs=[pl.BlockSpec(memory_space=pl.ANY),
                  pl.BlockSpec((1, D_MODEL, D_EXPERT), wmap),
                  pl.BlockSpec((1, D_MODEL, D_EXPERT), wmap),
                  pl.BlockSpec((1, D_EXPERT, D_MODEL), wmap)],
        out_specs=pl.BlockSpec(memory_space=pl.ANY),
        scratch_shapes=[pltpu.VMEM((D_MODEL, D_EXPERT), BF16),
                        pltpu.VMEM((D_MODEL, D_EXPERT), BF16),
                        pltpu.VMEM((D_EXPERT, D_MODEL), BF16),
                        pltpu.VMEM((2, ROW_BLOCK, HALF), U32),
                        pltpu.VMEM((2, ROW_BLOCK, HALF), U32),
                        pltpu.SemaphoreType.DMA((2,)),
                        pltpu.SemaphoreType.DMA((2,))],
    )
    return pl.pallas_call(
        _experts_kernel,
        name="experts",
        grid_spec=grid_spec,
        out_shape=jax.ShapeDtypeStruct((n_rows, HALF), U32),
        compiler_params=_cparams(("arbitrary",)),
    )(block_start, block_count, xs, w_gate, w_up, w_down)


COMBINE_TOKENS = RANK_BLOCK // TOP_K


def _combine_kernel(dest_ref, dest_next_ref, h_ref, wts_ref, nw_ref, y_ref, o_ref, buf_ref, sem):
    ts = COMBINE_TOKENS
    i = pl.program_id(0)
    slot = i % 2

    def gather(idx_ref, dst_slot):
        def issue(g, carry):
            tok = g * (DMA_UNROLL // TOP_K)
            c0 = g * DMA_UNROLL
            for j in range(DMA_UNROLL):
                _row_copy(y_ref, idx_ref[0, 0, c0 + j], buf_ref.at[dst_slot],
                          (j % TOP_K) * ts + tok + j // TOP_K, sem.at[dst_slot]).start()
            return carry

        lax.fori_loop(0, RANK_BLOCK // DMA_UNROLL, issue, 0)

    @pl.when(i == 0)
    def _():
        gather(dest_ref, 0)

    @pl.when(i + 1 < pl.num_programs(0))
    def _():
        gather(dest_next_ref, 1 - slot)

    pltpu.make_async_copy(y_ref.at[pl.ds(0, RANK_BLOCK)], buf_ref.at[slot], sem.at[slot]).wait()

    w = wts_ref[...]
    w0 = w[:, 0:1]
    w1 = w[:, 1:2]
    lo0, hi0 = _unpack_bf16_pair(buf_ref[slot, 0:ts, :])
    lo1, hi1 = _unpack_bf16_pair(buf_ref[slot, ts:2 * ts, :])
    h = h_ref[...]
    out_lo = h[:, :HALF] + w0 * lo0 + w1 * lo1
    out_hi = h[:, HALF:] + w0 * hi0 + w1 * hi1
    ms = (jnp.sum(out_lo * out_lo, axis=-1, keepdims=True)
          + jnp.sum(out_hi * out_hi, axis=-1, keepdims=True)) * (1.0 / D_MODEL)
    inv = lax.rsqrt(ms + EPS)
    o_ref[:, :HALF] = out_lo * inv * nw_ref[:, :HALF]
    o_ref[:, HALF:] = out_hi * inv * nw_ref[:, HALF:]


def _combine(dest, h1, wts, nw, y):
    t = h1.shape[0]
    ts = COMBINE_TOKENS
    n_steps = t // ts
    return pl.pallas_call(
        _combine_kernel,
        name="combine",
        grid=(n_steps,),
        in_specs=[pl.BlockSpec((1, 1, RANK_BLOCK), lambda i: (i, 0, 0), memory_space=pltpu.SMEM),
                  pl.BlockSpec((1, 1, RANK_BLOCK), lambda i: (jnp.minimum(i + 1, n_steps - 1), 0, 0),
                               memory_space=pltpu.SMEM),
                  pl.BlockSpec((ts, D_MODEL), lambda i: (i, 0)),
                  pl.BlockSpec((ts, LANES), lambda i: (i, 0)),
                  pl.BlockSpec((1, D_MODEL), lambda i: (0, 0)),
                  pl.BlockSpec(memory_space=pl.ANY)],
        out_specs=pl.BlockSpec((ts, D_MODEL), lambda i: (i, 0)),
        out_shape=jax.ShapeDtypeStruct((t, D_MODEL), F32),
        scratch_shapes=[pltpu.VMEM((2, RANK_BLOCK, HALF), U32), pltpu.SemaphoreType.DMA((2,))],
        compiler_params=_cparams(("arbitrary",)),
    )(dest, dest, h1, wts, nw, y)


def _narrow_w_in(w_in):
    dt0 = SSD_DIM + CONV_DIM
    f0 = dt0 + SSD_HEADS + 3 * FOX_DIM
    dt = w_in[:, dt0:dt0 + SSD_HEADS]
    f = w_in[:, f0:f0 + FOX_HEADS]
    pad = jnp.zeros((w_in.shape[0], LANES - SMALL_F - FOX_HEADS), w_in.dtype)
    return jnp.concatenate([dt] * SMALL_DT_COPIES + [f, pad], axis=1).astype(BF16)


def _layer(h, p, bsz, seq):
    t = bsz * seq
    w_in = p["w_in"].astype(F32)
    u, small = _norm_small(h, p["norm_mix_w"].astype(F32).reshape(1, D_MODEL), _narrow_w_in(w_in))
    proj = _in_proj(u, w_in)
    y_ssd = _ssd(proj, small, p["conv_w"], p["conv_b"], p["dt_bias"], p["a_log"], p["d_skip"],
                 p["ssd_norm_w"], bsz, seq)
    cum = _fox_cum(small, p["fox_f_bias"], bsz, seq)
    y_fox = _fox(proj, cum, bsz, seq)
    mixed = _mix(y_ssd, y_fox, proj, p["w_proj_ssd"].astype(BF16), p["w_proj_fox"].astype(BF16))

    w_router = jnp.concatenate(
        [p["w_router_group"], p["w_router_expert"],
         jnp.zeros((D_MODEL, LANES - N_GROUPS - N_EXPERTS), F32)], axis=1).astype(F32)
    w_router_hi = w_router.astype(BF16)
    w_router_lo = (w_router - w_router_hi.astype(F32)).astype(BF16)
    w_router = jnp.concatenate([w_router_hi, w_router_lo], axis=1)
    b_router = jnp.concatenate(
        [p["b_router_group"], p["b_router_expert"],
         jnp.zeros((LANES - N_GROUPS - N_EXPERTS,), F32)]).astype(F32).reshape(1, LANES)
    h1, u2p, eidx, wts = _outproj(mixed, h, p["w_out"].astype(BF16),
                                  p["norm_moe_w"].astype(F32).reshape(1, D_MODEL), w_router, b_router)

    tk = t * TOP_K
    e_blocks = eidx[:, :TOP_K].reshape(tk // RANK_BLOCK, 1, RANK_BLOCK)
    rank, counts = _rank(e_blocks)
    counts = counts[:, 0].astype(I32)
    padded = ((counts + ROW_BLOCK - 1) // ROW_BLOCK) * ROW_BLOCK
    pend = jnp.cumsum(padded)
    pstart = pend - padded
    n_blocks = tk // ROW_BLOCK + N_EXPERTS
    dest = _dest(e_blocks, rank, jnp.broadcast_to(pstart[:, None], (N_EXPERTS, LANES)).astype(I32))
    xs = _dispatch(dest, u2p, n_blocks * ROW_BLOCK)
    y = _experts((pstart // ROW_BLOCK).astype(I32), (padded // ROW_BLOCK).astype(I32), xs,
                 p["w_gate_exp"], p["w_up_exp"], p["w_down_exp"])
    return dest, h1, wts, y


def kernel(x, norm_mix_w, w_in, conv_w, conv_b, dt_bias, a_log, d_skip, ssd_norm_w, fox_f_bias, w_proj_ssd,
           w_proj_fox, w_out, norm_moe_w, w_router_group, b_router_group, w_router_expert, b_router_expert,
           w_gate_exp, w_up_exp, w_down_exp, norm_final_w):
    bsz, seq, _ = x.shape
    depth = w_in.shape[0]
    assert depth == 1, "the fused final norm assumes a single layer"
    stacked = dict(norm_mix_w=norm_mix_w, w_in=w_in, conv_w=conv_w, conv_b=conv_b, dt_bias=dt_bias, a_log=a_log,
                   d_skip=d_skip, ssd_norm_w=ssd_norm_w, fox_f_bias=fox_f_bias, w_proj_ssd=w_proj_ssd,
                   w_proj_fox=w_proj_fox, w_out=w_out, norm_moe_w=norm_moe_w, w_router_group=w_router_group,
                   b_router_group=b_router_group, w_router_expert=w_router_expert,
                   b_router_expert=b_router_expert, w_gate_exp=w_gate_exp, w_up_exp=w_up_exp,
                   w_down_exp=w_down_exp)
    p = {name: v[0] for name, v in stacked.items()}
    h = x.reshape(bsz * seq, D_MODEL)
    dest, h1, wts, y = _layer(h, p, bsz, seq)
    out = _combine(dest, h1, wts, norm_final_w.astype(F32).reshape(1, D_MODEL), y)
    return out.reshape(bsz, seq, D_MODEL)
```

```python
import functools
import math

import jax
import jax.numpy as jnp
from jax import lax
from jax.experimental import pallas as pl
from jax.experimental.pallas import tpu as pltpu

F32 = jnp.float32
BF16 = jnp.bfloat16
I32 = jnp.int32
U32 = jnp.uint32

D_MODEL = 2048
SSD_HEADS = 32
SSD_HEAD_DIM = 64
SSD_DIM = SSD_HEADS * SSD_HEAD_DIM
SSD_GROUPS = 4
SSD_STATE = 128
SSD_CHUNK = 128
CONV_WIDTH = 4
CONV_DIM = SSD_DIM + 2 * SSD_GROUPS * SSD_STATE
FOX_HEADS = 16
FOX_HEAD_DIM = 128
FOX_DIM = FOX_HEADS * FOX_HEAD_DIM
N_GROUPS = 8
EXPERTS_PER_GROUP = 8
N_EXPERTS = N_GROUPS * EXPERTS_PER_GROUP
TOP_K = 2
D_EXPERT = 512
EPS = 1e-6

LANES = 128
SUBLANES = 8
VMEM_LIMIT = 52 * 1024 * 1024

COL_Z = 0
COL_Q = COL_Z + SSD_DIM
COL_K = COL_Q + FOX_DIM
COL_V = COL_K + FOX_DIM
COL_GA = COL_V + FOX_DIM
COL_GB = COL_GA + D_MODEL
COL_XBC = COL_GB + D_MODEL
PROJ_COLS = COL_XBC + CONV_DIM
SMALL_DT = 0
SMALL_DT_COPIES = 3
SMALL_F = SMALL_DT_COPIES * SSD_HEADS

ROW_BLOCK = 128
HALF = D_MODEL // 2


def _cparams(sem, vmem=VMEM_LIMIT):
    return pltpu.CompilerParams(dimension_semantics=sem, vmem_limit_bytes=vmem)


def _silu(x):
    return x * (1.0 / (1.0 + jnp.exp(-x)))


def _softplus(x):
    return jnp.maximum(x, 0.0) + jnp.log(1.0 + jnp.exp(-jnp.abs(x)))


def _log_sigmoid(x):
    return -_softplus(-x)


def _split3(x):
    hi = x.astype(BF16)
    rest = x - hi.astype(F32)
    mid = rest.astype(BF16)
    lo = (rest - mid.astype(F32)).astype(BF16)
    return hi, mid, lo


def _pack_bf16_pair(x):
    n = x.shape[1] // 2
    lo = pltpu.bitcast(x[:, :n].astype(BF16).astype(F32), U32)
    hi = pltpu.bitcast(x[:, n:].astype(BF16).astype(F32), U32)
    return (hi & jnp.uint32(0xFFFF0000)) | (lo >> 16)


def _unpack_bf16_pair(p):
    lo = pltpu.bitcast(p << 16, F32)
    hi = pltpu.bitcast(p & jnp.uint32(0xFFFF0000), F32)
    return lo, hi


def _norm_small_kernel(x_ref, nw_ref, ws_ref, u_ref, s_ref):
    x = x_ref[...]
    ms = jnp.mean(x * x, axis=-1, keepdims=True)
    u = (x * lax.rsqrt(ms + EPS) * nw_ref[...]).astype(BF16)
    u_ref[...] = u
    s_ref[...] = jnp.dot(u, ws_ref[...], preferred_element_type=F32)


def _norm_small(x2, nw, w_small, tm=512):
    t = x2.shape[0]
    return pl.pallas_call(
        _norm_small_kernel,
        name="norm_small",
        grid=(t // tm,),
        in_specs=[pl.BlockSpec((tm, D_MODEL), lambda i: (i, 0)),
                  pl.BlockSpec((1, D_MODEL), lambda i: (0, 0)),
                  pl.BlockSpec((D_MODEL, LANES), lambda i: (0, 0))],
        out_specs=[pl.BlockSpec((tm, D_MODEL), lambda i: (i, 0)),
                   pl.BlockSpec((tm, LANES), lambda i: (i, 0))],
        out_shape=[jax.ShapeDtypeStruct((t, D_MODEL), BF16),
                   jax.ShapeDtypeStruct((t, LANES), F32)],
        compiler_params=_cparams(("parallel",)),
    )(x2, nw, w_small)


IN_TN = 1024
IN_SRC_TILES = IN_TN // LANES + 1
_IN_SEGMENTS = ((COL_Z, 0, SSD_DIM),
                (COL_Q, SSD_DIM + CONV_DIM + SSD_HEADS, 3 * FOX_DIM),
                (COL_GA, SSD_DIM + CONV_DIM + SSD_HEADS + 3 * FOX_DIM + FOX_HEADS, 2 * D_MODEL),
                (COL_XBC, SSD_DIM, CONV_DIM))
IN_LANE_OFFSETS = tuple(sorted({src % LANES for _, src, _ in _IN_SEGMENTS}))


def _in_proj_source_table():
    src_col = [0] * (PROJ_COLS // IN_TN)
    for out0, src0, width in _IN_SEGMENTS:
        for c in range(0, width, IN_TN):
            src_col[(out0 + c) // IN_TN] = src0 + c
    return ([c // LANES for c in src_col], [c % LANES for c in src_col])


def _in_proj_kernel(tile_ref, off_ref, u_ref, *refs):
    w_refs = refs[:IN_SRC_TILES]
    o_ref, wbf_ref = refs[IN_SRC_TILES:]
    j = pl.program_id(0)
    i = pl.program_id(1)

    for off in IN_LANE_OFFSETS:
        @pl.when((i == 0) & (off_ref[j] == off))
        def _(off=off):
            for k in range(IN_TN // LANES):
                a = w_refs[k][...]
                if off:
                    a = jnp.concatenate([a[:, off:], w_refs[k + 1][:, :off]], axis=1)
                wbf_ref[:, k * LANES:(k + 1) * LANES] = a.astype(BF16)

    o_ref[...] = jnp.dot(u_ref[...], wbf_ref[...], preferred_element_type=F32).astype(o_ref.dtype)


def _in_proj(u, w_in, tm=1024):
    m, k = u.shape
    tiles, offs = _in_proj_source_table()
    last_tile = (w_in.shape[1] - 1) // LANES

    def wspec(kk):
        return pl.BlockSpec((k, LANES), lambda j, i, tile, off: (0, jnp.minimum(tile[j] + kk, last_tile)))

    grid_spec = pltpu.PrefetchScalarGridSpec(
        num_scalar_prefetch=2,
        grid=(PROJ_COLS // IN_TN, m // tm),
        in_specs=[pl.BlockSpec((tm, k), lambda j, i, tile, off: (i, 0))] + [wspec(kk) for kk in range(IN_SRC_TILES)],
        out_specs=pl.BlockSpec((tm, IN_TN), lambda j, i, tile, off: (i, j)),
        scratch_shapes=[pltpu.VMEM((k, IN_TN), BF16)],
    )
    return pl.pallas_call(
        _in_proj_kernel,
        name="in_proj",
        grid_spec=grid_spec,
        out_shape=jax.ShapeDtypeStruct((m, PROJ_COLS), BF16),
        compiler_params=_cparams(("arbitrary", "arbitrary")),
    )(jnp.asarray(tiles, I32), jnp.asarray(offs, I32), u, *([w_in] * IN_SRC_TILES))


HALO = 2 * SUBLANES


def _ssd_kernel(z_ref, xbc_ref, halo_ref, small_ref, cw_ref, cb_ref, dtb_ref, aneg_ref, dexp_ref,
                nw_ref, expand_ref, shift_ref, y_ref, state_ref, ydiag_ref):
    c = pl.program_id(1)
    l = SSD_CHUNK
    n = SSD_STATE

    @pl.when(c == 0)
    def _():
        state_ref[...] = jnp.zeros_like(state_ref)

    cur = xbc_ref[...]
    halo = halo_ref[...]
    halo = jnp.where(c == 0, jnp.zeros_like(halo), halo)
    ext = jnp.concatenate([halo, cur], axis=0)
    shifted = jnp.dot(shift_ref[...], ext, preferred_element_type=F32)
    conv = cb_ref[...] + cw_ref[CONV_WIDTH - 1:CONV_WIDTH, :] * cur.astype(F32)
    for j in range(CONV_WIDTH - 1):
        conv = conv + cw_ref[j:j + 1, :] * shifted[j * l:(j + 1) * l]
    xbc = _silu(conv)
    xs = xbc[:, :SSD_DIM]
    bm = xbc[:, SSD_DIM:SSD_DIM + SSD_GROUPS * n]
    cm = xbc[:, SSD_DIM + SSD_GROUPS * n:]

    h3 = SMALL_DT_COPIES * SSD_HEADS
    dt3 = _softplus(small_ref[:, SMALL_DT:SMALL_DT + h3] + dtb_ref[...])
    adt3 = dt3 * aneg_ref[...]
    row = lax.broadcasted_iota(I32, (l, l), 0)
    col = lax.broadcasted_iota(I32, (l, l), 1)
    causal = col <= row
    tril = jnp.where(causal, 1.0, 0.0).astype(BF16)
    a_cs3 = sum(jnp.dot(tril, piece, preferred_element_type=F32) for piece in _split3(adt3))
    a_cs = a_cs3[:, :SSD_HEADS]
    a_cs_t = jnp.transpose(a_cs)
    a_last3 = a_cs3[l - 1:l, :]

    lane3 = lax.broadcasted_iota(I32, (l, h3), 1)

    def pieces_by_lane_group(x3):
        hi, mid, lo = _split3(x3)
        return jnp.where(lane3 < SSD_HEADS, hi, jnp.where(lane3 < 2 * SSD_HEADS, mid, lo))

    lhs3 = jnp.concatenate([pieces_by_lane_group(dt3),
                            pieces_by_lane_group(jnp.exp(a_cs3)),
                            pieces_by_lane_group(jnp.exp(a_last3 - a_cs3))], axis=0)
    expanded = jnp.dot(lhs3, expand_ref[...], preferred_element_type=F32)
    dt_x = expanded[0:l]
    decay_in = expanded[l:2 * l]
    decay_out = expanded[2 * l:3 * l]
    chunk_decay = decay_in[l - 1:l, :]
    x_dt = xs * dt_x
    x_dt_b = x_dt.astype(BF16)
    xd_b = (x_dt * decay_out).astype(BF16)

    lane = lax.broadcasted_iota(I32, (l, LANES), 1)
    lo_mask = lane < SSD_HEAD_DIM
    heads_per_group = SSD_HEADS // SSD_GROUPS
    gw = heads_per_group * SSD_HEAD_DIM

    for g in range(SSD_GROUPS):
        bg = bm[:, g * n:(g + 1) * n].astype(BF16)
        cg = cm[:, g * n:(g + 1) * n].astype(BF16)
        cb = lax.dot_general(cg, bg, (((1,), (1,)), ((), ())), preferred_element_type=F32)
        for pair in range(heads_per_group // 2):
            h0 = g * heads_per_group + 2 * pair
            lane0 = h0 * SSD_HEAD_DIM
            ms = []
            for h in (h0, h0 + 1):
                seg = a_cs[:, h:h + 1] - a_cs_t[h:h + 1, :]
                lmat = jnp.exp(jnp.where(causal, seg, -jnp.inf))
                ms.append((cb * lmat).astype(BF16))
            lhs = jnp.concatenate(ms, axis=1)
            xp = x_dt_b[:, lane0:lane0 + LANES]
            zero = jnp.zeros_like(xp)
            rhs = jnp.concatenate([jnp.where(lo_mask, xp, zero), jnp.where(lo_mask, zero, xp)], axis=0)
            ydiag_ref[:, lane0:lane0 + LANES] = jnp.dot(lhs, rhs, preferred_element_type=F32)
        st = state_ref[:, g * gw:(g + 1) * gw]
        y_off = jnp.dot(cg, st.astype(BF16), preferred_element_type=F32)
        ydiag_ref[:, g * gw:(g + 1) * gw] += y_off * decay_in[:, g * gw:(g + 1) * gw]
        new = lax.dot_general(bg, xd_b[:, g * gw:(g + 1) * gw], (((0,), (0,)), ((), ())),
                              preferred_element_type=F32)
        state_ref[:, g * gw:(g + 1) * gw] = st * chunk_decay[:, g * gw:(g + 1) * gw] + new

    y = ydiag_ref[...] + dexp_ref[...] * xs
    y = y * _silu(z_ref[...].astype(F32))
    ms = jnp.mean(y * y, axis=-1, keepdims=True)
    y_ref[...] = (y * lax.rsqrt(ms + EPS) * nw_ref[...]).astype(y_ref.dtype)


def _ssd(proj, small, conv_w, conv_b, dt_bias, a_log, d_skip, norm_w, bsz, seq):
    l = SSD_CHUNK
    nc = seq // l
    t = bsz * seq
    h3 = SMALL_DT_COPIES * SSD_HEADS
    aneg = jnp.tile(-jnp.exp(a_log.astype(F32)), SMALL_DT_COPIES).reshape(1, h3)
    dtb = jnp.tile(dt_bias.astype(F32), SMALL_DT_COPIES).reshape(1, h3)
    dexp = jnp.repeat(d_skip.astype(F32), SSD_HEAD_DIM).reshape(1, SSD_DIM)
    expand = jnp.tile(jnp.repeat(jnp.eye(SSD_HEADS, dtype=BF16), SSD_HEAD_DIM, axis=1),
                      (SMALL_DT_COPIES, 1))
    out_row = jnp.arange((CONV_WIDTH - 1) * l)
    src_row = HALO + out_row % l - (CONV_WIDTH - 1) + out_row // l
    shift = (jnp.arange(HALO + l)[None, :] == src_row[:, None]).astype(BF16)
    xbc_blk = COL_XBC // CONV_DIM
    halo_per_chunk = l // HALO

    def row_map(b, c):
        return b * nc + c

    return pl.pallas_call(
        _ssd_kernel,
        name="ssd",
        grid=(bsz, nc),
        in_specs=[
            pl.BlockSpec((l, SSD_DIM), lambda b, c: (row_map(b, c), COL_Z // SSD_DIM)),
            pl.BlockSpec((l, CONV_DIM), lambda b, c: (row_map(b, c), xbc_blk)),
            pl.BlockSpec((HALO, CONV_DIM),
                         lambda b, c: (jnp.maximum(row_map(b, c) * halo_per_chunk - 1, 0), xbc_blk)),
            pl.BlockSpec((l, LANES), lambda b, c: (row_map(b, c), 0)),
            pl.BlockSpec((CONV_WIDTH, CONV_DIM), lambda b, c: (0, 0)),
            pl.BlockSpec((1, CONV_DIM), lambda b, c: (0, 0)),
            pl.BlockSpec((1, h3), lambda b, c: (0, 0)),
            pl.BlockSpec((1, h3), lambda b, c: (0, 0)),
            pl.BlockSpec((1, SSD_DIM), lambda b, c: (0, 0)),
            pl.BlockSpec((1, SSD_DIM), lambda b, c: (0, 0)),
            pl.BlockSpec((h3, SSD_DIM), lambda b, c: (0, 0)),
            pl.BlockSpec(((CONV_WIDTH - 1) * l, HALO + l), lambda b, c: (0, 0)),
        ],
        out_specs=pl.BlockSpec((l, SSD_DIM), lambda b, c: (row_map(b, c), 0)),
        out_shape=jax.ShapeDtypeStruct((t, SSD_DIM), BF16),
        scratch_shapes=[pltpu.VMEM((SSD_STATE, SSD_DIM), F32),
                        pltpu.VMEM((l, SSD_DIM), F32)],
        compiler_params=_cparams(("parallel", "arbitrary")),
    )(proj, proj, proj, small, conv_w.astype(F32), conv_b.astype(F32).reshape(1, CONV_DIM),
      dtb, aneg, dexp, norm_w.astype(F32).reshape(1, SSD_DIM), expand, shift)


CUM_ROWS = 256


def _cum_kernel(small_ref, fb_ref, cum_ref, carry_ref):
    j = pl.program_id(1)

    @pl.when(j == 0)
    def _():
        carry_ref[...] = jnp.zeros_like(carry_ref)

    lf = _log_sigmoid(small_ref[...] + fb_ref[...])
    row = lax.broadcasted_iota(I32, (CUM_ROWS, CUM_ROWS), 0)
    col = lax.broadcasted_iota(I32, (CUM_ROWS, CUM_ROWS), 1)
    tril = jnp.where(col <= row, 1.0, 0.0).astype(F32)
    cs = jnp.dot(tril, lf, preferred_element_type=F32, precision=lax.Precision.HIGHEST) + carry_ref[...]
    cum_ref[...] = cs * LOG2E
    carry_ref[...] = cs[CUM_ROWS - 1:CUM_ROWS, :]


def _fox_cum(small, fox_f_bias, bsz, seq):
    fb = jnp.zeros((1, LANES), F32).at[0, SMALL_F:SMALL_F + FOX_HEADS].set(fox_f_bias.astype(F32))
    nj = seq // CUM_ROWS
    return pl.pallas_call(
        _cum_kernel,
        name="fox_cum",
        grid=(bsz, nj),
        in_specs=[pl.BlockSpec((CUM_ROWS, LANES), lambda b, j: (b * nj + j, 0)),
                  pl.BlockSpec((1, LANES), lambda b, j: (0, 0))],
        out_specs=pl.BlockSpec((CUM_ROWS, LANES), lambda b, j: (b * nj + j, 0)),
        out_shape=jax.ShapeDtypeStruct((bsz * seq, LANES), F32),
        scratch_shapes=[pltpu.VMEM((1, LANES), F32)],
        compiler_params=_cparams(("parallel", "arbitrary")),
    )(small, fb)


FOX_HEADS_PER_STEP = 2
LOG2E = 1.4426950408889634


FOX_SLAB = 128


def _fox_kernel(q_ref, k_ref, v_ref, cq_ref, ck_ref, o_ref, sa_ref, sb_ref, p_ref, m_ref, l_ref, alpha_ref,
                cqrep_ref, acc_ref, *, tq, tk):
    hp = pl.program_id(1)
    qi = pl.program_id(2)
    d = FOX_HEAD_DIM
    c2 = LOG2E / math.sqrt(d)
    lane = lax.broadcasted_iota(I32, (tq, LANES), 1)
    cq_all = cq_ref[...]

    qs, cqs = [], []
    for hh in range(FOX_HEADS_PER_STEP):
        qs.append((q_ref[:, hh * d:(hh + 1) * d].astype(F32) * c2).astype(BF16))
        head_lane = SMALL_F + hp * FOX_HEADS_PER_STEP + hh
        cq_col = jnp.sum(jnp.where(lane == head_lane, cq_all, 0.0), axis=-1, keepdims=True)
        cqs.append(jnp.broadcast_to(cq_col, (tq, LANES)))

    s_slots = (sa_ref, sb_ref)

    def scores(ki, slot):
        row0 = pl.multiple_of(ki * tk, tk)
        for hh in range(FOX_HEADS_PER_STEP):
            k = k_ref[pl.ds(row0, tk), hh * d:(hh + 1) * d]
            s = lax.dot_general(qs[hh], k, (((1,), (1,)), ((), ())), preferred_element_type=F32)
            s_slots[slot][hh] = s - ck_ref[hh, pl.ds(ki, 1), :]

    def update(ki, slot, masked):
        row0 = pl.multiple_of(ki * tk, tk)
        s_ref = s_slots[slot]
        n_ct = tk // LANES
        for hh in range(FOX_HEADS_PER_STEP):
            for rc in range(tq // FOX_SLAB):
                rows = slice(rc * FOX_SLAB, (rc + 1) * FOX_SLAB)

                def slab(ct):
                    x = s_ref[hh, rows, ct * LANES:(ct + 1) * LANES]
                    if masked:
                        row = lax.broadcasted_iota(I32, (FOX_SLAB, LANES), 0) + (qi * tq + rc * FOX_SLAB)
                        col = lax.broadcasted_iota(I32, (FOX_SLAB, LANES), 1) + (ki * tk + ct * LANES)
                        x = jnp.where(col <= row, x, -jnp.inf)
                    return x

                tmax = slab(0)
                for ct in range(1, n_ct):
                    tmax = jnp.maximum(tmax, slab(ct))
                row_max = jnp.max(tmax, axis=-1, keepdims=True)
                cq = cqrep_ref[hh, rows, :]
                m_prev = m_ref[hh, rows, :]
                m_new = jnp.maximum(m_prev, jnp.broadcast_to(row_max, (FOX_SLAB, LANES)) + cq)
                r = m_new - cq
                alpha = jnp.exp2(m_prev - m_new)
                psum = None
                for ct in range(n_ct):
                    p = jnp.exp2(slab(ct) - r)
                    psum = p if psum is None else psum + p
                    p_ref[hh, rows, ct * LANES:(ct + 1) * LANES] = p.astype(BF16)
                l_ref[hh, rows, :] = alpha * l_ref[hh, rows, :] + psum
                m_ref[hh, rows, :] = m_new
                alpha_ref[hh, rows, :] = alpha
            v = v_ref[pl.ds(row0, tk), hh * d:(hh + 1) * d]
            acc_ref[hh] = alpha_ref[hh] * acc_ref[hh] + jnp.dot(p_ref[hh], v, preferred_element_type=F32)

    for hh in range(FOX_HEADS_PER_STEP):
        cqrep_ref[hh] = cqs[hh]

    m_ref[...] = jnp.full_like(m_ref, -jnp.inf)
    l_ref[...] = jnp.zeros_like(l_ref)
    acc_ref[...] = jnp.zeros_like(acc_ref)

    n_full = qi
    scores(0, 0)

    def pair(j, carry):
        b0 = 2 * j
        scores(b0 + 1, 1)
        update(b0, 0, False)
        scores(b0 + 2, 0)
        update(b0 + 1, 1, False)
        return carry

    lax.fori_loop(0, n_full // 2, pair, 0)

    @pl.when(n_full % 2 == 0)
    def _():
        update(n_full, 0, True)

    @pl.when(n_full % 2 == 1)
    def _():
        scores(n_full, 1)
        update(n_full - 1, 0, False)
        update(n_full, 1, True)

    for hh in range(FOX_HEADS_PER_STEP):
        l_fin = jnp.sum(l_ref[hh], axis=-1, keepdims=True)
        o_ref[:, hh * d:(hh + 1) * d] = (acc_ref[hh] / l_fin).astype(o_ref.dtype)


def _fox(proj, cum, bsz, seq, tq=512):
    tk = tq
    nq = seq // tq
    nk = seq // tk
    hps = FOX_HEADS_PER_STEP
    cum_row = cum[:, SMALL_F:SMALL_F + FOX_HEADS].reshape(bsz, seq, FOX_HEADS).transpose(0, 2, 1)
    cum_row = cum_row.reshape(bsz, FOX_HEADS, nk, tk)
    t = bsz * seq
    w = FOX_HEADS_PER_STEP * FOX_HEAD_DIM
    n_hp = FOX_HEADS // FOX_HEADS_PER_STEP
    kern = functools.partial(_fox_kernel, tq=tq, tk=tk)
    return pl.pallas_call(
        kern,
        name="fox",
        grid=(bsz, n_hp, nq),
        in_specs=[
            pl.BlockSpec((tq, w), lambda b, hp, qi: (b * nq + qi, COL_Q // w + hp)),
            pl.BlockSpec((seq, w), lambda b, hp, qi: (b, COL_K // w + hp)),
            pl.BlockSpec((seq, w), lambda b, hp, qi: (b, COL_V // w + hp)),
            pl.BlockSpec((tq, LANES), lambda b, hp, qi: (b * nq + qi, 0)),
            pl.BlockSpec((None, FOX_HEADS_PER_STEP, nk, tk), lambda b, hp, qi: (b, hp, 0, 0)),
        ],
        out_specs=pl.BlockSpec((tq, w), lambda b, hp, qi: (b * nq + qi, hp)),
        out_shape=jax.ShapeDtypeStruct((t, FOX_DIM), BF16),
        scratch_shapes=[pltpu.VMEM((hps, tq, tk), F32), pltpu.VMEM((hps, tq, tk), F32),
                        pltpu.VMEM((hps, tq, tk), BF16),
                        pltpu.VMEM((hps, tq, LANES), F32), pltpu.VMEM((hps, tq, LANES), F32),
                        pltpu.VMEM((hps, tq, LANES), F32), pltpu.VMEM((hps, tq, LANES), F32),
                        pltpu.VMEM((hps, tq, FOX_HEAD_DIM), F32)],
        compiler_params=_cparams(("parallel", "parallel", "arbitrary")),
    )(proj, proj, proj, cum, cum_row)


def _mix_kernel(ya_ref, yb_ref, ga_ref, gb_ref, wa_ref, wb_ref, o_ref):
    pa = jnp.dot(ya_ref[...], wa_ref[...], preferred_element_type=F32)
    pb = jnp.dot(yb_ref[...], wb_ref[...], preferred_element_type=F32)
    ga = 1.0 / (1.0 + jnp.exp(-ga_ref[...].astype(F32)))
    gb = 1.0 / (1.0 + jnp.exp(-gb_ref[...].astype(F32)))
    o_ref[...] = (ga * pa + gb * pb).astype(o_ref.dtype)


def _mix(y_a, y_b, proj, w_a, w_b, tm=512, tn=1024):
    t = y_a.shape[0]
    return pl.pallas_call(
        _mix_kernel,
        name="mix",
        grid=(D_MODEL // tn, t // tm),
        in_specs=[
            pl.BlockSpec((tm, SSD_DIM), lambda j, i: (i, 0)),
            pl.BlockSpec((tm, FOX_DIM), lambda j, i: (i, 0)),
            pl.BlockSpec((tm, tn), lambda j, i: (i, COL_GA // tn + j)),
            pl.BlockSpec((tm, tn), lambda j, i: (i, COL_GB // tn + j)),
            pl.BlockSpec((SSD_DIM, tn), lambda j, i: (0, j)),
            pl.BlockSpec((FOX_DIM, tn), lambda j, i: (0, j)),
        ],
        out_specs=pl.BlockSpec((tm, tn), lambda j, i: (i, j)),
        out_shape=jax.ShapeDtypeStruct((t, D_MODEL), BF16),
        compiler_params=_cparams(("parallel", "parallel")),
    )(y_a, y_b, proj, proj, w_a, w_b)


def _outproj_kernel(m_ref, x_ref, wo_ref, nw_ref, wr_ref, br_ref, h_ref, u_ref, eidx_ref, wts_ref):
    h1 = x_ref[...] + jnp.dot(m_ref[...], wo_ref[...], preferred_element_type=F32)
    h_ref[...] = h1
    ms = jnp.mean(h1 * h1, axis=-1, keepdims=True)
    u2 = h1 * lax.rsqrt(ms + EPS) * nw_ref[...]
    u_ref[...] = _pack_bf16_pair(u2)

    u_hi = u2.astype(BF16)
    u_lo = (u2 - u_hi.astype(F32)).astype(BF16)
    hh_hl = jnp.dot(u_hi, wr_ref[...], preferred_element_type=F32)
    lh = jnp.dot(u_lo, wr_ref[:, :LANES], preferred_element_type=F32)
    logits = hh_hl[:, :LANES] + (hh_hl[:, LANES:] + lh) + br_ref[...]
    tm = logits.shape[0]
    lane = lax.broadcasted_iota(I32, (tm, LANES), 1)
    neg = -jnp.inf
    big = jnp.int32(2 * LANES)
    gl = jnp.where(lane < N_GROUPS, logits, neg)
    gmax = jnp.max(gl, axis=-1, keepdims=True)
    gsum = jnp.sum(jnp.exp(gl - gmax), axis=-1, keepdims=True)
    g_p = 1.0 / gsum
    g_idx = jnp.min(jnp.where(gl == gmax, lane, big), axis=-1, keepdims=True)
    e_of_lane = lane - N_GROUPS
    in_grp = (e_of_lane >= g_idx * EXPERTS_PER_GROUP) & (e_of_lane < (g_idx + 1) * EXPERTS_PER_GROUP)
    el = jnp.where(in_grp, logits, neg)
    m1 = jnp.max(el, axis=-1, keepdims=True)
    i1 = jnp.min(jnp.where(el == m1, lane, big), axis=-1, keepdims=True)
    el2 = jnp.where(lane == i1, neg, el)
    m2 = jnp.max(el2, axis=-1, keepdims=True)
    i2 = jnp.min(jnp.where(el2 == m2, lane, big), axis=-1, keepdims=True)
    esum = jnp.sum(jnp.exp(el - m1), axis=-1, keepdims=True)
    p1 = 1.0 / esum
    p2 = jnp.exp(m2 - m1) / esum
    w1 = g_p * (p1 / (p1 + p2))
    w2 = g_p * (p2 / (p1 + p2))
    eidx_ref[...] = jnp.where(lane == 0, i1 - N_GROUPS, jnp.where(lane == 1, i2 - N_GROUPS, 0))
    wts_ref[...] = jnp.where(lane == 0, w1, jnp.where(lane == 1, w2, 0.0))


def _outproj(mixed, x2, w_o, nw, w_router, b_router, tm=512):
    t = mixed.shape[0]
    resident = pl.Buffered(1)
    return pl.pallas_call(
        _outproj_kernel,
        name="outproj",
        grid=(t // tm,),
        in_specs=[
            pl.BlockSpec((tm, D_MODEL), lambda i: (i, 0)),
            pl.BlockSpec((tm, D_MODEL), lambda i: (i, 0)),
            pl.BlockSpec((D_MODEL, D_MODEL), lambda i: (0, 0), pipeline_mode=resident),
            pl.BlockSpec((1, D_MODEL), lambda i: (0, 0)),
            pl.BlockSpec((D_MODEL, 2 * LANES), lambda i: (0, 0), pipeline_mode=resident),
            pl.BlockSpec((1, LANES), lambda i: (0, 0)),
        ],
        out_specs=[
            pl.BlockSpec((tm, D_MODEL), lambda i: (i, 0)),
            pl.BlockSpec((tm, HALF), lambda i: (i, 0)),
            pl.BlockSpec((tm, LANES), lambda i: (i, 0)),
            pl.BlockSpec((tm, LANES), lambda i: (i, 0)),
        ],
        out_shape=[
            jax.ShapeDtypeStruct((t, D_MODEL), F32),
            jax.ShapeDtypeStruct((t, HALF), U32),
            jax.ShapeDtypeStruct((t, LANES), I32),
            jax.ShapeDtypeStruct((t, LANES), F32),
        ],
        compiler_params=_cparams(("parallel",)),
    )(mixed, x2, w_o, nw, w_router, b_router)


RANK_BLOCK = 512


def _rank_kernel(e_ref, rank_ref, cnt_ref, carry_ref):
    i = pl.program_id(0)
    r = RANK_BLOCK

    @pl.when(i == 0)
    def _():
        carry_ref[...] = jnp.zeros_like(carry_ref)

    e = e_ref[0]
    expert = lax.broadcasted_iota(I32, (N_EXPERTS, r), 0)
    onehot = jnp.where(expert == e, 1.0, 0.0).astype(F32)
    jrow = lax.broadcasted_iota(I32, (r, r), 0)
    jcol = lax.broadcasted_iota(I32, (r, r), 1)
    before = jnp.where(jrow < jcol, 1.0, 0.0).astype(BF16)
    cum = jnp.dot(onehot.astype(BF16), before, preferred_element_type=F32)
    carry = carry_ref[...]
    rank = jnp.sum(onehot * (cum + carry[:, 0:1]), axis=0, keepdims=True)
    rank_ref[0] = rank.astype(I32)
    carry = carry + jnp.sum(onehot, axis=1, keepdims=True)
    carry_ref[...] = carry
    cnt_ref[...] = carry


def _rank(e_blocks):
    nb = e_blocks.shape[0]
    return pl.pallas_call(
        _rank_kernel,
        name="rank",
        grid=(nb,),
        in_specs=[pl.BlockSpec((1, 1, RANK_BLOCK), lambda i: (i, 0, 0))],
        out_specs=[pl.BlockSpec((1, 1, RANK_BLOCK), lambda i: (i, 0, 0)),
                   pl.BlockSpec((N_EXPERTS, LANES), lambda i: (0, 0))],
        out_shape=[jax.ShapeDtypeStruct((nb, 1, RANK_BLOCK), I32),
                   jax.ShapeDtypeStruct((N_EXPERTS, LANES), F32)],
        scratch_shapes=[pltpu.VMEM((N_EXPERTS, LANES), F32)],
        compiler_params=_cparams(("arbitrary",)),
    )(e_blocks)


def _dest_kernel(e_ref, rank_ref, pstart_ref, dest_ref):
    e = e_ref[0]
    expert = lax.broadcasted_iota(I32, (N_EXPERTS, RANK_BLOCK), 0)
    start = jnp.sum(jnp.where(expert == e, pstart_ref[:, 0:1], 0), axis=0, keepdims=True)
    dest_ref[0] = rank_ref[0] + start


def _dest(e_blocks, rank, pstart):
    nb = e_blocks.shape[0]
    return pl.pallas_call(
        _dest_kernel,
        name="dest",
        grid=(nb,),
        in_specs=[pl.BlockSpec((1, 1, RANK_BLOCK), lambda i: (i, 0, 0)),
                  pl.BlockSpec((1, 1, RANK_BLOCK), lambda i: (i, 0, 0)),
                  pl.BlockSpec((N_EXPERTS, LANES), lambda i: (0, 0))],
        out_specs=pl.BlockSpec((1, 1, RANK_BLOCK), lambda i: (i, 0, 0)),
        out_shape=jax.ShapeDtypeStruct((nb, 1, RANK_BLOCK), I32),
        compiler_params=_cparams(("parallel",)),
    )(e_blocks, rank, pstart)


def _row_copy(src_ref, src_row, dst_ref, dst_row, sem):
    return pltpu.make_async_copy(src_ref.at[pl.ds(src_row, 1)], dst_ref.at[pl.ds(dst_row, 1)], sem)


DMA_UNROLL = 16


def _dispatch_kernel(dest_ref, u_ref, zeros_ref, xs_ref, sem):
    del zeros_ref

    def issue(g, carry):
        tok = g * (DMA_UNROLL // TOP_K)
        c0 = g * DMA_UNROLL
        for j in range(DMA_UNROLL):
            _row_copy(u_ref, tok + j // TOP_K, xs_ref, dest_ref[0, 0, c0 + j], sem).start()
        return carry

    lax.fori_loop(0, RANK_BLOCK // DMA_UNROLL, issue, 0)
    pltpu.make_async_copy(xs_ref.at[pl.ds(0, RANK_BLOCK)], xs_ref.at[pl.ds(0, RANK_BLOCK)], sem).wait()


def _dispatch(dest, u2p, n_rows):
    nb = dest.shape[0]
    zeros = jnp.zeros((n_rows, HALF), U32)
    return pl.pallas_call(
        _dispatch_kernel,
        name="dispatch",
        grid=(nb,),
        in_specs=[pl.BlockSpec((1, 1, RANK_BLOCK), lambda i: (i, 0, 0), memory_space=pltpu.SMEM),
                  pl.BlockSpec((RANK_BLOCK // TOP_K, HALF), lambda i: (i, 0)),
                  pl.BlockSpec(memory_space=pl.ANY)],
        out_specs=pl.BlockSpec(memory_space=pl.ANY),
        out_shape=jax.ShapeDtypeStruct((n_rows, HALF), U32),
        scratch_shapes=[pltpu.SemaphoreType.DMA(())],
        input_output_aliases={2: 0},
        compiler_params=_cparams(("arbitrary",)),
    )(dest, u2p, zeros)


ROW_DMA_PRIORITY = 1
ROW_RING = 8


def _experts_kernel(bstart_ref, nblk_ref, xs_ref, wg_ref, wu_ref, wd_ref, y_ref,
                    wgb_ref, wub_ref, wdb_ref, xbuf_ref, ybuf_ref, xsem, ysem):
    e = pl.program_id(0)
    n_e = pl.num_programs(0)
    g0 = bstart_ref[e]
    nb = nblk_ref[e]
    n_used = bstart_ref[n_e - 1] + nblk_ref[n_e - 1]
    n_total = y_ref.shape[0] // ROW_BLOCK

    def x_copy(g, slot):
        return pltpu.make_async_copy(xs_ref.at[pl.ds(g * ROW_BLOCK, ROW_BLOCK)], xbuf_ref.at[slot], xsem.at[slot])

    def y_copy(g, slot):
        return pltpu.make_async_copy(ybuf_ref.at[slot], y_ref.at[pl.ds(g * ROW_BLOCK, ROW_BLOCK)], ysem.at[slot])

    @pl.when(e == 0)
    def _():
        for g in range(ROW_RING - 1):
            @pl.when(g < n_used)
            def _(g=g):
                x_copy(g, g).start(priority=ROW_DMA_PRIORITY)

    @pl.when(nb > 0)
    def _():
        wgb_ref[...] = wg_ref[0].astype(BF16)
        wub_ref[...] = wu_ref[0].astype(BF16)
        wdb_ref[...] = wd_ref[0].astype(BF16)

    def block(j, carry):
        g = g0 + j
        slot = g % ROW_RING
        x_copy(g, slot).wait()

        @pl.when(g + ROW_RING - 1 < n_used)
        def _():
            x_copy(g + ROW_RING - 1, (g + ROW_RING - 1) % ROW_RING).start(priority=ROW_DMA_PRIORITY)

        @pl.when(g >= ROW_RING)
        def _():
            y_copy(g - ROW_RING, slot).wait()

        lo, hi = _unpack_bf16_pair(xbuf_ref[slot])
        lo = lo.astype(BF16)
        hi = hi.astype(BF16)
        gate = (jnp.dot(lo, wgb_ref[:HALF, :], preferred_element_type=F32)
                + jnp.dot(hi, wgb_ref[HALF:, :], preferred_element_type=F32))
        up = (jnp.dot(lo, wub_ref[:HALF, :], preferred_element_type=F32)
              + jnp.dot(hi, wub_ref[HALF:, :], preferred_element_type=F32))
        hdn = (_silu(gate) * up).astype(BF16)
        y = jnp.dot(hdn, wdb_ref[...], preferred_element_type=F32)
        ybuf_ref[slot] = _pack_bf16_pair(y)
        y_copy(g, slot).start(priority=ROW_DMA_PRIORITY)
        return carry

    lax.fori_loop(0, nb, block, 0)

    @pl.when(e == n_e - 1)
    def _():
        for back in range(ROW_RING, 0, -1):
            @pl.when(n_used >= back)
            def _(back=back):
                y_copy(n_used - back, (n_used - back) % ROW_RING).wait()

        ybuf_ref[0] = jnp.zeros((ROW_BLOCK, HALF), U32)

        def fill(g, carry):
            y_copy(g, 0).start()
            y_copy(g, 0).wait()
            return carry

        lax.fori_loop(n_used, n_total, fill, 0)


def _experts(block_start, block_count, xs, w_gate, w_up, w_down):
    n_rows = xs.shape[0]

    def wmap(e, bs, bc):
        return (e, 0, 0)

    grid_spec = pltpu.PrefetchScalarGridSpec(
        num_scalar_prefetch=2,
        grid=(N_EXPERTS,),
        in_specs=[pl.BlockSpec(memory_space=pl.ANY),
                  pl.BlockSpec((1, D_MODEL, D_EXPERT), wmap),
                  pl.BlockSpec((1, D_MODEL, D_EXPERT), wmap),
                  pl.BlockSpec((1, D_EXPERT, D_MODEL), wmap)],
        out_specs=pl.BlockSpec(memory_space=pl.ANY),
        scratch_shapes=[pltpu.VMEM((D_MODEL, D_EXPERT), BF16),
                        pltpu.VMEM((D_MODEL, D_EXPERT), BF16),
                        pltpu.VMEM((D_EXPERT, D_MODEL), BF16),
                        pltpu.VMEM((ROW_RING, ROW_BLOCK, HALF), U32),
                        pltpu.VMEM((ROW_RING, ROW_BLOCK, HALF), U32),
                        pltpu.SemaphoreType.DMA((ROW_RING,)),
                        pltpu.SemaphoreType.DMA((ROW_RING,))],
    )
    return pl.pallas_call(
        _experts_kernel,
        name="experts",
        grid_spec=grid_spec,
        out_shape=jax.ShapeDtypeStruct((n_rows, HALF), U32),
        compiler_params=_cparams(("arbitrary",)),
    )(block_start, block_count, xs, w_gate, w_up, w_down)


COMBINE_TOKENS = RANK_BLOCK // TOP_K


def _combine_kernel(dest_ref, dest_next_ref, h_ref, wts_ref, nw_ref, y_ref, o_ref, buf_ref, sem):
    ts = COMBINE_TOKENS
    i = pl.program_id(0)
    slot = i % 2

    def gather(idx_ref, dst_slot):
        def issue(g, carry):
            tok = g * (DMA_UNROLL // TOP_K)
            c0 = g * DMA_UNROLL
            for j in range(DMA_UNROLL):
                _row_copy(y_ref, idx_ref[0, 0, c0 + j], buf_ref.at[dst_slot],
                          (j % TOP_K) * ts + tok + j // TOP_K, sem.at[dst_slot]).start()
            return carry

        lax.fori_loop(0, RANK_BLOCK // DMA_UNROLL, issue, 0)

    @pl.when(i == 0)
    def _():
        gather(dest_ref, 0)

    @pl.when(i + 1 < pl.num_programs(0))
    def _():
        gather(dest_next_ref, 1 - slot)

    pltpu.make_async_copy(y_ref.at[pl.ds(0, RANK_BLOCK)], buf_ref.at[slot], sem.at[slot]).wait()

    w = wts_ref[...]
    w0 = w[:, 0:1]
    w1 = w[:, 1:2]
    lo0, hi0 = _unpack_bf16_pair(buf_ref[slot, 0:ts, :])
    lo1, hi1 = _unpack_bf16_pair(buf_ref[slot, ts:2 * ts, :])
    h = h_ref[...]
    out_lo = h[:, :HALF] + w0 * lo0 + w1 * lo1
    out_hi = h[:, HALF:] + w0 * hi0 + w1 * hi1
    ms = (jnp.sum(out_lo * out_lo, axis=-1, keepdims=True)
          + jnp.sum(out_hi * out_hi, axis=-1, keepdims=True)) * (1.0 / D_MODEL)
    inv = lax.rsqrt(ms + EPS)
    o_ref[:, :HALF] = out_lo * inv * nw_ref[:, :HALF]
    o_ref[:, HALF:] = out_hi * inv * nw_ref[:, HALF:]


def _combine(dest, h1, wts, nw, y):
    t = h1.shape[0]
    ts = COMBINE_TOKENS
    n_steps = t // ts
    return pl.pallas_call(
        _combine_kernel,
        name="combine",
        grid=(n_steps,),
        in_specs=[pl.BlockSpec((1, 1, RANK_BLOCK), lambda i: (i, 0, 0), memory_space=pltpu.SMEM),
                  pl.BlockSpec((1, 1, RANK_BLOCK), lambda i: (jnp.minimum(i + 1, n_steps - 1), 0, 0),
                               memory_space=pltpu.SMEM),
                  pl.BlockSpec((ts, D_MODEL), lambda i: (i, 0)),
                  pl.BlockSpec((ts, LANES), lambda i: (i, 0)),
                  pl.BlockSpec((1, D_MODEL), lambda i: (0, 0)),
                  pl.BlockSpec(memory_space=pl.ANY)],
        out_specs=pl.BlockSpec((ts, D_MODEL), lambda i: (i, 0)),
        out_shape=jax.ShapeDtypeStruct((t, D_MODEL), F32),
        scratch_shapes=[pltpu.VMEM((2, RANK_BLOCK, HALF), U32), pltpu.SemaphoreType.DMA((2,))],
        compiler_params=_cparams(("arbitrary",)),
    )(dest, dest, h1, wts, nw, y)


def _narrow_w_in(w_in):
    dt0 = SSD_DIM + CONV_DIM
    f0 = dt0 + SSD_HEADS + 3 * FOX_DIM
    dt = w_in[:, dt0:dt0 + SSD_HEADS]
    f = w_in[:, f0:f0 + FOX_HEADS]
    pad = jnp.zeros((w_in.shape[0], LANES - SMALL_F - FOX_HEADS), w_in.dtype)
    return jnp.concatenate([dt] * SMALL_DT_COPIES + [f, pad], axis=1).astype(BF16)


def _layer(h, p, bsz, seq):
    t = bsz * seq
    w_in = p["w_in"].astype(F32)
    u, small = _norm_small(h, p["norm_mix_w"].astype(F32).reshape(1, D_MODEL), _narrow_w_in(w_in))
    proj = _in_proj(u, w_in)
    y_ssd = _ssd(proj, small, p["conv_w"], p["conv_b"], p["dt_bias"], p["a_log"], p["d_skip"],
                 p["ssd_norm_w"], bsz, seq)
    cum = _fox_cum(small, p["fox_f_bias"], bsz, seq)
    y_fox = _fox(proj, cum, bsz, seq)
    mixed = _mix(y_ssd, y_fox, proj, p["w_proj_ssd"].astype(BF16), p["w_proj_fox"].astype(BF16))

    w_router = jnp.concatenate(
        [p["w_router_group"], p["w_router_expert"],
         jnp.zeros((D_MODEL, LANES - N_GROUPS - N_EXPERTS), F32)], axis=1).astype(F32)
    w_router_hi = w_router.astype(BF16)
    w_router_lo = (w_router - w_router_hi.astype(F32)).astype(BF16)
    w_router = jnp.concatenate([w_router_hi, w_router_lo], axis=1)
    b_router = jnp.concatenate(
        [p["b_router_group"], p["b_router_expert"],
         jnp.zeros((LANES - N_GROUPS - N_EXPERTS,), F32)]).astype(F32).reshape(1, LANES)
    h1, u2p, eidx, wts = _outproj(mixed, h, p["w_out"].astype(BF16),
                                  p["norm_moe_w"].astype(F32).reshape(1, D_MODEL), w_router, b_router)

    tk = t * TOP_K
    e_blocks = eidx[:, :TOP_K].reshape(tk // RANK_BLOCK, 1, RANK_BLOCK)
    rank, counts = _rank(e_blocks)
    counts = counts[:, 0].astype(I32)
    padded = ((counts + ROW_BLOCK - 1) // ROW_BLOCK) * ROW_BLOCK
    pend = jnp.cumsum(padded)
    pstart = pend - padded
    n_blocks = tk // ROW_BLOCK + N_EXPERTS
    dest = _dest(e_blocks, rank, jnp.broadcast_to(pstart[:, None], (N_EXPERTS, LANES)).astype(I32))
    xs = _dispatch(dest, u2p, n_blocks * ROW_BLOCK)
    y = _experts((pstart // ROW_BLOCK).astype(I32), (padded // ROW_BLOCK).astype(I32), xs,
                 p["w_gate_exp"], p["w_up_exp"], p["w_down_exp"])
    return dest, h1, wts, y


def kernel(x, norm_mix_w, w_in, conv_w, conv_b, dt_bias, a_log, d_skip, ssd_norm_w, fox_f_bias, w_proj_ssd,
           w_proj_fox, w_out, norm_moe_w, w_router_group, b_router_group, w_router_expert, b_router_expert,
           w_gate_exp, w_up_exp, w_down_exp, norm_final_w):
    bsz, seq, _ = x.shape
    depth = w_in.shape[0]
    assert depth == 1, "the fused final norm assumes a single layer"
    stacked = dict(norm_mix_w=norm_mix_w, w_in=w_in, conv_w=conv_w, conv_b=conv_b, dt_bias=dt_bias, a_log=a_log,
                   d_skip=d_skip, ssd_norm_w=ssd_norm_w, fox_f_bias=fox_f_bias, w_proj_ssd=w_proj_ssd,
                   w_proj_fox=w_proj_fox, w_out=w_out, norm_moe_w=norm_moe_w, w_router_group=w_router_group,
                   b_router_group=b_router_group, w_router_expert=w_router_expert,
                   b_router_expert=b_router_expert, w_gate_exp=w_gate_exp, w_up_exp=w_up_exp,
                   w_down_exp=w_down_exp)
    p = {name: v[0] for name, v in stacked.items()}
    h = x.reshape(bsz * seq, D_MODEL)
    dest, h1, wts, y = _layer(h, p, bsz, seq)
    out = _combine(dest, h1, wts, norm_final_w.astype(F32).reshape(1, D_MODEL), y)
    return out.reshape(bsz, seq, D_MODEL)
```

```python
import functools
import math

import jax
import jax.numpy as jnp
from jax import lax
from jax.experimental import pallas as pl
from jax.experimental.pallas import tpu as pltpu

F32 = jnp.float32
BF16 = jnp.bfloat16
I32 = jnp.int32
U32 = jnp.uint32

D_MODEL = 2048
SSD_HEADS = 32
SSD_HEAD_DIM = 64
SSD_DIM = SSD_HEADS * SSD_HEAD_DIM
SSD_GROUPS = 4
SSD_STATE = 128
SSD_CHUNK = 128
CONV_WIDTH = 4
CONV_DIM = SSD_DIM + 2 * SSD_GROUPS * SSD_STATE
FOX_HEADS = 16
FOX_HEAD_DIM = 128
FOX_DIM = FOX_HEADS * FOX_HEAD_DIM
N_GROUPS = 8
EXPERTS_PER_GROUP = 8
N_EXPERTS = N_GROUPS * EXPERTS_PER_GROUP
TOP_K = 2
D_EXPERT = 512
EPS = 1e-6

LANES = 128
SUBLANES = 8
VMEM_LIMIT = 52 * 1024 * 1024

COL_Z = 0
COL_Q = COL_Z + SSD_DIM
COL_K = COL_Q + FOX_DIM
COL_V = COL_K + FOX_DIM
COL_GA = COL_V + FOX_DIM
COL_GB = COL_GA + D_MODEL
COL_XBC = COL_GB + D_MODEL
PROJ_COLS = COL_XBC + CONV_DIM
SMALL_DT = 0
SMALL_DT_COPIES = 3
SMALL_F = SMALL_DT_COPIES * SSD_HEADS

ROW_BLOCK = 128
HALF = D_MODEL // 2


def _cparams(sem, vmem=VMEM_LIMIT):
    return pltpu.CompilerParams(dimension_semantics=sem, vmem_limit_bytes=vmem)


def _silu(x):
    return x * (1.0 / (1.0 + jnp.exp(-x)))


def _softplus(x):
    return jnp.maximum(x, 0.0) + jnp.log(1.0 + jnp.exp(-jnp.abs(x)))


def _log_sigmoid(x):
    return -_softplus(-x)


def _split3(x):
    hi = x.astype(BF16)
    rest = x - hi.astype(F32)
    mid = rest.astype(BF16)
    lo = (rest - mid.astype(F32)).astype(BF16)
    return hi, mid, lo


def _pack_bf16_pair(x):
    n = x.shape[1] // 2
    lo = pltpu.bitcast(x[:, :n].astype(BF16).astype(F32), U32)
    hi = pltpu.bitcast(x[:, n:].astype(BF16).astype(F32), U32)
    return (hi & jnp.uint32(0xFFFF0000)) | (lo >> 16)


def _unpack_bf16_pair(p):
    lo = pltpu.bitcast(p << 16, F32)
    hi = pltpu.bitcast(p & jnp.uint32(0xFFFF0000), F32)
    return lo, hi


def _norm_small_kernel(x_ref, nw_ref, ws_ref, u_ref, s_ref):
    x = x_ref[...]
    ms = jnp.mean(x * x, axis=-1, keepdims=True)
    u = (x * lax.rsqrt(ms + EPS) * nw_ref[...]).astype(BF16)
    u_ref[...] = u
    s_ref[...] = jnp.dot(u, ws_ref[...], preferred_element_type=F32)


def _norm_small(x2, nw, w_small, tm=512):
    t = x2.shape[0]
    return pl.pallas_call(
        _norm_small_kernel,
        name="norm_small",
        grid=(t // tm,),
        in_specs=[pl.BlockSpec((tm, D_MODEL), lambda i: (i, 0)),
                  pl.BlockSpec((1, D_MODEL), lambda i: (0, 0)),
                  pl.BlockSpec((D_MODEL, LANES), lambda i: (0, 0))],
        out_specs=[pl.BlockSpec((tm, D_MODEL), lambda i: (i, 0)),
                   pl.BlockSpec((tm, LANES), lambda i: (i, 0))],
        out_shape=[jax.ShapeDtypeStruct((t, D_MODEL), BF16),
                   jax.ShapeDtypeStruct((t, LANES), F32)],
        compiler_params=_cparams(("parallel",)),
    )(x2, nw, w_small)


IN_TN = 1024
IN_SRC_TILES = IN_TN // LANES + 1
_IN_SEGMENTS = ((COL_Z, 0, SSD_DIM),
                (COL_Q, SSD_DIM + CONV_DIM + SSD_HEADS, 3 * FOX_DIM),
                (COL_GA, SSD_DIM + CONV_DIM + SSD_HEADS + 3 * FOX_DIM + FOX_HEADS, 2 * D_MODEL),
                (COL_XBC, SSD_DIM, CONV_DIM))
IN_LANE_OFFSETS = tuple(sorted({src % LANES for _, src, _ in _IN_SEGMENTS}))


def _in_proj_source_table():
    src_col = [0] * (PROJ_COLS // IN_TN)
    for out0, src0, width in _IN_SEGMENTS:
        for c in range(0, width, IN_TN):
            src_col[(out0 + c) // IN_TN] = src0 + c
    return ([c // LANES for c in src_col], [c % LANES for c in src_col])


def _in_proj_kernel(tile_ref, off_ref, u_ref, *refs):
    w_refs = refs[:IN_SRC_TILES]
    o_ref, wbf_ref = refs[IN_SRC_TILES:]
    j = pl.program_id(0)
    i = pl.program_id(1)

    for off in IN_LANE_OFFSETS:
        @pl.when((i == 0) & (off_ref[j] == off))
        def _(off=off):
            for k in range(IN_TN // LANES):
                a = w_refs[k][...]
                if off:
                    a = jnp.concatenate([a[:, off:], w_refs[k + 1][:, :off]], axis=1)
                wbf_ref[:, k * LANES:(k + 1) * LANES] = a.astype(BF16)

    o_ref[...] = jnp.dot(u_ref[...], wbf_ref[...], preferred_element_type=F32).astype(o_ref.dtype)


def _in_proj(u, w_in, tm=1024):
    m, k = u.shape
    tiles, offs = _in_proj_source_table()
    last_tile = (w_in.shape[1] - 1) // LANES

    nj = PROJ_COLS // IN_TN

    def wspec(kk):
        def index(j, i, tile, off):
            j_read = jnp.minimum(j + jnp.minimum(i, 1), nj - 1)
            return (0, jnp.minimum(tile[j_read] + kk, last_tile))
        return pl.BlockSpec((k, LANES), index)

    grid_spec = pltpu.PrefetchScalarGridSpec(
        num_scalar_prefetch=2,
        grid=(PROJ_COLS // IN_TN, m // tm),
        in_specs=[pl.BlockSpec((tm, k), lambda j, i, tile, off: (i, 0))] + [wspec(kk) for kk in range(IN_SRC_TILES)],
        out_specs=pl.BlockSpec((tm, IN_TN), lambda j, i, tile, off: (i, j)),
        scratch_shapes=[pltpu.VMEM((k, IN_TN), BF16)],
    )
    return pl.pallas_call(
        _in_proj_kernel,
        name="in_proj",
        grid_spec=grid_spec,
        out_shape=jax.ShapeDtypeStruct((m, PROJ_COLS), BF16),
        compiler_params=_cparams(("arbitrary", "arbitrary")),
    )(jnp.asarray(tiles, I32), jnp.asarray(offs, I32), u, *([w_in] * IN_SRC_TILES))


HALO = 2 * SUBLANES


def _ssd_kernel(z_ref, xbc_ref, halo_ref, small_ref, cw_ref, cb_ref, dtb_ref, aneg_ref, dexp_ref,
                nw_ref, expand_ref, shift_ref, y_ref, state_ref, ydiag_ref):
    c = pl.program_id(1)
    l = SSD_CHUNK
    n = SSD_STATE

    @pl.when(c == 0)
    def _():
        state_ref[...] = jnp.zeros_like(state_ref)

    cur = xbc_ref[...]
    halo = halo_ref[...]
    halo = jnp.where(c == 0, jnp.zeros_like(halo), halo)
    ext = jnp.concatenate([halo, cur], axis=0)
    shifted = jnp.dot(shift_ref[...], ext, preferred_element_type=F32)
    conv = cb_ref[...] + cw_ref[CONV_WIDTH - 1:CONV_WIDTH, :] * cur.astype(F32)
    for j in range(CONV_WIDTH - 1):
        conv = conv + cw_ref[j:j + 1, :] * shifted[j * l:(j + 1) * l]
    xbc = _silu(conv)
    xs = xbc[:, :SSD_DIM]
    bm = xbc[:, SSD_DIM:SSD_DIM + SSD_GROUPS * n]
    cm = xbc[:, SSD_DIM + SSD_GROUPS * n:]

    h3 = SMALL_DT_COPIES * SSD_HEADS
    dt3 = _softplus(small_ref[:, SMALL_DT:SMALL_DT + h3] + dtb_ref[...])
    adt3 = dt3 * aneg_ref[...]
    row = lax.broadcasted_iota(I32, (l, l), 0)
    col = lax.broadcasted_iota(I32, (l, l), 1)
    causal = col <= row
    tril = jnp.where(causal, 1.0, 0.0).astype(BF16)
    a_cs3 = sum(jnp.dot(tril, piece, preferred_element_type=F32) for piece in _split3(adt3))
    a_cs = a_cs3[:, :SSD_HEADS]
    a_cs_t = jnp.transpose(a_cs)
    a_last3 = a_cs3[l - 1:l, :]

    lane3 = lax.broadcasted_iota(I32, (l, h3), 1)

    def pieces_by_lane_group(x3):
        hi, mid, lo = _split3(x3)
        return jnp.where(lane3 < SSD_HEADS, hi, jnp.where(lane3 < 2 * SSD_HEADS, mid, lo))

    lhs3 = jnp.concatenate([pieces_by_lane_group(dt3),
                            pieces_by_lane_group(jnp.exp(a_cs3)),
                            pieces_by_lane_group(jnp.exp(a_last3 - a_cs3))], axis=0)
    expanded = jnp.dot(lhs3, expand_ref[...], preferred_element_type=F32)
    dt_x = expanded[0:l]
    decay_in = expanded[l:2 * l]
    decay_out = expanded[2 * l:3 * l]
    chunk_decay = decay_in[l - 1:l, :]
    x_dt = xs * dt_x
    x_dt_b = x_dt.astype(BF16)
    xd_b = (x_dt * decay_out).astype(BF16)

    lane = lax.broadcasted_iota(I32, (l, LANES), 1)
    lo_mask = lane < SSD_HEAD_DIM
    heads_per_group = SSD_HEADS // SSD_GROUPS
    gw = heads_per_group * SSD_HEAD_DIM

    for g in range(SSD_GROUPS):
        bg = bm[:, g * n:(g + 1) * n].astype(BF16)
        cg = cm[:, g * n:(g + 1) * n].astype(BF16)
        cb = lax.dot_general(cg, bg, (((1,), (1,)), ((), ())), preferred_element_type=F32)
        for pair in range(heads_per_group // 2):
            h0 = g * heads_per_group + 2 * pair
            lane0 = h0 * SSD_HEAD_DIM
            ms = []
            for h in (h0, h0 + 1):
                seg = a_cs[:, h:h + 1] - a_cs_t[h:h + 1, :]
                lmat = jnp.exp(jnp.where(causal, seg, -jnp.inf))
                ms.append((cb * lmat).astype(BF16))
            lhs = jnp.concatenate(ms, axis=1)
            xp = x_dt_b[:, lane0:lane0 + LANES]
            zero = jnp.zeros_like(xp)
            rhs = jnp.concatenate([jnp.where(lo_mask, xp, zero), jnp.where(lo_mask, zero, xp)], axis=0)
            ydiag_ref[:, lane0:lane0 + LANES] = jnp.dot(lhs, rhs, preferred_element_type=F32)
        st = state_ref[:, g * gw:(g + 1) * gw]
        y_off = jnp.dot(cg, st.astype(BF16), preferred_element_type=F32)
        ydiag_ref[:, g * gw:(g + 1) * gw] += y_off * decay_in[:, g * gw:(g + 1) * gw]
        new = lax.dot_general(bg, xd_b[:, g * gw:(g + 1) * gw], (((0,), (0,)), ((), ())),
                              preferred_element_type=F32)
        state_ref[:, g * gw:(g + 1) * gw] = st * chunk_decay[:, g * gw:(g + 1) * gw] + new

    y = ydiag_ref[...] + dexp_ref[...] * xs
    y = y * _silu(z_ref[...].astype(F32))
    ms = jnp.mean(y * y, axis=-1, keepdims=True)
    y_ref[...] = (y * lax.rsqrt(ms + EPS) * nw_ref[...]).astype(y_ref.dtype)


def _ssd(proj, small, conv_w, conv_b, dt_bias, a_log, d_skip, norm_w, bsz, seq):
    l = SSD_CHUNK
    nc = seq // l
    t = bsz * seq
    h3 = SMALL_DT_COPIES * SSD_HEADS
    aneg = jnp.tile(-jnp.exp(a_log.astype(F32)), SMALL_DT_COPIES).reshape(1, h3)
    dtb = jnp.tile(dt_bias.astype(F32), SMALL_DT_COPIES).reshape(1, h3)
    dexp = jnp.repeat(d_skip.astype(F32), SSD_HEAD_DIM).reshape(1, SSD_DIM)
    expand = jnp.tile(jnp.repeat(jnp.eye(SSD_HEADS, dtype=BF16), SSD_HEAD_DIM, axis=1),
                      (SMALL_DT_COPIES, 1))
    out_row = jnp.arange((CONV_WIDTH - 1) * l)
    src_row = HALO + out_row % l - (CONV_WIDTH - 1) + out_row // l
    shift = (jnp.arange(HALO + l)[None, :] == src_row[:, None]).astype(BF16)
    xbc_blk = COL_XBC // CONV_DIM
    halo_per_chunk = l // HALO

    def row_map(b, c):
        return b * nc + c

    return pl.pallas_call(
        _ssd_kernel,
        name="ssd",
        grid=(bsz, nc),
        in_specs=[
            pl.BlockSpec((l, SSD_DIM), lambda b, c: (row_map(b, c), COL_Z // SSD_DIM)),
            pl.BlockSpec((l, CONV_DIM), lambda b, c: (row_map(b, c), xbc_blk)),
            pl.BlockSpec((HALO, CONV_DIM),
                         lambda b, c: (jnp.maximum(row_map(b, c) * halo_per_chunk - 1, 0), xbc_blk)),
            pl.BlockSpec((l, LANES), lambda b, c: (row_map(b, c), 0)),
            pl.BlockSpec((CONV_WIDTH, CONV_DIM), lambda b, c: (0, 0)),
            pl.BlockSpec((1, CONV_DIM), lambda b, c: (0, 0)),
            pl.BlockSpec((1, h3), lambda b, c: (0, 0)),
            pl.BlockSpec((1, h3), lambda b, c: (0, 0)),
            pl.BlockSpec((1, SSD_DIM), lambda b, c: (0, 0)),
            pl.BlockSpec((1, SSD_DIM), lambda b, c: (0, 0)),
            pl.BlockSpec((h3, SSD_DIM), lambda b, c: (0, 0)),
            pl.BlockSpec(((CONV_WIDTH - 1) * l, HALO + l), lambda b, c: (0, 0)),
        ],
        out_specs=pl.BlockSpec((l, SSD_DIM), lambda b, c: (row_map(b, c), 0)),
        out_shape=jax.ShapeDtypeStruct((t, SSD_DIM), BF16),
        scratch_shapes=[pltpu.VMEM((SSD_STATE, SSD_DIM), F32),
                        pltpu.VMEM((l, SSD_DIM), F32)],
        compiler_params=_cparams(("parallel", "arbitrary")),
    )(proj, proj, proj, small, conv_w.astype(F32), conv_b.astype(F32).reshape(1, CONV_DIM),
      dtb, aneg, dexp, norm_w.astype(F32).reshape(1, SSD_DIM), expand, shift)


CUM_ROWS = 256


def _cum_kernel(small_ref, fb_ref, cum_ref, carry_ref):
    j = pl.program_id(1)

    @pl.when(j == 0)
    def _():
        carry_ref[...] = jnp.zeros_like(carry_ref)

    lf = _log_sigmoid(small_ref[...] + fb_ref[...])
    row = lax.broadcasted_iota(I32, (CUM_ROWS, CUM_ROWS), 0)
    col = lax.broadcasted_iota(I32, (CUM_ROWS, CUM_ROWS), 1)
    tril = jnp.where(col <= row, 1.0, 0.0).astype(F32)
    cs = jnp.dot(tril, lf, preferred_element_type=F32, precision=lax.Precision.HIGHEST) + carry_ref[...]
    cum_ref[...] = cs * LOG2E
    carry_ref[...] = cs[CUM_ROWS - 1:CUM_ROWS, :]


def _fox_cum(small, fox_f_bias, bsz, seq):
    fb = jnp.zeros((1, LANES), F32).at[0, SMALL_F:SMALL_F + FOX_HEADS].set(fox_f_bias.astype(F32))
    nj = seq // CUM_ROWS
    return pl.pallas_call(
        _cum_kernel,
        name="fox_cum",
        grid=(bsz, nj),
        in_specs=[pl.BlockSpec((CUM_ROWS, LANES), lambda b, j: (b * nj + j, 0)),
                  pl.BlockSpec((1, LANES), lambda b, j: (0, 0))],
        out_specs=pl.BlockSpec((CUM_ROWS, LANES), lambda b, j: (b * nj + j, 0)),
        out_shape=jax.ShapeDtypeStruct((bsz * seq, LANES), F32),
        scratch_shapes=[pltpu.VMEM((1, LANES), F32)],
        compiler_params=_cparams(("parallel", "arbitrary")),
    )(small, fb)


FOX_HEADS_PER_STEP = 2
LOG2E = 1.4426950408889634


FOX_SLAB = 128


def _fox_kernel(q_ref, k_ref, v_ref, cq_ref, ck_ref, o_ref, sa_ref, sb_ref, p_ref, m_ref, l_ref, alpha_ref,
                cqrep_ref, acc_ref, *, tq, tk):
    hp = pl.program_id(1)
    qi = pl.program_id(2)
    d = FOX_HEAD_DIM
    c2 = LOG2E / math.sqrt(d)
    lane = lax.broadcasted_iota(I32, (tq, LANES), 1)
    cq_all = cq_ref[...]

    qs, cqs = [], []
    for hh in range(FOX_HEADS_PER_STEP):
        qs.append((q_ref[:, hh * d:(hh + 1) * d].astype(F32) * c2).astype(BF16))
        head_lane = SMALL_F + hp * FOX_HEADS_PER_STEP + hh
        cq_col = jnp.sum(jnp.where(lane == head_lane, cq_all, 0.0), axis=-1, keepdims=True)
        cqs.append(jnp.broadcast_to(cq_col, (tq, LANES)))

    s_slots = (sa_ref, sb_ref)

    def scores(ki, slot):
        row0 = pl.multiple_of(ki * tk, tk)
        for hh in range(FOX_HEADS_PER_STEP):
            k = k_ref[pl.ds(row0, tk), hh * d:(hh + 1) * d]
            s = lax.dot_general(qs[hh], k, (((1,), (1,)), ((), ())), preferred_element_type=F32)
            s_slots[slot][hh] = s - ck_ref[hh, pl.ds(ki, 1), :]

    def update(ki, slot, masked):
        row0 = pl.multiple_of(ki * tk, tk)
        s_ref = s_slots[slot]
        n_ct = tk // LANES
        for hh in range(FOX_HEADS_PER_STEP):
            for rc in range(tq // FOX_SLAB):
                rows = slice(rc * FOX_SLAB, (rc + 1) * FOX_SLAB)
                n_vis = rc + 1 if masked else n_ct

                def slab(ct):
                    x = s_ref[hh, rows, ct * LANES:(ct + 1) * LANES]
                    if masked and ct == rc:
                        row = lax.broadcasted_iota(I32, (FOX_SLAB, LANES), 0)
                        col = lax.broadcasted_iota(I32, (FOX_SLAB, LANES), 1)
                        x = jnp.where(col <= row, x, -jnp.inf)
                    return x

                tmax = slab(0)
                for ct in range(1, n_vis):
                    tmax = jnp.maximum(tmax, slab(ct))
                row_max = jnp.max(tmax, axis=-1, keepdims=True)
                cq = cqrep_ref[hh, rows, :]
                m_prev = m_ref[hh, rows, :]
                m_new = jnp.maximum(m_prev, jnp.broadcast_to(row_max, (FOX_SLAB, LANES)) + cq)
                r = m_new - cq
                alpha = jnp.exp2(m_prev - m_new)
                psum = None
                for ct in range(n_vis):
                    p = jnp.exp2(slab(ct) - r)
                    psum = p if psum is None else psum + p
                    p_ref[hh, rows, ct * LANES:(ct + 1) * LANES] = p.astype(BF16)
                for ct in range(n_vis, n_ct):
                    p_ref[hh, rows, ct * LANES:(ct + 1) * LANES] = jnp.zeros((FOX_SLAB, LANES), BF16)
                l_ref[hh, rows, :] = alpha * l_ref[hh, rows, :] + psum
                m_ref[hh, rows, :] = m_new
                alpha_ref[hh, rows, :] = alpha
            v = v_ref[pl.ds(row0, tk), hh * d:(hh + 1) * d]
            acc_ref[hh] = alpha_ref[hh] * acc_ref[hh] + jnp.dot(p_ref[hh], v, preferred_element_type=F32)

    for hh in range(FOX_HEADS_PER_STEP):
        cqrep_ref[hh] = cqs[hh]

    m_ref[...] = jnp.full_like(m_ref, -jnp.inf)
    l_ref[...] = jnp.zeros_like(l_ref)
    acc_ref[...] = jnp.zeros_like(acc_ref)

    n_full = qi
    scores(0, 0)

    def pair(j, carry):
        b0 = 2 * j
        scores(b0 + 1, 1)
        update(b0, 0, False)
        scores(b0 + 2, 0)
        update(b0 + 1, 1, False)
        return carry

    lax.fori_loop(0, n_full // 2, pair, 0)

    @pl.when(n_full % 2 == 0)
    def _():
        update(n_full, 0, True)

    @pl.when(n_full % 2 == 1)
    def _():
        scores(n_full, 1)
        update(n_full - 1, 0, False)
        update(n_full, 1, True)

    for hh in range(FOX_HEADS_PER_STEP):
        l_fin = jnp.sum(l_ref[hh], axis=-1, keepdims=True)
        o_ref[:, hh * d:(hh + 1) * d] = (acc_ref[hh] / l_fin).astype(o_ref.dtype)


def _fox(proj, cum, bsz, seq, tq=512):
    assert FOX_SLAB == LANES and tq % FOX_SLAB == 0 and seq % tq == 0
    tk = tq
    nq = seq // tq
    nk = seq // tk
    hps = FOX_HEADS_PER_STEP
    cum_row = cum[:, SMALL_F:SMALL_F + FOX_HEADS].reshape(bsz, seq, FOX_HEADS).transpose(0, 2, 1)
    cum_row = cum_row.reshape(bsz, FOX_HEADS, nk, tk)
    t = bsz * seq
    w = FOX_HEADS_PER_STEP * FOX_HEAD_DIM
    n_hp = FOX_HEADS // FOX_HEADS_PER_STEP
    kern = functools.partial(_fox_kernel, tq=tq, tk=tk)
    return pl.pallas_call(
        kern,
        name="fox",
        grid=(bsz, n_hp, nq),
        in_specs=[
            pl.BlockSpec((tq, w), lambda b, hp, qi: (b * nq + qi, COL_Q // w + hp)),
            pl.BlockSpec((seq, w), lambda b, hp, qi: (b, COL_K // w + hp)),
            pl.BlockSpec((seq, w), lambda b, hp, qi: (b, COL_V // w + hp)),
            pl.BlockSpec((tq, LANES), lambda b, hp, qi: (b * nq + qi, 0)),
            pl.BlockSpec((None, FOX_HEADS_PER_STEP, nk, tk), lambda b, hp, qi: (b, hp, 0, 0)),
        ],
        out_specs=pl.BlockSpec((tq, w), lambda b, hp, qi: (b * nq + qi, hp)),
        out_shape=jax.ShapeDtypeStruct((t, FOX_DIM), BF16),
        scratch_shapes=[pltpu.VMEM((hps, tq, tk), F32), pltpu.VMEM((hps, tq, tk), F32),
                        pltpu.VMEM((hps, tq, tk), BF16),
                        pltpu.VMEM((hps, tq, LANES), F32), pltpu.VMEM((hps, tq, LANES), F32),
                        pltpu.VMEM((hps, tq, LANES), F32), pltpu.VMEM((hps, tq, LANES), F32),
                        pltpu.VMEM((hps, tq, FOX_HEAD_DIM), F32)],
        compiler_params=_cparams(("parallel", "parallel", "arbitrary")),
    )(proj, proj, proj, cum, cum_row)


def _mix_kernel(ya_ref, yb_ref, ga_ref, gb_ref, wa_ref, wb_ref, o_ref):
    pa = jnp.dot(ya_ref[...], wa_ref[...], preferred_element_type=F32)
    pb = jnp.dot(yb_ref[...], wb_ref[...], preferred_element_type=F32)
    ga = 1.0 / (1.0 + jnp.exp(-ga_ref[...].astype(F32)))
    gb = 1.0 / (1.0 + jnp.exp(-gb_ref[...].astype(F32)))
    o_ref[...] = (ga * pa + gb * pb).astype(o_ref.dtype)


def _mix(y_a, y_b, proj, w_a, w_b, tm=512, tn=1024):
    t = y_a.shape[0]
    return pl.pallas_call(
        _mix_kernel,
        name="mix",
        grid=(D_MODEL // tn, t // tm),
        in_specs=[
            pl.BlockSpec((tm, SSD_DIM), lambda j, i: (i, 0)),
            pl.BlockSpec((tm, FOX_DIM), lambda j, i: (i, 0)),
            pl.BlockSpec((tm, tn), lambda j, i: (i, COL_GA // tn + j)),
            pl.BlockSpec((tm, tn), lambda j, i: (i, COL_GB // tn + j)),
            pl.BlockSpec((SSD_DIM, tn), lambda j, i: (0, j)),
            pl.BlockSpec((FOX_DIM, tn), lambda j, i: (0, j)),
        ],
        out_specs=pl.BlockSpec((tm, tn), lambda j, i: (i, j)),
        out_shape=jax.ShapeDtypeStruct((t, D_MODEL), BF16),
        compiler_params=_cparams(("parallel", "parallel")),
    )(y_a, y_b, proj, proj, w_a, w_b)


def _outproj_kernel(m_ref, x_ref, wo_ref, nw_ref, wr_ref, br_ref, h_ref, u_ref, eidx_ref, wts_ref):
    h1 = x_ref[...] + jnp.dot(m_ref[...], wo_ref[...], preferred_element_type=F32)
    h_ref[...] = h1
    ms = jnp.mean(h1 * h1, axis=-1, keepdims=True)
    u2 = h1 * lax.rsqrt(ms + EPS) * nw_ref[...]
    u_ref[...] = _pack_bf16_pair(u2)

    u_hi = u2.astype(BF16)
    u_lo = (u2 - u_hi.astype(F32)).astype(BF16)
    hh_hl = jnp.dot(u_hi, wr_ref[...], preferred_element_type=F32)
    lh = jnp.dot(u_lo, wr_ref[:, :LANES], preferred_element_type=F32)
    logits = hh_hl[:, :LANES] + (hh_hl[:, LANES:] + lh) + br_ref[...]
    tm = logits.shape[0]
    lane = lax.broadcasted_iota(I32, (tm, LANES), 1)
    neg = -jnp.inf
    big = jnp.int32(2 * LANES)
    gl = jnp.where(lane < N_GROUPS, logits, neg)
    gmax = jnp.max(gl, axis=-1, keepdims=True)
    gsum = jnp.sum(jnp.exp(gl - gmax), axis=-1, keepdims=True)
    g_p = 1.0 / gsum
    g_idx = jnp.min(jnp.where(gl == gmax, lane, big), axis=-1, keepdims=True)
    e_of_lane = lane - N_GROUPS
    in_grp = (e_of_lane >= g_idx * EXPERTS_PER_GROUP) & (e_of_lane < (g_idx + 1) * EXPERTS_PER_GROUP)
    el = jnp.where(in_grp, logits, neg)
    m1 = jnp.max(el, axis=-1, keepdims=True)
    i1 = jnp.min(jnp.where(el == m1, lane, big), axis=-1, keepdims=True)
    el2 = jnp.where(lane == i1, neg, el)
    m2 = jnp.max(el2, axis=-1, keepdims=True)
    i2 = jnp.min(jnp.where(el2 == m2, lane, big), axis=-1, keepdims=True)
    esum = jnp.sum(jnp.exp(el - m1), axis=-1, keepdims=True)
    p1 = 1.0 / esum
    p2 = jnp.exp(m2 - m1) / esum
    w1 = g_p * (p1 / (p1 + p2))
    w2 = g_p * (p2 / (p1 + p2))
    eidx_ref[...] = jnp.where(lane == 0, i1 - N_GROUPS, jnp.where(lane == 1, i2 - N_GROUPS, 0))
    wts_ref[...] = jnp.where(lane == 0, w1, jnp.where(lane == 1, w2, 0.0))


def _outproj(mixed, x2, w_o, nw, w_router, b_router, tm=512):
    t = mixed.shape[0]
    resident = pl.Buffered(1)
    return pl.pallas_call(
        _outproj_kernel,
        name="outproj",
        grid=(t // tm,),
        in_specs=[
            pl.BlockSpec((tm, D_MODEL), lambda i: (i, 0)),
            pl.BlockSpec((tm, D_MODEL), lambda i: (i, 0)),
            pl.BlockSpec((D_MODEL, D_MODEL), lambda i: (0, 0), pipeline_mode=resident),
            pl.BlockSpec((1, D_MODEL), lambda i: (0, 0)),
            pl.BlockSpec((D_MODEL, 2 * LANES), lambda i: (0, 0), pipeline_mode=resident),
            pl.BlockSpec((1, LANES), lambda i: (0, 0)),
        ],
        out_specs=[
            pl.BlockSpec((tm, D_MODEL), lambda i: (i, 0)),
            pl.BlockSpec((tm, HALF), lambda i: (i, 0)),
            pl.BlockSpec((tm, LANES), lambda i: (i, 0)),
            pl.BlockSpec((tm, LANES), lambda i: (i, 0)),
        ],
        out_shape=[
            jax.ShapeDtypeStruct((t, D_MODEL), F32),
            jax.ShapeDtypeStruct((t, HALF), U32),
            jax.ShapeDtypeStruct((t, LANES), I32),
            jax.ShapeDtypeStruct((t, LANES), F32),
        ],
        compiler_params=_cparams(("parallel",)),
    )(mixed, x2, w_o, nw, w_router, b_router)


RANK_BLOCK = 512


def _rank_kernel(e_ref, rank_ref, cnt_ref, carry_ref):
    i = pl.program_id(0)
    r = RANK_BLOCK

    @pl.when(i == 0)
    def _():
        carry_ref[...] = jnp.zeros_like(carry_ref)

    e = e_ref[0]
    expert = lax.broadcasted_iota(I32, (N_EXPERTS, r), 0)
    onehot = jnp.where(expert == e, 1.0, 0.0).astype(F32)
    jrow = lax.broadcasted_iota(I32, (r, r), 0)
    jcol = lax.broadcasted_iota(I32, (r, r), 1)
    before = jnp.where(jrow < jcol, 1.0, 0.0).astype(BF16)
    cum = jnp.dot(onehot.astype(BF16), before, preferred_element_type=F32)
    carry = carry_ref[...]
    rank = jnp.sum(onehot * (cum + carry[:, 0:1]), axis=0, keepdims=True)
    rank_ref[0] = rank.astype(I32)
    carry = carry + jnp.sum(onehot, axis=1, keepdims=True)
    carry_ref[...] = carry
    cnt_ref[...] = carry


def _rank(e_blocks):
    nb = e_blocks.shape[0]
    return pl.pallas_call(
        _rank_kernel,
        name="rank",
        grid=(nb,),
        in_specs=[pl.BlockSpec((1, 1, RANK_BLOCK), lambda i: (i, 0, 0))],
        out_specs=[pl.BlockSpec((1, 1, RANK_BLOCK), lambda i: (i, 0, 0)),
                   pl.BlockSpec((N_EXPERTS, LANES), lambda i: (0, 0))],
        out_shape=[jax.ShapeDtypeStruct((nb, 1, RANK_BLOCK), I32),
                   jax.ShapeDtypeStruct((N_EXPERTS, LANES), F32)],
        scratch_shapes=[pltpu.VMEM((N_EXPERTS, LANES), F32)],
        compiler_params=_cparams(("arbitrary",)),
    )(e_blocks)


def _dest_kernel(e_ref, rank_ref, pstart_ref, dest_ref):
    e = e_ref[0]
    expert = lax.broadcasted_iota(I32, (N_EXPERTS, RANK_BLOCK), 0)
    start = jnp.sum(jnp.where(expert == e, pstart_ref[:, 0:1], 0), axis=0, keepdims=True)
    dest_ref[0] = rank_ref[0] + start


def _dest(e_blocks, rank, pstart):
    nb = e_blocks.shape[0]
    return pl.pallas_call(
        _dest_kernel,
        name="dest",
        grid=(nb,),
        in_specs=[pl.BlockSpec((1, 1, RANK_BLOCK), lambda i: (i, 0, 0)),
                  pl.BlockSpec((1, 1, RANK_BLOCK), lambda i: (i, 0, 0)),
                  pl.BlockSpec((N_EXPERTS, LANES), lambda i: (0, 0))],
        out_specs=pl.BlockSpec((1, 1, RANK_BLOCK), lambda i: (i, 0, 0)),
        out_shape=jax.ShapeDtypeStruct((nb, 1, RANK_BLOCK), I32),
        compiler_params=_cparams(("parallel",)),
    )(e_blocks, rank, pstart)


def _row_copy(src_ref, src_row, dst_ref, dst_row, sem):
    return pltpu.make_async_copy(src_ref.at[pl.ds(src_row, 1)], dst_ref.at[pl.ds(dst_row, 1)], sem)


DMA_UNROLL = 16


def _dispatch_kernel(last_blk_ref, nused_ref, dest_ref, u_ref, xs_ref, zbuf_ref, sem, zsem):
    i = pl.program_id(0)
    n_total = xs_ref.shape[0] // ROW_BLOCK

    @pl.when(i == 0)
    def _():
        zbuf_ref[...] = jnp.zeros_like(zbuf_ref)

        def zero_block(blk):
            return pltpu.make_async_copy(zbuf_ref, xs_ref.at[pl.ds(blk * ROW_BLOCK, ROW_BLOCK)], zsem)

        def fill_expert(e, count):
            blk = last_blk_ref[e]

            @pl.when(blk >= 0)
            def _():
                zero_block(blk).start()

            return count + jnp.where(blk >= 0, 1, 0)

        n_fill = lax.fori_loop(0, N_EXPERTS, fill_expert, 0)

        def fill_tail(blk, carry):
            zero_block(blk).start()
            return carry

        lax.fori_loop(nused_ref[0], n_total, fill_tail, 0)

        def drain(k, carry):
            zero_block(0).wait()
            return carry

        lax.fori_loop(0, n_fill + n_total - nused_ref[0], drain, 0)

    def issue(g, carry):
        tok = g * (DMA_UNROLL // TOP_K)
        c0 = g * DMA_UNROLL
        for j in range(DMA_UNROLL):
            _row_copy(u_ref, tok + j // TOP_K, xs_ref, dest_ref[0, 0, c0 + j], sem).start()
        return carry

    lax.fori_loop(0, RANK_BLOCK // DMA_UNROLL, issue, 0)
    pltpu.make_async_copy(xs_ref.at[pl.ds(0, RANK_BLOCK)], xs_ref.at[pl.ds(0, RANK_BLOCK)], sem).wait()


def _dispatch(dest, u2p, n_rows, last_blk, n_used):
    nb = dest.shape[0]
    grid_spec = pltpu.PrefetchScalarGridSpec(
        num_scalar_prefetch=2,
        grid=(nb,),
        in_specs=[pl.BlockSpec((1, 1, RANK_BLOCK), lambda i, lb, nu: (i, 0, 0), memory_space=pltpu.SMEM),
                  pl.BlockSpec((RANK_BLOCK // TOP_K, HALF), lambda i, lb, nu: (i, 0))],
        out_specs=pl.BlockSpec(memory_space=pl.ANY),
        scratch_shapes=[pltpu.VMEM((ROW_BLOCK, HALF), U32),
                        pltpu.SemaphoreType.DMA(()),
                        pltpu.SemaphoreType.DMA(())],
    )
    return pl.pallas_call(
        _dispatch_kernel,
        name="dispatch",
        grid_spec=grid_spec,
        out_shape=jax.ShapeDtypeStruct((n_rows, HALF), U32),
        compiler_params=_cparams(("arbitrary",)),
    )(last_blk, n_used, dest, u2p)


ROW_DMA_PRIORITY = 1
ROW_RING = 8


def _experts_kernel(bstart_ref, nblk_ref, xs_ref, wg_ref, wu_ref, wd_ref, y_ref,
                    wgb_ref, wub_ref, wdb_ref, xbuf_ref, ybuf_ref, xsem, ysem):
    e = pl.program_id(0)
    n_e = pl.num_programs(0)
    g0 = bstart_ref[e]
    nb = nblk_ref[e]
    n_used = bstart_ref[n_e - 1] + nblk_ref[n_e - 1]
    n_total = y_ref.shape[0] // ROW_BLOCK

    def x_copy(g, slot):
        return pltpu.make_async_copy(xs_ref.at[pl.ds(g * ROW_BLOCK, ROW_BLOCK)], xbuf_ref.at[slot], xsem.at[slot])

    def y_copy(g, slot):
        return pltpu.make_async_copy(ybuf_ref.at[slot], y_ref.at[pl.ds(g * ROW_BLOCK, ROW_BLOCK)], ysem.at[slot])

    @pl.when(e == 0)
    def _():
        for g in range(ROW_RING - 1):
            @pl.when(g < n_used)
            def _(g=g):
                x_copy(g, g).start(priority=ROW_DMA_PRIORITY)

    @pl.when(nb > 0)
    def _():
        wgb_ref[...] = wg_ref[0].astype(BF16)
        wub_ref[...] = wu_ref[0].astype(BF16)
        wdb_ref[...] = wd_ref[0].astype(BF16)

    def block(j, carry):
        g = g0 + j
        slot = g % ROW_RING
        x_copy(g, slot).wait()

        @pl.when(g + ROW_RING - 1 < n_used)
        def _():
            x_copy(g + ROW_RING - 1, (g + ROW_RING - 1) % ROW_RING).start(priority=ROW_DMA_PRIORITY)

        @pl.when(g >= ROW_RING)
        def _():
            y_copy(g - ROW_RING, slot).wait()

        lo, hi = _unpack_bf16_pair(xbuf_ref[slot])
        lo = lo.astype(BF16)
        hi = hi.astype(BF16)
        gate = (jnp.dot(lo, wgb_ref[:HALF, :], preferred_element_type=F32)
                + jnp.dot(hi, wgb_ref[HALF:, :], preferred_element_type=F32))
        up = (jnp.dot(lo, wub_ref[:HALF, :], preferred_element_type=F32)
              + jnp.dot(hi, wub_ref[HALF:, :], preferred_element_type=F32))
        hdn = (_silu(gate) * up).astype(BF16)
        y = jnp.dot(hdn, wdb_ref[...], preferred_element_type=F32)
        ybuf_ref[slot] = _pack_bf16_pair(y)
        y_copy(g, slot).start(priority=ROW_DMA_PRIORITY)
        return carry

    lax.fori_loop(0, nb, block, 0)

    @pl.when(e == n_e - 1)
    def _():
        for back in range(ROW_RING, 0, -1):
            @pl.when(n_used >= back)
            def _(back=back):
                y_copy(n_used - back, (n_used - back) % ROW_RING).wait()

        ybuf_ref[0] = jnp.zeros((ROW_BLOCK, HALF), U32)

        def fill(g, carry):
            y_copy(g, 0).start()
            y_copy(g, 0).wait()
            return carry

        lax.fori_loop(n_used, n_total, fill, 0)


def _experts(block_start, block_count, xs, w_gate, w_up, w_down):
    n_rows = xs.shape[0]

    def wmap(e, bs, bc):
        return (e, 0, 0)

    grid_spec = pltpu.PrefetchScalarGridSpec(
        num_scalar_prefetch=2,
        grid=(N_EXPERTS,),
        in_specs=[pl.BlockSpec(memory_space=pl.ANY),
                  pl.BlockSpec((1, D_MODEL, D_EXPERT), wmap),
                  pl.BlockSpec((1, D_MODEL, D_EXPERT), wmap),
                  pl.BlockSpec((1, D_EXPERT, D_MODEL), wmap)],
        out_specs=pl.BlockSpec(memory_space=pl.ANY),
        scratch_shapes=[pltpu.VMEM((D_MODEL, D_EXPERT), BF16),
                        pltpu.VMEM((D_MODEL, D_EXPERT), BF16),
                        pltpu.VMEM((D_EXPERT, D_MODEL), BF16),
                        pltpu.VMEM((ROW_RING, ROW_BLOCK, HALF), U32),
                        pltpu.VMEM((ROW_RING, ROW_BLOCK, HALF), U32),
                        pltpu.SemaphoreType.DMA((ROW_RING,)),
                        pltpu.SemaphoreType.DMA((ROW_RING,))],
    )
    return pl.pallas_call(
        _experts_kernel,
        name="experts",
        grid_spec=grid_spec,
        out_shape=jax.ShapeDtypeStruct((n_rows, HALF), U32),
        compiler_params=_cparams(("arbitrary",)),
    )(block_start, block_count, xs, w_gate, w_up, w_down)


COMBINE_TOKENS = RANK_BLOCK // TOP_K


def _combine_kernel(dest_ref, dest_next_ref, h_ref, wts_ref, nw_ref, y_ref, o_ref, buf_ref, sem):
    ts = COMBINE_TOKENS
    i = pl.program_id(0)
    slot = i % 2

    def gather(idx_ref, dst_slot):
        def issue(g, carry):
            tok = g * (DMA_UNROLL // TOP_K)
            c0 = g * DMA_UNROLL
            for j in range(DMA_UNROLL):
                _row_copy(y_ref, idx_ref[0, 0, c0 + j], buf_ref.at[dst_slot],
                          (j % TOP_K) * ts + tok + j // TOP_K, sem.at[dst_slot]).start()
            return carry

        lax.fori_loop(0, RANK_BLOCK // DMA_UNROLL, issue, 0)

    @pl.when(i == 0)
    def _():
        gather(dest_ref, 0)

    @pl.when(i + 1 < pl.num_programs(0))
    def _():
        gather(dest_next_ref, 1 - slot)

    pltpu.make_async_copy(y_ref.at[pl.ds(0, RANK_BLOCK)], buf_ref.at[slot], sem.at[slot]).wait()

    w = wts_ref[...]
    w0 = w[:, 0:1]
    w1 = w[:, 1:2]
    lo0, hi0 = _unpack_bf16_pair(buf_ref[slot, 0:ts, :])
    lo1, hi1 = _unpack_bf16_pair(buf_ref[slot, ts:2 * ts, :])
    h = h_ref[...]
    out_lo = h[:, :HALF] + w0 * lo0 + w1 * lo1
    out_hi = h[:, HALF:] + w0 * hi0 + w1 * hi1
    ms = (jnp.sum(out_lo * out_lo, axis=-1, keepdims=True)
          + jnp.sum(out_hi * out_hi, axis=-1, keepdims=True)) * (1.0 / D_MODEL)
    inv = lax.rsqrt(ms + EPS)
    o_ref[:, :HALF] = out_lo * inv * nw_ref[:, :HALF]
    o_ref[:, HALF:] = out_hi * inv * nw_ref[:, HALF:]


def _combine(dest, h1, wts, nw, y):
    t = h1.shape[0]
    ts = COMBINE_TOKENS
    n_steps = t // ts
    return pl.pallas_call(
        _combine_kernel,
        name="combine",
        grid=(n_steps,),
        in_specs=[pl.BlockSpec((1, 1, RANK_BLOCK), lambda i: (i, 0, 0), memory_space=pltpu.SMEM),
                  pl.BlockSpec((1, 1, RANK_BLOCK), lambda i: (jnp.minimum(i + 1, n_steps - 1), 0, 0),
                               memory_space=pltpu.SMEM),
                  pl.BlockSpec((ts, D_MODEL), lambda i: (i, 0)),
                  pl.BlockSpec((ts, LANES), lambda i: (i, 0)),
                  pl.BlockSpec((1, D_MODEL), lambda i: (0, 0)),
                  pl.BlockSpec(memory_space=pl.ANY)],
        out_specs=pl.BlockSpec((ts, D_MODEL), lambda i: (i, 0)),
        out_shape=jax.ShapeDtypeStruct((t, D_MODEL), F32),
        scratch_shapes=[pltpu.VMEM((2, RANK_BLOCK, HALF), U32), pltpu.SemaphoreType.DMA((2,))],
        compiler_params=_cparams(("arbitrary",)),
    )(dest, dest, h1, wts, nw, y)


def _narrow_w_in(w_in):
    dt0 = SSD_DIM + CONV_DIM
    f0 = dt0 + SSD_HEADS + 3 * FOX_DIM
    dt = w_in[:, dt0:dt0 + SSD_HEADS]
    f = w_in[:, f0:f0 + FOX_HEADS]
    pad = jnp.zeros((w_in.shape[0], LANES - SMALL_F - FOX_HEADS), w_in.dtype)
    return jnp.concatenate([dt] * SMALL_DT_COPIES + [f, pad], axis=1).astype(BF16)


def _layer(h, p, bsz, seq):
    t = bsz * seq
    w_in = p["w_in"].astype(F32)
    u, small = _norm_small(h, p["norm_mix_w"].astype(F32).reshape(1, D_MODEL), _narrow_w_in(w_in))
    proj = _in_proj(u, w_in)
    y_ssd = _ssd(proj, small, p["conv_w"], p["conv_b"], p["dt_bias"], p["a_log"], p["d_skip"],
                 p["ssd_norm_w"], bsz, seq)
    cum = _fox_cum(small, p["fox_f_bias"], bsz, seq)
    y_fox = _fox(proj, cum, bsz, seq)
    mixed = _mix(y_ssd, y_fox, proj, p["w_proj_ssd"].astype(BF16), p["w_proj_fox"].astype(BF16))

    w_router = jnp.concatenate(
        [p["w_router_group"], p["w_router_expert"],
         jnp.zeros((D_MODEL, LANES - N_GROUPS - N_EXPERTS), F32)], axis=1).astype(F32)
    w_router_hi = w_router.astype(BF16)
    w_router_lo = (w_router - w_router_hi.astype(F32)).astype(BF16)
    w_router = jnp.concatenate([w_router_hi, w_router_lo], axis=1)
    b_router = jnp.concatenate(
        [p["b_router_group"], p["b_router_expert"],
         jnp.zeros((LANES - N_GROUPS - N_EXPERTS,), F32)]).astype(F32).reshape(1, LANES)
    h1, u2p, eidx, wts = _outproj(mixed, h, p["w_out"].astype(BF16),
                                  p["norm_moe_w"].astype(F32).reshape(1, D_MODEL), w_router, b_router)

    tk = t * TOP_K
    e_blocks = eidx[:, :TOP_K].reshape(tk // RANK_BLOCK, 1, RANK_BLOCK)
    rank, counts = _rank(e_blocks)
    counts = counts[:, 0].astype(I32)
    padded = ((counts + ROW_BLOCK - 1) // ROW_BLOCK) * ROW_BLOCK
    pend = jnp.cumsum(padded)
    pstart = pend - padded
    n_blocks = tk // ROW_BLOCK + N_EXPERTS
    dest = _dest(e_blocks, rank, jnp.broadcast_to(pstart[:, None], (N_EXPERTS, LANES)).astype(I32))
    block_start = (pstart // ROW_BLOCK).astype(I32)
    block_count = (padded // ROW_BLOCK).astype(I32)
    last_blk = jnp.where(block_count > 0, block_start + block_count - 1, -1).astype(I32)
    n_used = (pend[-1:] // ROW_BLOCK).astype(I32)
    xs = _dispatch(dest, u2p, n_blocks * ROW_BLOCK, last_blk, n_used)
    y = _experts(block_start, block_count, xs, p["w_gate_exp"], p["w_up_exp"], p["w_down_exp"])
    return dest, h1, wts, y


def kernel(x, norm_mix_w, w_in, conv_w, conv_b, dt_bias, a_log, d_skip, ssd_norm_w, fox_f_bias, w_proj_ssd,
           w_proj_fox, w_out, norm_moe_w, w_router_group, b_router_group, w_router_expert, b_router_expert,
           w_gate_exp, w_up_exp, w_down_exp, norm_final_w):
    bsz, seq, _ = x.shape
    depth = w_in.shape[0]
    assert depth == 1, "the fused final norm assumes a single layer"
    stacked = dict(norm_mix_w=norm_mix_w, w_in=w_in, conv_w=conv_w, conv_b=conv_b, dt_bias=dt_bias, a_log=a_log,
                   d_skip=d_skip, ssd_norm_w=ssd_norm_w, fox_f_bias=fox_f_bias, w_proj_ssd=w_proj_ssd,
                   w_proj_fox=w_proj_fox, w_out=w_out, norm_moe_w=norm_moe_w, w_router_group=w_router_group,
                   b_router_group=b_router_group, w_router_expert=w_router_expert,
                   b_router_expert=b_router_expert, w_gate_exp=w_gate_exp, w_up_exp=w_up_exp,
                   w_down_exp=w_down_exp)
    p = {name: v[0] for name, v in stacked.items()}
    h = x.reshape(bsz * seq, D_MODEL)
    dest, h1, wts, y = _layer(h, p, bsz, seq)
    out = _combine(dest, h1, wts, norm_final_w.astype(F32).reshape(1, D_MODEL), y)
    return out.reshape(bsz, seq, D_MODEL)
```

```python
import functools
import math

import jax
import jax.numpy as jnp
from jax import lax
from jax.experimental import pallas as pl
from jax.experimental.pallas import tpu as pltpu

F32 = jnp.float32
BF16 = jnp.bfloat16
I32 = jnp.int32
U32 = jnp.uint32

D_MODEL = 2048
SSD_HEADS = 32
SSD_HEAD_DIM = 64
SSD_DIM = SSD_HEADS * SSD_HEAD_DIM
SSD_GROUPS = 4
SSD_STATE = 128
SSD_CHUNK = 128
CONV_WIDTH = 4
CONV_DIM = SSD_DIM + 2 * SSD_GROUPS * SSD_STATE
FOX_HEADS = 16
FOX_HEAD_DIM = 128
FOX_DIM = FOX_HEADS * FOX_HEAD_DIM
N_GROUPS = 8
EXPERTS_PER_GROUP = 8
N_EXPERTS = N_GROUPS * EXPERTS_PER_GROUP
TOP_K = 2
D_EXPERT = 512
EPS = 1e-6

LANES = 128
SUBLANES = 8
VMEM_LIMIT = 52 * 1024 * 1024

COL_Z = 0
COL_Q = COL_Z + SSD_DIM
COL_K = COL_Q + FOX_DIM
COL_V = COL_K + FOX_DIM
COL_GA = COL_V + FOX_DIM
COL_GB = COL_GA + D_MODEL
COL_XBC = COL_GB + D_MODEL
PROJ_COLS = COL_XBC + CONV_DIM
SMALL_DT = 0
SMALL_DT_COPIES = 3
SMALL_F = SMALL_DT_COPIES * SSD_HEADS

ROW_BLOCK = 128
HALF = D_MODEL // 2


def _cparams(sem, vmem=VMEM_LIMIT):
    return pltpu.CompilerParams(dimension_semantics=sem, vmem_limit_bytes=vmem)


def _silu(x):
    return x * (1.0 / (1.0 + jnp.exp(-x)))


def _softplus(x):
    return jnp.maximum(x, 0.0) + jnp.log(1.0 + jnp.exp(-jnp.abs(x)))


def _log_sigmoid(x):
    return -_softplus(-x)


def _split3(x):
    hi = x.astype(BF16)
    rest = x - hi.astype(F32)
    mid = rest.astype(BF16)
    lo = (rest - mid.astype(F32)).astype(BF16)
    return hi, mid, lo


def _pack_bf16_pair(x):
    n = x.shape[1] // 2
    lo = pltpu.bitcast(x[:, :n].astype(BF16).astype(F32), U32)
    hi = pltpu.bitcast(x[:, n:].astype(BF16).astype(F32), U32)
    return (hi & jnp.uint32(0xFFFF0000)) | (lo >> 16)


def _unpack_bf16_pair(p):
    lo = pltpu.bitcast(p << 16, F32)
    hi = pltpu.bitcast(p & jnp.uint32(0xFFFF0000), F32)
    return lo, hi


def _norm_small_kernel(x_ref, nw_ref, ws_ref, u_ref, s_ref):
    x = x_ref[...]
    ms = jnp.mean(x * x, axis=-1, keepdims=True)
    u = (x * lax.rsqrt(ms + EPS) * nw_ref[...]).astype(BF16)
    u_ref[...] = u
    s_ref[...] = jnp.dot(u, ws_ref[...], preferred_element_type=F32)


def _norm_small(x2, nw, w_small, tm=512):
    t = x2.shape[0]
    return pl.pallas_call(
        _norm_small_kernel,
        name="norm_small",
        grid=(t // tm,),
        in_specs=[pl.BlockSpec((tm, D_MODEL), lambda i: (i, 0)),
                  pl.BlockSpec((1, D_MODEL), lambda i: (0, 0)),
                  pl.BlockSpec((D_MODEL, LANES), lambda i: (0, 0))],
        out_specs=[pl.BlockSpec((tm, D_MODEL), lambda i: (i, 0)),
                   pl.BlockSpec((tm, LANES), lambda i: (i, 0))],
        out_shape=[jax.ShapeDtypeStruct((t, D_MODEL), BF16),
                   jax.ShapeDtypeStruct((t, LANES), F32)],
        compiler_params=_cparams(("parallel",)),
    )(x2, nw, w_small)


IN_TN = 1024
IN_XPOSE = 256
_IN_SEGMENTS = ((COL_Z, 0, SSD_DIM),
                (COL_Q, SSD_DIM + CONV_DIM + SSD_HEADS, 3 * FOX_DIM),
                (COL_GA, SSD_DIM + CONV_DIM + SSD_HEADS + 3 * FOX_DIM + FOX_HEADS, 2 * D_MODEL),
                (COL_XBC, SSD_DIM, CONV_DIM))


def _in_proj_source_rows():
    src = [0] * (PROJ_COLS // IN_TN)
    for out0, src0, width in _IN_SEGMENTS:
        for c in range(0, width, IN_TN):
            src[(out0 + c) // IN_TN] = src0 + c
    assert all(s % SUBLANES == 0 for s in src)
    return src


def _in_proj_kernel(row_ref, u_ref, wt_hbm, o_ref, st_ref, wbf_ref, sem):
    j = pl.program_id(0)
    i = pl.program_id(1)
    k = wbf_ref.shape[0]

    def window(jj, slot):
        row0 = pl.multiple_of(row_ref[jj], SUBLANES)
        return pltpu.make_async_copy(wt_hbm.at[pl.ds(row0, IN_TN), :], st_ref.at[slot], sem.at[slot])

    @pl.when((j == 0) & (i == 0))
    def _():
        window(0, 0).start()

    @pl.when(i == 0)
    def _():
        @pl.when(j + 1 < pl.num_programs(0))
        def _():
            window(j + 1, (j + 1) % 2).start()

        slot = j % 2
        window(j, slot).wait()
        for c in range(0, k, IN_XPOSE):
            wbf_ref[c:c + IN_XPOSE, :] = jnp.transpose(st_ref[slot, :, c:c + IN_XPOSE]).astype(BF16)

    o_ref[...] = jnp.dot(u_ref[...], wbf_ref[...], preferred_element_type=F32).astype(o_ref.dtype)


def _in_proj(u, w_in_t, tm=1024):
    m, k = u.shape
    grid_spec = pltpu.PrefetchScalarGridSpec(
        num_scalar_prefetch=1,
        grid=(PROJ_COLS // IN_TN, m // tm),
        in_specs=[pl.BlockSpec((tm, k), lambda j, i, rows: (i, 0)),
                  pl.BlockSpec(memory_space=pl.ANY)],
        out_specs=pl.BlockSpec((tm, IN_TN), lambda j, i, rows: (i, j)),
        scratch_shapes=[pltpu.VMEM((2, IN_TN, k), F32),
                        pltpu.VMEM((k, IN_TN), BF16),
                        pltpu.SemaphoreType.DMA((2,))],
    )
    return pl.pallas_call(
        _in_proj_kernel,
        name="in_proj",
        grid_spec=grid_spec,
        out_shape=jax.ShapeDtypeStruct((m, PROJ_COLS), BF16),
        compiler_params=_cparams(("arbitrary", "arbitrary")),
    )(jnp.asarray(_in_proj_source_rows(), I32), u, w_in_t)


HALO = 2 * SUBLANES


def _ssd_kernel(z_ref, xbc_ref, halo_ref, small_ref, cw_ref, cb_ref, dtb_ref, aneg_ref, dexp_ref,
                nw_ref, expand_ref, shift_ref, y_ref, state_ref, ydiag_ref):
    c = pl.program_id(1)
    l = SSD_CHUNK
    n = SSD_STATE

    @pl.when(c == 0)
    def _():
        state_ref[...] = jnp.zeros_like(state_ref)

    cur = xbc_ref[...]
    halo = halo_ref[...]
    halo = jnp.where(c == 0, jnp.zeros_like(halo), halo)
    ext = jnp.concatenate([halo, cur], axis=0)
    shifted = jnp.dot(shift_ref[...], ext, preferred_element_type=F32)
    conv = cb_ref[...] + cw_ref[CONV_WIDTH - 1:CONV_WIDTH, :] * cur.astype(F32)
    for j in range(CONV_WIDTH - 1):
        conv = conv + cw_ref[j:j + 1, :] * shifted[j * l:(j + 1) * l]
    xbc = _silu(conv)
    xs = xbc[:, :SSD_DIM]
    bm = xbc[:, SSD_DIM:SSD_DIM + SSD_GROUPS * n]
    cm = xbc[:, SSD_DIM + SSD_GROUPS * n:]

    h3 = SMALL_DT_COPIES * SSD_HEADS
    dt3 = _softplus(small_ref[:, SMALL_DT:SMALL_DT + h3] + dtb_ref[...])
    adt3 = dt3 * aneg_ref[...]
    row = lax.broadcasted_iota(I32, (l, l), 0)
    col = lax.broadcasted_iota(I32, (l, l), 1)
    causal = col <= row
    tril = jnp.where(causal, 1.0, 0.0).astype(BF16)
    a_cs3 = sum(jnp.dot(tril, piece, preferred_element_type=F32) for piece in _split3(adt3))
    a_cs = a_cs3[:, :SSD_HEADS]
    a_cs_t = jnp.transpose(a_cs)
    a_last3 = a_cs3[l - 1:l, :]

    lane3 = lax.broadcasted_iota(I32, (l, h3), 1)

    def pieces_by_lane_group(x3):
        hi, mid, lo = _split3(x3)
        return jnp.where(lane3 < SSD_HEADS, hi, jnp.where(lane3 < 2 * SSD_HEADS, mid, lo))

    lhs3 = jnp.concatenate([pieces_by_lane_group(dt3),
                            pieces_by_lane_group(jnp.exp(a_cs3)),
                            pieces_by_lane_group(jnp.exp(a_last3 - a_cs3))], axis=0)
    expanded = jnp.dot(lhs3, expand_ref[...], preferred_element_type=F32)
    dt_x = expanded[0:l]
    decay_in = expanded[l:2 * l]
    decay_out = expanded[2 * l:3 * l]
    chunk_decay = decay_in[l - 1:l, :]
    x_dt = xs * dt_x
    x_dt_b = x_dt.astype(BF16)
    xd_b = (x_dt * decay_out).astype(BF16)

    lane = lax.broadcasted_iota(I32, (l, LANES), 1)
    lo_mask = lane < SSD_HEAD_DIM
    heads_per_group = SSD_HEADS // SSD_GROUPS
    gw = heads_per_group * SSD_HEAD_DIM

    for g in range(SSD_GROUPS):
        bg = bm[:, g * n:(g + 1) * n].astype(BF16)
        cg = cm[:, g * n:(g + 1) * n].astype(BF16)
        cb = lax.dot_general(cg, bg, (((1,), (1,)), ((), ())), preferred_element_type=F32)
        for pair in range(heads_per_group // 2):
            h0 = g * heads_per_group + 2 * pair
            lane0 = h0 * SSD_HEAD_DIM
            ms = []
            for h in (h0, h0 + 1):
                seg = a_cs[:, h:h + 1] - a_cs_t[h:h + 1, :]
                lmat = jnp.exp(jnp.where(causal, seg, -jnp.inf))
                ms.append((cb * lmat).astype(BF16))
            lhs = jnp.concatenate(ms, axis=1)
            xp = x_dt_b[:, lane0:lane0 + LANES]
            zero = jnp.zeros_like(xp)
            rhs = jnp.concatenate([jnp.where(lo_mask, xp, zero), jnp.where(lo_mask, zero, xp)], axis=0)
            ydiag_ref[:, lane0:lane0 + LANES] = jnp.dot(lhs, rhs, preferred_element_type=F32)
        st = state_ref[:, g * gw:(g + 1) * gw]
        y_off = jnp.dot(cg, st.astype(BF16), preferred_element_type=F32)
        ydiag_ref[:, g * gw:(g + 1) * gw] += y_off * decay_in[:, g * gw:(g + 1) * gw]
        new = lax.dot_general(bg, xd_b[:, g * gw:(g + 1) * gw], (((0,), (0,)), ((), ())),
                              preferred_element_type=F32)
        state_ref[:, g * gw:(g + 1) * gw] = st * chunk_decay[:, g * gw:(g + 1) * gw] + new

    y = ydiag_ref[...] + dexp_ref[...] * xs
    y = y * _silu(z_ref[...].astype(F32))
    ms = jnp.mean(y * y, axis=-1, keepdims=True)
    y_ref[...] = (y * lax.rsqrt(ms + EPS) * nw_ref[...]).astype(y_ref.dtype)


def _ssd(proj, small, conv_w, conv_b, dt_bias, a_log, d_skip, norm_w, bsz, seq):
    l = SSD_CHUNK
    nc = seq // l
    t = bsz * seq
    h3 = SMALL_DT_COPIES * SSD_HEADS
    aneg = jnp.tile(-jnp.exp(a_log.astype(F32)), SMALL_DT_COPIES).reshape(1, h3)
    dtb = jnp.tile(dt_bias.astype(F32), SMALL_DT_COPIES).reshape(1, h3)
    dexp = jnp.repeat(d_skip.astype(F32), SSD_HEAD_DIM).reshape(1, SSD_DIM)
    expand = jnp.tile(jnp.repeat(jnp.eye(SSD_HEADS, dtype=BF16), SSD_HEAD_DIM, axis=1),
                      (SMALL_DT_COPIES, 1))
    out_row = jnp.arange((CONV_WIDTH - 1) * l)
    src_row = HALO + out_row % l - (CONV_WIDTH - 1) + out_row // l
    shift = (jnp.arange(HALO + l)[None, :] == src_row[:, None]).astype(BF16)
    xbc_blk = COL_XBC // CONV_DIM
    halo_per_chunk = l // HALO

    def row_map(b, c):
        return b * nc + c

    return pl.pallas_call(
        _ssd_kernel,
        name="ssd",
        grid=(bsz, nc),
        in_specs=[
            pl.BlockSpec((l, SSD_DIM), lambda b, c: (row_map(b, c), COL_Z // SSD_DIM)),
            pl.BlockSpec((l, CONV_DIM), lambda b, c: (row_map(b, c), xbc_blk)),
            pl.BlockSpec((HALO, CONV_DIM),
                         lambda b, c: (jnp.maximum(row_map(b, c) * halo_per_chunk - 1, 0), xbc_blk)),
            pl.BlockSpec((l, LANES), lambda b, c: (row_map(b, c), 0)),
            pl.BlockSpec((CONV_WIDTH, CONV_DIM), lambda b, c: (0, 0)),
            pl.BlockSpec((1, CONV_DIM), lambda b, c: (0, 0)),
            pl.BlockSpec((1, h3), lambda b, c: (0, 0)),
            pl.BlockSpec((1, h3), lambda b, c: (0, 0)),
            pl.BlockSpec((1, SSD_DIM), lambda b, c: (0, 0)),
            pl.BlockSpec((1, SSD_DIM), lambda b, c: (0, 0)),
            pl.BlockSpec((h3, SSD_DIM), lambda b, c: (0, 0)),
            pl.BlockSpec(((CONV_WIDTH - 1) * l, HALO + l), lambda b, c: (0, 0)),
        ],
        out_specs=pl.BlockSpec((l, SSD_DIM), lambda b, c: (row_map(b, c), 0)),
        out_shape=jax.ShapeDtypeStruct((t, SSD_DIM), BF16),
        scratch_shapes=[pltpu.VMEM((SSD_STATE, SSD_DIM), F32),
                        pltpu.VMEM((l, SSD_DIM), F32)],
        compiler_params=_cparams(("parallel", "arbitrary")),
    )(proj, proj, proj, small, conv_w.astype(F32), conv_b.astype(F32).reshape(1, CONV_DIM),
      dtb, aneg, dexp, norm_w.astype(F32).reshape(1, SSD_DIM), expand, shift)


CUM_ROWS = 256


def _cum_kernel(small_ref, fb_ref, cum_ref, carry_ref):
    j = pl.program_id(1)

    @pl.when(j == 0)
    def _():
        carry_ref[...] = jnp.zeros_like(carry_ref)

    lf = _log_sigmoid(small_ref[...] + fb_ref[...])
    row = lax.broadcasted_iota(I32, (CUM_ROWS, CUM_ROWS), 0)
    col = lax.broadcasted_iota(I32, (CUM_ROWS, CUM_ROWS), 1)
    tril = jnp.where(col <= row, 1.0, 0.0).astype(F32)
    cs = jnp.dot(tril, lf, preferred_element_type=F32, precision=lax.Precision.HIGHEST) + carry_ref[...]
    cum_ref[...] = cs * LOG2E
    carry_ref[...] = cs[CUM_ROWS - 1:CUM_ROWS, :]


def _fox_cum(small, fox_f_bias, bsz, seq):
    fb = jnp.zeros((1, LANES), F32).at[0, SMALL_F:SMALL_F + FOX_HEADS].set(fox_f_bias.astype(F32))
    nj = seq // CUM_ROWS
    return pl.pallas_call(
        _cum_kernel,
        name="fox_cum",
        grid=(bsz, nj),
        in_specs=[pl.BlockSpec((CUM_ROWS, LANES), lambda b, j: (b * nj + j, 0)),
                  pl.BlockSpec((1, LANES), lambda b, j: (0, 0))],
        out_specs=pl.BlockSpec((CUM_ROWS, LANES), lambda b, j: (b * nj + j, 0)),
        out_shape=jax.ShapeDtypeStruct((bsz * seq, LANES), F32),
        scratch_shapes=[pltpu.VMEM((1, LANES), F32)],
        compiler_params=_cparams(("parallel", "arbitrary")),
    )(small, fb)


FOX_HEADS_PER_STEP = 2
LOG2E = 1.4426950408889634


FOX_SLAB = 128


def _fox_kernel(q_ref, k_ref, v_ref, cq_ref, ck_ref, o_ref, sa_ref, sb_ref, p_ref, m_ref, l_ref, alpha_ref,
                cqrep_ref, acc_ref, *, tq, tk):
    hp = pl.program_id(1)
    qi = pl.program_id(2)
    d = FOX_HEAD_DIM
    c2 = LOG2E / math.sqrt(d)
    lane = lax.broadcasted_iota(I32, (tq, LANES), 1)
    cq_all = cq_ref[...]

    qs, cqs = [], []
    for hh in range(FOX_HEADS_PER_STEP):
        qs.append((q_ref[:, hh * d:(hh + 1) * d].astype(F32) * c2).astype(BF16))
        head_lane = SMALL_F + hp * FOX_HEADS_PER_STEP + hh
        cq_col = jnp.sum(jnp.where(lane == head_lane, cq_all, 0.0), axis=-1, keepdims=True)
        cqs.append(jnp.broadcast_to(cq_col, (tq, LANES)))

    s_slots = (sa_ref, sb_ref)

    def scores(ki, slot):
        row0 = pl.multiple_of(ki * tk, tk)
        for hh in range(FOX_HEADS_PER_STEP):
            k = k_ref[pl.ds(row0, tk), hh * d:(hh + 1) * d]
            s = lax.dot_general(qs[hh], k, (((1,), (1,)), ((), ())), preferred_element_type=F32)
            s_slots[slot][hh] = s - ck_ref[hh, pl.ds(ki, 1), :]

    def update(ki, slot, masked):
        row0 = pl.multiple_of(ki * tk, tk)
        s_ref = s_slots[slot]
        n_ct = tk // LANES
        for hh in range(FOX_HEADS_PER_STEP):
            for rc in range(tq // FOX_SLAB):
                rows = slice(rc * FOX_SLAB, (rc + 1) * FOX_SLAB)
                n_vis = rc + 1 if masked else n_ct

                def slab(ct):
                    x = s_ref[hh, rows, ct * LANES:(ct + 1) * LANES]
                    if masked and ct == rc:
                        row = lax.broadcasted_iota(I32, (FOX_SLAB, LANES), 0)
                        col = lax.broadcasted_iota(I32, (FOX_SLAB, LANES), 1)
                        x = jnp.where(col <= row, x, -jnp.inf)
                    return x

                tmax = slab(0)
                for ct in range(1, n_vis):
                    tmax = jnp.maximum(tmax, slab(ct))
                row_max = jnp.max(tmax, axis=-1, keepdims=True)
                cq = cqrep_ref[hh, rows, :]
                m_prev = m_ref[hh, rows, :]
                m_new = jnp.maximum(m_prev, jnp.broadcast_to(row_max, (FOX_SLAB, LANES)) + cq)
                r = m_new - cq
                alpha = jnp.exp2(m_prev - m_new)
                psum = None
                for ct in range(n_vis):
                    p = jnp.exp2(slab(ct) - r)
                    psum = p if psum is None else psum + p
                    p_ref[hh, rows, ct * LANES:(ct + 1) * LANES] = p.astype(BF16)
                for ct in range(n_vis, n_ct):
                    p_ref[hh, rows, ct * LANES:(ct + 1) * LANES] = jnp.zeros((FOX_SLAB, LANES), BF16)
                l_ref[hh, rows, :] = alpha * l_ref[hh, rows, :] + psum
                m_ref[hh, rows, :] = m_new
                alpha_ref[hh, rows, :] = alpha
            v = v_ref[pl.ds(row0, tk), hh * d:(hh + 1) * d]
            acc_ref[hh] = alpha_ref[hh] * acc_ref[hh] + jnp.dot(p_ref[hh], v, preferred_element_type=F32)

    for hh in range(FOX_HEADS_PER_STEP):
        cqrep_ref[hh] = cqs[hh]

    m_ref[...] = jnp.full_like(m_ref, -jnp.inf)
    l_ref[...] = jnp.zeros_like(l_ref)
    acc_ref[...] = jnp.zeros_like(acc_ref)

    n_full = qi
    scores(0, 0)

    def pair(j, carry):
        b0 = 2 * j
        scores(b0 + 1, 1)
        update(b0, 0, False)
        scores(b0 + 2, 0)
        update(b0 + 1, 1, False)
        return carry

    lax.fori_loop(0, n_full // 2, pair, 0)

    @pl.when(n_full % 2 == 0)
    def _():
        update(n_full, 0, True)

    @pl.when(n_full % 2 == 1)
    def _():
        scores(n_full, 1)
        update(n_full - 1, 0, False)
        update(n_full, 1, True)

    for hh in range(FOX_HEADS_PER_STEP):
        l_fin = jnp.sum(l_ref[hh], axis=-1, keepdims=True)
        o_ref[:, hh * d:(hh + 1) * d] = (acc_ref[hh] / l_fin).astype(o_ref.dtype)


def _fox(proj, cum, bsz, seq, tq=512):
    assert FOX_SLAB == LANES and tq % FOX_SLAB == 0 and seq % tq == 0
    tk = tq
    nq = seq // tq
    nk = seq // tk
    hps = FOX_HEADS_PER_STEP
    cum_row = cum[:, SMALL_F:SMALL_F + FOX_HEADS].reshape(bsz, seq, FOX_HEADS).transpose(0, 2, 1)
    cum_row = cum_row.reshape(bsz, FOX_HEADS, nk, tk)
    t = bsz * seq
    w = FOX_HEADS_PER_STEP * FOX_HEAD_DIM
    n_hp = FOX_HEADS // FOX_HEADS_PER_STEP
    kern = functools.partial(_fox_kernel, tq=tq, tk=tk)
    return pl.pallas_call(
        kern,
        name="fox",
        grid=(bsz, n_hp, nq),
        in_specs=[
            pl.BlockSpec((tq, w), lambda b, hp, qi: (b * nq + qi, COL_Q // w + hp)),
            pl.BlockSpec((seq, w), lambda b, hp, qi: (b, COL_K // w + hp)),
            pl.BlockSpec((seq, w), lambda b, hp, qi: (b, COL_V // w + hp)),
            pl.BlockSpec((tq, LANES), lambda b, hp, qi: (b * nq + qi, 0)),
            pl.BlockSpec((None, FOX_HEADS_PER_STEP, nk, tk), lambda b, hp, qi: (b, hp, 0, 0)),
        ],
        out_specs=pl.BlockSpec((tq, w), lambda b, hp, qi: (b * nq + qi, hp)),
        out_shape=jax.ShapeDtypeStruct((t, FOX_DIM), BF16),
        scratch_shapes=[pltpu.VMEM((hps, tq, tk), F32), pltpu.VMEM((hps, tq, tk), F32),
                        pltpu.VMEM((hps, tq, tk), BF16),
                        pltpu.VMEM((hps, tq, LANES), F32), pltpu.VMEM((hps, tq, LANES), F32),
                        pltpu.VMEM((hps, tq, LANES), F32), pltpu.VMEM((hps, tq, LANES), F32),
                        pltpu.VMEM((hps, tq, FOX_HEAD_DIM), F32)],
        compiler_params=_cparams(("parallel", "parallel", "arbitrary")),
    )(proj, proj, proj, cum, cum_row)


def _mix_kernel(ya_ref, yb_ref, ga_ref, gb_ref, wa_ref, wb_ref, o_ref):
    pa = jnp.dot(ya_ref[...], wa_ref[...], preferred_element_type=F32)
    pb = jnp.dot(yb_ref[...], wb_ref[...], preferred_element_type=F32)
    ga = 1.0 / (1.0 + jnp.exp(-ga_ref[...].astype(F32)))
    gb = 1.0 / (1.0 + jnp.exp(-gb_ref[...].astype(F32)))
    o_ref[...] = (ga * pa + gb * pb).astype(o_ref.dtype)


def _mix(y_a, y_b, proj, w_a, w_b, tm=512, tn=1024):
    t = y_a.shape[0]
    return pl.pallas_call(
        _mix_kernel,
        name="mix",
        grid=(D_MODEL // tn, t // tm),
        in_specs=[
            pl.BlockSpec((tm, SSD_DIM), lambda j, i: (i, 0)),
            pl.BlockSpec((tm, FOX_DIM), lambda j, i: (i, 0)),
            pl.BlockSpec((tm, tn), lambda j, i: (i, COL_GA // tn + j)),
            pl.BlockSpec((tm, tn), lambda j, i: (i, COL_GB // tn + j)),
            pl.BlockSpec((SSD_DIM, tn), lambda j, i: (0, j)),
            pl.BlockSpec((FOX_DIM, tn), lambda j, i: (0, j)),
        ],
        out_specs=pl.BlockSpec((tm, tn), lambda j, i: (i, j)),
        out_shape=jax.ShapeDtypeStruct((t, D_MODEL), BF16),
        compiler_params=_cparams(("parallel", "parallel")),
    )(y_a, y_b, proj, proj, w_a, w_b)


def _outproj_kernel(m_ref, x_ref, wo_ref, nw_ref, wr_ref, br_ref, h_ref, u_ref, eidx_ref, wts_ref):
    h1 = x_ref[...] + jnp.dot(m_ref[...], wo_ref[...], preferred_element_type=F32)
    h_ref[...] = h1
    ms = jnp.mean(h1 * h1, axis=-1, keepdims=True)
    u2 = h1 * lax.rsqrt(ms + EPS) * nw_ref[...]
    u_ref[...] = _pack_bf16_pair(u2)

    u_hi = u2.astype(BF16)
    u_lo = (u2 - u_hi.astype(F32)).astype(BF16)
    hh_hl = jnp.dot(u_hi, wr_ref[...], preferred_element_type=F32)
    lh = jnp.dot(u_lo, wr_ref[:, :LANES], preferred_element_type=F32)
    logits = hh_hl[:, :LANES] + (hh_hl[:, LANES:] + lh) + br_ref[...]
    tm = logits.shape[0]
    lane = lax.broadcasted_iota(I32, (tm, LANES), 1)
    neg = -jnp.inf
    big = jnp.int32(2 * LANES)
    gl = jnp.where(lane < N_GROUPS, logits, neg)
    gmax = jnp.max(gl, axis=-1, keepdims=True)
    gsum = jnp.sum(jnp.exp(gl - gmax), axis=-1, keepdims=True)
    g_p = 1.0 / gsum
    g_idx = jnp.min(jnp.where(gl == gmax, lane, big), axis=-1, keepdims=True)
    e_of_lane = lane - N_GROUPS
    in_grp = (e_of_lane >= g_idx * EXPERTS_PER_GROUP) & (e_of_lane < (g_idx + 1) * EXPERTS_PER_GROUP)
    el = jnp.where(in_grp, logits, neg)
    m1 = jnp.max(el, axis=-1, keepdims=True)
    i1 = jnp.min(jnp.where(el == m1, lane, big), axis=-1, keepdims=True)
    el2 = jnp.where(lane == i1, neg, el)
    m2 = jnp.max(el2, axis=-1, keepdims=True)
    i2 = jnp.min(jnp.where(el2 == m2, lane, big), axis=-1, keepdims=True)
    esum = jnp.sum(jnp.exp(el - m1), axis=-1, keepdims=True)
    p1 = 1.0 / esum
    p2 = jnp.exp(m2 - m1) / esum
    w1 = g_p * (p1 / (p1 + p2))
    w2 = g_p * (p2 / (p1 + p2))
    eidx_ref[...] = jnp.where(lane == 0, i1 - N_GROUPS, jnp.where(lane == 1, i2 - N_GROUPS, 0))
    wts_ref[...] = jnp.where(lane == 0, w1, jnp.where(lane == 1, w2, 0.0))


def _outproj(mixed, x2, w_o, nw, w_router, b_router, tm=512):
    t = mixed.shape[0]
    resident = pl.Buffered(1)
    return pl.pallas_call(
        _outproj_kernel,
        name="outproj",
        grid=(t // tm,),
        in_specs=[
            pl.BlockSpec((tm, D_MODEL), lambda i: (i, 0)),
            pl.BlockSpec((tm, D_MODEL), lambda i: (i, 0)),
            pl.BlockSpec((D_MODEL, D_MODEL), lambda i: (0, 0), pipeline_mode=resident),
            pl.BlockSpec((1, D_MODEL), lambda i: (0, 0)),
            pl.BlockSpec((D_MODEL, 2 * LANES), lambda i: (0, 0), pipeline_mode=resident),
            pl.BlockSpec((1, LANES), lambda i: (0, 0)),
        ],
        out_specs=[
            pl.BlockSpec((tm, D_MODEL), lambda i: (i, 0)),
            pl.BlockSpec((tm, HALF), lambda i: (i, 0)),
            pl.BlockSpec((tm, LANES), lambda i: (i, 0)),
            pl.BlockSpec((tm, LANES), lambda i: (i, 0)),
        ],
        out_shape=[
            jax.ShapeDtypeStruct((t, D_MODEL), F32),
            jax.ShapeDtypeStruct((t, HALF), U32),
            jax.ShapeDtypeStruct((t, LANES), I32),
            jax.ShapeDtypeStruct((t, LANES), F32),
        ],
        compiler_params=_cparams(("parallel",)),
    )(mixed, x2, w_o, nw, w_router, b_router)


RANK_BLOCK = 512


def _rank_kernel(e_ref, rank_ref, cnt_ref, carry_ref):
    i = pl.program_id(0)
    r = RANK_BLOCK

    @pl.when(i == 0)
    def _():
        carry_ref[...] = jnp.zeros_like(carry_ref)

    e = e_ref[0]
    expert = lax.broadcasted_iota(I32, (N_EXPERTS, r), 0)
    onehot = jnp.where(expert == e, 1.0, 0.0).astype(F32)
    jrow = lax.broadcasted_iota(I32, (r, r), 0)
    jcol = lax.broadcasted_iota(I32, (r, r), 1)
    before = jnp.where(jrow < jcol, 1.0, 0.0).astype(BF16)
    cum = jnp.dot(onehot.astype(BF16), before, preferred_element_type=F32)
    carry = carry_ref[...]
    rank = jnp.sum(onehot * (cum + carry[:, 0:1]), axis=0, keepdims=True)
    rank_ref[0] = rank.astype(I32)
    carry = carry + jnp.sum(onehot, axis=1, keepdims=True)
    carry_ref[...] = carry
    cnt_ref[...] = carry


def _rank(e_blocks):
    nb = e_blocks.shape[0]
    return pl.pallas_call(
        _rank_kernel,
        name="rank",
        grid=(nb,),
        in_specs=[pl.BlockSpec((1, 1, RANK_BLOCK), lambda i: (i, 0, 0))],
        out_specs=[pl.BlockSpec((1, 1, RANK_BLOCK), lambda i: (i, 0, 0)),
                   pl.BlockSpec((N_EXPERTS, LANES), lambda i: (0, 0))],
        out_shape=[jax.ShapeDtypeStruct((nb, 1, RANK_BLOCK), I32),
                   jax.ShapeDtypeStruct((N_EXPERTS, LANES), F32)],
        scratch_shapes=[pltpu.VMEM((N_EXPERTS, LANES), F32)],
        compiler_params=_cparams(("arbitrary",)),
    )(e_blocks)


def _dest_kernel(e_ref, rank_ref, pstart_ref, dest_ref):
    e = e_ref[0]
    expert = lax.broadcasted_iota(I32, (N_EXPERTS, RANK_BLOCK), 0)
    start = jnp.sum(jnp.where(expert == e, pstart_ref[:, 0:1], 0), axis=0, keepdims=True)
    dest_ref[0] = rank_ref[0] + start


def _dest(e_blocks, rank, pstart):
    nb = e_blocks.shape[0]
    return pl.pallas_call(
        _dest_kernel,
        name="dest",
        grid=(nb,),
        in_specs=[pl.BlockSpec((1, 1, RANK_BLOCK), lambda i: (i, 0, 0)),
                  pl.BlockSpec((1, 1, RANK_BLOCK), lambda i: (i, 0, 0)),
                  pl.BlockSpec((N_EXPERTS, LANES), lambda i: (0, 0))],
        out_specs=pl.BlockSpec((1, 1, RANK_BLOCK), lambda i: (i, 0, 0)),
        out_shape=jax.ShapeDtypeStruct((nb, 1, RANK_BLOCK), I32),
        compiler_params=_cparams(("parallel",)),
    )(e_blocks, rank, pstart)


def _row_copy(src_ref, src_row, dst_ref, dst_row, sem):
    return pltpu.make_async_copy(src_ref.at[pl.ds(src_row, 1)], dst_ref.at[pl.ds(dst_row, 1)], sem)


DMA_UNROLL = 16


def _dispatch_kernel(last_blk_ref, nused_ref, dest_ref, u_ref, xs_ref, zbuf_ref, sem, zsem):
    i = pl.program_id(0)
    n_total = xs_ref.shape[0] // ROW_BLOCK

    @pl.when(i == 0)
    def _():
        zbuf_ref[...] = jnp.zeros_like(zbuf_ref)

        def zero_block(blk):
            return pltpu.make_async_copy(zbuf_ref, xs_ref.at[pl.ds(blk * ROW_BLOCK, ROW_BLOCK)], zsem)

        def fill_expert(e, count):
            blk = last_blk_ref[e]

            @pl.when(blk >= 0)
            def _():
                zero_block(blk).start()

            return count + jnp.where(blk >= 0, 1, 0)

        n_fill = lax.fori_loop(0, N_EXPERTS, fill_expert, 0)

        def fill_tail(blk, carry):
            zero_block(blk).start()
            return carry

        lax.fori_loop(nused_ref[0], n_total, fill_tail, 0)

        def drain(k, carry):
            zero_block(0).wait()
            return carry

        lax.fori_loop(0, n_fill + n_total - nused_ref[0], drain, 0)

    def issue(g, carry):
        tok = g * (DMA_UNROLL // TOP_K)
        c0 = g * DMA_UNROLL
        for j in range(DMA_UNROLL):
            _row_copy(u_ref, tok + j // TOP_K, xs_ref, dest_ref[0, 0, c0 + j], sem).start()
        return carry

    lax.fori_loop(0, RANK_BLOCK // DMA_UNROLL, issue, 0)
    pltpu.make_async_copy(xs_ref.at[pl.ds(0, RANK_BLOCK)], xs_ref.at[pl.ds(0, RANK_BLOCK)], sem).wait()


def _dispatch(dest, u2p, n_rows, last_blk, n_used):
    nb = dest.shape[0]
    grid_spec = pltpu.PrefetchScalarGridSpec(
        num_scalar_prefetch=2,
        grid=(nb,),
        in_specs=[pl.BlockSpec((1, 1, RANK_BLOCK), lambda i, lb, nu: (i, 0, 0), memory_space=pltpu.SMEM),
                  pl.BlockSpec((RANK_BLOCK // TOP_K, HALF), lambda i, lb, nu: (i, 0))],
        out_specs=pl.BlockSpec(memory_space=pl.ANY),
        scratch_shapes=[pltpu.VMEM((ROW_BLOCK, HALF), U32),
                        pltpu.SemaphoreType.DMA(()),
                        pltpu.SemaphoreType.DMA(())],
    )
    return pl.pallas_call(
        _dispatch_kernel,
        name="dispatch",
        grid_spec=grid_spec,
        out_shape=jax.ShapeDtypeStruct((n_rows, HALF), U32),
        compiler_params=_cparams(("arbitrary",)),
    )(last_blk, n_used, dest, u2p)


ROW_DMA_PRIORITY = 1
ROW_RING = 8


def _experts_kernel(bstart_ref, nblk_ref, xs_ref, wg_ref, wu_ref, wd_ref, y_ref,
                    wgb_ref, wub_ref, wdb_ref, xbuf_ref, ybuf_ref, xsem, ysem):
    e = pl.program_id(0)
    n_e = pl.num_programs(0)
    g0 = bstart_ref[e]
    nb = nblk_ref[e]
    n_used = bstart_ref[n_e - 1] + nblk_ref[n_e - 1]
    n_total = y_ref.shape[0] // ROW_BLOCK

    def x_copy(g, slot):
        return pltpu.make_async_copy(xs_ref.at[pl.ds(g * ROW_BLOCK, ROW_BLOCK)], xbuf_ref.at[slot], xsem.at[slot])

    def y_copy(g, slot):
        return pltpu.make_async_copy(ybuf_ref.at[slot], y_ref.at[pl.ds(g * ROW_BLOCK, ROW_BLOCK)], ysem.at[slot])

    @pl.when(e == 0)
    def _():
        for g in range(ROW_RING - 1):
            @pl.when(g < n_used)
            def _(g=g):
                x_copy(g, g).start(priority=ROW_DMA_PRIORITY)

    @pl.when(nb > 0)
    def _():
        wgb_ref[...] = wg_ref[0].astype(BF16)
        wub_ref[...] = wu_ref[0].astype(BF16)
        wdb_ref[...] = wd_ref[0].astype(BF16)

    def block(j, carry):
        g = g0 + j
        slot = g % ROW_RING
        x_copy(g, slot).wait()

        @pl.when(g + ROW_RING - 1 < n_used)
        def _():
            x_copy(g + ROW_RING - 1, (g + ROW_RING - 1) % ROW_RING).start(priority=ROW_DMA_PRIORITY)

        @pl.when(g >= ROW_RING)
        def _():
            y_copy(g - ROW_RING, slot).wait()

        lo, hi = _unpack_bf16_pair(xbuf_ref[slot])
        lo = lo.astype(BF16)
        hi = hi.astype(BF16)
        gate = (jnp.dot(lo, wgb_ref[:HALF, :], preferred_element_type=F32)
                + jnp.dot(hi, wgb_ref[HALF:, :], preferred_element_type=F32))
        up = (jnp.dot(lo, wub_ref[:HALF, :], preferred_element_type=F32)
              + jnp.dot(hi, wub_ref[HALF:, :], preferred_element_type=F32))
        hdn = (_silu(gate) * up).astype(BF16)
        y = jnp.dot(hdn, wdb_ref[...], preferred_element_type=F32)
        ybuf_ref[slot] = _pack_bf16_pair(y)
        y_copy(g, slot).start(priority=ROW_DMA_PRIORITY)
        return carry

    lax.fori_loop(0, nb, block, 0)

    @pl.when(e == n_e - 1)
    def _():
        for back in range(ROW_RING, 0, -1):
            @pl.when(n_used >= back)
            def _(back=back):
                y_copy(n_used - back, (n_used - back) % ROW_RING).wait()

        ybuf_ref[0] = jnp.zeros((ROW_BLOCK, HALF), U32)

        def fill(g, carry):
            y_copy(g, 0).start()
            y_copy(g, 0).wait()
            return carry

        lax.fori_loop(n_used, n_total, fill, 0)


def _experts(block_start, block_count, xs, w_gate, w_up, w_down):
    n_rows = xs.shape[0]

    def wmap(e, bs, bc):
        return (e, 0, 0)

    grid_spec = pltpu.PrefetchScalarGridSpec(
        num_scalar_prefetch=2,
        grid=(N_EXPERTS,),
        in_specs=[pl.BlockSpec(memory_space=pl.ANY),
                  pl.BlockSpec((1, D_MODEL, D_EXPERT), wmap),
                  pl.BlockSpec((1, D_MODEL, D_EXPERT), wmap),
                  pl.BlockSpec((1, D_EXPERT, D_MODEL), wmap)],
        out_specs=pl.BlockSpec(memory_space=pl.ANY),
        scratch_shapes=[pltpu.VMEM((D_MODEL, D_EXPERT), BF16),
                        pltpu.VMEM((D_MODEL, D_EXPERT), BF16),
                        pltpu.VMEM((D_EXPERT, D_MODEL), BF16),
                        pltpu.VMEM((ROW_RING, ROW_BLOCK, HALF), U32),
                        pltpu.VMEM((ROW_RING, ROW_BLOCK, HALF), U32),
                        pltpu.SemaphoreType.DMA((ROW_RING,)),
                        pltpu.SemaphoreType.DMA((ROW_RING,))],
    )
    return pl.pallas_call(
        _experts_kernel,
        name="experts",
        grid_spec=grid_spec,
        out_shape=jax.ShapeDtypeStruct((n_rows, HALF), U32),
        compiler_params=_cparams(("arbitrary",)),
    )(block_start, block_count, xs, w_gate, w_up, w_down)


COMBINE_TOKENS = RANK_BLOCK // TOP_K


def _combine_kernel(dest_ref, dest_next_ref, h_ref, wts_ref, nw_ref, y_ref, o_ref, buf_ref, sem):
    ts = COMBINE_TOKENS
    i = pl.program_id(0)
    slot = i % 2

    def gather(idx_ref, dst_slot):
        def issue(g, carry):
            tok = g * (DMA_UNROLL // TOP_K)
            c0 = g * DMA_UNROLL
            for j in range(DMA_UNROLL):
                _row_copy(y_ref, idx_ref[0, 0, c0 + j], buf_ref.at[dst_slot],
                          (j % TOP_K) * ts + tok + j // TOP_K, sem.at[dst_slot]).start()
            return carry

        lax.fori_loop(0, RANK_BLOCK // DMA_UNROLL, issue, 0)

    @pl.when(i == 0)
    def _():
        gather(dest_ref, 0)

    @pl.when(i + 1 < pl.num_programs(0))
    def _():
        gather(dest_next_ref, 1 - slot)

    pltpu.make_async_copy(y_ref.at[pl.ds(0, RANK_BLOCK)], buf_ref.at[slot], sem.at[slot]).wait()

    w = wts_ref[...]
    w0 = w[:, 0:1]
    w1 = w[:, 1:2]
    lo0, hi0 = _unpack_bf16_pair(buf_ref[slot, 0:ts, :])
    lo1, hi1 = _unpack_bf16_pair(buf_ref[slot, ts:2 * ts, :])
    h = h_ref[...]
    out_lo = h[:, :HALF] + w0 * lo0 + w1 * lo1
    out_hi = h[:, HALF:] + w0 * hi0 + w1 * hi1
    ms = (jnp.sum(out_lo * out_lo, axis=-1, keepdims=True)
          + jnp.sum(out_hi * out_hi, axis=-1, keepdims=True)) * (1.0 / D_MODEL)
    inv = lax.rsqrt(ms + EPS)
    o_ref[:, :HALF] = out_lo * inv * nw_ref[:, :HALF]
    o_ref[:, HALF:] = out_hi * inv * nw_ref[:, HALF:]


def _combine(dest, h1, wts, nw, y):
    t = h1.shape[0]
    ts = COMBINE_TOKENS
    n_steps = t // ts
    return pl.pallas_call(
        _combine_kernel,
        name="combine",
        grid=(n_steps,),
        in_specs=[pl.BlockSpec((1, 1, RANK_BLOCK), lambda i: (i, 0, 0), memory_space=pltpu.SMEM),
                  pl.BlockSpec((1, 1, RANK_BLOCK), lambda i: (jnp.minimum(i + 1, n_steps - 1), 0, 0),
                               memory_space=pltpu.SMEM),
                  pl.BlockSpec((ts, D_MODEL), lambda i: (i, 0)),
                  pl.BlockSpec((ts, LANES), lambda i: (i, 0)),
                  pl.BlockSpec((1, D_MODEL), lambda i: (0, 0)),
                  pl.BlockSpec(memory_space=pl.ANY)],
        out_specs=pl.BlockSpec((ts, D_MODEL), lambda i: (i, 0)),
        out_shape=jax.ShapeDtypeStruct((t, D_MODEL), F32),
        scratch_shapes=[pltpu.VMEM((2, RANK_BLOCK, HALF), U32), pltpu.SemaphoreType.DMA((2,))],
        compiler_params=_cparams(("arbitrary",)),
    )(dest, dest, h1, wts, nw, y)


def _narrow_w_in(w_in_t):
    dt0 = SSD_DIM + CONV_DIM
    f0 = dt0 + SSD_HEADS + 3 * FOX_DIM
    dt = w_in_t[dt0:dt0 + SSD_HEADS]
    f = w_in_t[f0:f0 + FOX_HEADS]
    pad = jnp.zeros((LANES - SMALL_F - FOX_HEADS, w_in_t.shape[1]), w_in_t.dtype)
    return jnp.concatenate([dt] * SMALL_DT_COPIES + [f, pad], axis=0).T.astype(BF16)


def _layer(h, p, bsz, seq):
    t = bsz * seq
    w_in_t = jnp.swapaxes(p["w_in"].astype(F32), 0, 1)
    u, small = _norm_small(h, p["norm_mix_w"].astype(F32).reshape(1, D_MODEL), _narrow_w_in(w_in_t))
    proj = _in_proj(u, w_in_t)
    y_ssd = _ssd(proj, small, p["conv_w"], p["conv_b"], p["dt_bias"], p["a_log"], p["d_skip"],
                 p["ssd_norm_w"], bsz, seq)
    cum = _fox_cum(small, p["fox_f_bias"], bsz, seq)
    y_fox = _fox(proj, cum, bsz, seq)
    mixed = _mix(y_ssd, y_fox, proj, p["w_proj_ssd"].astype(BF16), p["w_proj_fox"].astype(BF16))

    w_router = jnp.concatenate(
        [p["w_router_group"], p["w_router_expert"],
         jnp.zeros((D_MODEL, LANES - N_GROUPS - N_EXPERTS), F32)], axis=1).astype(F32)
    w_router_hi = w_router.astype(BF16)
    w_router_lo = (w_router - w_router_hi.astype(F32)).astype(BF16)
    w_router = jnp.concatenate([w_router_hi, w_router_lo], axis=1)
    b_router = jnp.concatenate(
        [p["b_router_group"], p["b_router_expert"],
         jnp.zeros((LANES - N_GROUPS - N_EXPERTS,), F32)]).astype(F32).reshape(1, LANES)
    h1, u2p, eidx, wts = _outproj(mixed, h, p["w_out"].astype(BF16),
                                  p["norm_moe_w"].astype(F32).reshape(1, D_MODEL), w_router, b_router)

    tk = t * TOP_K
    e_blocks = eidx[:, :TOP_K].reshape(tk // RANK_BLOCK, 1, RANK_BLOCK)
    rank, counts = _rank(e_blocks)
    counts = counts[:, 0].astype(I32)
    padded = ((counts + ROW_BLOCK - 1) // ROW_BLOCK) * ROW_BLOCK
    pend = jnp.cumsum(padded)
    pstart = pend - padded
    n_blocks = tk // ROW_BLOCK + N_EXPERTS
    dest = _dest(e_blocks, rank, jnp.broadcast_to(pstart[:, None], (N_EXPERTS, LANES)).astype(I32))
    block_start = (pstart // ROW_BLOCK).astype(I32)
    block_count = (padded // ROW_BLOCK).astype(I32)
    last_blk = jnp.where(block_count > 0, block_start + block_count - 1, -1).astype(I32)
    n_used = (pend[-1:] // ROW_BLOCK).astype(I32)
    xs = _dispatch(dest, u2p, n_blocks * ROW_BLOCK, last_blk, n_used)
    y = _experts(block_start, block_count, xs, p["w_gate_exp"], p["w_up_exp"], p["w_down_exp"])
    return dest, h1, wts, y


def kernel(x, norm_mix_w, w_in, conv_w, conv_b, dt_bias, a_log, d_skip, ssd_norm_w, fox_f_bias, w_proj_ssd,
           w_proj_fox, w_out, norm_moe_w, w_router_group, b_router_group, w_router_expert, b_router_expert,
           w_gate_exp, w_up_exp, w_down_exp, norm_final_w):
    bsz, seq, _ = x.shape
    depth = w_in.shape[0]
    assert depth == 1, "the fused final norm assumes a single layer"
    stacked = dict(norm_mix_w=norm_mix_w, w_in=w_in, conv_w=conv_w, conv_b=conv_b, dt_bias=dt_bias, a_log=a_log,
                   d_skip=d_skip, ssd_norm_w=ssd_norm_w, fox_f_bias=fox_f_bias, w_proj_ssd=w_proj_ssd,
                   w_proj_fox=w_proj_fox, w_out=w_out, norm_moe_w=norm_moe_w, w_router_group=w_router_group,
                   b_router_group=b_router_group, w_router_expert=w_router_expert,
                   b_router_expert=b_router_expert, w_gate_exp=w_gate_exp, w_up_exp=w_up_exp,
                   w_down_exp=w_down_exp)
    p = {name: v[0] for name, v in stacked.items()}
    h = x.reshape(bsz * seq, D_MODEL)
    dest, h1, wts, y = _layer(h, p, bsz, seq)
    out = _combine(dest, h1, wts, norm_final_w.astype(F32).reshape(1, D_MODEL), y)
    return out.reshape(bsz, seq, D_MODEL)
```

```python
import functools
import math

import jax
import jax.numpy as jnp
from jax import lax
from jax.experimental import pallas as pl
from jax.experimental.pallas import tpu as pltpu

F32 = jnp.float32
BF16 = jnp.bfloat16
I32 = jnp.int32
U32 = jnp.uint32

D_MODEL = 2048
SSD_HEADS = 32
SSD_HEAD_DIM = 64
SSD_DIM = SSD_HEADS * SSD_HEAD_DIM
SSD_GROUPS = 4
SSD_STATE = 128
SSD_CHUNK = 128
CONV_WIDTH = 4
CONV_DIM = SSD_DIM + 2 * SSD_GROUPS * SSD_STATE
FOX_HEADS = 16
FOX_HEAD_DIM = 128
FOX_DIM = FOX_HEADS * FOX_HEAD_DIM
N_GROUPS = 8
EXPERTS_PER_GROUP = 8
N_EXPERTS = N_GROUPS * EXPERTS_PER_GROUP
TOP_K = 2
D_EXPERT = 512
EPS = 1e-6

LANES = 128
SUBLANES = 8
VMEM_LIMIT = 52 * 1024 * 1024

COL_Z = 0
COL_Q = COL_Z + SSD_DIM
COL_K = COL_Q + FOX_DIM
COL_V = COL_K + FOX_DIM
COL_GA = COL_V + FOX_DIM
COL_GB = COL_GA + D_MODEL
COL_XBC = COL_GB + D_MODEL
PROJ_COLS = COL_XBC + CONV_DIM
SMALL_DT = 0
SMALL_DT_COPIES = 3
SMALL_F = SMALL_DT_COPIES * SSD_HEADS

ROW_BLOCK = 128
HALF = D_MODEL // 2


def _cparams(sem, vmem=VMEM_LIMIT):
    return pltpu.CompilerParams(dimension_semantics=sem, vmem_limit_bytes=vmem)


def _silu(x):
    return x * (1.0 / (1.0 + jnp.exp(-x)))


def _softplus(x):
    return jnp.maximum(x, 0.0) + jnp.log(1.0 + jnp.exp(-jnp.abs(x)))


def _log_sigmoid(x):
    return -_softplus(-x)


def _split3(x):
    hi = x.astype(BF16)
    rest = x - hi.astype(F32)
    mid = rest.astype(BF16)
    lo = (rest - mid.astype(F32)).astype(BF16)
    return hi, mid, lo


def _pack_bf16_pair(x):
    n = x.shape[1] // 2
    lo = pltpu.bitcast(x[:, :n].astype(BF16).astype(F32), U32)
    hi = pltpu.bitcast(x[:, n:].astype(BF16).astype(F32), U32)
    return (hi & jnp.uint32(0xFFFF0000)) | (lo >> 16)


def _unpack_bf16_pair(p):
    lo = pltpu.bitcast(p << 16, F32)
    hi = pltpu.bitcast(p & jnp.uint32(0xFFFF0000), F32)
    return lo, hi


def _norm_small_kernel(x_ref, nw_ref, ws_ref, u_ref, s_ref):
    x = x_ref[...]
    ms = jnp.mean(x * x, axis=-1, keepdims=True)
    u = (x * lax.rsqrt(ms + EPS) * nw_ref[...]).astype(BF16)
    u_ref[...] = u
    s_ref[...] = jnp.dot(u, ws_ref[...], preferred_element_type=F32)


def _norm_small(x2, nw, w_small, tm=512):
    t = x2.shape[0]
    return pl.pallas_call(
        _norm_small_kernel,
        name="norm_small",
        grid=(t // tm,),
        in_specs=[pl.BlockSpec((tm, D_MODEL), lambda i: (i, 0)),
                  pl.BlockSpec((1, D_MODEL), lambda i: (0, 0)),
                  pl.BlockSpec((D_MODEL, LANES), lambda i: (0, 0))],
        out_specs=[pl.BlockSpec((tm, D_MODEL), lambda i: (i, 0)),
                   pl.BlockSpec((tm, LANES), lambda i: (i, 0))],
        out_shape=[jax.ShapeDtypeStruct((t, D_MODEL), BF16),
                   jax.ShapeDtypeStruct((t, LANES), F32)],
        compiler_params=_cparams(("parallel",)),
    )(x2, nw, w_small)


IN_TN = 1024
IN_XPOSE = 256
_IN_SEGMENTS = ((COL_Z, 0, SSD_DIM),
                (COL_Q, SSD_DIM + CONV_DIM + SSD_HEADS, 3 * FOX_DIM),
                (COL_GA, SSD_DIM + CONV_DIM + SSD_HEADS + 3 * FOX_DIM + FOX_HEADS, 2 * D_MODEL),
                (COL_XBC, SSD_DIM, CONV_DIM))


def _in_proj_source_rows():
    src = [0] * (PROJ_COLS // IN_TN)
    for out0, src0, width in _IN_SEGMENTS:
        for c in range(0, width, IN_TN):
            src[(out0 + c) // IN_TN] = src0 + c
    assert all(s % SUBLANES == 0 for s in src)
    return src


def _in_proj_kernel(row_ref, u_ref, wt_hbm, o_ref, st_ref, wbf_ref, sem):
    j = pl.program_id(0)
    i = pl.program_id(1)
    k = wbf_ref.shape[0]

    def window(jj, slot):
        row0 = pl.multiple_of(row_ref[jj], SUBLANES)
        return pltpu.make_async_copy(wt_hbm.at[pl.ds(row0, IN_TN), :], st_ref.at[slot], sem.at[slot])

    @pl.when((j == 0) & (i == 0))
    def _():
        window(0, 0).start()

    @pl.when(i == 0)
    def _():
        @pl.when(j + 1 < pl.num_programs(0))
        def _():
            window(j + 1, (j + 1) % 2).start()

        slot = j % 2
        window(j, slot).wait()
        for c in range(0, k, IN_XPOSE):
            wbf_ref[c:c + IN_XPOSE, :] = jnp.transpose(st_ref[slot, :, c:c + IN_XPOSE]).astype(BF16)

    o_ref[...] = jnp.dot(u_ref[...], wbf_ref[...], preferred_element_type=F32).astype(o_ref.dtype)


def _in_proj(u, w_in_t, tm=1024):
    m, k = u.shape
    grid_spec = pltpu.PrefetchScalarGridSpec(
        num_scalar_prefetch=1,
        grid=(PROJ_COLS // IN_TN, m // tm),
        in_specs=[pl.BlockSpec((tm, k), lambda j, i, rows: (i, 0)),
                  pl.BlockSpec(memory_space=pl.ANY)],
        out_specs=pl.BlockSpec((tm, IN_TN), lambda j, i, rows: (i, j)),
        scratch_shapes=[pltpu.VMEM((2, IN_TN, k), F32),
                        pltpu.VMEM((k, IN_TN), BF16),
                        pltpu.SemaphoreType.DMA((2,))],
    )
    return pl.pallas_call(
        _in_proj_kernel,
        name="in_proj",
        grid_spec=grid_spec,
        out_shape=jax.ShapeDtypeStruct((m, PROJ_COLS), BF16),
        compiler_params=_cparams(("arbitrary", "arbitrary")),
    )(jnp.asarray(_in_proj_source_rows(), I32), u, w_in_t)


HALO = 2 * SUBLANES


def _ssd_kernel(z_ref, xbc_ref, halo_ref, small_ref, cw_ref, cb_ref, dtb_ref, aneg_ref, dexp_ref,
                nw_ref, expand_ref, shift_ref, y_ref, state_ref, ydiag_ref):
    c = pl.program_id(1)
    l = SSD_CHUNK
    n = SSD_STATE

    @pl.when(c == 0)
    def _():
        state_ref[...] = jnp.zeros_like(state_ref)

    cur = xbc_ref[...]
    halo = halo_ref[...]
    halo = jnp.where(c == 0, jnp.zeros_like(halo), halo)
    ext = jnp.concatenate([halo, cur], axis=0)
    shifted = jnp.dot(shift_ref[...], ext, preferred_element_type=F32)
    conv = cb_ref[...] + cw_ref[CONV_WIDTH - 1:CONV_WIDTH, :] * cur.astype(F32)
    for j in range(CONV_WIDTH - 1):
        conv = conv + cw_ref[j:j + 1, :] * shifted[j * l:(j + 1) * l]
    xbc = _silu(conv)
    xs = xbc[:, :SSD_DIM]
    bm = xbc[:, SSD_DIM:SSD_DIM + SSD_GROUPS * n]
    cm = xbc[:, SSD_DIM + SSD_GROUPS * n:]

    h3 = SMALL_DT_COPIES * SSD_HEADS
    dt3 = _softplus(small_ref[:, SMALL_DT:SMALL_DT + h3] + dtb_ref[...])
    adt3 = dt3 * aneg_ref[...]
    row = lax.broadcasted_iota(I32, (l, l), 0)
    col = lax.broadcasted_iota(I32, (l, l), 1)
    causal = col <= row
    tril = jnp.where(causal, 1.0, 0.0).astype(BF16)
    a_cs3 = sum(jnp.dot(tril, piece, preferred_element_type=F32) for piece in _split3(adt3))
    a_cs = a_cs3[:, :SSD_HEADS]
    a_cs_t = jnp.transpose(a_cs)
    a_last3 = a_cs3[l - 1:l, :]

    lane3 = lax.broadcasted_iota(I32, (l, h3), 1)

    def pieces_by_lane_group(x3):
        hi, mid, lo = _split3(x3)
        return jnp.where(lane3 < SSD_HEADS, hi, jnp.where(lane3 < 2 * SSD_HEADS, mid, lo))

    lhs3 = jnp.concatenate([pieces_by_lane_group(dt3),
                            pieces_by_lane_group(jnp.exp(a_cs3)),
                            pieces_by_lane_group(jnp.exp(a_last3 - a_cs3))], axis=0)
    expanded = jnp.dot(lhs3, expand_ref[...], preferred_element_type=F32)
    dt_x = expanded[0:l]
    decay_in = expanded[l:2 * l]
    decay_out = expanded[2 * l:3 * l]
    chunk_decay = decay_in[l - 1:l, :]
    x_dt = xs * dt_x
    x_dt_b = x_dt.astype(BF16)
    xd_b = (x_dt * decay_out).astype(BF16)

    lane = lax.broadcasted_iota(I32, (l, LANES), 1)
    lo_mask = lane < SSD_HEAD_DIM
    heads_per_group = SSD_HEADS // SSD_GROUPS
    gw = heads_per_group * SSD_HEAD_DIM

    for g in range(SSD_GROUPS):
        bg = bm[:, g * n:(g + 1) * n].astype(BF16)
        cg = cm[:, g * n:(g + 1) * n].astype(BF16)
        cb = lax.dot_general(cg, bg, (((1,), (1,)), ((), ())), preferred_element_type=F32)
        for pair in range(heads_per_group // 2):
            h0 = g * heads_per_group + 2 * pair
            lane0 = h0 * SSD_HEAD_DIM
            ms = []
            for h in (h0, h0 + 1):
                seg = a_cs[:, h:h + 1] - a_cs_t[h:h + 1, :]
                lmat = jnp.exp(jnp.where(causal, seg, -jnp.inf))
                ms.append((cb * lmat).astype(BF16))
            lhs = jnp.concatenate(ms, axis=1)
            xp = x_dt_b[:, lane0:lane0 + LANES]
            zero = jnp.zeros_like(xp)
            rhs = jnp.concatenate([jnp.where(lo_mask, xp, zero), jnp.where(lo_mask, zero, xp)], axis=0)
            ydiag_ref[:, lane0:lane0 + LANES] = jnp.dot(lhs, rhs, preferred_element_type=F32)
        st = state_ref[:, g * gw:(g + 1) * gw]
        y_off = jnp.dot(cg, st.astype(BF16), preferred_element_type=F32)
        ydiag_ref[:, g * gw:(g + 1) * gw] += y_off * decay_in[:, g * gw:(g + 1) * gw]
        new = lax.dot_general(bg, xd_b[:, g * gw:(g + 1) * gw], (((0,), (0,)), ((), ())),
                              preferred_element_type=F32)
        state_ref[:, g * gw:(g + 1) * gw] = st * chunk_decay[:, g * gw:(g + 1) * gw] + new

    y = ydiag_ref[...] + dexp_ref[...] * xs
    y = y * _silu(z_ref[...].astype(F32))
    ms = jnp.mean(y * y, axis=-1, keepdims=True)
    y_ref[...] = (y * lax.rsqrt(ms + EPS) * nw_ref[...]).astype(y_ref.dtype)


def _ssd(proj, small, conv_w, conv_b, dt_bias, a_log, d_skip, norm_w, bsz, seq):
    l = SSD_CHUNK
    nc = seq // l
    t = bsz * seq
    h3 = SMALL_DT_COPIES * SSD_HEADS
    aneg = jnp.tile(-jnp.exp(a_log.astype(F32)), SMALL_DT_COPIES).reshape(1, h3)
    dtb = jnp.tile(dt_bias.astype(F32), SMALL_DT_COPIES).reshape(1, h3)
    dexp = jnp.repeat(d_skip.astype(F32), SSD_HEAD_DIM).reshape(1, SSD_DIM)
    expand = jnp.tile(jnp.repeat(jnp.eye(SSD_HEADS, dtype=BF16), SSD_HEAD_DIM, axis=1),
                      (SMALL_DT_COPIES, 1))
    out_row = jnp.arange((CONV_WIDTH - 1) * l)
    src_row = HALO + out_row % l - (CONV_WIDTH - 1) + out_row // l
    shift = (jnp.arange(HALO + l)[None, :] == src_row[:, None]).astype(BF16)
    xbc_blk = COL_XBC // CONV_DIM
    halo_per_chunk = l // HALO

    def row_map(b, c):
        return b * nc + c

    return pl.pallas_call(
        _ssd_kernel,
        name="ssd",
        grid=(bsz, nc),
        in_specs=[
            pl.BlockSpec((l, SSD_DIM), lambda b, c: (row_map(b, c), COL_Z // SSD_DIM)),
            pl.BlockSpec((l, CONV_DIM), lambda b, c: (row_map(b, c), xbc_blk)),
            pl.BlockSpec((HALO, CONV_DIM),
                         lambda b, c: (jnp.maximum(row_map(b, c) * halo_per_chunk - 1, 0), xbc_blk)),
            pl.BlockSpec((l, LANES), lambda b, c: (row_map(b, c), 0)),
            pl.BlockSpec((CONV_WIDTH, CONV_DIM), lambda b, c: (0, 0)),
            pl.BlockSpec((1, CONV_DIM), lambda b, c: (0, 0)),
            pl.BlockSpec((1, h3), lambda b, c: (0, 0)),
            pl.BlockSpec((1, h3), lambda b, c: (0, 0)),
            pl.BlockSpec((1, SSD_DIM), lambda b, c: (0, 0)),
            pl.BlockSpec((1, SSD_DIM), lambda b, c: (0, 0)),
            pl.BlockSpec((h3, SSD_DIM), lambda b, c: (0, 0)),
            pl.BlockSpec(((CONV_WIDTH - 1) * l, HALO + l), lambda b, c: (0, 0)),
        ],
        out_specs=pl.BlockSpec((l, SSD_DIM), lambda b, c: (row_map(b, c), 0)),
        out_shape=jax.ShapeDtypeStruct((t, SSD_DIM), BF16),
        scratch_shapes=[pltpu.VMEM((SSD_STATE, SSD_DIM), F32),
                        pltpu.VMEM((l, SSD_DIM), F32)],
        compiler_params=_cparams(("parallel", "arbitrary")),
    )(proj, proj, proj, small, conv_w.astype(F32), conv_b.astype(F32).reshape(1, CONV_DIM),
      dtb, aneg, dexp, norm_w.astype(F32).reshape(1, SSD_DIM), expand, shift)


CUM_ROWS = 256


def _cum_kernel(small_ref, fb_ref, cum_ref, carry_ref):
    j = pl.program_id(1)

    @pl.when(j == 0)
    def _():
        carry_ref[...] = jnp.zeros_like(carry_ref)

    lf = _log_sigmoid(small_ref[...] + fb_ref[...])
    row = lax.broadcasted_iota(I32, (CUM_ROWS, CUM_ROWS), 0)
    col = lax.broadcasted_iota(I32, (CUM_ROWS, CUM_ROWS), 1)
    tril = jnp.where(col <= row, 1.0, 0.0).astype(F32)
    cs = jnp.dot(tril, lf, preferred_element_type=F32, precision=lax.Precision.HIGHEST) + carry_ref[...]
    cum_ref[...] = cs * LOG2E
    carry_ref[...] = cs[CUM_ROWS - 1:CUM_ROWS, :]


def _fox_cum(small, fox_f_bias, bsz, seq):
    fb = jnp.zeros((1, LANES), F32).at[0, SMALL_F:SMALL_F + FOX_HEADS].set(fox_f_bias.astype(F32))
    nj = seq // CUM_ROWS
    return pl.pallas_call(
        _cum_kernel,
        name="fox_cum",
        grid=(bsz, nj),
        in_specs=[pl.BlockSpec((CUM_ROWS, LANES), lambda b, j: (b * nj + j, 0)),
                  pl.BlockSpec((1, LANES), lambda b, j: (0, 0))],
        out_specs=pl.BlockSpec((CUM_ROWS, LANES), lambda b, j: (b * nj + j, 0)),
        out_shape=jax.ShapeDtypeStruct((bsz * seq, LANES), F32),
        scratch_shapes=[pltpu.VMEM((1, LANES), F32)],
        compiler_params=_cparams(("parallel", "arbitrary")),
    )(small, fb)


FOX_HEADS_PER_STEP = 2
LOG2E = 1.4426950408889634


FOX_SLAB = 128


def _fox_kernel(q_ref, k_ref, v_ref, cq_ref, ck_ref, o_ref, sa_ref, sb_ref, p_ref, m_ref, l_ref, alpha_ref,
                cqrep_ref, acc_ref, *, tq, tk):
    hp = pl.program_id(1)
    qi = pl.program_id(2)
    d = FOX_HEAD_DIM
    c2 = LOG2E / math.sqrt(d)
    lane = lax.broadcasted_iota(I32, (tq, LANES), 1)
    cq_all = cq_ref[...]

    qs, cqs = [], []
    for hh in range(FOX_HEADS_PER_STEP):
        qs.append((q_ref[:, hh * d:(hh + 1) * d].astype(F32) * c2).astype(BF16))
        head_lane = SMALL_F + hp * FOX_HEADS_PER_STEP + hh
        cq_col = jnp.sum(jnp.where(lane == head_lane, cq_all, 0.0), axis=-1, keepdims=True)
        cqs.append(jnp.broadcast_to(cq_col, (tq, LANES)))

    s_slots = (sa_ref, sb_ref)

    def scores(ki, slot):
        row0 = pl.multiple_of(ki * tk, tk)
        for hh in range(FOX_HEADS_PER_STEP):
            k = k_ref[pl.ds(row0, tk), hh * d:(hh + 1) * d]
            s = lax.dot_general(qs[hh], k, (((1,), (1,)), ((), ())), preferred_element_type=F32)
            s_slots[slot][hh] = s - ck_ref[hh, pl.ds(ki, 1), :]

    def update(ki, slot, masked):
        row0 = pl.multiple_of(ki * tk, tk)
        s_ref = s_slots[slot]
        n_ct = tk // LANES
        for hh in range(FOX_HEADS_PER_STEP):
            for rc in range(tq // FOX_SLAB):
                rows = slice(rc * FOX_SLAB, (rc + 1) * FOX_SLAB)
                n_vis = rc + 1 if masked else n_ct

                def slab(ct):
                    x = s_ref[hh, rows, ct * LANES:(ct + 1) * LANES]
                    if masked and ct == rc:
                        row = lax.broadcasted_iota(I32, (FOX_SLAB, LANES), 0)
                        col = lax.broadcasted_iota(I32, (FOX_SLAB, LANES), 1)
                        x = jnp.where(col <= row, x, -jnp.inf)
                    return x

                tmax = slab(0)
                for ct in range(1, n_vis):
                    tmax = jnp.maximum(tmax, slab(ct))
                row_max = jnp.max(tmax, axis=-1, keepdims=True)
                cq = cqrep_ref[hh, rows, :]
                m_prev = m_ref[hh, rows, :]
                m_new = jnp.maximum(m_prev, jnp.broadcast_to(row_max, (FOX_SLAB, LANES)) + cq)
                r = m_new - cq
                alpha = jnp.exp2(m_prev - m_new)
                psum = None
                for ct in range(n_vis):
                    p = jnp.exp2(slab(ct) - r)
                    psum = p if psum is None else psum + p
                    p_ref[hh, rows, ct * LANES:(ct + 1) * LANES] = p.astype(BF16)
                for ct in range(n_vis, n_ct):
                    p_ref[hh, rows, ct * LANES:(ct + 1) * LANES] = jnp.zeros((FOX_SLAB, LANES), BF16)
                l_ref[hh, rows, :] = alpha * l_ref[hh, rows, :] + psum
                m_ref[hh, rows, :] = m_new
                alpha_ref[hh, rows, :] = alpha
            v = v_ref[pl.ds(row0, tk), hh * d:(hh + 1) * d]
            acc_ref[hh] = alpha_ref[hh] * acc_ref[hh] + jnp.dot(p_ref[hh], v, preferred_element_type=F32)

    for hh in range(FOX_HEADS_PER_STEP):
        cqrep_ref[hh] = cqs[hh]

    m_ref[...] = jnp.full_like(m_ref, -jnp.inf)
    l_ref[...] = jnp.zeros_like(l_ref)
    acc_ref[...] = jnp.zeros_like(acc_ref)

    n_full = qi
    scores(0, 0)

    def pair(j, carry):
        b0 = 2 * j
        scores(b0 + 1, 1)
        update(b0, 0, False)
        scores(b0 + 2, 0)
        update(b0 + 1, 1, False)
        return carry

    lax.fori_loop(0, n_full // 2, pair, 0)

    @pl.when(n_full % 2 == 0)
    def _():
        update(n_full, 0, True)

    @pl.when(n_full % 2 == 1)
    def _():
        scores(n_full, 1)
        update(n_full - 1, 0, False)
        update(n_full, 1, True)

    for hh in range(FOX_HEADS_PER_STEP):
        l_fin = jnp.sum(l_ref[hh], axis=-1, keepdims=True)
        o_ref[:, hh * d:(hh + 1) * d] = (acc_ref[hh] / l_fin).astype(o_ref.dtype)


def _fox(proj, cum, bsz, seq, tq=512):
    assert FOX_SLAB == LANES and tq % FOX_SLAB == 0 and seq % tq == 0
    tk = tq
    nq = seq // tq
    nk = seq // tk
    hps = FOX_HEADS_PER_STEP
    cum_row = cum[:, SMALL_F:SMALL_F + FOX_HEADS].reshape(bsz, seq, FOX_HEADS).transpose(0, 2, 1)
    cum_row = cum_row.reshape(bsz, FOX_HEADS, nk, tk)
    t = bsz * seq
    w = FOX_HEADS_PER_STEP * FOX_HEAD_DIM
    n_hp = FOX_HEADS // FOX_HEADS_PER_STEP
    kern = functools.partial(_fox_kernel, tq=tq, tk=tk)
    return pl.pallas_call(
        kern,
        name="fox",
        grid=(bsz, n_hp, nq),
        in_specs=[
            pl.BlockSpec((tq, w), lambda b, hp, qi: (b * nq + qi, COL_Q // w + hp)),
            pl.BlockSpec((seq, w), lambda b, hp, qi: (b, COL_K // w + hp)),
            pl.BlockSpec((seq, w), lambda b, hp, qi: (b, COL_V // w + hp)),
            pl.BlockSpec((tq, LANES), lambda b, hp, qi: (b * nq + qi, 0)),
            pl.BlockSpec((None, FOX_HEADS_PER_STEP, nk, tk), lambda b, hp, qi: (b, hp, 0, 0)),
        ],
        out_specs=pl.BlockSpec((tq, w), lambda b, hp, qi: (b * nq + qi, hp)),
        out_shape=jax.ShapeDtypeStruct((t, FOX_DIM), BF16),
        scratch_shapes=[pltpu.VMEM((hps, tq, tk), F32), pltpu.VMEM((hps, tq, tk), F32),
                        pltpu.VMEM((hps, tq, tk), BF16),
                        pltpu.VMEM((hps, tq, LANES), F32), pltpu.VMEM((hps, tq, LANES), F32),
                        pltpu.VMEM((hps, tq, LANES), F32), pltpu.VMEM((hps, tq, LANES), F32),
                        pltpu.VMEM((hps, tq, FOX_HEAD_DIM), F32)],
        compiler_params=_cparams(("parallel", "parallel", "arbitrary")),
    )(proj, proj, proj, cum, cum_row)


def _mix_kernel(ya_ref, yb_ref, ga_ref, gb_ref, wa_ref, wb_ref, o_ref):
    pa = jnp.dot(ya_ref[...], wa_ref[...], preferred_element_type=F32)
    pb = jnp.dot(yb_ref[...], wb_ref[...], preferred_element_type=F32)
    ga = 1.0 / (1.0 + jnp.exp(-ga_ref[...].astype(F32)))
    gb = 1.0 / (1.0 + jnp.exp(-gb_ref[...].astype(F32)))
    o_ref[...] = (ga * pa + gb * pb).astype(o_ref.dtype)


def _mix(y_a, y_b, proj, w_a, w_b, tm=512, tn=1024):
    t = y_a.shape[0]
    return pl.pallas_call(
        _mix_kernel,
        name="mix",
        grid=(D_MODEL // tn, t // tm),
        in_specs=[
            pl.BlockSpec((tm, SSD_DIM), lambda j, i: (i, 0)),
            pl.BlockSpec((tm, FOX_DIM), lambda j, i: (i, 0)),
            pl.BlockSpec((tm, tn), lambda j, i: (i, COL_GA // tn + j)),
            pl.BlockSpec((tm, tn), lambda j, i: (i, COL_GB // tn + j)),
            pl.BlockSpec((SSD_DIM, tn), lambda j, i: (0, j)),
            pl.BlockSpec((FOX_DIM, tn), lambda j, i: (0, j)),
        ],
        out_specs=pl.BlockSpec((tm, tn), lambda j, i: (i, j)),
        out_shape=jax.ShapeDtypeStruct((t, D_MODEL), BF16),
        compiler_params=_cparams(("parallel", "parallel")),
    )(y_a, y_b, proj, proj, w_a, w_b)


def _outproj_kernel(m_ref, x_ref, wo_ref, nw_ref, wr_ref, br_ref, h_ref, u_ref, eidx_ref, wts_ref):
    h1 = x_ref[...] + jnp.dot(m_ref[...], wo_ref[...], preferred_element_type=F32)
    h_ref[...] = h1
    ms = jnp.mean(h1 * h1, axis=-1, keepdims=True)
    u2 = h1 * lax.rsqrt(ms + EPS) * nw_ref[...]
    u_ref[...] = _pack_bf16_pair(u2)

    u_hi = u2.astype(BF16)
    u_lo = (u2 - u_hi.astype(F32)).astype(BF16)
    hh_hl = jnp.dot(u_hi, wr_ref[...], preferred_element_type=F32)
    lh = jnp.dot(u_lo, wr_ref[:, :LANES], preferred_element_type=F32)
    logits = hh_hl[:, :LANES] + (hh_hl[:, LANES:] + lh) + br_ref[...]
    tm = logits.shape[0]
    lane = lax.broadcasted_iota(I32, (tm, LANES), 1)
    neg = -jnp.inf
    big = jnp.int32(2 * LANES)
    gl = jnp.where(lane < N_GROUPS, logits, neg)
    gmax = jnp.max(gl, axis=-1, keepdims=True)
    gsum = jnp.sum(jnp.exp(gl - gmax), axis=-1, keepdims=True)
    g_p = 1.0 / gsum
    g_idx = jnp.min(jnp.where(gl == gmax, lane, big), axis=-1, keepdims=True)
    e_of_lane = lane - N_GROUPS
    in_grp = (e_of_lane >= g_idx * EXPERTS_PER_GROUP) & (e_of_lane < (g_idx + 1) * EXPERTS_PER_GROUP)
    el = jnp.where(in_grp, logits, neg)
    m1 = jnp.max(el, axis=-1, keepdims=True)
    i1 = jnp.min(jnp.where(el == m1, lane, big), axis=-1, keepdims=True)
    el2 = jnp.where(lane == i1, neg, el)
    m2 = jnp.max(el2, axis=-1, keepdims=True)
    i2 = jnp.min(jnp.where(el2 == m2, lane, big), axis=-1, keepdims=True)
    esum = jnp.sum(jnp.exp(el - m1), axis=-1, keepdims=True)
    p1 = 1.0 / esum
    p2 = jnp.exp(m2 - m1) / esum
    w1 = g_p * (p1 / (p1 + p2))
    w2 = g_p * (p2 / (p1 + p2))
    eidx_ref[...] = jnp.where(lane == 0, i1 - N_GROUPS, jnp.where(lane == 1, i2 - N_GROUPS, 0))
    wts_ref[...] = jnp.where(lane == 0, w1, jnp.where(lane == 1, w2, 0.0))


def _outproj(mixed, x2, w_o, nw, w_router, b_router, tm=512):
    t = mixed.shape[0]
    resident = pl.Buffered(1)
    return pl.pallas_call(
        _outproj_kernel,
        name="outproj",
        grid=(t // tm,),
        in_specs=[
            pl.BlockSpec((tm, D_MODEL), lambda i: (i, 0)),
            pl.BlockSpec((tm, D_MODEL), lambda i: (i, 0)),
            pl.BlockSpec((D_MODEL, D_MODEL), lambda i: (0, 0), pipeline_mode=resident),
            pl.BlockSpec((1, D_MODEL), lambda i: (0, 0)),
            pl.BlockSpec((D_MODEL, 2 * LANES), lambda i: (0, 0), pipeline_mode=resident),
            pl.BlockSpec((1, LANES), lambda i: (0, 0)),
        ],
        out_specs=[
            pl.BlockSpec((tm, D_MODEL), lambda i: (i, 0)),
            pl.BlockSpec((tm, HALF), lambda i: (i, 0)),
            pl.BlockSpec((tm, LANES), lambda i: (i, 0)),
            pl.BlockSpec((tm, LANES), lambda i: (i, 0)),
        ],
        out_shape=[
            jax.ShapeDtypeStruct((t, D_MODEL), F32),
            jax.ShapeDtypeStruct((t, HALF), U32),
            jax.ShapeDtypeStruct((t, LANES), I32),
            jax.ShapeDtypeStruct((t, LANES), F32),
        ],
        compiler_params=_cparams(("parallel",)),
    )(mixed, x2, w_o, nw, w_router, b_router)


RANK_BLOCK = 512


def _rank_kernel(e_ref, rank_ref, cnt_ref, carry_ref):
    i = pl.program_id(0)
    r = RANK_BLOCK

    @pl.when(i == 0)
    def _():
        carry_ref[...] = jnp.zeros_like(carry_ref)

    e = e_ref[0]
    expert = lax.broadcasted_iota(I32, (N_EXPERTS, r), 0)
    onehot = jnp.where(expert == e, 1.0, 0.0).astype(F32)
    jrow = lax.broadcasted_iota(I32, (r, r), 0)
    jcol = lax.broadcasted_iota(I32, (r, r), 1)
    before = jnp.where(jrow < jcol, 1.0, 0.0).astype(BF16)
    cum = jnp.dot(onehot.astype(BF16), before, preferred_element_type=F32)
    carry = carry_ref[...]
    rank = jnp.sum(onehot * (cum + carry[:, 0:1]), axis=0, keepdims=True)
    rank_ref[0] = rank.astype(I32)
    carry = carry + jnp.sum(onehot, axis=1, keepdims=True)
    carry_ref[...] = carry
    cnt_ref[...] = carry


def _rank(e_blocks):
    nb = e_blocks.shape[0]
    return pl.pallas_call(
        _rank_kernel,
        name="rank",
        grid=(nb,),
        in_specs=[pl.BlockSpec((1, 1, RANK_BLOCK), lambda i: (i, 0, 0))],
        out_specs=[pl.BlockSpec((1, 1, RANK_BLOCK), lambda i: (i, 0, 0)),
                   pl.BlockSpec((N_EXPERTS, LANES), lambda i: (0, 0))],
        out_shape=[jax.ShapeDtypeStruct((nb, 1, RANK_BLOCK), I32),
                   jax.ShapeDtypeStruct((N_EXPERTS, LANES), F32)],
        scratch_shapes=[pltpu.VMEM((N_EXPERTS, LANES), F32)],
        compiler_params=_cparams(("arbitrary",)),
    )(e_blocks)


def _dest_kernel(e_ref, rank_ref, pstart_ref, dest_ref):
    e = e_ref[0]
    expert = lax.broadcasted_iota(I32, (N_EXPERTS, RANK_BLOCK), 0)
    start = jnp.sum(jnp.where(expert == e, pstart_ref[:, 0:1], 0), axis=0, keepdims=True)
    dest_ref[0] = rank_ref[0] + start


def _dest(e_blocks, rank, pstart):
    nb = e_blocks.shape[0]
    return pl.pallas_call(
        _dest_kernel,
        name="dest",
        grid=(nb,),
        in_specs=[pl.BlockSpec((1, 1, RANK_BLOCK), lambda i: (i, 0, 0)),
                  pl.BlockSpec((1, 1, RANK_BLOCK), lambda i: (i, 0, 0)),
                  pl.BlockSpec((N_EXPERTS, LANES), lambda i: (0, 0))],
        out_specs=pl.BlockSpec((1, 1, RANK_BLOCK), lambda i: (i, 0, 0)),
        out_shape=jax.ShapeDtypeStruct((nb, 1, RANK_BLOCK), I32),
        compiler_params=_cparams(("parallel",)),
    )(e_blocks, rank, pstart)


def _row_copy(src_ref, src_row, dst_ref, dst_row, sem):
    return pltpu.make_async_copy(src_ref.at[pl.ds(src_row, 1)], dst_ref.at[pl.ds(dst_row, 1)], sem)


DMA_UNROLL = 16


def _dispatch_kernel(last_blk_ref, nused_ref, dest_ref, u_ref, xs_ref, zbuf_ref, sem, zsem):
    i = pl.program_id(0)
    n_total = xs_ref.shape[0] // ROW_BLOCK

    @pl.when(i == 0)
    def _():
        zbuf_ref[...] = jnp.zeros_like(zbuf_ref)

        def zero_block(blk):
            return pltpu.make_async_copy(zbuf_ref, xs_ref.at[pl.ds(blk * ROW_BLOCK, ROW_BLOCK)], zsem)

        def fill_expert(e, count):
            blk = last_blk_ref[e]

            @pl.when(blk >= 0)
            def _():
                zero_block(blk).start()

            return count + jnp.where(blk >= 0, 1, 0)

        n_fill = lax.fori_loop(0, N_EXPERTS, fill_expert, 0)

        def fill_tail(blk, carry):
            zero_block(blk).start()
            return carry

        lax.fori_loop(nused_ref[0], n_total, fill_tail, 0)

        def drain(k, carry):
            zero_block(0).wait()
            return carry

        lax.fori_loop(0, n_fill + n_total - nused_ref[0], drain, 0)

    def issue(g, carry):
        tok = g * (DMA_UNROLL // TOP_K)
        c0 = g * DMA_UNROLL
        for j in range(DMA_UNROLL):
            _row_copy(u_ref, tok + j // TOP_K, xs_ref, dest_ref[0, 0, c0 + j], sem).start()
        return carry

    lax.fori_loop(0, RANK_BLOCK // DMA_UNROLL, issue, 0)
    pltpu.make_async_copy(xs_ref.at[pl.ds(0, RANK_BLOCK)], xs_ref.at[pl.ds(0, RANK_BLOCK)], sem).wait()


def _dispatch(dest, u2p, n_rows, last_blk, n_used):
    nb = dest.shape[0]
    grid_spec = pltpu.PrefetchScalarGridSpec(
        num_scalar_prefetch=2,
        grid=(nb,),
        in_specs=[pl.BlockSpec((1, 1, RANK_BLOCK), lambda i, lb, nu: (i, 0, 0), memory_space=pltpu.SMEM),
                  pl.BlockSpec((RANK_BLOCK // TOP_K, HALF), lambda i, lb, nu: (i, 0))],
        out_specs=pl.BlockSpec(memory_space=pl.ANY),
        scratch_shapes=[pltpu.VMEM((ROW_BLOCK, HALF), U32),
                        pltpu.SemaphoreType.DMA(()),
                        pltpu.SemaphoreType.DMA(())],
    )
    return pl.pallas_call(
        _dispatch_kernel,
        name="dispatch",
        grid_spec=grid_spec,
        out_shape=jax.ShapeDtypeStruct((n_rows, HALF), U32),
        compiler_params=_cparams(("arbitrary",)),
    )(last_blk, n_used, dest, u2p)


ROW_DMA_PRIORITY = 1
ROW_RING = 8


def _experts_kernel(bstart_ref, nblk_ref, xs_ref, wg_ref, wu_ref, wd_ref, y_ref,
                    wgb_ref, wub_ref, wdb_ref, xbuf_ref, ybuf_ref, xsem, ysem):
    e = pl.program_id(0)
    n_e = pl.num_programs(0)
    g0 = bstart_ref[e]
    nb = nblk_ref[e]
    n_used = bstart_ref[n_e - 1] + nblk_ref[n_e - 1]
    n_total = y_ref.shape[0] // ROW_BLOCK

    def x_copy(g, slot):
        return pltpu.make_async_copy(xs_ref.at[pl.ds(g * ROW_BLOCK, ROW_BLOCK)], xbuf_ref.at[slot], xsem.at[slot])

    def y_copy(g, slot):
        return pltpu.make_async_copy(ybuf_ref.at[slot], y_ref.at[pl.ds(g * ROW_BLOCK, ROW_BLOCK)], ysem.at[slot])

    @pl.when(e == 0)
    def _():
        for g in range(ROW_RING - 1):
            @pl.when(g < n_used)
            def _(g=g):
                x_copy(g, g).start(priority=ROW_DMA_PRIORITY)

    @pl.when(nb > 0)
    def _():
        wgb_ref[...] = wg_ref[0].astype(BF16)
        wub_ref[...] = wu_ref[0].astype(BF16)
        wdb_ref[...] = wd_ref[0].astype(BF16)

    def block(j, carry):
        g = g0 + j
        slot = g % ROW_RING
        x_copy(g, slot).wait()

        @pl.when(g + ROW_RING - 1 < n_used)
        def _():
            x_copy(g + ROW_RING - 1, (g + ROW_RING - 1) % ROW_RING).start(priority=ROW_DMA_PRIORITY)

        @pl.when(g >= ROW_RING)
        def _():
            y_copy(g - ROW_RING, slot).wait()

        lo, hi = _unpack_bf16_pair(xbuf_ref[slot])
        lo = lo.astype(BF16)
        hi = hi.astype(BF16)
        gate = (jnp.dot(lo, wgb_ref[:HALF, :], preferred_element_type=F32)
                + jnp.dot(hi, wgb_ref[HALF:, :], preferred_element_type=F32))
        up = (jnp.dot(lo, wub_ref[:HALF, :], preferred_element_type=F32)
              + jnp.dot(hi, wub_ref[HALF:, :], preferred_element_type=F32))
        hdn = (_silu(gate) * up).astype(BF16)
        y = jnp.dot(hdn, wdb_ref[...], preferred_element_type=F32)
        ybuf_ref[slot] = _pack_bf16_pair(y)
        y_copy(g, slot).start(priority=ROW_DMA_PRIORITY)
        return carry

    lax.fori_loop(0, nb, block, 0)

    @pl.when(e == n_e - 1)
    def _():
        for back in range(ROW_RING, 0, -1):
            @pl.when(n_used >= back)
            def _(back=back):
                y_copy(n_used - back, (n_used - back) % ROW_RING).wait()

        ybuf_ref[0] = jnp.zeros((ROW_BLOCK, HALF), U32)

        def fill(g, carry):
            y_copy(g, 0).start()
            return carry

        def fill_done(g, carry):
            y_copy(g, 0).wait()
            return carry

        lax.fori_loop(n_used, n_total, fill, 0)
        lax.fori_loop(n_used, n_total, fill_done, 0)


def _experts(block_start, block_count, xs, w_gate, w_up, w_down):
    n_rows = xs.shape[0]

    def wmap(e, bs, bc):
        return (e, 0, 0)

    grid_spec = pltpu.PrefetchScalarGridSpec(
        num_scalar_prefetch=2,
        grid=(N_EXPERTS,),
        in_specs=[pl.BlockSpec(memory_space=pl.ANY),
                  pl.BlockSpec((1, D_MODEL, D_EXPERT), wmap),
                  pl.BlockSpec((1, D_MODEL, D_EXPERT), wmap),
                  pl.BlockSpec((1, D_EXPERT, D_MODEL), wmap)],
        out_specs=pl.BlockSpec(memory_space=pl.ANY),
        scratch_shapes=[pltpu.VMEM((D_MODEL, D_EXPERT), BF16),
                        pltpu.VMEM((D_MODEL, D_EXPERT), BF16),
                        pltpu.VMEM((D_EXPERT, D_MODEL), BF16),
                        pltpu.VMEM((ROW_RING, ROW_BLOCK, HALF), U32),
                        pltpu.VMEM((ROW_RING, ROW_BLOCK, HALF), U32),
                        pltpu.SemaphoreType.DMA((ROW_RING,)),
                        pltpu.SemaphoreType.DMA((ROW_RING,))],
    )
    return pl.pallas_call(
        _experts_kernel,
        name="experts",
        grid_spec=grid_spec,
        out_shape=jax.ShapeDtypeStruct((n_rows, HALF), U32),
        compiler_params=_cparams(("arbitrary",)),
    )(block_start, block_count, xs, w_gate, w_up, w_down)


COMBINE_TOKENS = RANK_BLOCK // TOP_K


def _combine_kernel(dest_ref, dest_next_ref, h_ref, wts_ref, nw_ref, y_ref, o_ref, buf_ref, sem):
    ts = COMBINE_TOKENS
    i = pl.program_id(0)
    slot = i % 2

    def gather(idx_ref, dst_slot):
        def issue(g, carry):
            tok = g * (DMA_UNROLL // TOP_K)
            c0 = g * DMA_UNROLL
            for j in range(DMA_UNROLL):
                _row_copy(y_ref, idx_ref[0, 0, c0 + j], buf_ref.at[dst_slot],
                          (j % TOP_K) * ts + tok + j // TOP_K, sem.at[dst_slot]).start()
            return carry

        lax.fori_loop(0, RANK_BLOCK // DMA_UNROLL, issue, 0)

    @pl.when(i == 0)
    def _():
        gather(dest_ref, 0)

    @pl.when(i + 1 < pl.num_programs(0))
    def _():
        gather(dest_next_ref, 1 - slot)

    pltpu.make_async_copy(y_ref.at[pl.ds(0, RANK_BLOCK)], buf_ref.at[slot], sem.at[slot]).wait()

    w = wts_ref[...]
    w0 = w[:, 0:1]
    w1 = w[:, 1:2]
    lo0, hi0 = _unpack_bf16_pair(buf_ref[slot, 0:ts, :])
    lo1, hi1 = _unpack_bf16_pair(buf_ref[slot, ts:2 * ts, :])
    h = h_ref[...]
    out_lo = h[:, :HALF] + w0 * lo0 + w1 * lo1
    out_hi = h[:, HALF:] + w0 * hi0 + w1 * hi1
    ms = (jnp.sum(out_lo * out_lo, axis=-1, keepdims=True)
          + jnp.sum(out_hi * out_hi, axis=-1, keepdims=True)) * (1.0 / D_MODEL)
    inv = lax.rsqrt(ms + EPS)
    o_ref[:, :HALF] = out_lo * inv * nw_ref[:, :HALF]
    o_ref[:, HALF:] = out_hi * inv * nw_ref[:, HALF:]


def _combine(dest, h1, wts, nw, y):
    t = h1.shape[0]
    ts = COMBINE_TOKENS
    n_steps = t // ts
    return pl.pallas_call(
        _combine_kernel,
        name="combine",
        grid=(n_steps,),
        in_specs=[pl.BlockSpec((1, 1, RANK_BLOCK), lambda i: (i, 0, 0), memory_space=pltpu.SMEM),
                  pl.BlockSpec((1, 1, RANK_BLOCK), lambda i: (jnp.minimum(i + 1, n_steps - 1), 0, 0),
                               memory_space=pltpu.SMEM),
                  pl.BlockSpec((ts, D_MODEL), lambda i: (i, 0)),
                  pl.BlockSpec((ts, LANES), lambda i: (i, 0)),
                  pl.BlockSpec((1, D_MODEL), lambda i: (0, 0)),
                  pl.BlockSpec(memory_space=pl.ANY)],
        out_specs=pl.BlockSpec((ts, D_MODEL), lambda i: (i, 0)),
        out_shape=jax.ShapeDtypeStruct((t, D_MODEL), F32),
        scratch_shapes=[pltpu.VMEM((2, RANK_BLOCK, HALF), U32), pltpu.SemaphoreType.DMA((2,))],
        compiler_params=_cparams(("arbitrary",)),
    )(dest, dest, h1, wts, nw, y)


def _narrow_w_in(w_in_t):
    dt0 = SSD_DIM + CONV_DIM
    f0 = dt0 + SSD_HEADS + 3 * FOX_DIM
    dt = w_in_t[dt0:dt0 + SSD_HEADS]
    f = w_in_t[f0:f0 + FOX_HEADS]
    pad = jnp.zeros((LANES - SMALL_F - FOX_HEADS, w_in_t.shape[1]), w_in_t.dtype)
    return jnp.concatenate([dt] * SMALL_DT_COPIES + [f, pad], axis=0).T.astype(BF16)


def _layer(h, p, bsz, seq):
    t = bsz * seq
    w_in_t = jnp.swapaxes(p["w_in"].astype(F32), 0, 1)
    u, small = _norm_small(h, p["norm_mix_w"].astype(F32).reshape(1, D_MODEL), _narrow_w_in(w_in_t))
    proj = _in_proj(u, w_in_t)
    y_ssd = _ssd(proj, small, p["conv_w"], p["conv_b"], p["dt_bias"], p["a_log"], p["d_skip"],
                 p["ssd_norm_w"], bsz, seq)
    cum = _fox_cum(small, p["fox_f_bias"], bsz, seq)
    y_fox = _fox(proj, cum, bsz, seq)
    mixed = _mix(y_ssd, y_fox, proj, p["w_proj_ssd"].astype(BF16), p["w_proj_fox"].astype(BF16))

    w_router = jnp.concatenate(
        [p["w_router_group"], p["w_router_expert"],
         jnp.zeros((D_MODEL, LANES - N_GROUPS - N_EXPERTS), F32)], axis=1).astype(F32)
    w_router_hi = w_router.astype(BF16)
    w_router_lo = (w_router - w_router_hi.astype(F32)).astype(BF16)
    w_router = jnp.concatenate([w_router_hi, w_router_lo], axis=1)
    b_router = jnp.concatenate(
        [p["b_router_group"], p["b_router_expert"],
         jnp.zeros((LANES - N_GROUPS - N_EXPERTS,), F32)]).astype(F32).reshape(1, LANES)
    h1, u2p, eidx, wts = _outproj(mixed, h, p["w_out"].astype(BF16),
                                  p["norm_moe_w"].astype(F32).reshape(1, D_MODEL), w_router, b_router)

    tk = t * TOP_K
    e_blocks = eidx[:, :TOP_K].reshape(tk // RANK_BLOCK, 1, RANK_BLOCK)
    rank, counts = _rank(e_blocks)
    counts = counts[:, 0].astype(I32)
    padded = ((counts + ROW_BLOCK - 1) // ROW_BLOCK) * ROW_BLOCK
    pend = jnp.cumsum(padded)
    pstart = pend - padded
    n_blocks = tk // ROW_BLOCK + N_EXPERTS
    dest = _dest(e_blocks, rank, jnp.broadcast_to(pstart[:, None], (N_EXPERTS, LANES)).astype(I32))
    block_start = (pstart // ROW_BLOCK).astype(I32)
    block_count = (padded // ROW_BLOCK).astype(I32)
    last_blk = jnp.where(block_count > 0, block_start + block_count - 1, -1).astype(I32)
    n_used = (pend[-1:] // ROW_BLOCK).astype(I32)
    xs = _dispatch(dest, u2p, n_blocks * ROW_BLOCK, last_blk, n_used)
    y = _experts(block_start, block_count, xs, p["w_gate_exp"], p["w_up_exp"], p["w_down_exp"])
    return dest, h1, wts, y


def kernel(x, norm_mix_w, w_in, conv_w, conv_b, dt_bias, a_log, d_skip, ssd_norm_w, fox_f_bias, w_proj_ssd,
           w_proj_fox, w_out, norm_moe_w, w_router_group, b_router_group, w_router_expert, b_router_expert,
           w_gate_exp, w_up_exp, w_down_exp, norm_final_w):
    bsz, seq, _ = x.shape
    depth = w_in.shape[0]
    assert depth == 1, "the fused final norm assumes a single layer"
    stacked = dict(norm_mix_w=norm_mix_w, w_in=w_in, conv_w=conv_w, conv_b=conv_b, dt_bias=dt_bias, a_log=a_log,
                   d_skip=d_skip, ssd_norm_w=ssd_norm_w, fox_f_bias=fox_f_bias, w_proj_ssd=w_proj_ssd,
                   w_proj_fox=w_proj_fox, w_out=w_out, norm_moe_w=norm_moe_w, w_router_group=w_router_group,
                   b_router_group=b_router_group, w_router_expert=w_router_expert,
                   b_router_expert=b_router_expert, w_gate_exp=w_gate_exp, w_up_exp=w_up_exp,
                   w_down_exp=w_down_exp)
    p = {name: v[0] for name, v in stacked.items()}
    h = x.reshape(bsz * seq, D_MODEL)
    dest, h1, wts, y = _layer(h, p, bsz, seq)
    out = _combine(dest, h1, wts, norm_final_w.astype(F32).reshape(1, D_MODEL), y)
    return out.reshape(bsz, seq, D_MODEL)
```

```python
import functools
import math

import jax
import jax.numpy as jnp
from jax import lax
from jax.experimental import pallas as pl
from jax.experimental.pallas import tpu as pltpu

F32 = jnp.float32
BF16 = jnp.bfloat16
I32 = jnp.int32
U32 = jnp.uint32

D_MODEL = 2048
SSD_HEADS = 32
SSD_HEAD_DIM = 64
SSD_DIM = SSD_HEADS * SSD_HEAD_DIM
SSD_GROUPS = 4
SSD_STATE = 128
SSD_CHUNK = 128
CONV_WIDTH = 4
CONV_DIM = SSD_DIM + 2 * SSD_GROUPS * SSD_STATE
FOX_HEADS = 16
FOX_HEAD_DIM = 128
FOX_DIM = FOX_HEADS * FOX_HEAD_DIM
N_GROUPS = 8
EXPERTS_PER_GROUP = 8
N_EXPERTS = N_GROUPS * EXPERTS_PER_GROUP
TOP_K = 2
D_EXPERT = 512
EPS = 1e-6

LANES = 128
SUBLANES = 8
VMEM_LIMIT = 52 * 1024 * 1024

COL_Z = 0
COL_Q = COL_Z + SSD_DIM
COL_K = COL_Q + FOX_DIM
COL_V = COL_K + FOX_DIM
COL_GA = COL_V + FOX_DIM
COL_GB = COL_GA + D_MODEL
COL_XBC = COL_GB + D_MODEL
PROJ_COLS = COL_XBC + CONV_DIM
SMALL_DT = 0
SMALL_DT_COPIES = 3
SMALL_F = SMALL_DT_COPIES * SSD_HEADS

ROW_BLOCK = 128
HALF = D_MODEL // 2


def _cparams(sem, vmem=VMEM_LIMIT):
    return pltpu.CompilerParams(dimension_semantics=sem, vmem_limit_bytes=vmem)


def _silu(x):
    return x * (1.0 / (1.0 + jnp.exp(-x)))


def _softplus(x):
    return jnp.maximum(x, 0.0) + jnp.log(1.0 + jnp.exp(-jnp.abs(x)))


def _log_sigmoid(x):
    return -_softplus(-x)


def _split3(x):
    hi = x.astype(BF16)
    rest = x - hi.astype(F32)
    mid = rest.astype(BF16)
    lo = (rest - mid.astype(F32)).astype(BF16)
    return hi, mid, lo


def _pack_bf16_pair(x):
    n = x.shape[1] // 2
    lo = pltpu.bitcast(x[:, :n].astype(BF16).astype(F32), U32)
    hi = pltpu.bitcast(x[:, n:].astype(BF16).astype(F32), U32)
    return (hi & jnp.uint32(0xFFFF0000)) | (lo >> 16)


def _unpack_bf16_pair(p):
    lo = pltpu.bitcast(p << 16, F32)
    hi = pltpu.bitcast(p & jnp.uint32(0xFFFF0000), F32)
    return lo, hi


def _norm_small_kernel(x_ref, nw_ref, ws_ref, u_ref, s_ref):
    x = x_ref[...]
    ms = jnp.mean(x * x, axis=-1, keepdims=True)
    u = (x * lax.rsqrt(ms + EPS) * nw_ref[...]).astype(BF16)
    u_ref[...] = u
    s_ref[...] = jnp.dot(u, ws_ref[...], preferred_element_type=F32)


def _norm_small(x2, nw, w_small, tm=512):
    t = x2.shape[0]
    return pl.pallas_call(
        _norm_small_kernel,
        name="norm_small",
        grid=(t // tm,),
        in_specs=[pl.BlockSpec((tm, D_MODEL), lambda i: (i, 0)),
                  pl.BlockSpec((1, D_MODEL), lambda i: (0, 0)),
                  pl.BlockSpec((D_MODEL, LANES), lambda i: (0, 0))],
        out_specs=[pl.BlockSpec((tm, D_MODEL), lambda i: (i, 0)),
                   pl.BlockSpec((tm, LANES), lambda i: (i, 0))],
        out_shape=[jax.ShapeDtypeStruct((t, D_MODEL), BF16),
                   jax.ShapeDtypeStruct((t, LANES), F32)],
        compiler_params=_cparams(("parallel",)),
    )(x2, nw, w_small)


IN_TN = 1024
IN_XPOSE = 256
_IN_SEGMENTS = ((COL_Z, 0, SSD_DIM),
                (COL_Q, SSD_DIM + CONV_DIM + SSD_HEADS, 3 * FOX_DIM),
                (COL_GA, SSD_DIM + CONV_DIM + SSD_HEADS + 3 * FOX_DIM + FOX_HEADS, 2 * D_MODEL),
                (COL_XBC, SSD_DIM, CONV_DIM))


def _in_proj_source_rows():
    src = [0] * (PROJ_COLS // IN_TN)
    for out0, src0, width in _IN_SEGMENTS:
        for c in range(0, width, IN_TN):
            src[(out0 + c) // IN_TN] = src0 + c
    assert all(s % SUBLANES == 0 for s in src)
    return src


def _in_proj_kernel(row_ref, u_ref, wt_hbm, o_ref, st_ref, wbf_ref, sem):
    j = pl.program_id(0)
    i = pl.program_id(1)
    k = wbf_ref.shape[0]

    def window(jj, slot):
        row0 = pl.multiple_of(row_ref[jj], SUBLANES)
        return pltpu.make_async_copy(wt_hbm.at[pl.ds(row0, IN_TN), :], st_ref.at[slot], sem.at[slot])

    @pl.when((j == 0) & (i == 0))
    def _():
        window(0, 0).start()

    @pl.when(i == 0)
    def _():
        @pl.when(j + 1 < pl.num_programs(0))
        def _():
            window(j + 1, (j + 1) % 2).start()

        slot = j % 2
        window(j, slot).wait()
        for c in range(0, k, IN_XPOSE):
            wbf_ref[c:c + IN_XPOSE, :] = jnp.transpose(st_ref[slot, :, c:c + IN_XPOSE]).astype(BF16)

    o_ref[...] = jnp.dot(u_ref[...], wbf_ref[...], preferred_element_type=F32).astype(o_ref.dtype)


def _in_proj(u, w_in_t, tm=1024):
    m, k = u.shape
    grid_spec = pltpu.PrefetchScalarGridSpec(
        num_scalar_prefetch=1,
        grid=(PROJ_COLS // IN_TN, m // tm),
        in_specs=[pl.BlockSpec((tm, k), lambda j, i, rows: (i, 0)),
                  pl.BlockSpec(memory_space=pl.ANY)],
        out_specs=pl.BlockSpec((tm, IN_TN), lambda j, i, rows: (i, j)),
        scratch_shapes=[pltpu.VMEM((2, IN_TN, k), F32),
                        pltpu.VMEM((k, IN_TN), BF16),
                        pltpu.SemaphoreType.DMA((2,))],
    )
    return pl.pallas_call(
        _in_proj_kernel,
        name="in_proj",
        grid_spec=grid_spec,
        out_shape=jax.ShapeDtypeStruct((m, PROJ_COLS), BF16),
        compiler_params=_cparams(("arbitrary", "arbitrary")),
    )(jnp.asarray(_in_proj_source_rows(), I32), u, w_in_t)


HALO = 2 * SUBLANES


def _ssd_kernel(z_ref, xbc_ref, halo_ref, small_ref, cw_ref, cb_ref, dtb_ref, aneg_ref, dexp_ref,
                nw_ref, expand_ref, shift_ref, y_ref, state_ref, ydiag_ref):
    c = pl.program_id(1)
    l = SSD_CHUNK
    n = SSD_STATE

    @pl.when(c == 0)
    def _():
        state_ref[...] = jnp.zeros_like(state_ref)

    cur = xbc_ref[...]
    halo = halo_ref[...]
    halo = jnp.where(c == 0, jnp.zeros_like(halo), halo)
    ext = jnp.concatenate([halo, cur], axis=0)
    shifted = jnp.dot(shift_ref[...], ext, preferred_element_type=F32)
    conv = cb_ref[...] + cw_ref[CONV_WIDTH - 1:CONV_WIDTH, :] * cur.astype(F32)
    for j in range(CONV_WIDTH - 1):
        conv = conv + cw_ref[j:j + 1, :] * shifted[j * l:(j + 1) * l]
    xbc = _silu(conv)
    xs = xbc[:, :SSD_DIM]
    bm = xbc[:, SSD_DIM:SSD_DIM + SSD_GROUPS * n]
    cm = xbc[:, SSD_DIM + SSD_GROUPS * n:]

    h3 = SMALL_DT_COPIES * SSD_HEADS
    dt3 = _softplus(small_ref[:, SMALL_DT:SMALL_DT + h3] + dtb_ref[...])
    adt3 = dt3 * aneg_ref[...]
    row = lax.broadcasted_iota(I32, (l, l), 0)
    col = lax.broadcasted_iota(I32, (l, l), 1)
    causal = col <= row
    tril = jnp.where(causal, 1.0, 0.0).astype(BF16)
    a_cs3 = sum(jnp.dot(tril, piece, preferred_element_type=F32) for piece in _split3(adt3))
    a_cs = a_cs3[:, :SSD_HEADS]
    a_cs_t = jnp.transpose(a_cs)
    a_last3 = a_cs3[l - 1:l, :]

    lane3 = lax.broadcasted_iota(I32, (l, h3), 1)

    def pieces_by_lane_group(x3):
        hi, mid, lo = _split3(x3)
        return jnp.where(lane3 < SSD_HEADS, hi, jnp.where(lane3 < 2 * SSD_HEADS, mid, lo))

    lhs3 = jnp.concatenate([pieces_by_lane_group(dt3),
                            pieces_by_lane_group(jnp.exp(a_cs3)),
                            pieces_by_lane_group(jnp.exp(a_last3 - a_cs3))], axis=0)
    expanded = jnp.dot(lhs3, expand_ref[...], preferred_element_type=F32)
    dt_x = expanded[0:l]
    decay_in = expanded[l:2 * l]
    decay_out = expanded[2 * l:3 * l]
    chunk_decay = decay_in[l - 1:l, :]
    x_dt = xs * dt_x
    x_dt_b = x_dt.astype(BF16)
    xd_b = (x_dt * decay_out).astype(BF16)

    lane = lax.broadcasted_iota(I32, (l, LANES), 1)
    lo_mask = lane < SSD_HEAD_DIM
    heads_per_group = SSD_HEADS // SSD_GROUPS
    gw = heads_per_group * SSD_HEAD_DIM

    for g in range(SSD_GROUPS):
        bg = bm[:, g * n:(g + 1) * n].astype(BF16)
        cg = cm[:, g * n:(g + 1) * n].astype(BF16)
        cb = lax.dot_general(cg, bg, (((1,), (1,)), ((), ())), preferred_element_type=F32)
        for pair in range(heads_per_group // 2):
            h0 = g * heads_per_group + 2 * pair
            lane0 = h0 * SSD_HEAD_DIM
            ms = []
            for h in (h0, h0 + 1):
                seg = a_cs[:, h:h + 1] - a_cs_t[h:h + 1, :]
                lmat = jnp.exp(jnp.where(causal, seg, -jnp.inf))
                ms.append((cb * lmat).astype(BF16))
            lhs = jnp.concatenate(ms, axis=1)
            xp = x_dt_b[:, lane0:lane0 + LANES]
            zero = jnp.zeros_like(xp)
            rhs = jnp.concatenate([jnp.where(lo_mask, xp, zero), jnp.where(lo_mask, zero, xp)], axis=0)
            ydiag_ref[:, lane0:lane0 + LANES] = jnp.dot(lhs, rhs, preferred_element_type=F32)
        st = state_ref[:, g * gw:(g + 1) * gw]
        y_off = jnp.dot(cg, st.astype(BF16), preferred_element_type=F32)
        ydiag_ref[:, g * gw:(g + 1) * gw] += y_off * decay_in[:, g * gw:(g + 1) * gw]
        new = lax.dot_general(bg, xd_b[:, g * gw:(g + 1) * gw], (((0,), (0,)), ((), ())),
                              preferred_element_type=F32)
        state_ref[:, g * gw:(g + 1) * gw] = st * chunk_decay[:, g * gw:(g + 1) * gw] + new

    y = ydiag_ref[...] + dexp_ref[...] * xs
    y = y * _silu(z_ref[...].astype(F32))
    ms = jnp.mean(y * y, axis=-1, keepdims=True)
    y_ref[...] = (y * lax.rsqrt(ms + EPS) * nw_ref[...]).astype(y_ref.dtype)


def _ssd(proj, small, conv_w, conv_b, dt_bias, a_log, d_skip, norm_w, bsz, seq):
    l = SSD_CHUNK
    nc = seq // l
    t = bsz * seq
    h3 = SMALL_DT_COPIES * SSD_HEADS
    aneg = jnp.tile(-jnp.exp(a_log.astype(F32)), SMALL_DT_COPIES).reshape(1, h3)
    dtb = jnp.tile(dt_bias.astype(F32), SMALL_DT_COPIES).reshape(1, h3)
    dexp = jnp.repeat(d_skip.astype(F32), SSD_HEAD_DIM).reshape(1, SSD_DIM)
    expand = jnp.tile(jnp.repeat(jnp.eye(SSD_HEADS, dtype=BF16), SSD_HEAD_DIM, axis=1),
                      (SMALL_DT_COPIES, 1))
    out_row = jnp.arange((CONV_WIDTH - 1) * l)
    src_row = HALO + out_row % l - (CONV_WIDTH - 1) + out_row // l
    shift = (jnp.arange(HALO + l)[None, :] == src_row[:, None]).astype(BF16)
    xbc_blk = COL_XBC // CONV_DIM
    halo_per_chunk = l // HALO

    def row_map(b, c):
        return b * nc + c

    return pl.pallas_call(
        _ssd_kernel,
        name="ssd",
        grid=(bsz, nc),
        in_specs=[
            pl.BlockSpec((l, SSD_DIM), lambda b, c: (row_map(b, c), COL_Z // SSD_DIM)),
            pl.BlockSpec((l, CONV_DIM), lambda b, c: (row_map(b, c), xbc_blk)),
            pl.BlockSpec((HALO, CONV_DIM),
                         lambda b, c: (jnp.maximum(row_map(b, c) * halo_per_chunk - 1, 0), xbc_blk)),
            pl.BlockSpec((l, LANES), lambda b, c: (row_map(b, c), 0)),
            pl.BlockSpec((CONV_WIDTH, CONV_DIM), lambda b, c: (0, 0)),
            pl.BlockSpec((1, CONV_DIM), lambda b, c: (0, 0)),
            pl.BlockSpec((1, h3), lambda b, c: (0, 0)),
            pl.BlockSpec((1, h3), lambda b, c: (0, 0)),
            pl.BlockSpec((1, SSD_DIM), lambda b, c: (0, 0)),
            pl.BlockSpec((1, SSD_DIM), lambda b, c: (0, 0)),
            pl.BlockSpec((h3, SSD_DIM), lambda b, c: (0, 0)),
            pl.BlockSpec(((CONV_WIDTH - 1) * l, HALO + l), lambda b, c: (0, 0)),
        ],
        out_specs=pl.BlockSpec((l, SSD_DIM), lambda b, c: (row_map(b, c), 0)),
        out_shape=jax.ShapeDtypeStruct((t, SSD_DIM), BF16),
        scratch_shapes=[pltpu.VMEM((SSD_STATE, SSD_DIM), F32),
                        pltpu.VMEM((l, SSD_DIM), F32)],
        compiler_params=_cparams(("parallel", "arbitrary")),
    )(proj, proj, proj, small, conv_w.astype(F32), conv_b.astype(F32).reshape(1, CONV_DIM),
      dtb, aneg, dexp, norm_w.astype(F32).reshape(1, SSD_DIM), expand, shift)


CUM_ROWS = 256


def _cum_kernel(small_ref, fb_ref, cum_ref, carry_ref):
    j = pl.program_id(1)

    @pl.when(j == 0)
    def _():
        carry_ref[...] = jnp.zeros_like(carry_ref)

    lf = _log_sigmoid(small_ref[...] + fb_ref[...])
    row = lax.broadcasted_iota(I32, (CUM_ROWS, CUM_ROWS), 0)
    col = lax.broadcasted_iota(I32, (CUM_ROWS, CUM_ROWS), 1)
    tril = jnp.where(col <= row, 1.0, 0.0).astype(F32)
    cs = jnp.dot(tril, lf, preferred_element_type=F32, precision=lax.Precision.HIGHEST) + carry_ref[...]
    cum_ref[...] = cs * LOG2E
    carry_ref[...] = cs[CUM_ROWS - 1:CUM_ROWS, :]


def _fox_cum(small, fox_f_bias, bsz, seq):
    fb = jnp.zeros((1, LANES), F32).at[0, SMALL_F:SMALL_F + FOX_HEADS].set(fox_f_bias.astype(F32))
    nj = seq // CUM_ROWS
    return pl.pallas_call(
        _cum_kernel,
        name="fox_cum",
        grid=(bsz, nj),
        in_specs=[pl.BlockSpec((CUM_ROWS, LANES), lambda b, j: (b * nj + j, 0)),
                  pl.BlockSpec((1, LANES), lambda b, j: (0, 0))],
        out_specs=pl.BlockSpec((CUM_ROWS, LANES), lambda b, j: (b * nj + j, 0)),
        out_shape=jax.ShapeDtypeStruct((bsz * seq, LANES), F32),
        scratch_shapes=[pltpu.VMEM((1, LANES), F32)],
        compiler_params=_cparams(("parallel", "arbitrary")),
    )(small, fb)


FOX_HEADS_PER_STEP = 2
LOG2E = 1.4426950408889634


FOX_SLAB = 128


def _fox_kernel(q_ref, k_ref, v_ref, cq_ref, ck_ref, o_ref, sa_ref, sb_ref, p_ref, m_ref, l_ref, alpha_ref,
                cqrep_ref, acc_ref, *, tq, tk):
    hp = pl.program_id(1)
    qi = pl.program_id(2)
    d = FOX_HEAD_DIM
    c2 = LOG2E / math.sqrt(d)
    lane = lax.broadcasted_iota(I32, (tq, LANES), 1)
    cq_all = cq_ref[...]

    qs, cqs = [], []
    for hh in range(FOX_HEADS_PER_STEP):
        qs.append((q_ref[:, hh * d:(hh + 1) * d].astype(F32) * c2).astype(BF16))
        head_lane = SMALL_F + hp * FOX_HEADS_PER_STEP + hh
        cq_col = jnp.sum(jnp.where(lane == head_lane, cq_all, 0.0), axis=-1, keepdims=True)
        cqs.append(jnp.broadcast_to(cq_col, (tq, LANES)))

    s_slots = (sa_ref, sb_ref)

    def scores(ki, slot):
        row0 = pl.multiple_of(ki * tk, tk)
        for hh in range(FOX_HEADS_PER_STEP):
            k = k_ref[pl.ds(row0, tk), hh * d:(hh + 1) * d]
            s = lax.dot_general(qs[hh], k, (((1,), (1,)), ((), ())), preferred_element_type=F32)
            s_slots[slot][hh] = s - ck_ref[hh, pl.ds(ki, 1), :]

    def update(ki, slot, masked):
        row0 = pl.multiple_of(ki * tk, tk)
        s_ref = s_slots[slot]
        n_ct = tk // LANES
        for hh in range(FOX_HEADS_PER_STEP):
            for rc in range(tq // FOX_SLAB):
                rows = slice(rc * FOX_SLAB, (rc + 1) * FOX_SLAB)
                n_vis = rc + 1 if masked else n_ct

                def slab(ct):
                    x = s_ref[hh, rows, ct * LANES:(ct + 1) * LANES]
                    if masked and ct == rc:
                        row = lax.broadcasted_iota(I32, (FOX_SLAB, LANES), 0)
                        col = lax.broadcasted_iota(I32, (FOX_SLAB, LANES), 1)
                        x = jnp.where(col <= row, x, -jnp.inf)
                    return x

                tmax = slab(0)
                for ct in range(1, n_vis):
                    tmax = jnp.maximum(tmax, slab(ct))
                row_max = jnp.max(tmax, axis=-1, keepdims=True)
                cq = cqrep_ref[hh, rows, :]
                m_prev = m_ref[hh, rows, :]
                m_new = jnp.maximum(m_prev, jnp.broadcast_to(row_max, (FOX_SLAB, LANES)) + cq)
                r = m_new - cq
                alpha = jnp.exp2(m_prev - m_new)
                psum = None
                for ct in range(n_vis):
                    p = jnp.exp2(slab(ct) - r)
                    psum = p if psum is None else psum + p
                    p_ref[hh, rows, ct * LANES:(ct + 1) * LANES] = p.astype(BF16)
                for ct in range(n_vis, n_ct):
                    p_ref[hh, rows, ct * LANES:(ct + 1) * LANES] = jnp.zeros((FOX_SLAB, LANES), BF16)
                l_ref[hh, rows, :] = alpha * l_ref[hh, rows, :] + psum
                m_ref[hh, rows, :] = m_new
                alpha_ref[hh, rows, :] = alpha
            v = v_ref[pl.ds(row0, tk), hh * d:(hh + 1) * d]
            acc_ref[hh] = alpha_ref[hh] * acc_ref[hh] + jnp.dot(p_ref[hh], v, preferred_element_type=F32)

    for hh in range(FOX_HEADS_PER_STEP):
        cqrep_ref[hh] = cqs[hh]

    m_ref[...] = jnp.full_like(m_ref, -jnp.inf)
    l_ref[...] = jnp.zeros_like(l_ref)
    acc_ref[...] = jnp.zeros_like(acc_ref)

    n_full = qi
    scores(0, 0)

    def pair(j, carry):
        b0 = 2 * j
        scores(b0 + 1, 1)
        update(b0, 0, False)
        scores(b0 + 2, 0)
        update(b0 + 1, 1, False)
        return carry

    lax.fori_loop(0, n_full // 2, pair, 0)

    @pl.when(n_full % 2 == 0)
    def _():
        update(n_full, 0, True)

    @pl.when(n_full % 2 == 1)
    def _():
        scores(n_full, 1)
        update(n_full - 1, 0, False)
        update(n_full, 1, True)

    for hh in range(FOX_HEADS_PER_STEP):
        l_fin = jnp.sum(l_ref[hh], axis=-1, keepdims=True)
        o_ref[:, hh * d:(hh + 1) * d] = (acc_ref[hh] / l_fin).astype(o_ref.dtype)


def _fox(proj, cum, bsz, seq, tq=512):
    assert FOX_SLAB == LANES and tq % FOX_SLAB == 0 and seq % tq == 0
    tk = tq
    nq = seq // tq
    nk = seq // tk
    hps = FOX_HEADS_PER_STEP
    cum_row = cum[:, SMALL_F:SMALL_F + FOX_HEADS].reshape(bsz, seq, FOX_HEADS).transpose(0, 2, 1)
    cum_row = cum_row.reshape(bsz, FOX_HEADS, nk, tk)
    t = bsz * seq
    w = FOX_HEADS_PER_STEP * FOX_HEAD_DIM
    n_hp = FOX_HEADS // FOX_HEADS_PER_STEP
    kern = functools.partial(_fox_kernel, tq=tq, tk=tk)
    return pl.pallas_call(
        kern,
        name="fox",
        grid=(bsz, n_hp, nq),
        in_specs=[
            pl.BlockSpec((tq, w), lambda b, hp, qi: (b * nq + qi, COL_Q // w + hp)),
            pl.BlockSpec((seq, w), lambda b, hp, qi: (b, COL_K // w + hp)),
            pl.BlockSpec((seq, w), lambda b, hp, qi: (b, COL_V // w + hp)),
            pl.BlockSpec((tq, LANES), lambda b, hp, qi: (b * nq + qi, 0)),
            pl.BlockSpec((None, FOX_HEADS_PER_STEP, nk, tk), lambda b, hp, qi: (b, hp, 0, 0)),
        ],
        out_specs=pl.BlockSpec((tq, w), lambda b, hp, qi: (b * nq + qi, hp)),
        out_shape=jax.ShapeDtypeStruct((t, FOX_DIM), BF16),
        scratch_shapes=[pltpu.VMEM((hps, tq, tk), F32), pltpu.VMEM((hps, tq, tk), F32),
                        pltpu.VMEM((hps, tq, tk), BF16),
                        pltpu.VMEM((hps, tq, LANES), F32), pltpu.VMEM((hps, tq, LANES), F32),
                        pltpu.VMEM((hps, tq, LANES), F32), pltpu.VMEM((hps, tq, LANES), F32),
                        pltpu.VMEM((hps, tq, FOX_HEAD_DIM), F32)],
        compiler_params=_cparams(("parallel", "parallel", "arbitrary")),
    )(proj, proj, proj, cum, cum_row)


def _mix_kernel(ya_ref, yb_ref, ga_ref, gb_ref, wa_ref, wb_ref, o_ref):
    pa = jnp.dot(ya_ref[...], wa_ref[...], preferred_element_type=F32)
    pb = jnp.dot(yb_ref[...], wb_ref[...], preferred_element_type=F32)
    ga = 1.0 / (1.0 + jnp.exp(-ga_ref[...].astype(F32)))
    gb = 1.0 / (1.0 + jnp.exp(-gb_ref[...].astype(F32)))
    o_ref[...] = (ga * pa + gb * pb).astype(o_ref.dtype)


def _mix(y_a, y_b, proj, w_a, w_b, tm=512, tn=1024):
    t = y_a.shape[0]
    return pl.pallas_call(
        _mix_kernel,
        name="mix",
        grid=(D_MODEL // tn, t // tm),
        in_specs=[
            pl.BlockSpec((tm, SSD_DIM), lambda j, i: (i, 0)),
            pl.BlockSpec((tm, FOX_DIM), lambda j, i: (i, 0)),
            pl.BlockSpec((tm, tn), lambda j, i: (i, COL_GA // tn + j)),
            pl.BlockSpec((tm, tn), lambda j, i: (i, COL_GB // tn + j)),
            pl.BlockSpec((SSD_DIM, tn), lambda j, i: (0, j)),
            pl.BlockSpec((FOX_DIM, tn), lambda j, i: (0, j)),
        ],
        out_specs=pl.BlockSpec((tm, tn), lambda j, i: (i, j)),
        out_shape=jax.ShapeDtypeStruct((t, D_MODEL), BF16),
        compiler_params=_cparams(("parallel", "parallel")),
    )(y_a, y_b, proj, proj, w_a, w_b)


def _outproj_kernel(m_ref, x_ref, wo_ref, nw_ref, wr_ref, br_ref, h_ref, u_ref, eidx_ref, wts_ref):
    h1 = x_ref[...] + jnp.dot(m_ref[...], wo_ref[...], preferred_element_type=F32)
    h_ref[...] = h1
    ms = jnp.mean(h1 * h1, axis=-1, keepdims=True)
    u2 = h1 * lax.rsqrt(ms + EPS) * nw_ref[...]
    u_ref[...] = _pack_bf16_pair(u2)

    u_hi = u2.astype(BF16)
    u_lo = (u2 - u_hi.astype(F32)).astype(BF16)
    hh_hl = jnp.dot(u_hi, wr_ref[...], preferred_element_type=F32)
    lh = jnp.dot(u_lo, wr_ref[:, :LANES], preferred_element_type=F32)
    logits = hh_hl[:, :LANES] + (hh_hl[:, LANES:] + lh) + br_ref[...]
    tm = logits.shape[0]
    lane = lax.broadcasted_iota(I32, (tm, LANES), 1)
    neg = -jnp.inf
    big = jnp.int32(2 * LANES)
    gl = jnp.where(lane < N_GROUPS, logits, neg)
    gmax = jnp.max(gl, axis=-1, keepdims=True)
    gsum = jnp.sum(jnp.exp(gl - gmax), axis=-1, keepdims=True)
    g_p = 1.0 / gsum
    g_idx = jnp.min(jnp.where(gl == gmax, lane, big), axis=-1, keepdims=True)
    e_of_lane = lane - N_GROUPS
    in_grp = (e_of_lane >= g_idx * EXPERTS_PER_GROUP) & (e_of_lane < (g_idx + 1) * EXPERTS_PER_GROUP)
    el = jnp.where(in_grp, logits, neg)
    m1 = jnp.max(el, axis=-1, keepdims=True)
    i1 = jnp.min(jnp.where(el == m1, lane, big), axis=-1, keepdims=True)
    el2 = jnp.where(lane == i1, neg, el)
    m2 = jnp.max(el2, axis=-1, keepdims=True)
    i2 = jnp.min(jnp.where(el2 == m2, lane, big), axis=-1, keepdims=True)
    esum = jnp.sum(jnp.exp(el - m1), axis=-1, keepdims=True)
    p1 = 1.0 / esum
    p2 = jnp.exp(m2 - m1) / esum
    w1 = g_p * (p1 / (p1 + p2))
    w2 = g_p * (p2 / (p1 + p2))
    eidx_ref[...] = jnp.where(lane == 0, i1 - N_GROUPS, jnp.where(lane == 1, i2 - N_GROUPS, 0))
    wts_ref[...] = jnp.where(lane == 0, w1, jnp.where(lane == 1, w2, 0.0))


def _outproj(mixed, x2, w_o, nw, w_router, b_router, tm=512):
    t = mixed.shape[0]
    resident = pl.Buffered(1)
    return pl.pallas_call(
        _outproj_kernel,
        name="outproj",
        grid=(t // tm,),
        in_specs=[
            pl.BlockSpec((tm, D_MODEL), lambda i: (i, 0)),
            pl.BlockSpec((tm, D_MODEL), lambda i: (i, 0)),
            pl.BlockSpec((D_MODEL, D_MODEL), lambda i: (0, 0), pipeline_mode=resident),
            pl.BlockSpec((1, D_MODEL), lambda i: (0, 0)),
            pl.BlockSpec((D_MODEL, 2 * LANES), lambda i: (0, 0), pipeline_mode=resident),
            pl.BlockSpec((1, LANES), lambda i: (0, 0)),
        ],
        out_specs=[
            pl.BlockSpec((tm, D_MODEL), lambda i: (i, 0)),
            pl.BlockSpec((tm, HALF), lambda i: (i, 0)),
            pl.BlockSpec((tm, LANES), lambda i: (i, 0)),
            pl.BlockSpec((tm, LANES), lambda i: (i, 0)),
        ],
        out_shape=[
            jax.ShapeDtypeStruct((t, D_MODEL), F32),
            jax.ShapeDtypeStruct((t, HALF), U32),
            jax.ShapeDtypeStruct((t, LANES), I32),
            jax.ShapeDtypeStruct((t, LANES), F32),
        ],
        compiler_params=_cparams(("parallel",)),
    )(mixed, x2, w_o, nw, w_router, b_router)


RANK_BLOCK = 512


def _rank_kernel(e_ref, rank_ref, cnt_ref, carry_ref):
    i = pl.program_id(0)
    r = RANK_BLOCK

    @pl.when(i == 0)
    def _():
        carry_ref[...] = jnp.zeros_like(carry_ref)

    e = e_ref[0]
    expert = lax.broadcasted_iota(I32, (N_EXPERTS, r), 0)
    onehot = jnp.where(expert == e, 1.0, 0.0).astype(F32)
    jrow = lax.broadcasted_iota(I32, (r, r), 0)
    jcol = lax.broadcasted_iota(I32, (r, r), 1)
    before = jnp.where(jrow < jcol, 1.0, 0.0).astype(BF16)
    cum = jnp.dot(onehot.astype(BF16), before, preferred_element_type=F32)
    carry = carry_ref[...]
    rank = jnp.sum(onehot * (cum + carry[:, 0:1]), axis=0, keepdims=True)
    rank_ref[0] = rank.astype(I32)
    carry = carry + jnp.sum(onehot, axis=1, keepdims=True)
    carry_ref[...] = carry
    cnt_ref[...] = carry


def _rank(e_blocks):
    nb = e_blocks.shape[0]
    return pl.pallas_call(
        _rank_kernel,
        name="rank",
        grid=(nb,),
        in_specs=[pl.BlockSpec((1, 1, RANK_BLOCK), lambda i: (i, 0, 0))],
        out_specs=[pl.BlockSpec((1, 1, RANK_BLOCK), lambda i: (i, 0, 0)),
                   pl.BlockSpec((N_EXPERTS, LANES), lambda i: (0, 0))],
        out_shape=[jax.ShapeDtypeStruct((nb, 1, RANK_BLOCK), I32),
                   jax.ShapeDtypeStruct((N_EXPERTS, LANES), F32)],
        scratch_shapes=[pltpu.VMEM((N_EXPERTS, LANES), F32)],
        compiler_params=_cparams(("arbitrary",)),
    )(e_blocks)


def _dest_kernel(e_ref, rank_ref, pstart_ref, dest_ref):
    e = e_ref[0]
    expert = lax.broadcasted_iota(I32, (N_EXPERTS, RANK_BLOCK), 0)
    start = jnp.sum(jnp.where(expert == e, pstart_ref[:, 0:1], 0), axis=0, keepdims=True)
    dest_ref[0] = rank_ref[0] + start


def _dest(e_blocks, rank, pstart):
    nb = e_blocks.shape[0]
    return pl.pallas_call(
        _dest_kernel,
        name="dest",
        grid=(nb,),
        in_specs=[pl.BlockSpec((1, 1, RANK_BLOCK), lambda i: (i, 0, 0)),
                  pl.BlockSpec((1, 1, RANK_BLOCK), lambda i: (i, 0, 0)),
                  pl.BlockSpec((N_EXPERTS, LANES), lambda i: (0, 0))],
        out_specs=pl.BlockSpec((1, 1, RANK_BLOCK), lambda i: (i, 0, 0)),
        out_shape=jax.ShapeDtypeStruct((nb, 1, RANK_BLOCK), I32),
        compiler_params=_cparams(("parallel",)),
    )(e_blocks, rank, pstart)


def _row_copy(src_ref, src_row, dst_ref, dst_row, sem):
    return pltpu.make_async_copy(src_ref.at[pl.ds(src_row, 1)], dst_ref.at[pl.ds(dst_row, 1)], sem)


DMA_UNROLL = 16


def _dispatch_kernel(last_blk_ref, nused_ref, dest_ref, u_ref, xs_ref, zbuf_ref, sem, zsem):
    i = pl.program_id(0)
    n_total = xs_ref.shape[0] // ROW_BLOCK

    @pl.when(i == 0)
    def _():
        zbuf_ref[...] = jnp.zeros_like(zbuf_ref)

        def zero_block(blk):
            return pltpu.make_async_copy(zbuf_ref, xs_ref.at[pl.ds(blk * ROW_BLOCK, ROW_BLOCK)], zsem)

        def fill_expert(e, count):
            blk = last_blk_ref[e]

            @pl.when(blk >= 0)
            def _():
                zero_block(blk).start()

            return count + jnp.where(blk >= 0, 1, 0)

        n_fill = lax.fori_loop(0, N_EXPERTS, fill_expert, 0)

        def fill_tail(blk, carry):
            zero_block(blk).start()
            return carry

        lax.fori_loop(nused_ref[0], n_total, fill_tail, 0)

        def drain(k, carry):
            zero_block(0).wait()
            return carry

        lax.fori_loop(0, n_fill + n_total - nused_ref[0], drain, 0)

    def issue(g, carry):
        c0 = g * DMA_UNROLL
        for j in range(DMA_UNROLL):
            dst_row = dest_ref[0, 0, c0 + j]
            pltpu.make_async_copy(u_ref.at[g, pl.ds(j // TOP_K, 1)], xs_ref.at[pl.ds(dst_row, 1)], sem).start()
        return carry

    lax.fori_loop(0, RANK_BLOCK // DMA_UNROLL, issue, 0)
    pltpu.make_async_copy(xs_ref.at[pl.ds(0, RANK_BLOCK)], xs_ref.at[pl.ds(0, RANK_BLOCK)], sem).wait()


def _dispatch(dest, u2p, n_rows, last_blk, n_used):
    nb = dest.shape[0]
    grid_spec = pltpu.PrefetchScalarGridSpec(
        num_scalar_prefetch=2,
        grid=(nb,),
        in_specs=[pl.BlockSpec((1, 1, RANK_BLOCK), lambda i, lb, nu: (i, 0, 0), memory_space=pltpu.SMEM),
                  pl.BlockSpec((RANK_BLOCK // DMA_UNROLL, SUBLANES, HALF), lambda i, lb, nu: (i, 0, 0))],
        out_specs=pl.BlockSpec(memory_space=pl.ANY),
        scratch_shapes=[pltpu.VMEM((ROW_BLOCK, HALF), U32),
                        pltpu.SemaphoreType.DMA(()),
                        pltpu.SemaphoreType.DMA(())],
    )
    assert DMA_UNROLL == TOP_K * SUBLANES
    return pl.pallas_call(
        _dispatch_kernel,
        name="dispatch",
        grid_spec=grid_spec,
        out_shape=jax.ShapeDtypeStruct((n_rows, HALF), U32),
        compiler_params=_cparams(("arbitrary",)),
    )(last_blk, n_used, dest, u2p.reshape(-1, SUBLANES, HALF))


ROW_DMA_PRIORITY = 1
ROW_RING = 8


def _experts_kernel(bstart_ref, nblk_ref, xs_ref, wg_ref, wu_ref, wd_ref, y_ref,
                    wgb_ref, wub_ref, wdb_ref, xbuf_ref, ybuf_ref, xsem, ysem):
    e = pl.program_id(0)
    n_e = pl.num_programs(0)
    g0 = bstart_ref[e]
    nb = nblk_ref[e]
    n_used = bstart_ref[n_e - 1] + nblk_ref[n_e - 1]
    n_total = y_ref.shape[0] // ROW_BLOCK

    def x_copy(g, slot):
        return pltpu.make_async_copy(xs_ref.at[pl.ds(g * ROW_BLOCK, ROW_BLOCK)], xbuf_ref.at[slot], xsem.at[slot])

    def y_copy(g, slot):
        return pltpu.make_async_copy(ybuf_ref.at[slot], y_ref.at[pl.ds(g * ROW_BLOCK, ROW_BLOCK)], ysem.at[slot])

    @pl.when(e == 0)
    def _():
        for g in range(ROW_RING - 1):
            @pl.when(g < n_used)
            def _(g=g):
                x_copy(g, g).start(priority=ROW_DMA_PRIORITY)

    @pl.when(nb > 0)
    def _():
        wgb_ref[...] = wg_ref[0].astype(BF16)
        wub_ref[...] = wu_ref[0].astype(BF16)
        wdb_ref[...] = wd_ref[0].astype(BF16)

    def block(j, carry):
        g = g0 + j
        slot = g % ROW_RING
        x_copy(g, slot).wait()

        @pl.when(g + ROW_RING - 1 < n_used)
        def _():
            x_copy(g + ROW_RING - 1, (g + ROW_RING - 1) % ROW_RING).start(priority=ROW_DMA_PRIORITY)

        @pl.when(g >= ROW_RING)
        def _():
            y_copy(g - ROW_RING, slot).wait()

        lo, hi = _unpack_bf16_pair(xbuf_ref[slot])
        lo = lo.astype(BF16)
        hi = hi.astype(BF16)
        gate = (jnp.dot(lo, wgb_ref[:HALF, :], preferred_element_type=F32)
                + jnp.dot(hi, wgb_ref[HALF:, :], preferred_element_type=F32))
        up = (jnp.dot(lo, wub_ref[:HALF, :], preferred_element_type=F32)
              + jnp.dot(hi, wub_ref[HALF:, :], preferred_element_type=F32))
        hdn = (_silu(gate) * up).astype(BF16)
        y = jnp.dot(hdn, wdb_ref[...], preferred_element_type=F32)
        ybuf_ref[slot] = _pack_bf16_pair(y)
        y_copy(g, slot).start(priority=ROW_DMA_PRIORITY)
        return carry

    lax.fori_loop(0, nb, block, 0)

    @pl.when(e == n_e - 1)
    def _():
        for back in range(ROW_RING, 0, -1):
            @pl.when(n_used >= back)
            def _(back=back):
                y_copy(n_used - back, (n_used - back) % ROW_RING).wait()

        ybuf_ref[0] = jnp.zeros((ROW_BLOCK, HALF), U32)

        def fill(g, carry):
            y_copy(g, 0).start()
            return carry

        def fill_done(g, carry):
            y_copy(g, 0).wait()
            return carry

        lax.fori_loop(n_used, n_total, fill, 0)
        lax.fori_loop(n_used, n_total, fill_done, 0)


def _experts(block_start, block_count, xs, w_gate, w_up, w_down):
    n_rows = xs.shape[0]

    def wmap(e, bs, bc):
        return (e, 0, 0)

    grid_spec = pltpu.PrefetchScalarGridSpec(
        num_scalar_prefetch=2,
        grid=(N_EXPERTS,),
        in_specs=[pl.BlockSpec(memory_space=pl.ANY),
                  pl.BlockSpec((1, D_MODEL, D_EXPERT), wmap),
                  pl.BlockSpec((1, D_MODEL, D_EXPERT), wmap),
                  pl.BlockSpec((1, D_EXPERT, D_MODEL), wmap)],
        out_specs=pl.BlockSpec(memory_space=pl.ANY),
        scratch_shapes=[pltpu.VMEM((D_MODEL, D_EXPERT), BF16),
                        pltpu.VMEM((D_MODEL, D_EXPERT), BF16),
                        pltpu.VMEM((D_EXPERT, D_MODEL), BF16),
                        pltpu.VMEM((ROW_RING, ROW_BLOCK, HALF), U32),
                        pltpu.VMEM((ROW_RING, ROW_BLOCK, HALF), U32),
                        pltpu.SemaphoreType.DMA((ROW_RING,)),
                        pltpu.SemaphoreType.DMA((ROW_RING,))],
    )
    return pl.pallas_call(
        _experts_kernel,
        name="experts",
        grid_spec=grid_spec,
        out_shape=jax.ShapeDtypeStruct((n_rows, HALF), U32),
        compiler_params=_cparams(("arbitrary",)),
    )(block_start, block_count, xs, w_gate, w_up, w_down)


COMBINE_TOKENS = RANK_BLOCK // TOP_K


def _combine_kernel(dest_ref, dest_next_ref, h_ref, wts_ref, nw_ref, y_ref, o_ref, buf_ref, sem):
    ts = COMBINE_TOKENS
    i = pl.program_id(0)
    slot = i % 2

    def gather(idx_ref, dst_slot):
        def issue(g, carry):
            c0 = g * DMA_UNROLL
            for j in range(DMA_UNROLL):
                src_row = idx_ref[0, 0, c0 + j]
                src_tile = lax.shift_right_logical(src_row, SUBLANES.bit_length() - 1)
                pltpu.make_async_copy(y_ref.at[src_tile, pl.ds(src_row & (SUBLANES - 1), 1)],
                                      buf_ref.at[dst_slot, j % TOP_K, g, pl.ds(j // TOP_K, 1)],
                                      sem.at[dst_slot]).start()
            return carry

        lax.fori_loop(0, RANK_BLOCK // DMA_UNROLL, issue, 0)

    @pl.when(i == 0)
    def _():
        gather(dest_ref, 0)

    @pl.when(i + 1 < pl.num_programs(0))
    def _():
        gather(dest_next_ref, 1 - slot)

    for k in range(TOP_K):
        pltpu.make_async_copy(y_ref.at[pl.ds(0, ts // SUBLANES)], buf_ref.at[slot, k], sem.at[slot]).wait()

    w = wts_ref[...]
    w0 = w[:, 0:1]
    w1 = w[:, 1:2]
    lo0, hi0 = _unpack_bf16_pair(buf_ref[slot, 0].reshape(ts, HALF))
    lo1, hi1 = _unpack_bf16_pair(buf_ref[slot, 1].reshape(ts, HALF))
    h = h_ref[...]
    out_lo = h[:, :HALF] + w0 * lo0 + w1 * lo1
    out_hi = h[:, HALF:] + w0 * hi0 + w1 * hi1
    ms = (jnp.sum(out_lo * out_lo, axis=-1, keepdims=True)
          + jnp.sum(out_hi * out_hi, axis=-1, keepdims=True)) * (1.0 / D_MODEL)
    inv = lax.rsqrt(ms + EPS)
    o_ref[:, :HALF] = out_lo * inv * nw_ref[:, :HALF]
    o_ref[:, HALF:] = out_hi * inv * nw_ref[:, HALF:]


def _combine(dest, h1, wts, nw, y):
    t = h1.shape[0]
    ts = COMBINE_TOKENS
    n_steps = t // ts
    return pl.pallas_call(
        _combine_kernel,
        name="combine",
        grid=(n_steps,),
        in_specs=[pl.BlockSpec((1, 1, RANK_BLOCK), lambda i: (i, 0, 0), memory_space=pltpu.SMEM),
                  pl.BlockSpec((1, 1, RANK_BLOCK), lambda i: (jnp.minimum(i + 1, n_steps - 1), 0, 0),
                               memory_space=pltpu.SMEM),
                  pl.BlockSpec((ts, D_MODEL), lambda i: (i, 0)),
                  pl.BlockSpec((ts, LANES), lambda i: (i, 0)),
                  pl.BlockSpec((1, D_MODEL), lambda i: (0, 0)),
                  pl.BlockSpec(memory_space=pl.ANY)],
        out_specs=pl.BlockSpec((ts, D_MODEL), lambda i: (i, 0)),
        out_shape=jax.ShapeDtypeStruct((t, D_MODEL), F32),
        scratch_shapes=[pltpu.VMEM((2, TOP_K, ts // SUBLANES, SUBLANES, HALF), U32),
                        pltpu.SemaphoreType.DMA((2,))],
        compiler_params=_cparams(("arbitrary",)),
    )(dest, dest, h1, wts, nw, y.reshape(-1, SUBLANES, HALF))


def _narrow_w_in(w_in_t):
    dt0 = SSD_DIM + CONV_DIM
    f0 = dt0 + SSD_HEADS + 3 * FOX_DIM
    dt = w_in_t[dt0:dt0 + SSD_HEADS]
    f = w_in_t[f0:f0 + FOX_HEADS]
    pad = jnp.zeros((LANES - SMALL_F - FOX_HEADS, w_in_t.shape[1]), w_in_t.dtype)
    return jnp.concatenate([dt] * SMALL_DT_COPIES + [f, pad], axis=0).T.astype(BF16)


def _layer(h, p, bsz, seq):
    t = bsz * seq
    w_in_t = jnp.swapaxes(p["w_in"].astype(F32), 0, 1)
    u, small = _norm_small(h, p["norm_mix_w"].astype(F32).reshape(1, D_MODEL), _narrow_w_in(w_in_t))
    proj = _in_proj(u, w_in_t)
    y_ssd = _ssd(proj, small, p["conv_w"], p["conv_b"], p["dt_bias"], p["a_log"], p["d_skip"],
                 p["ssd_norm_w"], bsz, seq)
    cum = _fox_cum(small, p["fox_f_bias"], bsz, seq)
    y_fox = _fox(proj, cum, bsz, seq)
    mixed = _mix(y_ssd, y_fox, proj, p["w_proj_ssd"].astype(BF16), p["w_proj_fox"].astype(BF16))

    w_router = jnp.concatenate(
        [p["w_router_group"], p["w_router_expert"],
         jnp.zeros((D_MODEL, LANES - N_GROUPS - N_EXPERTS), F32)], axis=1).astype(F32)
    w_router_hi = w_router.astype(BF16)
    w_router_lo = (w_router - w_router_hi.astype(F32)).astype(BF16)
    w_router = jnp.concatenate([w_router_hi, w_router_lo], axis=1)
    b_router = jnp.concatenate(
        [p["b_router_group"], p["b_router_expert"],
         jnp.zeros((LANES - N_GROUPS - N_EXPERTS,), F32)]).astype(F32).reshape(1, LANES)
    h1, u2p, eidx, wts = _outproj(mixed, h, p["w_out"].astype(BF16),
                                  p["norm_moe_w"].astype(F32).reshape(1, D_MODEL), w_router, b_router)

    tk = t * TOP_K
    e_blocks = eidx[:, :TOP_K].reshape(tk // RANK_BLOCK, 1, RANK_BLOCK)
    rank, counts = _rank(e_blocks)
    counts = counts[:, 0].astype(I32)
    padded = ((counts + ROW_BLOCK - 1) // ROW_BLOCK) * ROW_BLOCK
    pend = jnp.cumsum(padded)
    pstart = pend - padded
    n_blocks = tk // ROW_BLOCK + N_EXPERTS
    dest = _dest(e_blocks, rank, jnp.broadcast_to(pstart[:, None], (N_EXPERTS, LANES)).astype(I32))
    block_start = (pstart // ROW_BLOCK).astype(I32)
    block_count = (padded // ROW_BLOCK).astype(I32)
    last_blk = jnp.where(block_count > 0, block_start + block_count - 1, -1).astype(I32)
    n_used = (pend[-1:] // ROW_BLOCK).astype(I32)
    xs = _dispatch(dest, u2p, n_blocks * ROW_BLOCK, last_blk, n_used)
    y = _experts(block_start, block_count, xs, p["w_gate_exp"], p["w_up_exp"], p["w_down_exp"])
    return dest, h1, wts, y


def kernel(x, norm_mix_w, w_in, conv_w, conv_b, dt_bias, a_log, d_skip, ssd_norm_w, fox_f_bias, w_proj_ssd,
           w_proj_fox, w_out, norm_moe_w, w_router_group, b_router_group, w_router_expert, b_router_expert,
           w_gate_exp, w_up_exp, w_down_exp, norm_final_w):
    bsz, seq, _ = x.shape
    depth = w_in.shape[0]
    assert depth == 1, "the fused final norm assumes a single layer"
    stacked = dict(norm_mix_w=norm_mix_w, w_in=w_in, conv_w=conv_w, conv_b=conv_b, dt_bias=dt_bias, a_log=a_log,
                   d_skip=d_skip, ssd_norm_w=ssd_norm_w, fox_f_bias=fox_f_bias, w_proj_ssd=w_proj_ssd,
                   w_proj_fox=w_proj_fox, w_out=w_out, norm_moe_w=norm_moe_w, w_router_group=w_router_group,
                   b_router_group=b_router_group, w_router_expert=w_router_expert,
                   b_router_expert=b_router_expert, w_gate_exp=w_gate_exp, w_up_exp=w_up_exp,
                   w_down_exp=w_down_exp)
    p = {name: v[0] for name, v in stacked.items()}
    h = x.reshape(bsz * seq, D_MODEL)
    dest, h1, wts, y = _layer(h, p, bsz, seq)
    out = _combine(dest, h1, wts, norm_final_w.astype(F32).reshape(1, D_MODEL), y)
    return out.reshape(bsz, seq, D_MODEL)
```

```python
import functools
import math

import jax
import jax.numpy as jnp
from jax import lax
from jax.experimental import pallas as pl
from jax.experimental.pallas import tpu as pltpu

F32 = jnp.float32
BF16 = jnp.bfloat16
I32 = jnp.int32
U32 = jnp.uint32

D_MODEL = 2048
SSD_HEADS = 32
SSD_HEAD_DIM = 64
SSD_DIM = SSD_HEADS * SSD_HEAD_DIM
SSD_GROUPS = 4
SSD_STATE = 128
SSD_CHUNK = 128
CONV_WIDTH = 4
CONV_DIM = SSD_DIM + 2 * SSD_GROUPS * SSD_STATE
FOX_HEADS = 16
FOX_HEAD_DIM = 128
FOX_DIM = FOX_HEADS * FOX_HEAD_DIM
N_GROUPS = 8
EXPERTS_PER_GROUP = 8
N_EXPERTS = N_GROUPS * EXPERTS_PER_GROUP
TOP_K = 2
D_EXPERT = 512
EPS = 1e-6

LANES = 128
SUBLANES = 8
VMEM_LIMIT = 52 * 1024 * 1024

COL_Z = 0
COL_Q = COL_Z + SSD_DIM
COL_K = COL_Q + FOX_DIM
COL_V = COL_K + FOX_DIM
COL_GA = COL_V + FOX_DIM
COL_GB = COL_GA + D_MODEL
COL_XBC = COL_GB + D_MODEL
PROJ_COLS = COL_XBC + CONV_DIM
SMALL_DT = 0
SMALL_DT_COPIES = 3
SMALL_F = SMALL_DT_COPIES * SSD_HEADS

ROW_BLOCK = 128
HALF = D_MODEL // 2


def _cparams(sem, vmem=VMEM_LIMIT):
    return pltpu.CompilerParams(dimension_semantics=sem, vmem_limit_bytes=vmem)


def _silu(x):
    return x * (1.0 / (1.0 + jnp.exp(-x)))


def _softplus(x):
    return jnp.maximum(x, 0.0) + jnp.log(1.0 + jnp.exp(-jnp.abs(x)))


def _log_sigmoid(x):
    return -_softplus(-x)


def _split3(x):
    hi = x.astype(BF16)
    rest = x - hi.astype(F32)
    mid = rest.astype(BF16)
    lo = (rest - mid.astype(F32)).astype(BF16)
    return hi, mid, lo


def _pack_bf16_pair(x):
    n = x.shape[1] // 2
    lo = pltpu.bitcast(x[:, :n].astype(BF16).astype(F32), U32)
    hi = pltpu.bitcast(x[:, n:].astype(BF16).astype(F32), U32)
    return (hi & jnp.uint32(0xFFFF0000)) | (lo >> 16)


def _unpack_bf16_pair(p):
    lo = pltpu.bitcast(p << 16, F32)
    hi = pltpu.bitcast(p & jnp.uint32(0xFFFF0000), F32)
    return lo, hi


def _norm_small_kernel(x_ref, nw_ref, ws_ref, u_ref, s_ref):
    x = x_ref[...]
    ms = jnp.mean(x * x, axis=-1, keepdims=True)
    u = (x * lax.rsqrt(ms + EPS) * nw_ref[...]).astype(BF16)
    u_ref[...] = u
    s_ref[...] = jnp.dot(u, ws_ref[...], preferred_element_type=F32)


def _norm_small(x2, nw, w_small, tm=512):
    t = x2.shape[0]
    return pl.pallas_call(
        _norm_small_kernel,
        name="norm_small",
        grid=(t // tm,),
        in_specs=[pl.BlockSpec((tm, D_MODEL), lambda i: (i, 0)),
                  pl.BlockSpec((1, D_MODEL), lambda i: (0, 0)),
                  pl.BlockSpec((D_MODEL, LANES), lambda i: (0, 0))],
        out_specs=[pl.BlockSpec((tm, D_MODEL), lambda i: (i, 0)),
                   pl.BlockSpec((tm, LANES), lambda i: (i, 0))],
        out_shape=[jax.ShapeDtypeStruct((t, D_MODEL), BF16),
                   jax.ShapeDtypeStruct((t, LANES), F32)],
        compiler_params=_cparams(("parallel",)),
    )(x2, nw, w_small)


IN_TN = 1024
IN_XPOSE = 256
_IN_SEGMENTS = ((COL_Z, 0, SSD_DIM),
                (COL_Q, SSD_DIM + CONV_DIM + SSD_HEADS, 3 * FOX_DIM),
                (COL_GA, SSD_DIM + CONV_DIM + SSD_HEADS + 3 * FOX_DIM + FOX_HEADS, 2 * D_MODEL),
                (COL_XBC, SSD_DIM, CONV_DIM))


def _in_proj_source_rows():
    src = [0] * (PROJ_COLS // IN_TN)
    for out0, src0, width in _IN_SEGMENTS:
        for c in range(0, width, IN_TN):
            src[(out0 + c) // IN_TN] = src0 + c
    assert all(s % SUBLANES == 0 for s in src)
    return src


def _in_proj_kernel(row_ref, u_ref, wt_hbm, o_ref, st_ref, wbf_ref, sem):
    j = pl.program_id(0)
    i = pl.program_id(1)
    k = wbf_ref.shape[0]

    def window(jj, slot):
        row0 = pl.multiple_of(row_ref[jj], SUBLANES)
        return pltpu.make_async_copy(wt_hbm.at[pl.ds(row0, IN_TN), :], st_ref.at[slot], sem.at[slot])

    @pl.when((j == 0) & (i == 0))
    def _():
        window(0, 0).start()

    @pl.when(i == 0)
    def _():
        @pl.when(j + 1 < pl.num_programs(0))
        def _():
            window(j + 1, (j + 1) % 2).start()

        slot = j % 2
        window(j, slot).wait()
        for c in range(0, k, IN_XPOSE):
            wbf_ref[c:c + IN_XPOSE, :] = jnp.transpose(st_ref[slot, :, c:c + IN_XPOSE]).astype(BF16)

    o_ref[...] = jnp.dot(u_ref[...], wbf_ref[...], preferred_element_type=F32).astype(o_ref.dtype)


def _in_proj(u, w_in_t, tm=1024):
    m, k = u.shape
    grid_spec = pltpu.PrefetchScalarGridSpec(
        num_scalar_prefetch=1,
        grid=(PROJ_COLS // IN_TN, m // tm),
        in_specs=[pl.BlockSpec((tm, k), lambda j, i, rows: (i, 0)),
                  pl.BlockSpec(memory_space=pl.ANY)],
        out_specs=pl.BlockSpec((tm, IN_TN), lambda j, i, rows: (i, j)),
        scratch_shapes=[pltpu.VMEM((2, IN_TN, k), F32),
                        pltpu.VMEM((k, IN_TN), BF16),
                        pltpu.SemaphoreType.DMA((2,))],
    )
    return pl.pallas_call(
        _in_proj_kernel,
        name="in_proj",
        grid_spec=grid_spec,
        out_shape=jax.ShapeDtypeStruct((m, PROJ_COLS), BF16),
        compiler_params=_cparams(("arbitrary", "arbitrary")),
    )(jnp.asarray(_in_proj_source_rows(), I32), u, w_in_t)


HALO = 2 * SUBLANES


def _ssd_kernel(z_ref, xbc_ref, halo_ref, small_ref, cw_ref, cb_ref, dtb_ref, aneg_ref, dexp_ref,
                nw_ref, expand_ref, shift_ref, y_ref, state_ref, ydiag_ref):
    c = pl.program_id(1)
    l = SSD_CHUNK
    n = SSD_STATE

    @pl.when(c == 0)
    def _():
        state_ref[...] = jnp.zeros_like(state_ref)

    cur = xbc_ref[...]
    halo = halo_ref[...]
    halo = jnp.where(c == 0, jnp.zeros_like(halo), halo)
    ext = jnp.concatenate([halo, cur], axis=0)
    shifted = jnp.dot(shift_ref[...], ext, preferred_element_type=F32)
    conv = cb_ref[...] + cw_ref[CONV_WIDTH - 1:CONV_WIDTH, :] * cur.astype(F32)
    for j in range(CONV_WIDTH - 1):
        conv = conv + cw_ref[j:j + 1, :] * shifted[j * l:(j + 1) * l]
    xbc = _silu(conv)
    xs = xbc[:, :SSD_DIM]
    bm = xbc[:, SSD_DIM:SSD_DIM + SSD_GROUPS * n]
    cm = xbc[:, SSD_DIM + SSD_GROUPS * n:]

    h3 = SMALL_DT_COPIES * SSD_HEADS
    dt3 = _softplus(small_ref[:, SMALL_DT:SMALL_DT + h3] + dtb_ref[...])
    adt3 = dt3 * aneg_ref[...]
    row = lax.broadcasted_iota(I32, (l, l), 0)
    col = lax.broadcasted_iota(I32, (l, l), 1)
    causal = col <= row
    tril = jnp.where(causal, 1.0, 0.0).astype(BF16)
    a_cs3 = sum(jnp.dot(tril, piece, preferred_element_type=F32) for piece in _split3(adt3))
    a_cs = a_cs3[:, :SSD_HEADS]
    a_cs_t = jnp.transpose(a_cs)
    a_last3 = a_cs3[l - 1:l, :]

    lane3 = lax.broadcasted_iota(I32, (l, h3), 1)

    def pieces_by_lane_group(x3):
        hi, mid, lo = _split3(x3)
        return jnp.where(lane3 < SSD_HEADS, hi, jnp.where(lane3 < 2 * SSD_HEADS, mid, lo))

    lhs3 = jnp.concatenate([pieces_by_lane_group(dt3),
                            pieces_by_lane_group(jnp.exp(a_cs3)),
                            pieces_by_lane_group(jnp.exp(a_last3 - a_cs3))], axis=0)
    expanded = jnp.dot(lhs3, expand_ref[...], preferred_element_type=F32)
    dt_x = expanded[0:l]
    decay_in = expanded[l:2 * l]
    decay_out = expanded[2 * l:3 * l]
    chunk_decay = decay_in[l - 1:l, :]
    x_dt = xs * dt_x
    x_dt_b = x_dt.astype(BF16)
    xd_b = (x_dt * decay_out).astype(BF16)

    lane = lax.broadcasted_iota(I32, (l, LANES), 1)
    lo_mask = lane < SSD_HEAD_DIM
    heads_per_group = SSD_HEADS // SSD_GROUPS
    gw = heads_per_group * SSD_HEAD_DIM

    for g in range(SSD_GROUPS):
        bg = bm[:, g * n:(g + 1) * n].astype(BF16)
        cg = cm[:, g * n:(g + 1) * n].astype(BF16)
        cb = lax.dot_general(cg, bg, (((1,), (1,)), ((), ())), preferred_element_type=F32)
        for pair in range(heads_per_group // 2):
            h0 = g * heads_per_group + 2 * pair
            lane0 = h0 * SSD_HEAD_DIM
            ms = []
            for h in (h0, h0 + 1):
                seg = a_cs[:, h:h + 1] - a_cs_t[h:h + 1, :]
                lmat = jnp.exp(jnp.where(causal, seg, -jnp.inf))
                ms.append((cb * lmat).astype(BF16))
            lhs = jnp.concatenate(ms, axis=1)
            xp = x_dt_b[:, lane0:lane0 + LANES]
            zero = jnp.zeros_like(xp)
            rhs = jnp.concatenate([jnp.where(lo_mask, xp, zero), jnp.where(lo_mask, zero, xp)], axis=0)
            ydiag_ref[:, lane0:lane0 + LANES] = jnp.dot(lhs, rhs, preferred_element_type=F32)
        st = state_ref[:, g * gw:(g + 1) * gw]
        y_off = jnp.dot(cg, st.astype(BF16), preferred_element_type=F32)
        ydiag_ref[:, g * gw:(g + 1) * gw] += y_off * decay_in[:, g * gw:(g + 1) * gw]
        new = lax.dot_general(bg, xd_b[:, g * gw:(g + 1) * gw], (((0,), (0,)), ((), ())),
                              preferred_element_type=F32)
        state_ref[:, g * gw:(g + 1) * gw] = st * chunk_decay[:, g * gw:(g + 1) * gw] + new

    y = ydiag_ref[...] + dexp_ref[...] * xs
    y = y * _silu(z_ref[...].astype(F32))
    ms = jnp.mean(y * y, axis=-1, keepdims=True)
    y_ref[...] = (y * lax.rsqrt(ms + EPS) * nw_ref[...]).astype(y_ref.dtype)


def _ssd(proj, small, conv_w, conv_b, dt_bias, a_log, d_skip, norm_w, bsz, seq):
    l = SSD_CHUNK
    nc = seq // l
    t = bsz * seq
    h3 = SMALL_DT_COPIES * SSD_HEADS
    aneg = jnp.tile(-jnp.exp(a_log.astype(F32)), SMALL_DT_COPIES).reshape(1, h3)
    dtb = jnp.tile(dt_bias.astype(F32), SMALL_DT_COPIES).reshape(1, h3)
    dexp = jnp.repeat(d_skip.astype(F32), SSD_HEAD_DIM).reshape(1, SSD_DIM)
    expand = jnp.tile(jnp.repeat(jnp.eye(SSD_HEADS, dtype=BF16), SSD_HEAD_DIM, axis=1),
                      (SMALL_DT_COPIES, 1))
    out_row = jnp.arange((CONV_WIDTH - 1) * l)
    src_row = HALO + out_row % l - (CONV_WIDTH - 1) + out_row // l
    shift = (jnp.arange(HALO + l)[None, :] == src_row[:, None]).astype(BF16)
    xbc_blk = COL_XBC // CONV_DIM
    halo_per_chunk = l // HALO

    def row_map(b, c):
        return b * nc + c

    return pl.pallas_call(
        _ssd_kernel,
        name="ssd",
        grid=(bsz, nc),
        in_specs=[
            pl.BlockSpec((l, SSD_DIM), lambda b, c: (row_map(b, c), COL_Z // SSD_DIM)),
            pl.BlockSpec((l, CONV_DIM), lambda b, c: (row_map(b, c), xbc_blk)),
            pl.BlockSpec((HALO, CONV_DIM),
                         lambda b, c: (jnp.maximum(row_map(b, c) * halo_per_chunk - 1, 0), xbc_blk)),
            pl.BlockSpec((l, LANES), lambda b, c: (row_map(b, c), 0)),
            pl.BlockSpec((CONV_WIDTH, CONV_DIM), lambda b, c: (0, 0)),
            pl.BlockSpec((1, CONV_DIM), lambda b, c: (0, 0)),
            pl.BlockSpec((1, h3), lambda b, c: (0, 0)),
            pl.BlockSpec((1, h3), lambda b, c: (0, 0)),
            pl.BlockSpec((1, SSD_DIM), lambda b, c: (0, 0)),
            pl.BlockSpec((1, SSD_DIM), lambda b, c: (0, 0)),
            pl.BlockSpec((h3, SSD_DIM), lambda b, c: (0, 0)),
            pl.BlockSpec(((CONV_WIDTH - 1) * l, HALO + l), lambda b, c: (0, 0)),
        ],
        out_specs=pl.BlockSpec((l, SSD_DIM), lambda b, c: (row_map(b, c), 0)),
        out_shape=jax.ShapeDtypeStruct((t, SSD_DIM), BF16),
        scratch_shapes=[pltpu.VMEM((SSD_STATE, SSD_DIM), F32),
                        pltpu.VMEM((l, SSD_DIM), F32)],
        compiler_params=_cparams(("parallel", "arbitrary")),
    )(proj, proj, proj, small, conv_w.astype(F32), conv_b.astype(F32).reshape(1, CONV_DIM),
      dtb, aneg, dexp, norm_w.astype(F32).reshape(1, SSD_DIM), expand, shift)


CUM_ROWS = 256


def _cum_kernel(small_ref, fb_ref, cum_ref, carry_ref):
    j = pl.program_id(1)

    @pl.when(j == 0)
    def _():
        carry_ref[...] = jnp.zeros_like(carry_ref)

    lf = _log_sigmoid(small_ref[...] + fb_ref[...])
    row = lax.broadcasted_iota(I32, (CUM_ROWS, CUM_ROWS), 0)
    col = lax.broadcasted_iota(I32, (CUM_ROWS, CUM_ROWS), 1)
    tril = jnp.where(col <= row, 1.0, 0.0).astype(F32)
    cs = jnp.dot(tril, lf, preferred_element_type=F32, precision=lax.Precision.HIGHEST) + carry_ref[...]
    cum_ref[...] = cs * LOG2E
    carry_ref[...] = cs[CUM_ROWS - 1:CUM_ROWS, :]


def _fox_cum(small, fox_f_bias, bsz, seq):
    fb = jnp.zeros((1, LANES), F32).at[0, SMALL_F:SMALL_F + FOX_HEADS].set(fox_f_bias.astype(F32))
    nj = seq // CUM_ROWS
    return pl.pallas_call(
        _cum_kernel,
        name="fox_cum",
        grid=(bsz, nj),
        in_specs=[pl.BlockSpec((CUM_ROWS, LANES), lambda b, j: (b * nj + j, 0)),
                  pl.BlockSpec((1, LANES), lambda b, j: (0, 0))],
        out_specs=pl.BlockSpec((CUM_ROWS, LANES), lambda b, j: (b * nj + j, 0)),
        out_shape=jax.ShapeDtypeStruct((bsz * seq, LANES), F32),
        scratch_shapes=[pltpu.VMEM((1, LANES), F32)],
        compiler_params=_cparams(("parallel", "arbitrary")),
    )(small, fb)


FOX_HEADS_PER_STEP = 2
LOG2E = 1.4426950408889634


FOX_SLAB = 128


def _fox_kernel(q_ref, k_ref, v_ref, cq_ref, ck_ref, o_ref, sa_ref, sb_ref, p_ref, m_ref, l_ref, alpha_ref,
                cqrep_ref, acc_ref, *, tq, tk):
    hp = pl.program_id(1)
    qi = pl.program_id(2)
    d = FOX_HEAD_DIM
    c2 = LOG2E / math.sqrt(d)
    lane = lax.broadcasted_iota(I32, (tq, LANES), 1)
    cq_all = cq_ref[...]

    qs, cqs = [], []
    for hh in range(FOX_HEADS_PER_STEP):
        qs.append((q_ref[:, hh * d:(hh + 1) * d].astype(F32) * c2).astype(BF16))
        head_lane = SMALL_F + hp * FOX_HEADS_PER_STEP + hh
        cq_col = jnp.sum(jnp.where(lane == head_lane, cq_all, 0.0), axis=-1, keepdims=True)
        cqs.append(jnp.broadcast_to(cq_col, (tq, LANES)))

    s_slots = (sa_ref, sb_ref)

    def scores(ki, slot):
        row0 = pl.multiple_of(ki * tk, tk)
        for hh in range(FOX_HEADS_PER_STEP):
            k = k_ref[pl.ds(row0, tk), hh * d:(hh + 1) * d]
            s = lax.dot_general(qs[hh], k, (((1,), (1,)), ((), ())), preferred_element_type=F32)
            s_slots[slot][hh] = s - ck_ref[hh, pl.ds(ki, 1), :]

    def update(ki, slot, masked):
        row0 = pl.multiple_of(ki * tk, tk)
        s_ref = s_slots[slot]
        n_ct = tk // LANES
        for hh in range(FOX_HEADS_PER_STEP):
            for rc in range(tq // FOX_SLAB):
                rows = slice(rc * FOX_SLAB, (rc + 1) * FOX_SLAB)
                n_vis = rc + 1 if masked else n_ct

                def slab(ct):
                    x = s_ref[hh, rows, ct * LANES:(ct + 1) * LANES]
                    if masked and ct == rc:
                        row = lax.broadcasted_iota(I32, (FOX_SLAB, LANES), 0)
                        col = lax.broadcasted_iota(I32, (FOX_SLAB, LANES), 1)
                        x = jnp.where(col <= row, x, -jnp.inf)
                    return x

                tmax = slab(0)
                for ct in range(1, n_vis):
                    tmax = jnp.maximum(tmax, slab(ct))
                row_max = jnp.max(tmax, axis=-1, keepdims=True)
                cq = cqrep_ref[hh, rows, :]
                m_prev = m_ref[hh, rows, :]
                m_new = jnp.maximum(m_prev, jnp.broadcast_to(row_max, (FOX_SLAB, LANES)) + cq)
                r = m_new - cq
                alpha = jnp.exp2(m_prev - m_new)
                psum = None
                for ct in range(n_vis):
                    p = jnp.exp2(slab(ct) - r)
                    psum = p if psum is None else psum + p
                    p_ref[hh, rows, ct * LANES:(ct + 1) * LANES] = p.astype(BF16)
                for ct in range(n_vis, n_ct):
                    p_ref[hh, rows, ct * LANES:(ct + 1) * LANES] = jnp.zeros((FOX_SLAB, LANES), BF16)
                l_ref[hh, rows, :] = alpha * l_ref[hh, rows, :] + psum
                m_ref[hh, rows, :] = m_new
                alpha_ref[hh, rows, :] = alpha
            v = v_ref[pl.ds(row0, tk), hh * d:(hh + 1) * d]
            acc_ref[hh] = alpha_ref[hh] * acc_ref[hh] + jnp.dot(p_ref[hh], v, preferred_element_type=F32)

    for hh in range(FOX_HEADS_PER_STEP):
        cqrep_ref[hh] = cqs[hh]

    m_ref[...] = jnp.full_like(m_ref, -jnp.inf)
    l_ref[...] = jnp.zeros_like(l_ref)
    acc_ref[...] = jnp.zeros_like(acc_ref)

    n_full = qi
    scores(0, 0)

    def pair(j, carry):
        b0 = 2 * j
        scores(b0 + 1, 1)
        update(b0, 0, False)
        scores(b0 + 2, 0)
        update(b0 + 1, 1, False)
        return carry

    lax.fori_loop(0, n_full // 2, pair, 0)

    @pl.when(n_full % 2 == 0)
    def _():
        update(n_full, 0, True)

    @pl.when(n_full % 2 == 1)
    def _():
        scores(n_full, 1)
        update(n_full - 1, 0, False)
        update(n_full, 1, True)

    for hh in range(FOX_HEADS_PER_STEP):
        l_fin = jnp.sum(l_ref[hh], axis=-1, keepdims=True)
        o_ref[:, hh * d:(hh + 1) * d] = (acc_ref[hh] / l_fin).astype(o_ref.dtype)


def _fox(proj, cum, bsz, seq, tq=1024):
    assert FOX_SLAB == LANES and tq % FOX_SLAB == 0 and seq % tq == 0
    tk = tq
    nq = seq // tq
    nk = seq // tk
    hps = FOX_HEADS_PER_STEP
    cum_row = cum[:, SMALL_F:SMALL_F + FOX_HEADS].reshape(bsz, seq, FOX_HEADS).transpose(0, 2, 1)
    cum_row = cum_row.reshape(bsz, FOX_HEADS, nk, tk)
    t = bsz * seq
    w = FOX_HEADS_PER_STEP * FOX_HEAD_DIM
    n_hp = FOX_HEADS // FOX_HEADS_PER_STEP
    kern = functools.partial(_fox_kernel, tq=tq, tk=tk)
    return pl.pallas_call(
        kern,
        name="fox",
        grid=(bsz, n_hp, nq),
        in_specs=[
            pl.BlockSpec((tq, w), lambda b, hp, qi: (b * nq + qi, COL_Q // w + hp)),
            pl.BlockSpec((seq, w), lambda b, hp, qi: (b, COL_K // w + hp)),
            pl.BlockSpec((seq, w), lambda b, hp, qi: (b, COL_V // w + hp)),
            pl.BlockSpec((tq, LANES), lambda b, hp, qi: (b * nq + qi, 0)),
            pl.BlockSpec((None, FOX_HEADS_PER_STEP, nk, tk), lambda b, hp, qi: (b, hp, 0, 0)),
        ],
        out_specs=pl.BlockSpec((tq, w), lambda b, hp, qi: (b * nq + qi, hp)),
        out_shape=jax.ShapeDtypeStruct((t, FOX_DIM), BF16),
        scratch_shapes=[pltpu.VMEM((hps, tq, tk), F32), pltpu.VMEM((hps, tq, tk), F32),
                        pltpu.VMEM((hps, tq, tk), BF16),
                        pltpu.VMEM((hps, tq, LANES), F32), pltpu.VMEM((hps, tq, LANES), F32),
                        pltpu.VMEM((hps, tq, LANES), F32), pltpu.VMEM((hps, tq, LANES), F32),
                        pltpu.VMEM((hps, tq, FOX_HEAD_DIM), F32)],
        compiler_params=_cparams(("parallel", "parallel", "arbitrary")),
    )(proj, proj, proj, cum, cum_row)


def _mix_kernel(ya_ref, yb_ref, ga_ref, gb_ref, wa_ref, wb_ref, o_ref):
    pa = jnp.dot(ya_ref[...], wa_ref[...], preferred_element_type=F32)
    pb = jnp.dot(yb_ref[...], wb_ref[...], preferred_element_type=F32)
    ga = 1.0 / (1.0 + jnp.exp(-ga_ref[...].astype(F32)))
    gb = 1.0 / (1.0 + jnp.exp(-gb_ref[...].astype(F32)))
    o_ref[...] = (ga * pa + gb * pb).astype(o_ref.dtype)


def _mix(y_a, y_b, proj, w_a, w_b, tm=512, tn=1024):
    t = y_a.shape[0]
    return pl.pallas_call(
        _mix_kernel,
        name="mix",
        grid=(D_MODEL // tn, t // tm),
        in_specs=[
            pl.BlockSpec((tm, SSD_DIM), lambda j, i: (i, 0)),
            pl.BlockSpec((tm, FOX_DIM), lambda j, i: (i, 0)),
            pl.BlockSpec((tm, tn), lambda j, i: (i, COL_GA // tn + j)),
            pl.BlockSpec((tm, tn), lambda j, i: (i, COL_GB // tn + j)),
            pl.BlockSpec((SSD_DIM, tn), lambda j, i: (0, j)),
            pl.BlockSpec((FOX_DIM, tn), lambda j, i: (0, j)),
        ],
        out_specs=pl.BlockSpec((tm, tn), lambda j, i: (i, j)),
        out_shape=jax.ShapeDtypeStruct((t, D_MODEL), BF16),
        compiler_params=_cparams(("parallel", "parallel")),
    )(y_a, y_b, proj, proj, w_a, w_b)


def _outproj_kernel(m_ref, x_ref, wo_ref, nw_ref, wr_ref, br_ref, h_ref, u_ref, eidx_ref, wts_ref):
    h1 = x_ref[...] + jnp.dot(m_ref[...], wo_ref[...], preferred_element_type=F32)
    h_ref[...] = h1
    ms = jnp.mean(h1 * h1, axis=-1, keepdims=True)
    u2 = h1 * lax.rsqrt(ms + EPS) * nw_ref[...]
    u_ref[...] = _pack_bf16_pair(u2)

    u_hi = u2.astype(BF16)
    u_lo = (u2 - u_hi.astype(F32)).astype(BF16)
    hh_hl = jnp.dot(u_hi, wr_ref[...], preferred_element_type=F32)
    lh = jnp.dot(u_lo, wr_ref[:, :LANES], preferred_element_type=F32)
    logits = hh_hl[:, :LANES] + (hh_hl[:, LANES:] + lh) + br_ref[...]
    tm = logits.shape[0]
    lane = lax.broadcasted_iota(I32, (tm, LANES), 1)
    neg = -jnp.inf
    big = jnp.int32(2 * LANES)
    gl = jnp.where(lane < N_GROUPS, logits, neg)
    gmax = jnp.max(gl, axis=-1, keepdims=True)
    gsum = jnp.sum(jnp.exp(gl - gmax), axis=-1, keepdims=True)
    g_p = 1.0 / gsum
    g_idx = jnp.min(jnp.where(gl == gmax, lane, big), axis=-1, keepdims=True)
    e_of_lane = lane - N_GROUPS
    in_grp = (e_of_lane >= g_idx * EXPERTS_PER_GROUP) & (e_of_lane < (g_idx + 1) * EXPERTS_PER_GROUP)
    el = jnp.where(in_grp, logits, neg)
    m1 = jnp.max(el, axis=-1, keepdims=True)
    i1 = jnp.min(jnp.where(el == m1, lane, big), axis=-1, keepdims=True)
    el2 = jnp.where(lane == i1, neg, el)
    m2 = jnp.max(el2, axis=-1, keepdims=True)
    i2 = jnp.min(jnp.where(el2 == m2, lane, big), axis=-1, keepdims=True)
    esum = jnp.sum(jnp.exp(el - m1), axis=-1, keepdims=True)
    p1 = 1.0 / esum
    p2 = jnp.exp(m2 - m1) / esum
    w1 = g_p * (p1 / (p1 + p2))
    w2 = g_p * (p2 / (p1 + p2))
    eidx_ref[...] = jnp.where(lane == 0, i1 - N_GROUPS, jnp.where(lane == 1, i2 - N_GROUPS, 0))
    wts_ref[...] = jnp.where(lane == 0, w1, jnp.where(lane == 1, w2, 0.0))


def _outproj(mixed, x2, w_o, nw, w_router, b_router, tm=512):
    t = mixed.shape[0]
    resident = pl.Buffered(1)
    return pl.pallas_call(
        _outproj_kernel,
        name="outproj",
        grid=(t // tm,),
        in_specs=[
            pl.BlockSpec((tm, D_MODEL), lambda i: (i, 0)),
            pl.BlockSpec((tm, D_MODEL), lambda i: (i, 0)),
            pl.BlockSpec((D_MODEL, D_MODEL), lambda i: (0, 0), pipeline_mode=resident),
            pl.BlockSpec((1, D_MODEL), lambda i: (0, 0)),
            pl.BlockSpec((D_MODEL, 2 * LANES), lambda i: (0, 0), pipeline_mode=resident),
            pl.BlockSpec((1, LANES), lambda i: (0, 0)),
        ],
        out_specs=[
            pl.BlockSpec((tm, D_MODEL), lambda i: (i, 0)),
            pl.BlockSpec((tm, HALF), lambda i: (i, 0)),
            pl.BlockSpec((tm, LANES), lambda i: (i, 0)),
            pl.BlockSpec((tm, LANES), lambda i: (i, 0)),
        ],
        out_shape=[
            jax.ShapeDtypeStruct((t, D_MODEL), F32),
            jax.ShapeDtypeStruct((t, HALF), U32),
            jax.ShapeDtypeStruct((t, LANES), I32),
            jax.ShapeDtypeStruct((t, LANES), F32),
        ],
        compiler_params=_cparams(("parallel",)),
    )(mixed, x2, w_o, nw, w_router, b_router)


RANK_BLOCK = 512


def _rank_kernel(e_ref, rank_ref, cnt_ref, carry_ref):
    i = pl.program_id(0)
    r = RANK_BLOCK

    @pl.when(i == 0)
    def _():
        carry_ref[...] = jnp.zeros_like(carry_ref)

    e = e_ref[0]
    expert = lax.broadcasted_iota(I32, (N_EXPERTS, r), 0)
    onehot = jnp.where(expert == e, 1.0, 0.0).astype(F32)
    jrow = lax.broadcasted_iota(I32, (r, r), 0)
    jcol = lax.broadcasted_iota(I32, (r, r), 1)
    before = jnp.where(jrow < jcol, 1.0, 0.0).astype(BF16)
    cum = jnp.dot(onehot.astype(BF16), before, preferred_element_type=F32)
    carry = carry_ref[...]
    rank = jnp.sum(onehot * (cum + carry[:, 0:1]), axis=0, keepdims=True)
    rank_ref[0] = rank.astype(I32)
    carry = carry + jnp.sum(onehot, axis=1, keepdims=True)
    carry_ref[...] = carry
    cnt_ref[...] = carry


def _rank(e_blocks):
    nb = e_blocks.shape[0]
    return pl.pallas_call(
        _rank_kernel,
        name="rank",
        grid=(nb,),
        in_specs=[pl.BlockSpec((1, 1, RANK_BLOCK), lambda i: (i, 0, 0))],
        out_specs=[pl.BlockSpec((1, 1, RANK_BLOCK), lambda i: (i, 0, 0)),
                   pl.BlockSpec((N_EXPERTS, LANES), lambda i: (0, 0))],
        out_shape=[jax.ShapeDtypeStruct((nb, 1, RANK_BLOCK), I32),
                   jax.ShapeDtypeStruct((N_EXPERTS, LANES), F32)],
        scratch_shapes=[pltpu.VMEM((N_EXPERTS, LANES), F32)],
        compiler_params=_cparams(("arbitrary",)),
    )(e_blocks)


def _dest_kernel(e_ref, rank_ref, pstart_ref, dest_ref):
    e = e_ref[0]
    expert = lax.broadcasted_iota(I32, (N_EXPERTS, RANK_BLOCK), 0)
    start = jnp.sum(jnp.where(expert == e, pstart_ref[:, 0:1], 0), axis=0, keepdims=True)
    dest_ref[0] = rank_ref[0] + start


def _dest(e_blocks, rank, pstart):
    nb = e_blocks.shape[0]
    return pl.pallas_call(
        _dest_kernel,
        name="dest",
        grid=(nb,),
        in_specs=[pl.BlockSpec((1, 1, RANK_BLOCK), lambda i: (i, 0, 0)),
                  pl.BlockSpec((1, 1, RANK_BLOCK), lambda i: (i, 0, 0)),
                  pl.BlockSpec((N_EXPERTS, LANES), lambda i: (0, 0))],
        out_specs=pl.BlockSpec((1, 1, RANK_BLOCK), lambda i: (i, 0, 0)),
        out_shape=jax.ShapeDtypeStruct((nb, 1, RANK_BLOCK), I32),
        compiler_params=_cparams(("parallel",)),
    )(e_blocks, rank, pstart)


def _row_copy(src_ref, src_row, dst_ref, dst_row, sem):
    return pltpu.make_async_copy(src_ref.at[pl.ds(src_row, 1)], dst_ref.at[pl.ds(dst_row, 1)], sem)


DMA_UNROLL = 16


def _dispatch_kernel(last_blk_ref, nused_ref, dest_ref, u_ref, xs_ref, zbuf_ref, sem, zsem):
    i = pl.program_id(0)
    n_total = xs_ref.shape[0] // ROW_BLOCK

    @pl.when(i == 0)
    def _():
        zbuf_ref[...] = jnp.zeros_like(zbuf_ref)

        def zero_block(blk):
            return pltpu.make_async_copy(zbuf_ref, xs_ref.at[pl.ds(blk * ROW_BLOCK, ROW_BLOCK)], zsem)

        def fill_expert(e, count):
            blk = last_blk_ref[e]

            @pl.when(blk >= 0)
            def _():
                zero_block(blk).start()

            return count + jnp.where(blk >= 0, 1, 0)

        n_fill = lax.fori_loop(0, N_EXPERTS, fill_expert, 0)

        def fill_tail(blk, carry):
            zero_block(blk).start()
            return carry

        lax.fori_loop(nused_ref[0], n_total, fill_tail, 0)

        def drain(k, carry):
            zero_block(0).wait()
            return carry

        lax.fori_loop(0, n_fill + n_total - nused_ref[0], drain, 0)

    def issue(g, carry):
        c0 = g * DMA_UNROLL
        for j in range(DMA_UNROLL):
            dst_row = dest_ref[0, 0, c0 + j]
            pltpu.make_async_copy(u_ref.at[g, pl.ds(j // TOP_K, 1)], xs_ref.at[pl.ds(dst_row, 1)], sem).start()
        return carry

    lax.fori_loop(0, RANK_BLOCK // DMA_UNROLL, issue, 0)
    pltpu.make_async_copy(xs_ref.at[pl.ds(0, RANK_BLOCK)], xs_ref.at[pl.ds(0, RANK_BLOCK)], sem).wait()


def _dispatch(dest, u2p, n_rows, last_blk, n_used):
    nb = dest.shape[0]
    grid_spec = pltpu.PrefetchScalarGridSpec(
        num_scalar_prefetch=2,
        grid=(nb,),
        in_specs=[pl.BlockSpec((1, 1, RANK_BLOCK), lambda i, lb, nu: (i, 0, 0), memory_space=pltpu.SMEM),
                  pl.BlockSpec((RANK_BLOCK // DMA_UNROLL, SUBLANES, HALF), lambda i, lb, nu: (i, 0, 0))],
        out_specs=pl.BlockSpec(memory_space=pl.ANY),
        scratch_shapes=[pltpu.VMEM((ROW_BLOCK, HALF), U32),
                        pltpu.SemaphoreType.DMA(()),
                        pltpu.SemaphoreType.DMA(())],
    )
    assert DMA_UNROLL == TOP_K * SUBLANES
    return pl.pallas_call(
        _dispatch_kernel,
        name="dispatch",
        grid_spec=grid_spec,
        out_shape=jax.ShapeDtypeStruct((n_rows, HALF), U32),
        compiler_params=_cparams(("arbitrary",)),
    )(last_blk, n_used, dest, u2p.reshape(-1, SUBLANES, HALF))


ROW_DMA_PRIORITY = 1
ROW_RING = 8


def _experts_kernel(bstart_ref, nblk_ref, xs_ref, wg_ref, wu_ref, wd_ref, y_ref,
                    wgb_ref, wub_ref, wdb_ref, xbuf_ref, ybuf_ref, xsem, ysem):
    e = pl.program_id(0)
    n_e = pl.num_programs(0)
    g0 = bstart_ref[e]
    nb = nblk_ref[e]
    n_used = bstart_ref[n_e - 1] + nblk_ref[n_e - 1]
    n_total = y_ref.shape[0] // ROW_BLOCK

    def x_copy(g, slot):
        return pltpu.make_async_copy(xs_ref.at[pl.ds(g * ROW_BLOCK, ROW_BLOCK)], xbuf_ref.at[slot], xsem.at[slot])

    def y_copy(g, slot):
        return pltpu.make_async_copy(ybuf_ref.at[slot], y_ref.at[pl.ds(g * ROW_BLOCK, ROW_BLOCK)], ysem.at[slot])

    @pl.when(e == 0)
    def _():
        for g in range(ROW_RING - 1):
            @pl.when(g < n_used)
            def _(g=g):
                x_copy(g, g).start(priority=ROW_DMA_PRIORITY)

    @pl.when(nb > 0)
    def _():
        wgb_ref[...] = wg_ref[0].astype(BF16)
        wub_ref[...] = wu_ref[0].astype(BF16)
        wdb_ref[...] = wd_ref[0].astype(BF16)

    def block(j, carry):
        g = g0 + j
        slot = g % ROW_RING
        x_copy(g, slot).wait()

        @pl.when(g + ROW_RING - 1 < n_used)
        def _():
            x_copy(g + ROW_RING - 1, (g + ROW_RING - 1) % ROW_RING).start(priority=ROW_DMA_PRIORITY)

        @pl.when(g >= ROW_RING)
        def _():
            y_copy(g - ROW_RING, slot).wait()

        lo, hi = _unpack_bf16_pair(xbuf_ref[slot])
        lo = lo.astype(BF16)
        hi = hi.astype(BF16)
        gate = (jnp.dot(lo, wgb_ref[:HALF, :], preferred_element_type=F32)
                + jnp.dot(hi, wgb_ref[HALF:, :], preferred_element_type=F32))
        up = (jnp.dot(lo, wub_ref[:HALF, :], preferred_element_type=F32)
              + jnp.dot(hi, wub_ref[HALF:, :], preferred_element_type=F32))
        hdn = (_silu(gate) * up).astype(BF16)
        y = jnp.dot(hdn, wdb_ref[...], preferred_element_type=F32)
        ybuf_ref[slot] = _pack_bf16_pair(y)
        y_copy(g, slot).start(priority=ROW_DMA_PRIORITY)
        return carry

    lax.fori_loop(0, nb, block, 0)

    @pl.when(e == n_e - 1)
    def _():
        for back in range(ROW_RING, 0, -1):
            @pl.when(n_used >= back)
            def _(back=back):
                y_copy(n_used - back, (n_used - back) % ROW_RING).wait()

        ybuf_ref[0] = jnp.zeros((ROW_BLOCK, HALF), U32)

        def fill(g, carry):
            y_copy(g, 0).start()
            return carry

        def fill_done(g, carry):
            y_copy(g, 0).wait()
            return carry

        lax.fori_loop(n_used, n_total, fill, 0)
        lax.fori_loop(n_used, n_total, fill_done, 0)


def _experts(block_start, block_count, xs, w_gate, w_up, w_down):
    n_rows = xs.shape[0]

    def wmap(e, bs, bc):
        return (e, 0, 0)

    grid_spec = pltpu.PrefetchScalarGridSpec(
        num_scalar_prefetch=2,
        grid=(N_EXPERTS,),
        in_specs=[pl.BlockSpec(memory_space=pl.ANY),
                  pl.BlockSpec((1, D_MODEL, D_EXPERT), wmap),
                  pl.BlockSpec((1, D_MODEL, D_EXPERT), wmap),
                  pl.BlockSpec((1, D_EXPERT, D_MODEL), wmap)],
        out_specs=pl.BlockSpec(memory_space=pl.ANY),
        scratch_shapes=[pltpu.VMEM((D_MODEL, D_EXPERT), BF16),
                        pltpu.VMEM((D_MODEL, D_EXPERT), BF16),
                        pltpu.VMEM((D_EXPERT, D_MODEL), BF16),
                        pltpu.VMEM((ROW_RING, ROW_BLOCK, HALF), U32),
                        pltpu.VMEM((ROW_RING, ROW_BLOCK, HALF), U32),
                        pltpu.SemaphoreType.DMA((ROW_RING,)),
                        pltpu.SemaphoreType.DMA((ROW_RING,))],
    )
    return pl.pallas_call(
        _experts_kernel,
        name="experts",
        grid_spec=grid_spec,
        out_shape=jax.ShapeDtypeStruct((n_rows, HALF), U32),
        compiler_params=_cparams(("arbitrary",)),
    )(block_start, block_count, xs, w_gate, w_up, w_down)


COMBINE_TOKENS = RANK_BLOCK // TOP_K


def _combine_kernel(dest_ref, dest_next_ref, h_ref, wts_ref, nw_ref, y_ref, o_ref, buf_ref, sem):
    ts = COMBINE_TOKENS
    i = pl.program_id(0)
    slot = i % 2

    def gather(idx_ref, dst_slot):
        def issue(g, carry):
            c0 = g * DMA_UNROLL
            for j in range(DMA_UNROLL):
                src_row = idx_ref[0, 0, c0 + j]
                src_tile = lax.shift_right_logical(src_row, SUBLANES.bit_length() - 1)
                pltpu.make_async_copy(y_ref.at[src_tile, pl.ds(src_row & (SUBLANES - 1), 1)],
                                      buf_ref.at[dst_slot, j % TOP_K, g, pl.ds(j // TOP_K, 1)],
                                      sem.at[dst_slot]).start()
            return carry

        lax.fori_loop(0, RANK_BLOCK // DMA_UNROLL, issue, 0)

    @pl.when(i == 0)
    def _():
        gather(dest_ref, 0)

    @pl.when(i + 1 < pl.num_programs(0))
    def _():
        gather(dest_next_ref, 1 - slot)

    for k in range(TOP_K):
        pltpu.make_async_copy(y_ref.at[pl.ds(0, ts // SUBLANES)], buf_ref.at[slot, k], sem.at[slot]).wait()

    w = wts_ref[...]
    w0 = w[:, 0:1]
    w1 = w[:, 1:2]
    lo0, hi0 = _unpack_bf16_pair(buf_ref[slot, 0].reshape(ts, HALF))
    lo1, hi1 = _unpack_bf16_pair(buf_ref[slot, 1].reshape(ts, HALF))
    h = h_ref[...]
    out_lo = h[:, :HALF] + w0 * lo0 + w1 * lo1
    out_hi = h[:, HALF:] + w0 * hi0 + w1 * hi1
    ms = (jnp.sum(out_lo * out_lo, axis=-1, keepdims=True)
          + jnp.sum(out_hi * out_hi, axis=-1, keepdims=True)) * (1.0 / D_MODEL)
    inv = lax.rsqrt(ms + EPS)
    o_ref[:, :HALF] = out_lo * inv * nw_ref[:, :HALF]
    o_ref[:, HALF:] = out_hi * inv * nw_ref[:, HALF:]


def _combine(dest, h1, wts, nw, y):
    t = h1.shape[0]
    ts = COMBINE_TOKENS
    n_steps = t // ts
    return pl.pallas_call(
        _combine_kernel,
        name="combine",
        grid=(n_steps,),
        in_specs=[pl.BlockSpec((1, 1, RANK_BLOCK), lambda i: (i, 0, 0), memory_space=pltpu.SMEM),
                  pl.BlockSpec((1, 1, RANK_BLOCK), lambda i: (jnp.minimum(i + 1, n_steps - 1), 0, 0),
                               memory_space=pltpu.SMEM),
                  pl.BlockSpec((ts, D_MODEL), lambda i: (i, 0)),
                  pl.BlockSpec((ts, LANES), lambda i: (i, 0)),
                  pl.BlockSpec((1, D_MODEL), lambda i: (0, 0)),
                  pl.BlockSpec(memory_space=pl.ANY)],
        out_specs=pl.BlockSpec((ts, D_MODEL), lambda i: (i, 0)),
        out_shape=jax.ShapeDtypeStruct((t, D_MODEL), F32),
        scratch_shapes=[pltpu.VMEM((2, TOP_K, ts // SUBLANES, SUBLANES, HALF), U32),
                        pltpu.SemaphoreType.DMA((2,))],
        compiler_params=_cparams(("arbitrary",)),
    )(dest, dest, h1, wts, nw, y.reshape(-1, SUBLANES, HALF))


def _narrow_w_in(w_in_t):
    dt0 = SSD_DIM + CONV_DIM
    f0 = dt0 + SSD_HEADS + 3 * FOX_DIM
    dt = w_in_t[dt0:dt0 + SSD_HEADS]
    f = w_in_t[f0:f0 + FOX_HEADS]
    pad = jnp.zeros((LANES - SMALL_F - FOX_HEADS, w_in_t.shape[1]), w_in_t.dtype)
    return jnp.concatenate([dt] * SMALL_DT_COPIES + [f, pad], axis=0).T.astype(BF16)


def _layer(h, p, bsz, seq):
    t = bsz * seq
    w_in_t = jnp.swapaxes(p["w_in"].astype(F32), 0, 1)
    u, small = _norm_small(h, p["norm_mix_w"].astype(F32).reshape(1, D_MODEL), _narrow_w_in(w_in_t))
    proj = _in_proj(u, w_in_t)
    y_ssd = _ssd(proj, small, p["conv_w"], p["conv_b"], p["dt_bias"], p["a_log"], p["d_skip"],
                 p["ssd_norm_w"], bsz, seq)
    cum = _fox_cum(small, p["fox_f_bias"], bsz, seq)
    y_fox = _fox(proj, cum, bsz, seq)
    mixed = _mix(y_ssd, y_fox, proj, p["w_proj_ssd"].astype(BF16), p["w_proj_fox"].astype(BF16))

    w_router = jnp.concatenate(
        [p["w_router_group"], p["w_router_expert"],
         jnp.zeros((D_MODEL, LANES - N_GROUPS - N_EXPERTS), F32)], axis=1).astype(F32)
    w_router_hi = w_router.astype(BF16)
    w_router_lo = (w_router - w_router_hi.astype(F32)).astype(BF16)
    w_router = jnp.concatenate([w_router_hi, w_router_lo], axis=1)
    b_router = jnp.concatenate(
        [p["b_router_group"], p["b_router_expert"],
         jnp.zeros((LANES - N_GROUPS - N_EXPERTS,), F32)]).astype(F32).reshape(1, LANES)
    h1, u2p, eidx, wts = _outproj(mixed, h, p["w_out"].astype(BF16),
                                  p["norm_moe_w"].astype(F32).reshape(1, D_MODEL), w_router, b_router)

    tk = t * TOP_K
    e_blocks = eidx[:, :TOP_K].reshape(tk // RANK_BLOCK, 1, RANK_BLOCK)
    rank, counts = _rank(e_blocks)
    counts = counts[:, 0].astype(I32)
    padded = ((counts + ROW_BLOCK - 1) // ROW_BLOCK) * ROW_BLOCK
    pend = jnp.cumsum(padded)
    pstart = pend - padded
    n_blocks = tk // ROW_BLOCK + N_EXPERTS
    dest = _dest(e_blocks, rank, jnp.broadcast_to(pstart[:, None], (N_EXPERTS, LANES)).astype(I32))
    block_start = (pstart // ROW_BLOCK).astype(I32)
    block_count = (padded // ROW_BLOCK).astype(I32)
    last_blk = jnp.where(block_count > 0, block_start + block_count - 1, -1).astype(I32)
    n_used = (pend[-1:] // ROW_BLOCK).astype(I32)
    xs = _dispatch(dest, u2p, n_blocks * ROW_BLOCK, last_blk, n_used)
    y = _experts(block_start, block_count, xs, p["w_gate_exp"], p["w_up_exp"], p["w_down_exp"])
    return dest, h1, wts, y


def kernel(x, norm_mix_w, w_in, conv_w, conv_b, dt_bias, a_log, d_skip, ssd_norm_w, fox_f_bias, w_proj_ssd,
           w_proj_fox, w_out, norm_moe_w, w_router_group, b_router_group, w_router_expert, b_router_expert,
           w_gate_exp, w_up_exp, w_down_exp, norm_final_w):
    bsz, seq, _ = x.shape
    depth = w_in.shape[0]
    assert depth == 1, "the fused final norm assumes a single layer"
    stacked = dict(norm_mix_w=norm_mix_w, w_in=w_in, conv_w=conv_w, conv_b=conv_b, dt_bias=dt_bias, a_log=a_log,
                   d_skip=d_skip, ssd_norm_w=ssd_norm_w, fox_f_bias=fox_f_bias, w_proj_ssd=w_proj_ssd,
                   w_proj_fox=w_proj_fox, w_out=w_out, norm_moe_w=norm_moe_w, w_router_group=w_router_group,
                   b_router_group=b_router_group, w_router_expert=w_router_expert,
                   b_router_expert=b_router_expert, w_gate_exp=w_gate_exp, w_up_exp=w_up_exp,
                   w_down_exp=w_down_exp)
    p = {name: v[0] for name, v in stacked.items()}
    h = x.reshape(bsz * seq, D_MODEL)
    dest, h1, wts, y = _layer(h, p, bsz, seq)
    out = _combine(dest, h1, wts, norm_final_w.astype(F32).reshape(1, D_MODEL), y)
    return out.reshape(bsz, seq, D_MODEL)
```

```python
import functools
import math

import jax
import jax.numpy as jnp
from jax import lax
from jax.experimental import pallas as pl
from jax.experimental.pallas import tpu as pltpu

F32 = jnp.float32
BF16 = jnp.bfloat16
I32 = jnp.int32
U32 = jnp.uint32

D_MODEL = 2048
SSD_HEADS = 32
SSD_HEAD_DIM = 64
SSD_DIM = SSD_HEADS * SSD_HEAD_DIM
SSD_GROUPS = 4
SSD_STATE = 128
SSD_CHUNK = 128
CONV_WIDTH = 4
CONV_DIM = SSD_DIM + 2 * SSD_GROUPS * SSD_STATE
FOX_HEADS = 16
FOX_HEAD_DIM = 128
FOX_DIM = FOX_HEADS * FOX_HEAD_DIM
N_GROUPS = 8
EXPERTS_PER_GROUP = 8
N_EXPERTS = N_GROUPS * EXPERTS_PER_GROUP
TOP_K = 2
D_EXPERT = 512
EPS = 1e-6

LANES = 128
SUBLANES = 8
VMEM_LIMIT = 52 * 1024 * 1024

COL_Z = 0
COL_Q = COL_Z + SSD_DIM
COL_K = COL_Q + FOX_DIM
COL_V = COL_K + FOX_DIM
COL_GA = COL_V + FOX_DIM
COL_GB = COL_GA + D_MODEL
COL_XBC = COL_GB + D_MODEL
PROJ_COLS = COL_XBC + CONV_DIM
SMALL_DT = 0
SMALL_DT_COPIES = 3
SMALL_F = SMALL_DT_COPIES * SSD_HEADS

ROW_BLOCK = 128
HALF = D_MODEL // 2


def _cparams(sem, vmem=VMEM_LIMIT):
    return pltpu.CompilerParams(dimension_semantics=sem, vmem_limit_bytes=vmem)


def _silu(x):
    return x * (1.0 / (1.0 + jnp.exp(-x)))


def _softplus(x):
    return jnp.maximum(x, 0.0) + jnp.log(1.0 + jnp.exp(-jnp.abs(x)))


def _log_sigmoid(x):
    return -_softplus(-x)


def _split3(x):
    hi = x.astype(BF16)
    rest = x - hi.astype(F32)
    mid = rest.astype(BF16)
    lo = (rest - mid.astype(F32)).astype(BF16)
    return hi, mid, lo


def _pack_bf16_pair(x):
    n = x.shape[1] // 2
    lo = pltpu.bitcast(x[:, :n].astype(BF16).astype(F32), U32)
    hi = pltpu.bitcast(x[:, n:].astype(BF16).astype(F32), U32)
    return (hi & jnp.uint32(0xFFFF0000)) | (lo >> 16)


def _unpack_bf16_pair(p):
    lo = pltpu.bitcast(p << 16, F32)
    hi = pltpu.bitcast(p & jnp.uint32(0xFFFF0000), F32)
    return lo, hi


def _norm_small_kernel(x_ref, nw_ref, ws_ref, u_ref, s_ref):
    x = x_ref[...]
    ms = jnp.mean(x * x, axis=-1, keepdims=True)
    u = (x * lax.rsqrt(ms + EPS) * nw_ref[...]).astype(BF16)
    u_ref[...] = u
    s_ref[...] = jnp.dot(u, ws_ref[...], preferred_element_type=F32)


def _norm_small(x2, nw, w_small, tm=512):
    t = x2.shape[0]
    return pl.pallas_call(
        _norm_small_kernel,
        name="norm_small",
        grid=(t // tm,),
        in_specs=[pl.BlockSpec((tm, D_MODEL), lambda i: (i, 0)),
                  pl.BlockSpec((1, D_MODEL), lambda i: (0, 0)),
                  pl.BlockSpec((D_MODEL, LANES), lambda i: (0, 0))],
        out_specs=[pl.BlockSpec((tm, D_MODEL), lambda i: (i, 0)),
                   pl.BlockSpec((tm, LANES), lambda i: (i, 0))],
        out_shape=[jax.ShapeDtypeStruct((t, D_MODEL), BF16),
                   jax.ShapeDtypeStruct((t, LANES), F32)],
        compiler_params=_cparams(("parallel",)),
    )(x2, nw, w_small)


IN_TN = 1024
IN_XPOSE = 256
_IN_SEGMENTS = ((COL_Z, 0, SSD_DIM),
                (COL_Q, SSD_DIM + CONV_DIM + SSD_HEADS, 3 * FOX_DIM),
                (COL_GA, SSD_DIM + CONV_DIM + SSD_HEADS + 3 * FOX_DIM + FOX_HEADS, 2 * D_MODEL),
                (COL_XBC, SSD_DIM, CONV_DIM))


def _in_proj_source_rows():
    src = [0] * (PROJ_COLS // IN_TN)
    for out0, src0, width in _IN_SEGMENTS:
        for c in range(0, width, IN_TN):
            src[(out0 + c) // IN_TN] = src0 + c
    assert all(s % SUBLANES == 0 for s in src)
    return src


def _in_proj_kernel(row_ref, u_ref, wt_hbm, o_ref, st_ref, wbf_ref, sem):
    j = pl.program_id(0)
    i = pl.program_id(1)
    k = wbf_ref.shape[0]

    def window(jj, slot):
        row0 = pl.multiple_of(row_ref[jj], SUBLANES)
        return pltpu.make_async_copy(wt_hbm.at[pl.ds(row0, IN_TN), :], st_ref.at[slot], sem.at[slot])

    @pl.when((j == 0) & (i == 0))
    def _():
        window(0, 0).start()

    @pl.when(i == 0)
    def _():
        @pl.when(j + 1 < pl.num_programs(0))
        def _():
            window(j + 1, (j + 1) % 2).start()

        slot = j % 2
        window(j, slot).wait()
        for c in range(0, k, IN_XPOSE):
            wbf_ref[c:c + IN_XPOSE, :] = jnp.transpose(st_ref[slot, :, c:c + IN_XPOSE]).astype(BF16)

    o_ref[...] = jnp.dot(u_ref[...], wbf_ref[...], preferred_element_type=F32).astype(o_ref.dtype)


def _in_proj(u, w_in_t, tm=1024):
    m, k = u.shape
    grid_spec = pltpu.PrefetchScalarGridSpec(
        num_scalar_prefetch=1,
        grid=(PROJ_COLS // IN_TN, m // tm),
        in_specs=[pl.BlockSpec((tm, k), lambda j, i, rows: (i, 0)),
                  pl.BlockSpec(memory_space=pl.ANY)],
        out_specs=pl.BlockSpec((tm, IN_TN), lambda j, i, rows: (i, j)),
        scratch_shapes=[pltpu.VMEM((2, IN_TN, k), F32),
                        pltpu.VMEM((k, IN_TN), BF16),
                        pltpu.SemaphoreType.DMA((2,))],
    )
    return pl.pallas_call(
        _in_proj_kernel,
        name="in_proj",
        grid_spec=grid_spec,
        out_shape=jax.ShapeDtypeStruct((m, PROJ_COLS), BF16),
        compiler_params=_cparams(("arbitrary", "arbitrary")),
    )(jnp.asarray(_in_proj_source_rows(), I32), u, w_in_t)


HALO = 2 * SUBLANES


def _ssd_kernel(z_ref, xbc_ref, halo_ref, small_ref, cw_ref, cb_ref, dtb_ref, aneg_ref, dexp_ref,
                nw_ref, expand_ref, shift_ref, y_ref, state_ref, ydiag_ref):
    c = pl.program_id(1)
    l = SSD_CHUNK
    n = SSD_STATE

    @pl.when(c == 0)
    def _():
        state_ref[...] = jnp.zeros_like(state_ref)

    cur = xbc_ref[...]
    halo = halo_ref[...]
    halo = jnp.where(c == 0, jnp.zeros_like(halo), halo)
    ext = jnp.concatenate([halo, cur], axis=0)
    shifted = jnp.dot(shift_ref[...], ext, preferred_element_type=F32)
    conv = cb_ref[...] + cw_ref[CONV_WIDTH - 1:CONV_WIDTH, :] * cur.astype(F32)
    for j in range(CONV_WIDTH - 1):
        conv = conv + cw_ref[j:j + 1, :] * shifted[j * l:(j + 1) * l]
    xbc = _silu(conv)
    xs = xbc[:, :SSD_DIM]
    bm = xbc[:, SSD_DIM:SSD_DIM + SSD_GROUPS * n]
    cm = xbc[:, SSD_DIM + SSD_GROUPS * n:]

    h3 = SMALL_DT_COPIES * SSD_HEADS
    dt3 = _softplus(small_ref[:, SMALL_DT:SMALL_DT + h3] + dtb_ref[...])
    adt3 = dt3 * aneg_ref[...]
    row = lax.broadcasted_iota(I32, (l, l), 0)
    col = lax.broadcasted_iota(I32, (l, l), 1)
    causal = col <= row
    tril = jnp.where(causal, 1.0, 0.0).astype(BF16)
    a_cs3 = sum(jnp.dot(tril, piece, preferred_element_type=F32) for piece in _split3(adt3))
    a_cs = a_cs3[:, :SSD_HEADS]
    a_cs_t = jnp.transpose(a_cs)
    a_last3 = a_cs3[l - 1:l, :]

    lane3 = lax.broadcasted_iota(I32, (l, h3), 1)

    def pieces_by_lane_group(x3):
        hi, mid, lo = _split3(x3)
        return jnp.where(lane3 < SSD_HEADS, hi, jnp.where(lane3 < 2 * SSD_HEADS, mid, lo))

    lhs3 = jnp.concatenate([pieces_by_lane_group(dt3),
                            pieces_by_lane_group(jnp.exp(a_cs3)),
                            pieces_by_lane_group(jnp.exp(a_last3 - a_cs3))], axis=0)
    expanded = jnp.dot(lhs3, expand_ref[...], preferred_element_type=F32)
    dt_x = expanded[0:l]
    decay_in = expanded[l:2 * l]
    decay_out = expanded[2 * l:3 * l]
    chunk_decay = decay_in[l - 1:l, :]
    x_dt = xs * dt_x
    x_dt_b = x_dt.astype(BF16)
    xd_b = (x_dt * decay_out).astype(BF16)

    lane = lax.broadcasted_iota(I32, (l, LANES), 1)
    lo_mask = lane < SSD_HEAD_DIM
    heads_per_group = SSD_HEADS // SSD_GROUPS
    gw = heads_per_group * SSD_HEAD_DIM

    for g in range(SSD_GROUPS):
        bg = bm[:, g * n:(g + 1) * n].astype(BF16)
        cg = cm[:, g * n:(g + 1) * n].astype(BF16)
        cb = lax.dot_general(cg, bg, (((1,), (1,)), ((), ())), preferred_element_type=F32)
        for pair in range(heads_per_group // 2):
            h0 = g * heads_per_group + 2 * pair
            lane0 = h0 * SSD_HEAD_DIM
            ms = []
            for h in (h0, h0 + 1):
                seg = a_cs[:, h:h + 1] - a_cs_t[h:h + 1, :]
                lmat = jnp.exp(jnp.where(causal, seg, -jnp.inf))
                ms.append((cb * lmat).astype(BF16))
            lhs = jnp.concatenate(ms, axis=1)
            xp = x_dt_b[:, lane0:lane0 + LANES]
            zero = jnp.zeros_like(xp)
            rhs = jnp.concatenate([jnp.where(lo_mask, xp, zero), jnp.where(lo_mask, zero, xp)], axis=0)
            ydiag_ref[:, lane0:lane0 + LANES] = jnp.dot(lhs, rhs, preferred_element_type=F32)
        st = state_ref[:, g * gw:(g + 1) * gw]
        y_off = jnp.dot(cg, st.astype(BF16), preferred_element_type=F32)
        ydiag_ref[:, g * gw:(g + 1) * gw] += y_off * decay_in[:, g * gw:(g + 1) * gw]
        new = lax.dot_general(bg, xd_b[:, g * gw:(g + 1) * gw], (((0,), (0,)), ((), ())),
                              preferred_element_type=F32)
        state_ref[:, g * gw:(g + 1) * gw] = st * chunk_decay[:, g * gw:(g + 1) * gw] + new

    y = ydiag_ref[...] + dexp_ref[...] * xs
    y = y * _silu(z_ref[...].astype(F32))
    ms = jnp.mean(y * y, axis=-1, keepdims=True)
    y_ref[...] = (y * lax.rsqrt(ms + EPS) * nw_ref[...]).astype(y_ref.dtype)


def _ssd(proj, small, conv_w, conv_b, dt_bias, a_log, d_skip, norm_w, bsz, seq):
    l = SSD_CHUNK
    nc = seq // l
    t = bsz * seq
    h3 = SMALL_DT_COPIES * SSD_HEADS
    aneg = jnp.tile(-jnp.exp(a_log.astype(F32)), SMALL_DT_COPIES).reshape(1, h3)
    dtb = jnp.tile(dt_bias.astype(F32), SMALL_DT_COPIES).reshape(1, h3)
    dexp = jnp.repeat(d_skip.astype(F32), SSD_HEAD_DIM).reshape(1, SSD_DIM)
    expand = jnp.tile(jnp.repeat(jnp.eye(SSD_HEADS, dtype=BF16), SSD_HEAD_DIM, axis=1),
                      (SMALL_DT_COPIES, 1))
    out_row = jnp.arange((CONV_WIDTH - 1) * l)
    src_row = HALO + out_row % l - (CONV_WIDTH - 1) + out_row // l
    shift = (jnp.arange(HALO + l)[None, :] == src_row[:, None]).astype(BF16)
    xbc_blk = COL_XBC // CONV_DIM
    halo_per_chunk = l // HALO

    def row_map(b, c):
        return b * nc + c

    return pl.pallas_call(
        _ssd_kernel,
        name="ssd",
        grid=(bsz, nc),
        in_specs=[
            pl.BlockSpec((l, SSD_DIM), lambda b, c: (row_map(b, c), COL_Z // SSD_DIM)),
            pl.BlockSpec((l, CONV_DIM), lambda b, c: (row_map(b, c), xbc_blk)),
            pl.BlockSpec((HALO, CONV_DIM),
                         lambda b, c: (jnp.maximum(row_map(b, c) * halo_per_chunk - 1, 0), xbc_blk)),
            pl.BlockSpec((l, LANES), lambda b, c: (row_map(b, c), 0)),
            pl.BlockSpec((CONV_WIDTH, CONV_DIM), lambda b, c: (0, 0)),
            pl.BlockSpec((1, CONV_DIM), lambda b, c: (0, 0)),
            pl.BlockSpec((1, h3), lambda b, c: (0, 0)),
            pl.BlockSpec((1, h3), lambda b, c: (0, 0)),
            pl.BlockSpec((1, SSD_DIM), lambda b, c: (0, 0)),
            pl.BlockSpec((1, SSD_DIM), lambda b, c: (0, 0)),
            pl.BlockSpec((h3, SSD_DIM), lambda b, c: (0, 0)),
            pl.BlockSpec(((CONV_WIDTH - 1) * l, HALO + l), lambda b, c: (0, 0)),
        ],
        out_specs=pl.BlockSpec((l, SSD_DIM), lambda b, c: (row_map(b, c), 0)),
        out_shape=jax.ShapeDtypeStruct((t, SSD_DIM), BF16),
        scratch_shapes=[pltpu.VMEM((SSD_STATE, SSD_DIM), F32),
                        pltpu.VMEM((l, SSD_DIM), F32)],
        compiler_params=_cparams(("parallel", "arbitrary")),
    )(proj, proj, proj, small, conv_w.astype(F32), conv_b.astype(F32).reshape(1, CONV_DIM),
      dtb, aneg, dexp, norm_w.astype(F32).reshape(1, SSD_DIM), expand, shift)


CUM_ROWS = 256


def _cum_kernel(small_ref, fb_ref, cum_ref, carry_ref):
    j = pl.program_id(1)

    @pl.when(j == 0)
    def _():
        carry_ref[...] = jnp.zeros_like(carry_ref)

    lf = _log_sigmoid(small_ref[...] + fb_ref[...])
    row = lax.broadcasted_iota(I32, (CUM_ROWS, CUM_ROWS), 0)
    col = lax.broadcasted_iota(I32, (CUM_ROWS, CUM_ROWS), 1)
    tril = jnp.where(col <= row, 1.0, 0.0).astype(F32)
    cs = jnp.dot(tril, lf, preferred_element_type=F32, precision=lax.Precision.HIGHEST) + carry_ref[...]
    cum_ref[...] = cs * LOG2E
    carry_ref[...] = cs[CUM_ROWS - 1:CUM_ROWS, :]


def _fox_cum(small, fox_f_bias, bsz, seq):
    fb = jnp.zeros((1, LANES), F32).at[0, SMALL_F:SMALL_F + FOX_HEADS].set(fox_f_bias.astype(F32))
    nj = seq // CUM_ROWS
    return pl.pallas_call(
        _cum_kernel,
        name="fox_cum",
        grid=(bsz, nj),
        in_specs=[pl.BlockSpec((CUM_ROWS, LANES), lambda b, j: (b * nj + j, 0)),
                  pl.BlockSpec((1, LANES), lambda b, j: (0, 0))],
        out_specs=pl.BlockSpec((CUM_ROWS, LANES), lambda b, j: (b * nj + j, 0)),
        out_shape=jax.ShapeDtypeStruct((bsz * seq, LANES), F32),
        scratch_shapes=[pltpu.VMEM((1, LANES), F32)],
        compiler_params=_cparams(("parallel", "arbitrary")),
    )(small, fb)


FOX_HEADS_PER_STEP = 2
LOG2E = 1.4426950408889634


FOX_SLAB = 128


def _fox_kernel(q_ref, k_ref, v_ref, cq_ref, ck_ref, o_ref, sa_ref, sb_ref, p_ref, m_ref, l_ref, m_alt_ref,
                cqrep_ref, psum_ref, acc_ref, *, tq, tk):
    hp = pl.program_id(1)
    qi = pl.program_id(2)
    d = FOX_HEAD_DIM
    c2 = LOG2E / math.sqrt(d)
    lane = lax.broadcasted_iota(I32, (tq, LANES), 1)
    cq_all = cq_ref[...]

    qs, cqs = [], []
    for hh in range(FOX_HEADS_PER_STEP):
        qs.append((q_ref[:, hh * d:(hh + 1) * d].astype(F32) * c2).astype(BF16))
        head_lane = SMALL_F + hp * FOX_HEADS_PER_STEP + hh
        cq_col = jnp.sum(jnp.where(lane == head_lane, cq_all, 0.0), axis=-1, keepdims=True)
        cqs.append(jnp.broadcast_to(cq_col, (tq, LANES)))

    s_slots = (sa_ref, sb_ref)

    def scores(ki, slot):
        row0 = pl.multiple_of(ki * tk, tk)
        for hh in range(FOX_HEADS_PER_STEP):
            k = k_ref[pl.ds(row0, tk), hh * d:(hh + 1) * d]
            s = lax.dot_general(qs[hh], k, (((1,), (1,)), ((), ())), preferred_element_type=F32)
            s_slots[slot][hh] = s - ck_ref[hh, pl.ds(ki, 1), :]

    def update(ki, slot, masked):
        row0 = pl.multiple_of(ki * tk, tk)
        s_ref = s_slots[slot]
        m_bufs = (m_ref, m_alt_ref)
        n_ct = tk // LANES
        for hh in range(FOX_HEADS_PER_STEP):
            for rc in range(tq // FOX_SLAB):
                rows = slice(rc * FOX_SLAB, (rc + 1) * FOX_SLAB)
                n_vis = rc + 1 if masked else n_ct

                def slab(ct):
                    x = s_ref[hh, rows, ct * LANES:(ct + 1) * LANES]
                    if masked and ct == rc:
                        row = lax.broadcasted_iota(I32, (FOX_SLAB, LANES), 0)
                        col = lax.broadcasted_iota(I32, (FOX_SLAB, LANES), 1)
                        x = jnp.where(col <= row, x, -jnp.inf)
                    return x

                tmax = slab(0)
                for ct in range(1, n_vis):
                    tmax = jnp.maximum(tmax, slab(ct))
                row_max = jnp.max(tmax, axis=-1, keepdims=True)
                cq = cqrep_ref[hh, rows, :]
                m_new = jnp.maximum(m_bufs[slot][hh, rows, :], jnp.broadcast_to(row_max, (FOX_SLAB, LANES)) + cq)
                m_bufs[1 - slot][hh, rows, :] = m_new
                r = m_new - cq
                psum = None
                for ct in range(n_vis):
                    p = jnp.exp2(slab(ct) - r)
                    psum = p if psum is None else psum + p
                    p_ref[hh, rows, ct * LANES:(ct + 1) * LANES] = p.astype(BF16)
                for ct in range(n_vis, n_ct):
                    p_ref[hh, rows, ct * LANES:(ct + 1) * LANES] = jnp.zeros((FOX_SLAB, LANES), BF16)
                psum_ref[hh, rows, :] = psum
            m_old = m_bufs[slot][hh]
            m_cur = m_bufs[1 - slot][hh]
            l_ref[hh] = jnp.exp2(m_old - m_cur) * l_ref[hh] + psum_ref[hh]
            v = v_ref[pl.ds(row0, tk), hh * d:(hh + 1) * d]
            acc_ref[hh] = jnp.exp2(m_old - m_cur) * acc_ref[hh] + jnp.dot(p_ref[hh], v, preferred_element_type=F32)

    for hh in range(FOX_HEADS_PER_STEP):
        cqrep_ref[hh] = cqs[hh]

    m_ref[...] = jnp.full_like(m_ref, -jnp.inf)
    l_ref[...] = jnp.zeros_like(l_ref)
    acc_ref[...] = jnp.zeros_like(acc_ref)

    n_full = qi
    scores(0, 0)

    def pair(j, carry):
        b0 = 2 * j
        scores(b0 + 1, 1)
        update(b0, 0, False)
        scores(b0 + 2, 0)
        update(b0 + 1, 1, False)
        return carry

    lax.fori_loop(0, n_full // 2, pair, 0)

    @pl.when(n_full % 2 == 0)
    def _():
        update(n_full, 0, True)

    @pl.when(n_full % 2 == 1)
    def _():
        scores(n_full, 1)
        update(n_full - 1, 0, False)
        update(n_full, 1, True)

    for hh in range(FOX_HEADS_PER_STEP):
        l_fin = jnp.sum(l_ref[hh], axis=-1, keepdims=True)
        o_ref[:, hh * d:(hh + 1) * d] = (acc_ref[hh] / l_fin).astype(o_ref.dtype)


def _fox(proj, cum, bsz, seq, tq=1024):
    assert FOX_SLAB == LANES and tq % FOX_SLAB == 0 and seq % tq == 0
    tk = tq
    nq = seq // tq
    nk = seq // tk
    hps = FOX_HEADS_PER_STEP
    cum_row = cum[:, SMALL_F:SMALL_F + FOX_HEADS].reshape(bsz, seq, FOX_HEADS).transpose(0, 2, 1)
    cum_row = cum_row.reshape(bsz, FOX_HEADS, nk, tk)
    t = bsz * seq
    w = FOX_HEADS_PER_STEP * FOX_HEAD_DIM
    n_hp = FOX_HEADS // FOX_HEADS_PER_STEP
    kern = functools.partial(_fox_kernel, tq=tq, tk=tk)
    return pl.pallas_call(
        kern,
        name="fox",
        grid=(bsz, n_hp, nq),
        in_specs=[
            pl.BlockSpec((tq, w), lambda b, hp, qi: (b * nq + qi, COL_Q // w + hp)),
            pl.BlockSpec((seq, w), lambda b, hp, qi: (b, COL_K // w + hp)),
            pl.BlockSpec((seq, w), lambda b, hp, qi: (b, COL_V // w + hp)),
            pl.BlockSpec((tq, LANES), lambda b, hp, qi: (b * nq + qi, 0)),
            pl.BlockSpec((None, FOX_HEADS_PER_STEP, nk, tk), lambda b, hp, qi: (b, hp, 0, 0)),
        ],
        out_specs=pl.BlockSpec((tq, w), lambda b, hp, qi: (b * nq + qi, hp)),
        out_shape=jax.ShapeDtypeStruct((t, FOX_DIM), BF16),
        scratch_shapes=[pltpu.VMEM((hps, tq, tk), F32), pltpu.VMEM((hps, tq, tk), F32),
                        pltpu.VMEM((hps, tq, tk), BF16),
                        pltpu.VMEM((hps, tq, LANES), F32), pltpu.VMEM((hps, tq, LANES), F32),
                        pltpu.VMEM((hps, tq, LANES), F32), pltpu.VMEM((hps, tq, LANES), F32),
                        pltpu.VMEM((hps, tq, LANES), F32),
                        pltpu.VMEM((hps, tq, FOX_HEAD_DIM), F32)],
        compiler_params=_cparams(("parallel", "parallel", "arbitrary")),
    )(proj, proj, proj, cum, cum_row)


def _mix_kernel(ya_ref, yb_ref, ga_ref, gb_ref, wa_ref, wb_ref, o_ref):
    pa = jnp.dot(ya_ref[...], wa_ref[...], preferred_element_type=F32)
    pb = jnp.dot(yb_ref[...], wb_ref[...], preferred_element_type=F32)
    ga = 1.0 / (1.0 + jnp.exp(-ga_ref[...].astype(F32)))
    gb = 1.0 / (1.0 + jnp.exp(-gb_ref[...].astype(F32)))
    o_ref[...] = (ga * pa + gb * pb).astype(o_ref.dtype)


def _mix(y_a, y_b, proj, w_a, w_b, tm=512, tn=1024):
    t = y_a.shape[0]
    return pl.pallas_call(
        _mix_kernel,
        name="mix",
        grid=(D_MODEL // tn, t // tm),
        in_specs=[
            pl.BlockSpec((tm, SSD_DIM), lambda j, i: (i, 0)),
            pl.BlockSpec((tm, FOX_DIM), lambda j, i: (i, 0)),
            pl.BlockSpec((tm, tn), lambda j, i: (i, COL_GA // tn + j)),
            pl.BlockSpec((tm, tn), lambda j, i: (i, COL_GB // tn + j)),
            pl.BlockSpec((SSD_DIM, tn), lambda j, i: (0, j)),
            pl.BlockSpec((FOX_DIM, tn), lambda j, i: (0, j)),
        ],
        out_specs=pl.BlockSpec((tm, tn), lambda j, i: (i, j)),
        out_shape=jax.ShapeDtypeStruct((t, D_MODEL), BF16),
        compiler_params=_cparams(("parallel", "parallel")),
    )(y_a, y_b, proj, proj, w_a, w_b)


def _outproj_kernel(m_ref, x_ref, wo_ref, nw_ref, wr_ref, br_ref, h_ref, u_ref, eidx_ref, wts_ref):
    h1 = x_ref[...] + jnp.dot(m_ref[...], wo_ref[...], preferred_element_type=F32)
    h_ref[...] = h1
    ms = jnp.mean(h1 * h1, axis=-1, keepdims=True)
    u2 = h1 * lax.rsqrt(ms + EPS) * nw_ref[...]
    u_ref[...] = _pack_bf16_pair(u2)

    u_hi = u2.astype(BF16)
    u_lo = (u2 - u_hi.astype(F32)).astype(BF16)
    hh_hl = jnp.dot(u_hi, wr_ref[...], preferred_element_type=F32)
    lh = jnp.dot(u_lo, wr_ref[:, :LANES], preferred_element_type=F32)
    logits = hh_hl[:, :LANES] + (hh_hl[:, LANES:] + lh) + br_ref[...]
    tm = logits.shape[0]
    lane = lax.broadcasted_iota(I32, (tm, LANES), 1)
    neg = -jnp.inf
    big = jnp.int32(2 * LANES)
    gl = jnp.where(lane < N_GROUPS, logits, neg)
    gmax = jnp.max(gl, axis=-1, keepdims=True)
    gsum = jnp.sum(jnp.exp(gl - gmax), axis=-1, keepdims=True)
    g_p = 1.0 / gsum
    g_idx = jnp.min(jnp.where(gl == gmax, lane, big), axis=-1, keepdims=True)
    e_of_lane = lane - N_GROUPS
    in_grp = (e_of_lane >= g_idx * EXPERTS_PER_GROUP) & (e_of_lane < (g_idx + 1) * EXPERTS_PER_GROUP)
    el = jnp.where(in_grp, logits, neg)
    m1 = jnp.max(el, axis=-1, keepdims=True)
    i1 = jnp.min(jnp.where(el == m1, lane, big), axis=-1, keepdims=True)
    el2 = jnp.where(lane == i1, neg, el)
    m2 = jnp.max(el2, axis=-1, keepdims=True)
    i2 = jnp.min(jnp.where(el2 == m2, lane, big), axis=-1, keepdims=True)
    esum = jnp.sum(jnp.exp(el - m1), axis=-1, keepdims=True)
    p1 = 1.0 / esum
    p2 = jnp.exp(m2 - m1) / esum
    w1 = g_p * (p1 / (p1 + p2))
    w2 = g_p * (p2 / (p1 + p2))
    eidx_ref[...] = jnp.where(lane == 0, i1 - N_GROUPS, jnp.where(lane == 1, i2 - N_GROUPS, 0))
    wts_ref[...] = jnp.where(lane == 0, w1, jnp.where(lane == 1, w2, 0.0))


def _outproj(mixed, x2, w_o, nw, w_router, b_router, tm=512):
    t = mixed.shape[0]
    resident = pl.Buffered(1)
    return pl.pallas_call(
        _outproj_kernel,
        name="outproj",
        grid=(t // tm,),
        in_specs=[
            pl.BlockSpec((tm, D_MODEL), lambda i: (i, 0)),
            pl.BlockSpec((tm, D_MODEL), lambda i: (i, 0)),
            pl.BlockSpec((D_MODEL, D_MODEL), lambda i: (0, 0), pipeline_mode=resident),
            pl.BlockSpec((1, D_MODEL), lambda i: (0, 0)),
            pl.BlockSpec((D_MODEL, 2 * LANES), lambda i: (0, 0), pipeline_mode=resident),
            pl.BlockSpec((1, LANES), lambda i: (0, 0)),
        ],
        out_specs=[
            pl.BlockSpec((tm, D_MODEL), lambda i: (i, 0)),
            pl.BlockSpec((tm, HALF), lambda i: (i, 0)),
            pl.BlockSpec((tm, LANES), lambda i: (i, 0)),
            pl.BlockSpec((tm, LANES), lambda i: (i, 0)),
        ],
        out_shape=[
            jax.ShapeDtypeStruct((t, D_MODEL), F32),
            jax.ShapeDtypeStruct((t, HALF), U32),
            jax.ShapeDtypeStruct((t, LANES), I32),
            jax.ShapeDtypeStruct((t, LANES), F32),
        ],
        compiler_params=_cparams(("parallel",)),
    )(mixed, x2, w_o, nw, w_router, b_router)


RANK_BLOCK = 512


def _rank_kernel(e_ref, rank_ref, cnt_ref, carry_ref):
    i = pl.program_id(0)
    r = RANK_BLOCK

    @pl.when(i == 0)
    def _():
        carry_ref[...] = jnp.zeros_like(carry_ref)

    e = e_ref[0]
    expert = lax.broadcasted_iota(I32, (N_EXPERTS, r), 0)
    onehot = jnp.where(expert == e, 1.0, 0.0).astype(F32)
    jrow = lax.broadcasted_iota(I32, (r, r), 0)
    jcol = lax.broadcasted_iota(I32, (r, r), 1)
    before = jnp.where(jrow < jcol, 1.0, 0.0).astype(BF16)
    cum = jnp.dot(onehot.astype(BF16), before, preferred_element_type=F32)
    carry = carry_ref[...]
    rank = jnp.sum(onehot * (cum + carry[:, 0:1]), axis=0, keepdims=True)
    rank_ref[0] = rank.astype(I32)
    carry = carry + jnp.sum(onehot, axis=1, keepdims=True)
    carry_ref[...] = carry
    cnt_ref[...] = carry


def _rank(e_blocks):
    nb = e_blocks.shape[0]
    return pl.pallas_call(
        _rank_kernel,
        name="rank",
        grid=(nb,),
        in_specs=[pl.BlockSpec((1, 1, RANK_BLOCK), lambda i: (i, 0, 0))],
        out_specs=[pl.BlockSpec((1, 1, RANK_BLOCK), lambda i: (i, 0, 0)),
                   pl.BlockSpec((N_EXPERTS, LANES), lambda i: (0, 0))],
        out_shape=[jax.ShapeDtypeStruct((nb, 1, RANK_BLOCK), I32),
                   jax.ShapeDtypeStruct((N_EXPERTS, LANES), F32)],
        scratch_shapes=[pltpu.VMEM((N_EXPERTS, LANES), F32)],
        compiler_params=_cparams(("arbitrary",)),
    )(e_blocks)


def _dest_kernel(e_ref, rank_ref, pstart_ref, dest_ref):
    e = e_ref[0]
    expert = lax.broadcasted_iota(I32, (N_EXPERTS, RANK_BLOCK), 0)
    start = jnp.sum(jnp.where(expert == e, pstart_ref[:, 0:1], 0), axis=0, keepdims=True)
    dest_ref[0] = rank_ref[0] + start


def _dest(e_blocks, rank, pstart):
    nb = e_blocks.shape[0]
    return pl.pallas_call(
        _dest_kernel,
        name="dest",
        grid=(nb,),
        in_specs=[pl.BlockSpec((1, 1, RANK_BLOCK), lambda i: (i, 0, 0)),
                  pl.BlockSpec((1, 1, RANK_BLOCK), lambda i: (i, 0, 0)),
                  pl.BlockSpec((N_EXPERTS, LANES), lambda i: (0, 0))],
        out_specs=pl.BlockSpec((1, 1, RANK_BLOCK), lambda i: (i, 0, 0)),
        out_shape=jax.ShapeDtypeStruct((nb, 1, RANK_BLOCK), I32),
        compiler_params=_cparams(("parallel",)),
    )(e_blocks, rank, pstart)


def _row_copy(src_ref, src_row, dst_ref, dst_row, sem):
    return pltpu.make_async_copy(src_ref.at[pl.ds(src_row, 1)], dst_ref.at[pl.ds(dst_row, 1)], sem)


DMA_UNROLL = 16


def _dispatch_kernel(last_blk_ref, nused_ref, dest_ref, u_ref, xs_ref, zbuf_ref, sem, zsem):
    i = pl.program_id(0)
    n_total = xs_ref.shape[0] // ROW_BLOCK

    @pl.when(i == 0)
    def _():
        zbuf_ref[...] = jnp.zeros_like(zbuf_ref)

        def zero_block(blk):
            return pltpu.make_async_copy(zbuf_ref, xs_ref.at[pl.ds(blk * ROW_BLOCK, ROW_BLOCK)], zsem)

        def fill_expert(e, count):
            blk = last_blk_ref[e]

            @pl.when(blk >= 0)
            def _():
                zero_block(blk).start()

            return count + jnp.where(blk >= 0, 1, 0)

        n_fill = lax.fori_loop(0, N_EXPERTS, fill_expert, 0)

        def fill_tail(blk, carry):
            zero_block(blk).start()
            return carry

        lax.fori_loop(nused_ref[0], n_total, fill_tail, 0)

        def drain(k, carry):
            zero_block(0).wait()
            return carry

        lax.fori_loop(0, n_fill + n_total - nused_ref[0], drain, 0)

    def issue(g, carry):
        c0 = g * DMA_UNROLL
        for j in range(DMA_UNROLL):
            dst_row = dest_ref[0, 0, c0 + j]
            pltpu.make_async_copy(u_ref.at[g, pl.ds(j // TOP_K, 1)], xs_ref.at[pl.ds(dst_row, 1)], sem).start()
        return carry

    lax.fori_loop(0, RANK_BLOCK // DMA_UNROLL, issue, 0)
    pltpu.make_async_copy(xs_ref.at[pl.ds(0, RANK_BLOCK)], xs_ref.at[pl.ds(0, RANK_BLOCK)], sem).wait()


def _dispatch(dest, u2p, n_rows, last_blk, n_used):
    nb = dest.shape[0]
    grid_spec = pltpu.PrefetchScalarGridSpec(
        num_scalar_prefetch=2,
        grid=(nb,),
        in_specs=[pl.BlockSpec((1, 1, RANK_BLOCK), lambda i, lb, nu: (i, 0, 0), memory_space=pltpu.SMEM),
                  pl.BlockSpec((RANK_BLOCK // DMA_UNROLL, SUBLANES, HALF), lambda i, lb, nu: (i, 0, 0))],
        out_specs=pl.BlockSpec(memory_space=pl.ANY),
        scratch_shapes=[pltpu.VMEM((ROW_BLOCK, HALF), U32),
                        pltpu.SemaphoreType.DMA(()),
                        pltpu.SemaphoreType.DMA(())],
    )
    assert DMA_UNROLL == TOP_K * SUBLANES
    return pl.pallas_call(
        _dispatch_kernel,
        name="dispatch",
        grid_spec=grid_spec,
        out_shape=jax.ShapeDtypeStruct((n_rows, HALF), U32),
        compiler_params=_cparams(("arbitrary",)),
    )(last_blk, n_used, dest, u2p.reshape(-1, SUBLANES, HALF))


ROW_DMA_PRIORITY = 1
ROW_RING = 8


def _experts_kernel(bstart_ref, nblk_ref, xs_ref, wg_ref, wu_ref, wd_ref, y_ref,
                    wgb_ref, wub_ref, wdb_ref, xbuf_ref, ybuf_ref, xsem, ysem):
    e = pl.program_id(0)
    n_e = pl.num_programs(0)
    g0 = bstart_ref[e]
    nb = nblk_ref[e]
    n_used = bstart_ref[n_e - 1] + nblk_ref[n_e - 1]
    n_total = y_ref.shape[0] // ROW_BLOCK

    def x_copy(g, slot):
        return pltpu.make_async_copy(xs_ref.at[pl.ds(g * ROW_BLOCK, ROW_BLOCK)], xbuf_ref.at[slot], xsem.at[slot])

    def y_copy(g, slot):
        return pltpu.make_async_copy(ybuf_ref.at[slot], y_ref.at[pl.ds(g * ROW_BLOCK, ROW_BLOCK)], ysem.at[slot])

    @pl.when(e == 0)
    def _():
        for g in range(ROW_RING - 1):
            @pl.when(g < n_used)
            def _(g=g):
                x_copy(g, g).start(priority=ROW_DMA_PRIORITY)

    @pl.when(nb > 0)
    def _():
        wgb_ref[...] = wg_ref[0].astype(BF16)
        wub_ref[...] = wu_ref[0].astype(BF16)
        wdb_ref[...] = wd_ref[0].astype(BF16)

    def block(j, carry):
        g = g0 + j
        slot = g % ROW_RING
        x_copy(g, slot).wait()

        @pl.when(g + ROW_RING - 1 < n_used)
        def _():
            x_copy(g + ROW_RING - 1, (g + ROW_RING - 1) % ROW_RING).start(priority=ROW_DMA_PRIORITY)

        @pl.when(g >= ROW_RING)
        def _():
            y_copy(g - ROW_RING, slot).wait()

        lo, hi = _unpack_bf16_pair(xbuf_ref[slot])
        lo = lo.astype(BF16)
        hi = hi.astype(BF16)
        gate = (jnp.dot(lo, wgb_ref[:HALF, :], preferred_element_type=F32)
                + jnp.dot(hi, wgb_ref[HALF:, :], preferred_element_type=F32))
        up = (jnp.dot(lo, wub_ref[:HALF, :], preferred_element_type=F32)
              + jnp.dot(hi, wub_ref[HALF:, :], preferred_element_type=F32))
        hdn = (_silu(gate) * up).astype(BF16)
        y = jnp.dot(hdn, wdb_ref[...], preferred_element_type=F32)
        ybuf_ref[slot] = _pack_bf16_pair(y)
        y_copy(g, slot).start(priority=ROW_DMA_PRIORITY)
        return carry

    lax.fori_loop(0, nb, block, 0)

    @pl.when(e == n_e - 1)
    def _():
        for back in range(ROW_RING, 0, -1):
            @pl.when(n_used >= back)
            def _(back=back):
                y_copy(n_used - back, (n_used - back) % ROW_RING).wait()

        ybuf_ref[0] = jnp.zeros((ROW_BLOCK, HALF), U32)

        def fill(g, carry):
            y_copy(g, 0).start()
            return carry

        def fill_done(g, carry):
            y_copy(g, 0).wait()
            return carry

        lax.fori_loop(n_used, n_total, fill, 0)
        lax.fori_loop(n_used, n_total, fill_done, 0)


def _experts(block_start, block_count, xs, w_gate, w_up, w_down):
    n_rows = xs.shape[0]

    def wmap(e, bs, bc):
        return (e, 0, 0)

    grid_spec = pltpu.PrefetchScalarGridSpec(
        num_scalar_prefetch=2,
        grid=(N_EXPERTS,),
        in_specs=[pl.BlockSpec(memory_space=pl.ANY),
                  pl.BlockSpec((1, D_MODEL, D_EXPERT), wmap),
                  pl.BlockSpec((1, D_MODEL, D_EXPERT), wmap),
                  pl.BlockSpec((1, D_EXPERT, D_MODEL), wmap)],
        out_specs=pl.BlockSpec(memory_space=pl.ANY),
        scratch_shapes=[pltpu.VMEM((D_MODEL, D_EXPERT), BF16),
                        pltpu.VMEM((D_MODEL, D_EXPERT), BF16),
                        pltpu.VMEM((D_EXPERT, D_MODEL), BF16),
                        pltpu.VMEM((ROW_RING, ROW_BLOCK, HALF), U32),
                        pltpu.VMEM((ROW_RING, ROW_BLOCK, HALF), U32),
                        pltpu.SemaphoreType.DMA((ROW_RING,)),
                        pltpu.SemaphoreType.DMA((ROW_RING,))],
    )
    return pl.pallas_call(
        _experts_kernel,
        name="experts",
        grid_spec=grid_spec,
        out_shape=jax.ShapeDtypeStruct((n_rows, HALF), U32),
        compiler_params=_cparams(("arbitrary",)),
    )(block_start, block_count, xs, w_gate, w_up, w_down)


COMBINE_TOKENS = RANK_BLOCK // TOP_K


def _combine_kernel(dest_ref, dest_next_ref, h_ref, wts_ref, nw_ref, y_ref, o_ref, buf_ref, sem):
    ts = COMBINE_TOKENS
    i = pl.program_id(0)
    slot = i % 2

    def gather(idx_ref, dst_slot):
        def issue(g, carry):
            c0 = g * DMA_UNROLL
            for j in range(DMA_UNROLL):
                src_row = idx_ref[0, 0, c0 + j]
                src_tile = lax.shift_right_logical(src_row, SUBLANES.bit_length() - 1)
                pltpu.make_async_copy(y_ref.at[src_tile, pl.ds(src_row & (SUBLANES - 1), 1)],
                                      buf_ref.at[dst_slot, j % TOP_K, g, pl.ds(j // TOP_K, 1)],
                                      sem.at[dst_slot]).start()
            return carry

        lax.fori_loop(0, RANK_BLOCK // DMA_UNROLL, issue, 0)

    @pl.when(i == 0)
    def _():
        gather(dest_ref, 0)

    @pl.when(i + 1 < pl.num_programs(0))
    def _():
        gather(dest_next_ref, 1 - slot)

    for k in range(TOP_K):
        pltpu.make_async_copy(y_ref.at[pl.ds(0, ts // SUBLANES)], buf_ref.at[slot, k], sem.at[slot]).wait()

    w = wts_ref[...]
    w0 = w[:, 0:1]
    w1 = w[:, 1:2]
    lo0, hi0 = _unpack_bf16_pair(buf_ref[slot, 0].reshape(ts, HALF))
    lo1, hi1 = _unpack_bf16_pair(buf_ref[slot, 1].reshape(ts, HALF))
    h = h_ref[...]
    out_lo = h[:, :HALF] + w0 * lo0 + w1 * lo1
    out_hi = h[:, HALF:] + w0 * hi0 + w1 * hi1
    ms = (jnp.sum(out_lo * out_lo, axis=-1, keepdims=True)
          + jnp.sum(out_hi * out_hi, axis=-1, keepdims=True)) * (1.0 / D_MODEL)
    inv = lax.rsqrt(ms + EPS)
    o_ref[:, :HALF] = out_lo * inv * nw_ref[:, :HALF]
    o_ref[:, HALF:] = out_hi * inv * nw_ref[:, HALF:]


def _combine(dest, h1, wts, nw, y):
    t = h1.shape[0]
    ts = COMBINE_TOKENS
    n_steps = t // ts
    return pl.pallas_call(
        _combine_kernel,
        name="combine",
        grid=(n_steps,),
        in_specs=[pl.BlockSpec((1, 1, RANK_BLOCK), lambda i: (i, 0, 0), memory_space=pltpu.SMEM),
                  pl.BlockSpec((1, 1, RANK_BLOCK), lambda i: (jnp.minimum(i + 1, n_steps - 1), 0, 0),
                               memory_space=pltpu.SMEM),
                  pl.BlockSpec((ts, D_MODEL), lambda i: (i, 0)),
                  pl.BlockSpec((ts, LANES), lambda i: (i, 0)),
                  pl.BlockSpec((1, D_MODEL), lambda i: (0, 0)),
                  pl.BlockSpec(memory_space=pl.ANY)],
        out_specs=pl.BlockSpec((ts, D_MODEL), lambda i: (i, 0)),
        out_shape=jax.ShapeDtypeStruct((t, D_MODEL), F32),
        scratch_shapes=[pltpu.VMEM((2, TOP_K, ts // SUBLANES, SUBLANES, HALF), U32),
                        pltpu.SemaphoreType.DMA((2,))],
        compiler_params=_cparams(("arbitrary",)),
    )(dest, dest, h1, wts, nw, y.reshape(-1, SUBLANES, HALF))


def _narrow_w_in(w_in_t):
    dt0 = SSD_DIM + CONV_DIM
    f0 = dt0 + SSD_HEADS + 3 * FOX_DIM
    dt = w_in_t[dt0:dt0 + SSD_HEADS]
    f = w_in_t[f0:f0 + FOX_HEADS]
    pad = jnp.zeros((LANES - SMALL_F - FOX_HEADS, w_in_t.shape[1]), w_in_t.dtype)
    return jnp.concatenate([dt] * SMALL_DT_COPIES + [f, pad], axis=0).T.astype(BF16)


def _layer(h, p, bsz, seq):
    t = bsz * seq
    w_in_t = jnp.swapaxes(p["w_in"].astype(F32), 0, 1)
    u, small = _norm_small(h, p["norm_mix_w"].astype(F32).reshape(1, D_MODEL), _narrow_w_in(w_in_t))
    proj = _in_proj(u, w_in_t)
    y_ssd = _ssd(proj, small, p["conv_w"], p["conv_b"], p["dt_bias"], p["a_log"], p["d_skip"],
                 p["ssd_norm_w"], bsz, seq)
    cum = _fox_cum(small, p["fox_f_bias"], bsz, seq)
    y_fox = _fox(proj, cum, bsz, seq)
    mixed = _mix(y_ssd, y_fox, proj, p["w_proj_ssd"].astype(BF16), p["w_proj_fox"].astype(BF16))

    w_router = jnp.concatenate(
        [p["w_router_group"], p["w_router_expert"],
         jnp.zeros((D_MODEL, LANES - N_GROUPS - N_EXPERTS), F32)], axis=1).astype(F32)
    w_router_hi = w_router.astype(BF16)
    w_router_lo = (w_router - w_router_hi.astype(F32)).astype(BF16)
    w_router = jnp.concatenate([w_router_hi, w_router_lo], axis=1)
    b_router = jnp.concatenate(
        [p["b_router_group"], p["b_router_expert"],
         jnp.zeros((LANES - N_GROUPS - N_EXPERTS,), F32)]).astype(F32).reshape(1, LANES)
    h1, u2p, eidx, wts = _outproj(mixed, h, p["w_out"].astype(BF16),
                                  p["norm_moe_w"].astype(F32).reshape(1, D_MODEL), w_router, b_router)

    tk = t * TOP_K
    e_blocks = eidx[:, :TOP_K].reshape(tk // RANK_BLOCK, 1, RANK_BLOCK)
    rank, counts = _rank(e_blocks)
    counts = counts[:, 0].astype(I32)
    padded = ((counts + ROW_BLOCK - 1) // ROW_BLOCK) * ROW_BLOCK
    pend = jnp.cumsum(padded)
    pstart = pend - padded
    n_blocks = tk // ROW_BLOCK + N_EXPERTS
    dest = _dest(e_blocks, rank, jnp.broadcast_to(pstart[:, None], (N_EXPERTS, LANES)).astype(I32))
    block_start = (pstart // ROW_BLOCK).astype(I32)
    block_count = (padded // ROW_BLOCK).astype(I32)
    last_blk = jnp.where(block_count > 0, block_start + block_count - 1, -1).astype(I32)
    n_used = (pend[-1:] // ROW_BLOCK).astype(I32)
    xs = _dispatch(dest, u2p, n_blocks * ROW_BLOCK, last_blk, n_used)
    y = _experts(block_start, block_count, xs, p["w_gate_exp"], p["w_up_exp"], p["w_down_exp"])
    return dest, h1, wts, y


def kernel(x, norm_mix_w, w_in, conv_w, conv_b, dt_bias, a_log, d_skip, ssd_norm_w, fox_f_bias, w_proj_ssd,
           w_proj_fox, w_out, norm_moe_w, w_router_group, b_router_group, w_router_expert, b_router_expert,
           w_gate_exp, w_up_exp, w_down_exp, norm_final_w):
    bsz, seq, _ = x.shape
    depth = w_in.shape[0]
    assert depth == 1, "the fused final norm assumes a single layer"
    stacked = dict(norm_mix_w=norm_mix_w, w_in=w_in, conv_w=conv_w, conv_b=conv_b, dt_bias=dt_bias, a_log=a_log,
                   d_skip=d_skip, ssd_norm_w=ssd_norm_w, fox_f_bias=fox_f_bias, w_proj_ssd=w_proj_ssd,
                   w_proj_fox=w_proj_fox, w_out=w_out, norm_moe_w=norm_moe_w, w_router_group=w_router_group,
                   b_router_group=b_router_group, w_router_expert=w_router_expert,
                   b_router_expert=b_router_expert, w_gate_exp=w_gate_exp, w_up_exp=w_up_exp,
                   w_down_exp=w_down_exp)
    p = {name: v[0] for name, v in stacked.items()}
    h = x.reshape(bsz * seq, D_MODEL)
    dest, h1, wts, y = _layer(h, p, bsz, seq)
    out = _combine(dest, h1, wts, norm_final_w.astype(F32).reshape(1, D_MODEL), y)
    return out.reshape(bsz, seq, D_MODEL)
```

```python
import functools
import math

import jax
import jax.numpy as jnp
from jax import lax
from jax.experimental import pallas as pl
from jax.experimental.pallas import tpu as pltpu

F32 = jnp.float32
BF16 = jnp.bfloat16
I32 = jnp.int32
U32 = jnp.uint32

D_MODEL = 2048
SSD_HEADS = 32
SSD_HEAD_DIM = 64
SSD_DIM = SSD_HEADS * SSD_HEAD_DIM
SSD_GROUPS = 4
SSD_STATE = 128
SSD_CHUNK = 128
CONV_WIDTH = 4
CONV_DIM = SSD_DIM + 2 * SSD_GROUPS * SSD_STATE
FOX_HEADS = 16
FOX_HEAD_DIM = 128
FOX_DIM = FOX_HEADS * FOX_HEAD_DIM
N_GROUPS = 8
EXPERTS_PER_GROUP = 8
N_EXPERTS = N_GROUPS * EXPERTS_PER_GROUP
TOP_K = 2
D_EXPERT = 512
EPS = 1e-6
LOG2E = 1.4426950408889634

LANES = 128
SUBLANES = 8
VMEM_LIMIT = 52 * 1024 * 1024

COL_Z = 0
COL_Q = COL_Z + SSD_DIM
COL_K = COL_Q + FOX_DIM
COL_V = COL_K + FOX_DIM
COL_GA = COL_V + FOX_DIM
COL_GB = COL_GA + D_MODEL
COL_XBC = COL_GB + D_MODEL
PROJ_COLS = COL_XBC + CONV_DIM
SMALL_DT = 0
SMALL_DT_COPIES = 3
SMALL_F = SMALL_DT_COPIES * SSD_HEADS

ROW_BLOCK = 128
HALF = D_MODEL // 2


def _cparams(sem, vmem=VMEM_LIMIT):
    return pltpu.CompilerParams(dimension_semantics=sem, vmem_limit_bytes=vmem)


def _silu(x):
    half = 0.5 * x
    return half + half * jnp.tanh(half)


def _softplus(x):
    return jnp.maximum(x, 0.0) + jnp.log(1.0 + jnp.exp(-jnp.abs(x)))


def _log_sigmoid(x):
    return -_softplus(-x)


def _split3(x):
    hi = x.astype(BF16)
    rest = x - hi.astype(F32)
    mid = rest.astype(BF16)
    lo = (rest - mid.astype(F32)).astype(BF16)
    return hi, mid, lo


def _pack_bf16_pair(x):
    n = x.shape[1] // 2
    lo = pltpu.bitcast(x[:, :n].astype(BF16).astype(F32), U32)
    hi = pltpu.bitcast(x[:, n:].astype(BF16).astype(F32), U32)
    return (hi & jnp.uint32(0xFFFF0000)) | (lo >> 16)


def _unpack_bf16_pair(p):
    lo = pltpu.bitcast(p << 16, F32)
    hi = pltpu.bitcast(p & jnp.uint32(0xFFFF0000), F32)
    return lo, hi


def _norm_small_kernel(x_ref, nw_ref, ws_ref, u_ref, s_ref):
    x = x_ref[...]
    ms = jnp.mean(x * x, axis=-1, keepdims=True)
    u = (x * lax.rsqrt(ms + EPS) * nw_ref[...]).astype(BF16)
    u_ref[...] = u
    s_ref[...] = jnp.dot(u, ws_ref[...], preferred_element_type=F32)


def _norm_small(x2, nw, w_small, tm=512):
    t = x2.shape[0]
    return pl.pallas_call(
        _norm_small_kernel,
        name="norm_small",
        grid=(t // tm,),
        in_specs=[pl.BlockSpec((tm, D_MODEL), lambda i: (i, 0)),
                  pl.BlockSpec((1, D_MODEL), lambda i: (0, 0)),
                  pl.BlockSpec((D_MODEL, LANES), lambda i: (0, 0))],
        out_specs=[pl.BlockSpec((tm, D_MODEL), lambda i: (i, 0)),
                   pl.BlockSpec((tm, LANES), lambda i: (i, 0))],
        out_shape=[jax.ShapeDtypeStruct((t, D_MODEL), BF16),
                   jax.ShapeDtypeStruct((t, LANES), F32)],
        compiler_params=_cparams(("parallel",)),
    )(x2, nw, w_small)


IN_TN = 1024
IN_XPOSE = 256
_IN_SEGMENTS = ((COL_Z, 0, SSD_DIM),
                (COL_Q, SSD_DIM + CONV_DIM + SSD_HEADS, 3 * FOX_DIM),
                (COL_GA, SSD_DIM + CONV_DIM + SSD_HEADS + 3 * FOX_DIM + FOX_HEADS, 2 * D_MODEL),
                (COL_XBC, SSD_DIM, CONV_DIM))


def _in_proj_source_rows():
    src = [0] * (PROJ_COLS // IN_TN)
    for out0, src0, width in _IN_SEGMENTS:
        for c in range(0, width, IN_TN):
            src[(out0 + c) // IN_TN] = src0 + c
    assert all(s % SUBLANES == 0 for s in src)
    return src


def _in_proj_kernel(row_ref, u_ref, wt_hbm, o_ref, st_ref, wbf_ref, sem):
    j = pl.program_id(0)
    i = pl.program_id(1)
    k = wbf_ref.shape[0]

    def window(jj, slot):
        row0 = pl.multiple_of(row_ref[jj], SUBLANES)
        return pltpu.make_async_copy(wt_hbm.at[pl.ds(row0, IN_TN), :], st_ref.at[slot], sem.at[slot])

    @pl.when((j == 0) & (i == 0))
    def _():
        window(0, 0).start()

    @pl.when(i == 0)
    def _():
        @pl.when(j + 1 < pl.num_programs(0))
        def _():
            window(j + 1, (j + 1) % 2).start()

        slot = j % 2
        window(j, slot).wait()
        for c in range(0, k, IN_XPOSE):
            wbf_ref[c:c + IN_XPOSE, :] = jnp.transpose(st_ref[slot, :, c:c + IN_XPOSE]).astype(BF16)

    o_ref[...] = jnp.dot(u_ref[...], wbf_ref[...], preferred_element_type=F32).astype(o_ref.dtype)


def _in_proj(u, w_in_t, tm=1024):
    m, k = u.shape
    grid_spec = pltpu.PrefetchScalarGridSpec(
        num_scalar_prefetch=1,
        grid=(PROJ_COLS // IN_TN, m // tm),
        in_specs=[pl.BlockSpec((tm, k), lambda j, i, rows: (i, 0)),
                  pl.BlockSpec(memory_space=pl.ANY)],
        out_specs=pl.BlockSpec((tm, IN_TN), lambda j, i, rows: (i, j)),
        scratch_shapes=[pltpu.VMEM((2, IN_TN, k), F32),
                        pltpu.VMEM((k, IN_TN), BF16),
                        pltpu.SemaphoreType.DMA((2,))],
    )
    return pl.pallas_call(
        _in_proj_kernel,
        name="in_proj",
        grid_spec=grid_spec,
        out_shape=jax.ShapeDtypeStruct((m, PROJ_COLS), BF16),
        compiler_params=_cparams(("arbitrary", "arbitrary")),
    )(jnp.asarray(_in_proj_source_rows(), I32), u, w_in_t)


HALO = 2 * SUBLANES


def _ssd_kernel(z_ref, xbc_ref, halo_ref, small_ref, cw_ref, cb_ref, dtb_ref, aneg_ref, dexp_ref,
                nw_ref, expand_ref, shift_ref, y_ref, state_ref, ydiag_ref):
    c = pl.program_id(1)
    l = SSD_CHUNK
    n = SSD_STATE

    @pl.when(c == 0)
    def _():
        state_ref[...] = jnp.zeros_like(state_ref)

    cur = xbc_ref[...]
    halo = halo_ref[...]
    halo = jnp.where(c == 0, jnp.zeros_like(halo), halo)
    ext = jnp.concatenate([halo, cur], axis=0)
    shifted = jnp.dot(shift_ref[...], ext, preferred_element_type=F32)
    conv = cb_ref[...] + cw_ref[CONV_WIDTH - 1:CONV_WIDTH, :] * cur.astype(F32)
    for j in range(CONV_WIDTH - 1):
        conv = conv + cw_ref[j:j + 1, :] * shifted[j * l:(j + 1) * l]
    xbc = _silu(conv)
    xs = xbc[:, :SSD_DIM]
    bm = xbc[:, SSD_DIM:SSD_DIM + SSD_GROUPS * n]
    cm = xbc[:, SSD_DIM + SSD_GROUPS * n:]

    h3 = SMALL_DT_COPIES * SSD_HEADS
    dt3 = _softplus(small_ref[:, SMALL_DT:SMALL_DT + h3] + dtb_ref[...])
    adt3 = dt3 * aneg_ref[...]
    row = lax.broadcasted_iota(I32, (l, l), 0)
    col = lax.broadcasted_iota(I32, (l, l), 1)
    causal = col <= row
    tril = jnp.where(causal, 1.0, 0.0).astype(BF16)
    a_cs3 = sum(jnp.dot(tril, piece, preferred_element_type=F32) for piece in _split3(adt3))
    a_cs2 = a_cs3[:, :SSD_HEADS] * LOG2E
    a_cs2_t = jnp.transpose(a_cs2)
    a_last3 = a_cs3[l - 1:l, :]

    lane3 = lax.broadcasted_iota(I32, (l, h3), 1)

    def pieces_by_lane_group(x3):
        hi, mid, lo = _split3(x3)
        return jnp.where(lane3 < SSD_HEADS, hi, jnp.where(lane3 < 2 * SSD_HEADS, mid, lo))

    lhs3 = jnp.concatenate([pieces_by_lane_group(dt3),
                            pieces_by_lane_group(jnp.exp(a_cs3)),
                            pieces_by_lane_group(jnp.exp(a_last3 - a_cs3))], axis=0)
    expanded = jnp.dot(lhs3, expand_ref[...], preferred_element_type=F32)
    dt_x = expanded[0:l]
    decay_in = expanded[l:2 * l]
    decay_out = expanded[2 * l:3 * l]
    chunk_decay = decay_in[l - 1:l, :]
    x_dt = xs * dt_x
    x_dt_b = x_dt.astype(BF16)
    xd_b = (x_dt * decay_out).astype(BF16)

    lane = lax.broadcasted_iota(I32, (l, LANES), 1)
    lo_mask = lane < SSD_HEAD_DIM
    heads_per_group = SSD_HEADS // SSD_GROUPS
    gw = heads_per_group * SSD_HEAD_DIM

    for g in range(SSD_GROUPS):
        bg = bm[:, g * n:(g + 1) * n].astype(BF16)
        cg = cm[:, g * n:(g + 1) * n].astype(BF16)
        cb = lax.dot_general(cg, bg, (((1,), (1,)), ((), ())), preferred_element_type=F32)
        for pair in range(heads_per_group // 2):
            h0 = g * heads_per_group + 2 * pair
            lane0 = h0 * SSD_HEAD_DIM
            ms = []
            for h in (h0, h0 + 1):
                seg = a_cs2[:, h:h + 1] - a_cs2_t[h:h + 1, :]
                lmat = jnp.exp2(jnp.where(causal, seg, -jnp.inf))
                ms.append((cb * lmat).astype(BF16))
            lhs = jnp.concatenate(ms, axis=1)
            xp = x_dt_b[:, lane0:lane0 + LANES]
            zero = jnp.zeros_like(xp)
            rhs = jnp.concatenate([jnp.where(lo_mask, xp, zero), jnp.where(lo_mask, zero, xp)], axis=0)
            ydiag_ref[:, lane0:lane0 + LANES] = jnp.dot(lhs, rhs, preferred_element_type=F32)
        st = state_ref[:, g * gw:(g + 1) * gw]
        y_off = jnp.dot(cg, st.astype(BF16), preferred_element_type=F32)
        ydiag_ref[:, g * gw:(g + 1) * gw] += y_off * decay_in[:, g * gw:(g + 1) * gw]
        new = lax.dot_general(bg, xd_b[:, g * gw:(g + 1) * gw], (((0,), (0,)), ((), ())),
                              preferred_element_type=F32)
        state_ref[:, g * gw:(g + 1) * gw] = st * chunk_decay[:, g * gw:(g + 1) * gw] + new

    y = ydiag_ref[...] + dexp_ref[...] * xs
    y = y * _silu(z_ref[...].astype(F32))
    ms = jnp.mean(y * y, axis=-1, keepdims=True)
    y_ref[...] = (y * lax.rsqrt(ms + EPS) * nw_ref[...]).astype(y_ref.dtype)


def _ssd(proj, small, conv_w, conv_b, dt_bias, a_log, d_skip, norm_w, bsz, seq):
    l = SSD_CHUNK
    nc = seq // l
    t = bsz * seq
    h3 = SMALL_DT_COPIES * SSD_HEADS
    aneg = jnp.tile(-jnp.exp(a_log.astype(F32)), SMALL_DT_COPIES).reshape(1, h3)
    dtb = jnp.tile(dt_bias.astype(F32), SMALL_DT_COPIES).reshape(1, h3)
    dexp = jnp.repeat(d_skip.astype(F32), SSD_HEAD_DIM).reshape(1, SSD_DIM)
    expand = jnp.tile(jnp.repeat(jnp.eye(SSD_HEADS, dtype=BF16), SSD_HEAD_DIM, axis=1),
                      (SMALL_DT_COPIES, 1))
    out_row = jnp.arange((CONV_WIDTH - 1) * l)
    src_row = HALO + out_row % l - (CONV_WIDTH - 1) + out_row // l
    shift = (jnp.arange(HALO + l)[None, :] == src_row[:, None]).astype(BF16)
    xbc_blk = COL_XBC // CONV_DIM
    halo_per_chunk = l // HALO

    def row_map(b, c):
        return b * nc + c

    return pl.pallas_call(
        _ssd_kernel,
        name="ssd",
        grid=(bsz, nc),
        in_specs=[
            pl.BlockSpec((l, SSD_DIM), lambda b, c: (row_map(b, c), COL_Z // SSD_DIM)),
            pl.BlockSpec((l, CONV_DIM), lambda b, c: (row_map(b, c), xbc_blk)),
            pl.BlockSpec((HALO, CONV_DIM),
                         lambda b, c: (jnp.maximum(row_map(b, c) * halo_per_chunk - 1, 0), xbc_blk)),
            pl.BlockSpec((l, LANES), lambda b, c: (row_map(b, c), 0)),
            pl.BlockSpec((CONV_WIDTH, CONV_DIM), lambda b, c: (0, 0)),
            pl.BlockSpec((1, CONV_DIM), lambda b, c: (0, 0)),
            pl.BlockSpec((1, h3), lambda b, c: (0, 0)),
            pl.BlockSpec((1, h3), lambda b, c: (0, 0)),
            pl.BlockSpec((1, SSD_DIM), lambda b, c: (0, 0)),
            pl.BlockSpec((1, SSD_DIM), lambda b, c: (0, 0)),
            pl.BlockSpec((h3, SSD_DIM), lambda b, c: (0, 0)),
            pl.BlockSpec(((CONV_WIDTH - 1) * l, HALO + l), lambda b, c: (0, 0)),
        ],
        out_specs=pl.BlockSpec((l, SSD_DIM), lambda b, c: (row_map(b, c), 0)),
        out_shape=jax.ShapeDtypeStruct((t, SSD_DIM), BF16),
        scratch_shapes=[pltpu.VMEM((SSD_STATE, SSD_DIM), F32),
                        pltpu.VMEM((l, SSD_DIM), F32)],
        compiler_params=_cparams(("parallel", "arbitrary")),
    )(proj, proj, proj, small, conv_w.astype(F32), conv_b.astype(F32).reshape(1, CONV_DIM),
      dtb, aneg, dexp, norm_w.astype(F32).reshape(1, SSD_DIM), expand, shift)


CUM_ROWS = 256


def _cum_kernel(small_ref, fb_ref, cum_ref, carry_ref):
    j = pl.program_id(1)

    @pl.when(j == 0)
    def _():
        carry_ref[...] = jnp.zeros_like(carry_ref)

    lf = _log_sigmoid(small_ref[...] + fb_ref[...])
    row = lax.broadcasted_iota(I32, (CUM_ROWS, CUM_ROWS), 0)
    col = lax.broadcasted_iota(I32, (CUM_ROWS, CUM_ROWS), 1)
    tril = jnp.where(col <= row, 1.0, 0.0).astype(F32)
    cs = jnp.dot(tril, lf, preferred_element_type=F32, precision=lax.Precision.HIGHEST) + carry_ref[...]
    cum_ref[...] = cs * LOG2E
    carry_ref[...] = cs[CUM_ROWS - 1:CUM_ROWS, :]


def _fox_cum(small, fox_f_bias, bsz, seq):
    fb = jnp.zeros((1, LANES), F32).at[0, SMALL_F:SMALL_F + FOX_HEADS].set(fox_f_bias.astype(F32))
    nj = seq // CUM_ROWS
    return pl.pallas_call(
        _cum_kernel,
        name="fox_cum",
        grid=(bsz, nj),
        in_specs=[pl.BlockSpec((CUM_ROWS, LANES), lambda b, j: (b * nj + j, 0)),
                  pl.BlockSpec((1, LANES), lambda b, j: (0, 0))],
        out_specs=pl.BlockSpec((CUM_ROWS, LANES), lambda b, j: (b * nj + j, 0)),
        out_shape=jax.ShapeDtypeStruct((bsz * seq, LANES), F32),
        scratch_shapes=[pltpu.VMEM((1, LANES), F32)],
        compiler_params=_cparams(("parallel", "arbitrary")),
    )(small, fb)


FOX_HEADS_PER_STEP = 2


FOX_SLAB = 128


def _fox_kernel(q_ref, k_ref, v_ref, cq_ref, ck_ref, o_ref, sa_ref, sb_ref, p_ref, m_ref, l_ref, m_alt_ref,
                cqrep_ref, psum_ref, acc_ref, *, tq, tk):
    hp = pl.program_id(1)
    qi = pl.program_id(2)
    d = FOX_HEAD_DIM
    c2 = LOG2E / math.sqrt(d)
    lane = lax.broadcasted_iota(I32, (tq, LANES), 1)
    cq_all = cq_ref[...]

    qs, cqs = [], []
    for hh in range(FOX_HEADS_PER_STEP):
        qs.append((q_ref[:, hh * d:(hh + 1) * d].astype(F32) * c2).astype(BF16))
        head_lane = SMALL_F + hp * FOX_HEADS_PER_STEP + hh
        cq_col = jnp.sum(jnp.where(lane == head_lane, cq_all, 0.0), axis=-1, keepdims=True)
        cqs.append(jnp.broadcast_to(cq_col, (tq, LANES)))

    s_slots = (sa_ref, sb_ref)

    def scores(ki, slot):
        row0 = pl.multiple_of(ki * tk, tk)
        for hh in range(FOX_HEADS_PER_STEP):
            k = k_ref[pl.ds(row0, tk), hh * d:(hh + 1) * d]
            s = lax.dot_general(qs[hh], k, (((1,), (1,)), ((), ())), preferred_element_type=F32)
            s_slots[slot][hh] = s - ck_ref[hh, pl.ds(ki, 1), :]

    def update(ki, slot, masked):
        row0 = pl.multiple_of(ki * tk, tk)
        s_ref = s_slots[slot]
        m_bufs = (m_ref, m_alt_ref)
        n_ct = tk // LANES
        for hh in range(FOX_HEADS_PER_STEP):
            for rc in range(tq // FOX_SLAB):
                rows = slice(rc * FOX_SLAB, (rc + 1) * FOX_SLAB)
                n_vis = rc + 1 if masked else n_ct

                def slab(ct):
                    x = s_ref[hh, rows, ct * LANES:(ct + 1) * LANES]
                    if masked and ct == rc:
                        row = lax.broadcasted_iota(I32, (FOX_SLAB, LANES), 0)
                        col = lax.broadcasted_iota(I32, (FOX_SLAB, LANES), 1)
                        x = jnp.where(col <= row, x, -jnp.inf)
                    return x

                tmax = slab(0)
                for ct in range(1, n_vis):
                    tmax = jnp.maximum(tmax, slab(ct))
                row_max = jnp.max(tmax, axis=-1, keepdims=True)
                cq = cqrep_ref[hh, rows, :]
                m_new = jnp.maximum(m_bufs[slot][hh, rows, :], jnp.broadcast_to(row_max, (FOX_SLAB, LANES)) + cq)
                m_bufs[1 - slot][hh, rows, :] = m_new
                r = m_new - cq
                psum = None
                for ct in range(n_vis):
                    p = jnp.exp2(slab(ct) - r)
                    psum = p if psum is None else psum + p
                    p_ref[hh, rows, ct * LANES:(ct + 1) * LANES] = p.astype(BF16)
                for ct in range(n_vis, n_ct):
                    p_ref[hh, rows, ct * LANES:(ct + 1) * LANES] = jnp.zeros((FOX_SLAB, LANES), BF16)
                psum_ref[hh, rows, :] = psum
            m_old = m_bufs[slot][hh]
            m_cur = m_bufs[1 - slot][hh]
            l_ref[hh] = jnp.exp2(m_old - m_cur) * l_ref[hh] + psum_ref[hh]
            v = v_ref[pl.ds(row0, tk), hh * d:(hh + 1) * d]
            acc_ref[hh] = jnp.exp2(m_old - m_cur) * acc_ref[hh] + jnp.dot(p_ref[hh], v, preferred_element_type=F32)

    for hh in range(FOX_HEADS_PER_STEP):
        cqrep_ref[hh] = cqs[hh]

    m_ref[...] = jnp.full_like(m_ref, -jnp.inf)
    l_ref[...] = jnp.zeros_like(l_ref)
    acc_ref[...] = jnp.zeros_like(acc_ref)

    n_full = qi
    scores(0, 0)

    def pair(j, carry):
        b0 = 2 * j
        scores(b0 + 1, 1)
        update(b0, 0, False)
        scores(b0 + 2, 0)
        update(b0 + 1, 1, False)
        return carry

    lax.fori_loop(0, n_full // 2, pair, 0)

    @pl.when(n_full % 2 == 0)
    def _():
        update(n_full, 0, True)

    @pl.when(n_full % 2 == 1)
    def _():
        scores(n_full, 1)
        update(n_full - 1, 0, False)
        update(n_full, 1, True)

    for hh in range(FOX_HEADS_PER_STEP):
        l_fin = jnp.sum(l_ref[hh], axis=-1, keepdims=True)
        o_ref[:, hh * d:(hh + 1) * d] = (acc_ref[hh] / l_fin).astype(o_ref.dtype)


def _fox(proj, cum, bsz, seq, tq=1024):
    assert FOX_SLAB == LANES and tq % FOX_SLAB == 0 and seq % tq == 0
    tk = tq
    nq = seq // tq
    nk = seq // tk
    hps = FOX_HEADS_PER_STEP
    cum_row = cum[:, SMALL_F:SMALL_F + FOX_HEADS].reshape(bsz, seq, FOX_HEADS).transpose(0, 2, 1)
    cum_row = cum_row.reshape(bsz, FOX_HEADS, nk, tk)
    t = bsz * seq
    w = FOX_HEADS_PER_STEP * FOX_HEAD_DIM
    n_hp = FOX_HEADS // FOX_HEADS_PER_STEP
    kern = functools.partial(_fox_kernel, tq=tq, tk=tk)
    return pl.pallas_call(
        kern,
        name="fox",
        grid=(bsz, n_hp, nq),
        in_specs=[
            pl.BlockSpec((tq, w), lambda b, hp, qi: (b * nq + qi, COL_Q // w + hp)),
            pl.BlockSpec((seq, w), lambda b, hp, qi: (b, COL_K // w + hp)),
            pl.BlockSpec((seq, w), lambda b, hp, qi: (b, COL_V // w + hp)),
            pl.BlockSpec((tq, LANES), lambda b, hp, qi: (b * nq + qi, 0)),
            pl.BlockSpec((None, FOX_HEADS_PER_STEP, nk, tk), lambda b, hp, qi: (b, hp, 0, 0)),
        ],
        out_specs=pl.BlockSpec((tq, w), lambda b, hp, qi: (b * nq + qi, hp)),
        out_shape=jax.ShapeDtypeStruct((t, FOX_DIM), BF16),
        scratch_shapes=[pltpu.VMEM((hps, tq, tk), F32), pltpu.VMEM((hps, tq, tk), F32),
                        pltpu.VMEM((hps, tq, tk), BF16),
                        pltpu.VMEM((hps, tq, LANES), F32), pltpu.VMEM((hps, tq, LANES), F32),
                        pltpu.VMEM((hps, tq, LANES), F32), pltpu.VMEM((hps, tq, LANES), F32),
                        pltpu.VMEM((hps, tq, LANES), F32),
                        pltpu.VMEM((hps, tq, FOX_HEAD_DIM), F32)],
        compiler_params=_cparams(("parallel", "parallel", "arbitrary")),
    )(proj, proj, proj, cum, cum_row)


def _mix_kernel(ya_ref, yb_ref, ga_ref, gb_ref, wa_ref, wb_ref, o_ref):
    pa = jnp.dot(ya_ref[...], wa_ref[...], preferred_element_type=F32)
    pb = jnp.dot(yb_ref[...], wb_ref[...], preferred_element_type=F32)
    ga = 1.0 / (1.0 + jnp.exp(-ga_ref[...].astype(F32)))
    gb = 1.0 / (1.0 + jnp.exp(-gb_ref[...].astype(F32)))
    o_ref[...] = (ga * pa + gb * pb).astype(o_ref.dtype)


def _mix(y_a, y_b, proj, w_a, w_b, tm=512, tn=2048):
    t = y_a.shape[0]
    resident = pl.Buffered(1) if tn == D_MODEL else None
    return pl.pallas_call(
        _mix_kernel,
        name="mix",
        grid=(D_MODEL // tn, t // tm),
        in_specs=[
            pl.BlockSpec((tm, SSD_DIM), lambda j, i: (i, 0)),
            pl.BlockSpec((tm, FOX_DIM), lambda j, i: (i, 0)),
            pl.BlockSpec((tm, tn), lambda j, i: (i, COL_GA // tn + j)),
            pl.BlockSpec((tm, tn), lambda j, i: (i, COL_GB // tn + j)),
            pl.BlockSpec((SSD_DIM, tn), lambda j, i: (0, j), pipeline_mode=resident),
            pl.BlockSpec((FOX_DIM, tn), lambda j, i: (0, j), pipeline_mode=resident),
        ],
        out_specs=pl.BlockSpec((tm, tn), lambda j, i: (i, j)),
        out_shape=jax.ShapeDtypeStruct((t, D_MODEL), BF16),
        compiler_params=_cparams(("parallel", "parallel")),
    )(y_a, y_b, proj, proj, w_a, w_b)


def _outproj_kernel(m_ref, x_ref, wo_ref, nw_ref, wr_ref, br_ref, h_ref, u_ref, eidx_ref, wts_ref):
    h1 = x_ref[...] + jnp.dot(m_ref[...], wo_ref[...], preferred_element_type=F32)
    h_ref[...] = h1
    ms = jnp.mean(h1 * h1, axis=-1, keepdims=True)
    u2 = h1 * lax.rsqrt(ms + EPS) * nw_ref[...]
    u_ref[...] = _pack_bf16_pair(u2)

    u_hi = u2.astype(BF16)
    u_lo = (u2 - u_hi.astype(F32)).astype(BF16)
    hh_hl = jnp.dot(u_hi, wr_ref[...], preferred_element_type=F32)
    lh = jnp.dot(u_lo, wr_ref[:, :LANES], preferred_element_type=F32)
    logits = hh_hl[:, :LANES] + (hh_hl[:, LANES:] + lh) + br_ref[...]
    tm = logits.shape[0]
    lane = lax.broadcasted_iota(I32, (tm, LANES), 1)
    neg = -jnp.inf
    big = jnp.int32(2 * LANES)
    gl = jnp.where(lane < N_GROUPS, logits, neg)
    gmax = jnp.max(gl, axis=-1, keepdims=True)
    gsum = jnp.sum(jnp.exp(gl - gmax), axis=-1, keepdims=True)
    g_p = 1.0 / gsum
    g_idx = jnp.min(jnp.where(gl == gmax, lane, big), axis=-1, keepdims=True)
    e_of_lane = lane - N_GROUPS
    in_grp = (e_of_lane >= g_idx * EXPERTS_PER_GROUP) & (e_of_lane < (g_idx + 1) * EXPERTS_PER_GROUP)
    el = jnp.where(in_grp, logits, neg)
    m1 = jnp.max(el, axis=-1, keepdims=True)
    i1 = jnp.min(jnp.where(el == m1, lane, big), axis=-1, keepdims=True)
    el2 = jnp.where(lane == i1, neg, el)
    m2 = jnp.max(el2, axis=-1, keepdims=True)
    i2 = jnp.min(jnp.where(el2 == m2, lane, big), axis=-1, keepdims=True)
    esum = jnp.sum(jnp.exp(el - m1), axis=-1, keepdims=True)
    p1 = 1.0 / esum
    p2 = jnp.exp(m2 - m1) / esum
    w1 = g_p * (p1 / (p1 + p2))
    w2 = g_p * (p2 / (p1 + p2))
    eidx_ref[...] = jnp.where(lane == 0, i1 - N_GROUPS, jnp.where(lane == 1, i2 - N_GROUPS, 0))
    wts_ref[...] = jnp.where(lane == 0, w1, jnp.where(lane == 1, w2, 0.0))


def _outproj(mixed, x2, w_o, nw, w_router, b_router, tm=512):
    t = mixed.shape[0]
    resident = pl.Buffered(1)
    return pl.pallas_call(
        _outproj_kernel,
        name="outproj",
        grid=(t // tm,),
        in_specs=[
            pl.BlockSpec((tm, D_MODEL), lambda i: (i, 0)),
            pl.BlockSpec((tm, D_MODEL), lambda i: (i, 0)),
            pl.BlockSpec((D_MODEL, D_MODEL), lambda i: (0, 0), pipeline_mode=resident),
            pl.BlockSpec((1, D_MODEL), lambda i: (0, 0)),
            pl.BlockSpec((D_MODEL, 2 * LANES), lambda i: (0, 0), pipeline_mode=resident),
            pl.BlockSpec((1, LANES), lambda i: (0, 0)),
        ],
        out_specs=[
            pl.BlockSpec((tm, D_MODEL), lambda i: (i, 0)),
            pl.BlockSpec((tm, HALF), lambda i: (i, 0)),
            pl.BlockSpec((tm, LANES), lambda i: (i, 0)),
            pl.BlockSpec((tm, LANES), lambda i: (i, 0)),
        ],
        out_shape=[
            jax.ShapeDtypeStruct((t, D_MODEL), F32),
            jax.ShapeDtypeStruct((t, HALF), U32),
            jax.ShapeDtypeStruct((t, LANES), I32),
            jax.ShapeDtypeStruct((t, LANES), F32),
        ],
        compiler_params=_cparams(("parallel",)),
    )(mixed, x2, w_o, nw, w_router, b_router)


RANK_BLOCK = 512


def _rank_kernel(e_ref, rank_ref, cnt_ref, carry_ref):
    i = pl.program_id(0)
    r = RANK_BLOCK

    @pl.when(i == 0)
    def _():
        carry_ref[...] = jnp.zeros_like(carry_ref)

    e = e_ref[0]
    expert = lax.broadcasted_iota(I32, (N_EXPERTS, r), 0)
    onehot = jnp.where(expert == e, 1.0, 0.0).astype(F32)
    jrow = lax.broadcasted_iota(I32, (r, r), 0)
    jcol = lax.broadcasted_iota(I32, (r, r), 1)
    before = jnp.where(jrow < jcol, 1.0, 0.0).astype(BF16)
    cum = jnp.dot(onehot.astype(BF16), before, preferred_element_type=F32)
    carry = carry_ref[...]
    rank = jnp.sum(onehot * (cum + carry[:, 0:1]), axis=0, keepdims=True)
    rank_ref[0] = rank.astype(I32)
    carry = carry + jnp.sum(onehot, axis=1, keepdims=True)
    carry_ref[...] = carry
    cnt_ref[...] = carry


def _rank(e_blocks):
    nb = e_blocks.shape[0]
    return pl.pallas_call(
        _rank_kernel,
        name="rank",
        grid=(nb,),
        in_specs=[pl.BlockSpec((1, 1, RANK_BLOCK), lambda i: (i, 0, 0))],
        out_specs=[pl.BlockSpec((1, 1, RANK_BLOCK), lambda i: (i, 0, 0)),
                   pl.BlockSpec((N_EXPERTS, LANES), lambda i: (0, 0))],
        out_shape=[jax.ShapeDtypeStruct((nb, 1, RANK_BLOCK), I32),
                   jax.ShapeDtypeStruct((N_EXPERTS, LANES), F32)],
        scratch_shapes=[pltpu.VMEM((N_EXPERTS, LANES), F32)],
        compiler_params=_cparams(("arbitrary",)),
    )(e_blocks)


def _dest_kernel(e_ref, rank_ref, pstart_ref, dest_ref):
    e = e_ref[0]
    expert = lax.broadcasted_iota(I32, (N_EXPERTS, RANK_BLOCK), 0)
    start = jnp.sum(jnp.where(expert == e, pstart_ref[:, 0:1], 0), axis=0, keepdims=True)
    dest_ref[0] = rank_ref[0] + start


def _dest(e_blocks, rank, pstart):
    nb = e_blocks.shape[0]
    return pl.pallas_call(
        _dest_kernel,
        name="dest",
        grid=(nb,),
        in_specs=[pl.BlockSpec((1, 1, RANK_BLOCK), lambda i: (i, 0, 0)),
                  pl.BlockSpec((1, 1, RANK_BLOCK), lambda i: (i, 0, 0)),
                  pl.BlockSpec((N_EXPERTS, LANES), lambda i: (0, 0))],
        out_specs=pl.BlockSpec((1, 1, RANK_BLOCK), lambda i: (i, 0, 0)),
        out_shape=jax.ShapeDtypeStruct((nb, 1, RANK_BLOCK), I32),
        compiler_params=_cparams(("parallel",)),
    )(e_blocks, rank, pstart)


def _row_copy(src_ref, src_row, dst_ref, dst_row, sem):
    return pltpu.make_async_copy(src_ref.at[pl.ds(src_row, 1)], dst_ref.at[pl.ds(dst_row, 1)], sem)


DMA_UNROLL = 16


def _dispatch_kernel(last_blk_ref, nused_ref, dest_ref, u_ref, xs_ref, zbuf_ref, sem, zsem):
    i = pl.program_id(0)
    n_total = xs_ref.shape[0] // ROW_BLOCK

    @pl.when(i == 0)
    def _():
        zbuf_ref[...] = jnp.zeros_like(zbuf_ref)

        def zero_block(blk):
            return pltpu.make_async_copy(zbuf_ref, xs_ref.at[pl.ds(blk * ROW_BLOCK, ROW_BLOCK)], zsem)

        def fill_expert(e, count):
            blk = last_blk_ref[e]

            @pl.when(blk >= 0)
            def _():
                zero_block(blk).start()

            return count + jnp.where(blk >= 0, 1, 0)

        n_fill = lax.fori_loop(0, N_EXPERTS, fill_expert, 0)

        def fill_tail(blk, carry):
            zero_block(blk).start()
            return carry

        lax.fori_loop(nused_ref[0], n_total, fill_tail, 0)

        def drain(k, carry):
            zero_block(0).wait()
            return carry

        lax.fori_loop(0, n_fill + n_total - nused_ref[0], drain, 0)

    def issue(g, carry):
        c0 = g * DMA_UNROLL
        for j in range(DMA_UNROLL):
            dst_row = dest_ref[0, 0, c0 + j]
            pltpu.make_async_copy(u_ref.at[g, pl.ds(j // TOP_K, 1)], xs_ref.at[pl.ds(dst_row, 1)], sem).start()
        return carry

    lax.fori_loop(0, RANK_BLOCK // DMA_UNROLL, issue, 0)
    pltpu.make_async_copy(xs_ref.at[pl.ds(0, RANK_BLOCK)], xs_ref.at[pl.ds(0, RANK_BLOCK)], sem).wait()


def _dispatch(dest, u2p, n_rows, last_blk, n_used):
    nb = dest.shape[0]
    grid_spec = pltpu.PrefetchScalarGridSpec(
        num_scalar_prefetch=2,
        grid=(nb,),
        in_specs=[pl.BlockSpec((1, 1, RANK_BLOCK), lambda i, lb, nu: (i, 0, 0), memory_space=pltpu.SMEM),
                  pl.BlockSpec((RANK_BLOCK // DMA_UNROLL, SUBLANES, HALF), lambda i, lb, nu: (i, 0, 0))],
        out_specs=pl.BlockSpec(memory_space=pl.ANY),
        scratch_shapes=[pltpu.VMEM((ROW_BLOCK, HALF), U32),
                        pltpu.SemaphoreType.DMA(()),
                        pltpu.SemaphoreType.DMA(())],
    )
    assert DMA_UNROLL == TOP_K * SUBLANES
    return pl.pallas_call(
        _dispatch_kernel,
        name="dispatch",
        grid_spec=grid_spec,
        out_shape=jax.ShapeDtypeStruct((n_rows, HALF), U32),
        compiler_params=_cparams(("arbitrary",)),
    )(last_blk, n_used, dest, u2p.reshape(-1, SUBLANES, HALF))


ROW_DMA_PRIORITY = 1
ROW_RING = 8


def _experts_kernel(bstart_ref, nblk_ref, xs_ref, wg_ref, wu_ref, wd_ref, y_ref,
                    wgb_ref, wub_ref, wdb_ref, xbuf_ref, ybuf_ref, xsem, ysem):
    e = pl.program_id(0)
    n_e = pl.num_programs(0)
    g0 = bstart_ref[e]
    nb = nblk_ref[e]
    n_used = bstart_ref[n_e - 1] + nblk_ref[n_e - 1]
    n_total = y_ref.shape[0] // ROW_BLOCK

    def x_copy(g, slot):
        return pltpu.make_async_copy(xs_ref.at[pl.ds(g * ROW_BLOCK, ROW_BLOCK)], xbuf_ref.at[slot], xsem.at[slot])

    def y_copy(g, slot):
        return pltpu.make_async_copy(ybuf_ref.at[slot], y_ref.at[pl.ds(g * ROW_BLOCK, ROW_BLOCK)], ysem.at[slot])

    @pl.when(e == 0)
    def _():
        for g in range(ROW_RING - 1):
            @pl.when(g < n_used)
            def _(g=g):
                x_copy(g, g).start(priority=ROW_DMA_PRIORITY)

    @pl.when(nb > 0)
    def _():
        wgb_ref[...] = wg_ref[0].astype(BF16)
        wub_ref[...] = wu_ref[0].astype(BF16)
        wdb_ref[...] = wd_ref[0].astype(BF16)

    def block(j, carry):
        g = g0 + j
        slot = g % ROW_RING
        x_copy(g, slot).wait()

        @pl.when(g + ROW_RING - 1 < n_used)
        def _():
            x_copy(g + ROW_RING - 1, (g + ROW_RING - 1) % ROW_RING).start(priority=ROW_DMA_PRIORITY)

        @pl.when(g >= ROW_RING)
        def _():
            y_copy(g - ROW_RING, slot).wait()

        lo, hi = _unpack_bf16_pair(xbuf_ref[slot])
        lo = lo.astype(BF16)
        hi = hi.astype(BF16)
        gate = (jnp.dot(lo, wgb_ref[:HALF, :], preferred_element_type=F32)
                + jnp.dot(hi, wgb_ref[HALF:, :], preferred_element_type=F32))
        up = (jnp.dot(lo, wub_ref[:HALF, :], preferred_element_type=F32)
              + jnp.dot(hi, wub_ref[HALF:, :], preferred_element_type=F32))
        hdn = (_silu(gate) * up).astype(BF16)
        y = jnp.dot(hdn, wdb_ref[...], preferred_element_type=F32)
        ybuf_ref[slot] = _pack_bf16_pair(y)
        y_copy(g, slot).start(priority=ROW_DMA_PRIORITY)
        return carry

    lax.fori_loop(0, nb, block, 0)

    @pl.when(e == n_e - 1)
    def _():
        for back in range(ROW_RING, 0, -1):
            @pl.when(n_used >= back)
            def _(back=back):
                y_copy(n_used - back, (n_used - back) % ROW_RING).wait()

        ybuf_ref[0] = jnp.zeros((ROW_BLOCK, HALF), U32)

        def fill(g, carry):
            y_copy(g, 0).start()
            return carry

        def fill_done(g, carry):
            y_copy(g, 0).wait()
            return carry

        lax.fori_loop(n_used, n_total, fill, 0)
        lax.fori_loop(n_used, n_total, fill_done, 0)


def _experts(block_start, block_count, xs, w_gate, w_up, w_down):
    n_rows = xs.shape[0]

    def wmap(e, bs, bc):
        return (e, 0, 0)

    grid_spec = pltpu.PrefetchScalarGridSpec(
        num_scalar_prefetch=2,
        grid=(N_EXPERTS,),
        in_specs=[pl.BlockSpec(memory_space=pl.ANY),
                  pl.BlockSpec((1, D_MODEL, D_EXPERT), wmap),
                  pl.BlockSpec((1, D_MODEL, D_EXPERT), wmap),
                  pl.BlockSpec((1, D_EXPERT, D_MODEL), wmap)],
        out_specs=pl.BlockSpec(memory_space=pl.ANY),
        scratch_shapes=[pltpu.VMEM((D_MODEL, D_EXPERT), BF16),
                        pltpu.VMEM((D_MODEL, D_EXPERT), BF16),
                        pltpu.VMEM((D_EXPERT, D_MODEL), BF16),
                        pltpu.VMEM((ROW_RING, ROW_BLOCK, HALF), U32),
                        pltpu.VMEM((ROW_RING, ROW_BLOCK, HALF), U32),
                        pltpu.SemaphoreType.DMA((ROW_RING,)),
                        pltpu.SemaphoreType.DMA((ROW_RING,))],
    )
    return pl.pallas_call(
        _experts_kernel,
        name="experts",
        grid_spec=grid_spec,
        out_shape=jax.ShapeDtypeStruct((n_rows, HALF), U32),
        compiler_params=_cparams(("arbitrary",)),
    )(block_start, block_count, xs, w_gate, w_up, w_down)


COMBINE_TOKENS = RANK_BLOCK // TOP_K


def _combine_kernel(dest_ref, dest_next_ref, h_ref, wts_ref, nw_ref, y_ref, o_ref, buf_ref, sem):
    ts = COMBINE_TOKENS
    i = pl.program_id(0)
    slot = i % 2

    def gather(idx_ref, dst_slot):
        def issue(g, carry):
            c0 = g * DMA_UNROLL
            for j in range(DMA_UNROLL):
                src_row = idx_ref[0, 0, c0 + j]
                src_tile = lax.shift_right_logical(src_row, SUBLANES.bit_length() - 1)
                pltpu.make_async_copy(y_ref.at[src_tile, pl.ds(src_row & (SUBLANES - 1), 1)],
                                      buf_ref.at[dst_slot, j % TOP_K, g, pl.ds(j // TOP_K, 1)],
                                      sem.at[dst_slot]).start()
            return carry

        lax.fori_loop(0, RANK_BLOCK // DMA_UNROLL, issue, 0)

    @pl.when(i == 0)
    def _():
        gather(dest_ref, 0)

    @pl.when(i + 1 < pl.num_programs(0))
    def _():
        gather(dest_next_ref, 1 - slot)

    for k in range(TOP_K):
        pltpu.make_async_copy(y_ref.at[pl.ds(0, ts // SUBLANES)], buf_ref.at[slot, k], sem.at[slot]).wait()

    w = wts_ref[...]
    w0 = w[:, 0:1]
    w1 = w[:, 1:2]
    lo0, hi0 = _unpack_bf16_pair(buf_ref[slot, 0].reshape(ts, HALF))
    lo1, hi1 = _unpack_bf16_pair(buf_ref[slot, 1].reshape(ts, HALF))
    h = h_ref[...]
    out_lo = h[:, :HALF] + w0 * lo0 + w1 * lo1
    out_hi = h[:, HALF:] + w0 * hi0 + w1 * hi1
    ms = (jnp.sum(out_lo * out_lo, axis=-1, keepdims=True)
          + jnp.sum(out_hi * out_hi, axis=-1, keepdims=True)) * (1.0 / D_MODEL)
    inv = lax.rsqrt(ms + EPS)
    o_ref[:, :HALF] = out_lo * inv * nw_ref[:, :HALF]
    o_ref[:, HALF:] = out_hi * inv * nw_ref[:, HALF:]


def _combine(dest, h1, wts, nw, y):
    t = h1.shape[0]
    ts = COMBINE_TOKENS
    n_steps = t // ts
    return pl.pallas_call(
        _combine_kernel,
        name="combine",
        grid=(n_steps,),
        in_specs=[pl.BlockSpec((1, 1, RANK_BLOCK), lambda i: (i, 0, 0), memory_space=pltpu.SMEM),
                  pl.BlockSpec((1, 1, RANK_BLOCK), lambda i: (jnp.minimum(i + 1, n_steps - 1), 0, 0),
                               memory_space=pltpu.SMEM),
                  pl.BlockSpec((ts, D_MODEL), lambda i: (i, 0)),
                  pl.BlockSpec((ts, LANES), lambda i: (i, 0)),
                  pl.BlockSpec((1, D_MODEL), lambda i: (0, 0)),
                  pl.BlockSpec(memory_space=pl.ANY)],
        out_specs=pl.BlockSpec((ts, D_MODEL), lambda i: (i, 0)),
        out_shape=jax.ShapeDtypeStruct((t, D_MODEL), F32),
        scratch_shapes=[pltpu.VMEM((2, TOP_K, ts // SUBLANES, SUBLANES, HALF), U32),
                        pltpu.SemaphoreType.DMA((2,))],
        compiler_params=_cparams(("arbitrary",)),
    )(dest, dest, h1, wts, nw, y.reshape(-1, SUBLANES, HALF))


def _narrow_w_in(w_in_t):
    dt0 = SSD_DIM + CONV_DIM
    f0 = dt0 + SSD_HEADS + 3 * FOX_DIM
    dt = w_in_t[dt0:dt0 + SSD_HEADS]
    f = w_in_t[f0:f0 + FOX_HEADS]
    pad = jnp.zeros((LANES - SMALL_F - FOX_HEADS, w_in_t.shape[1]), w_in_t.dtype)
    return jnp.concatenate([dt] * SMALL_DT_COPIES + [f, pad], axis=0).T.astype(BF16)


def _layer(h, p, bsz, seq):
    t = bsz * seq
    w_in_t = jnp.swapaxes(p["w_in"].astype(F32), 0, 1)
    u, small = _norm_small(h, p["norm_mix_w"].astype(F32).reshape(1, D_MODEL), _narrow_w_in(w_in_t))
    proj = _in_proj(u, w_in_t)
    y_ssd = _ssd(proj, small, p["conv_w"], p["conv_b"], p["dt_bias"], p["a_log"], p["d_skip"],
                 p["ssd_norm_w"], bsz, seq)
    cum = _fox_cum(small, p["fox_f_bias"], bsz, seq)
    y_fox = _fox(proj, cum, bsz, seq)
    mixed = _mix(y_ssd, y_fox, proj, p["w_proj_ssd"].astype(BF16), p["w_proj_fox"].astype(BF16))

    w_router = jnp.concatenate(
        [p["w_router_group"], p["w_router_expert"],
         jnp.zeros((D_MODEL, LANES - N_GROUPS - N_EXPERTS), F32)], axis=1).astype(F32)
    w_router_hi = w_router.astype(BF16)
    w_router_lo = (w_router - w_router_hi.astype(F32)).astype(BF16)
    w_router = jnp.concatenate([w_router_hi, w_router_lo], axis=1)
    b_router = jnp.concatenate(
        [p["b_router_group"], p["b_router_expert"],
         jnp.zeros((LANES - N_GROUPS - N_EXPERTS,), F32)]).astype(F32).reshape(1, LANES)
    h1, u2p, eidx, wts = _outproj(mixed, h, p["w_out"].astype(BF16),
                                  p["norm_moe_w"].astype(F32).reshape(1, D_MODEL), w_router, b_router)

    tk = t * TOP_K
    e_blocks = eidx[:, :TOP_K].reshape(tk // RANK_BLOCK, 1, RANK_BLOCK)
    rank, counts = _rank(e_blocks)
    counts = counts[:, 0].astype(I32)
    padded = ((counts + ROW_BLOCK - 1) // ROW_BLOCK) * ROW_BLOCK
    pend = jnp.cumsum(padded)
    pstart = pend - padded
    n_blocks = tk // ROW_BLOCK + N_EXPERTS
    dest = _dest(e_blocks, rank, jnp.broadcast_to(pstart[:, None], (N_EXPERTS, LANES)).astype(I32))
    block_start = (pstart // ROW_BLOCK).astype(I32)
    block_count = (padded // ROW_BLOCK).astype(I32)
    last_blk = jnp.where(block_count > 0, block_start + block_count - 1, -1).astype(I32)
    n_used = (pend[-1:] // ROW_BLOCK).astype(I32)
    xs = _dispatch(dest, u2p, n_blocks * ROW_BLOCK, last_blk, n_used)
    y = _experts(block_start, block_count, xs, p["w_gate_exp"], p["w_up_exp"], p["w_down_exp"])
    return dest, h1, wts, y


def kernel(x, norm_mix_w, w_in, conv_w, conv_b, dt_bias, a_log, d_skip, ssd_norm_w, fox_f_bias, w_proj_ssd,
           w_proj_fox, w_out, norm_moe_w, w_router_group, b_router_group, w_router_expert, b_router_expert,
           w_gate_exp, w_up_exp, w_down_exp, norm_final_w):
    bsz, seq, _ = x.shape
    depth = w_in.shape[0]
    assert depth == 1, "the fused final norm assumes a single layer"
    stacked = dict(norm_mix_w=norm_mix_w, w_in=w_in, conv_w=conv_w, conv_b=conv_b, dt_bias=dt_bias, a_log=a_log,
                   d_skip=d_skip, ssd_norm_w=ssd_norm_w, fox_f_bias=fox_f_bias, w_proj_ssd=w_proj_ssd,
                   w_proj_fox=w_proj_fox, w_out=w_out, norm_moe_w=norm_moe_w, w_router_group=w_router_group,
                   b_router_group=b_router_group, w_router_expert=w_router_expert,
                   b_router_expert=b_router_expert, w_gate_exp=w_gate_exp, w_up_exp=w_up_exp,
                   w_down_exp=w_down_exp)
    p = {name: v[0] for name, v in stacked.items()}
    h = x.reshape(bsz * seq, D_MODEL)
    dest, h1, wts, y = _layer(h, p, bsz, seq)
    out = _combine(dest, h1, wts, norm_final_w.astype(F32).reshape(1, D_MODEL), y)
    return out.reshape(bsz, seq, D_MODEL)
```

```python
import functools
import math

import jax
import jax.numpy as jnp
from jax import lax
from jax.experimental import pallas as pl
from jax.experimental.pallas import tpu as pltpu

F32 = jnp.float32
BF16 = jnp.bfloat16
I32 = jnp.int32
U32 = jnp.uint32

D_MODEL = 2048
SSD_HEADS = 32
SSD_HEAD_DIM = 64
SSD_DIM = SSD_HEADS * SSD_HEAD_DIM
SSD_GROUPS = 4
SSD_STATE = 128
SSD_CHUNK = 128
CONV_WIDTH = 4
CONV_DIM = SSD_DIM + 2 * SSD_GROUPS * SSD_STATE
FOX_HEADS = 16
FOX_HEAD_DIM = 128
FOX_DIM = FOX_HEADS * FOX_HEAD_DIM
N_GROUPS = 8
EXPERTS_PER_GROUP = 8
N_EXPERTS = N_GROUPS * EXPERTS_PER_GROUP
TOP_K = 2
D_EXPERT = 512
EPS = 1e-6
LOG2E = 1.4426950408889634

LANES = 128
SUBLANES = 8
VMEM_LIMIT = 52 * 1024 * 1024

COL_Z = 0
COL_Q = COL_Z + SSD_DIM
COL_K = COL_Q + FOX_DIM
COL_V = COL_K + FOX_DIM
COL_GA = COL_V + FOX_DIM
COL_GB = COL_GA + D_MODEL
COL_XBC = COL_GB + D_MODEL
PROJ_COLS = COL_XBC + CONV_DIM
SMALL_DT = 0
SMALL_DT_COPIES = 3
SMALL_F = SMALL_DT_COPIES * SSD_HEADS

ROW_BLOCK = 128
HALF = D_MODEL // 2


def _cparams(sem, vmem=VMEM_LIMIT):
    return pltpu.CompilerParams(dimension_semantics=sem, vmem_limit_bytes=vmem)


def _silu(x):
    half = 0.5 * x
    return half + half * jnp.tanh(half)


def _softplus(x):
    return jnp.maximum(x, 0.0) + jnp.log(1.0 + jnp.exp(-jnp.abs(x)))


def _log_sigmoid(x):
    return -_softplus(-x)


def _split3(x):
    hi = x.astype(BF16)
    rest = x - hi.astype(F32)
    mid = rest.astype(BF16)
    lo = (rest - mid.astype(F32)).astype(BF16)
    return hi, mid, lo


def _pack_bf16_pair(x):
    n = x.shape[1] // 2
    lo = pltpu.bitcast(x[:, :n].astype(BF16).astype(F32), U32)
    hi = pltpu.bitcast(x[:, n:].astype(BF16).astype(F32), U32)
    return (hi & jnp.uint32(0xFFFF0000)) | (lo >> 16)


def _unpack_bf16_pair(p):
    lo = pltpu.bitcast(p << 16, F32)
    hi = pltpu.bitcast(p & jnp.uint32(0xFFFF0000), F32)
    return lo, hi


def _norm_small_kernel(x_ref, nw_ref, ws_ref, u_ref, s_ref):
    x = x_ref[...]
    ms = jnp.mean(x * x, axis=-1, keepdims=True)
    u = (x * lax.rsqrt(ms + EPS) * nw_ref[...]).astype(BF16)
    u_ref[...] = u
    s_ref[...] = jnp.dot(u, ws_ref[...], preferred_element_type=F32)


def _norm_small(x2, nw, w_small, tm=512):
    t = x2.shape[0]
    return pl.pallas_call(
        _norm_small_kernel,
        name="norm_small",
        grid=(t // tm,),
        in_specs=[pl.BlockSpec((tm, D_MODEL), lambda i: (i, 0)),
                  pl.BlockSpec((1, D_MODEL), lambda i: (0, 0)),
                  pl.BlockSpec((D_MODEL, LANES), lambda i: (0, 0))],
        out_specs=[pl.BlockSpec((tm, D_MODEL), lambda i: (i, 0)),
                   pl.BlockSpec((tm, LANES), lambda i: (i, 0))],
        out_shape=[jax.ShapeDtypeStruct((t, D_MODEL), BF16),
                   jax.ShapeDtypeStruct((t, LANES), F32)],
        compiler_params=_cparams(("parallel",)),
    )(x2, nw, w_small)


IN_TN = 1024
IN_XPOSE = 256
_IN_SEGMENTS = ((COL_Z, 0, SSD_DIM),
                (COL_Q, SSD_DIM + CONV_DIM + SSD_HEADS, 3 * FOX_DIM),
                (COL_GA, SSD_DIM + CONV_DIM + SSD_HEADS + 3 * FOX_DIM + FOX_HEADS, 2 * D_MODEL),
                (COL_XBC, SSD_DIM, CONV_DIM))


def _in_proj_source_rows():
    src = [0] * (PROJ_COLS // IN_TN)
    for out0, src0, width in _IN_SEGMENTS:
        for c in range(0, width, IN_TN):
            src[(out0 + c) // IN_TN] = src0 + c
    assert all(s % SUBLANES == 0 for s in src)
    return src


def _in_proj_kernel(row_ref, u_ref, wt_hbm, o_ref, st_ref, wbf_ref, sem):
    j = pl.program_id(0)
    i = pl.program_id(1)
    k = wbf_ref.shape[0]

    def window(jj, slot):
        row0 = pl.multiple_of(row_ref[jj], SUBLANES)
        return pltpu.make_async_copy(wt_hbm.at[pl.ds(row0, IN_TN), :], st_ref.at[slot], sem.at[slot])

    @pl.when((j == 0) & (i == 0))
    def _():
        window(0, 0).start()

    @pl.when(i == 0)
    def _():
        @pl.when(j + 1 < pl.num_programs(0))
        def _():
            window(j + 1, (j + 1) % 2).start()

        slot = j % 2
        window(j, slot).wait()
        for c in range(0, k, IN_XPOSE):
            wbf_ref[c:c + IN_XPOSE, :] = jnp.transpose(st_ref[slot, :, c:c + IN_XPOSE]).astype(BF16)

    o_ref[...] = jnp.dot(u_ref[...], wbf_ref[...], preferred_element_type=F32).astype(o_ref.dtype)


def _in_proj(u, w_in_t, tm=1024):
    m, k = u.shape
    grid_spec = pltpu.PrefetchScalarGridSpec(
        num_scalar_prefetch=1,
        grid=(PROJ_COLS // IN_TN, m // tm),
        in_specs=[pl.BlockSpec((tm, k), lambda j, i, rows: (i, 0)),
                  pl.BlockSpec(memory_space=pl.ANY)],
        out_specs=pl.BlockSpec((tm, IN_TN), lambda j, i, rows: (i, j)),
        scratch_shapes=[pltpu.VMEM((2, IN_TN, k), F32),
                        pltpu.VMEM((k, IN_TN), BF16),
                        pltpu.SemaphoreType.DMA((2,))],
    )
    return pl.pallas_call(
        _in_proj_kernel,
        name="in_proj",
        grid_spec=grid_spec,
        out_shape=jax.ShapeDtypeStruct((m, PROJ_COLS), BF16),
        compiler_params=_cparams(("arbitrary", "arbitrary")),
    )(jnp.asarray(_in_proj_source_rows(), I32), u, w_in_t)


HALO = 2 * SUBLANES


def _ssd_kernel(z_ref, xbc_ref, halo_ref, small_ref, cw_ref, cb_ref, dtb_ref, aneg_ref, dexp_ref,
                nw_ref, expand_ref, shift_ref, y_ref, state_ref, ydiag_ref):
    c = pl.program_id(1)
    l = SSD_CHUNK
    n = SSD_STATE

    @pl.when(c == 0)
    def _():
        state_ref[...] = jnp.zeros_like(state_ref)

    cur = xbc_ref[...]
    halo = halo_ref[...]
    halo = jnp.where(c == 0, jnp.zeros_like(halo), halo)
    ext = jnp.concatenate([halo, cur], axis=0)
    shifted = jnp.dot(shift_ref[...], ext, preferred_element_type=F32)
    conv = cb_ref[...] + cw_ref[CONV_WIDTH - 1:CONV_WIDTH, :] * cur.astype(F32)
    for j in range(CONV_WIDTH - 1):
        conv = conv + cw_ref[j:j + 1, :] * shifted[j * l:(j + 1) * l]
    xbc = _silu(conv)
    xs = xbc[:, :SSD_DIM]
    bm = xbc[:, SSD_DIM:SSD_DIM + SSD_GROUPS * n]
    cm = xbc[:, SSD_DIM + SSD_GROUPS * n:]

    h3 = SMALL_DT_COPIES * SSD_HEADS
    dt3 = _softplus(small_ref[:, SMALL_DT:SMALL_DT + h3] + dtb_ref[...])
    adt3 = dt3 * aneg_ref[...]
    row = lax.broadcasted_iota(I32, (l, l), 0)
    col = lax.broadcasted_iota(I32, (l, l), 1)
    causal = col <= row
    tril = jnp.where(causal, 1.0, 0.0).astype(BF16)
    a_cs3 = sum(jnp.dot(tril, piece, preferred_element_type=F32) for piece in _split3(adt3))
    a_cs2 = a_cs3[:, :SSD_HEADS] * LOG2E
    a_cs2_t = jnp.transpose(a_cs2)
    a_last3 = a_cs3[l - 1:l, :]

    lane3 = lax.broadcasted_iota(I32, (l, h3), 1)

    def pieces_by_lane_group(x3):
        hi, mid, lo = _split3(x3)
        return jnp.where(lane3 < SSD_HEADS, hi, jnp.where(lane3 < 2 * SSD_HEADS, mid, lo))

    lhs3 = jnp.concatenate([pieces_by_lane_group(dt3),
                            pieces_by_lane_group(jnp.exp(a_cs3)),
                            pieces_by_lane_group(jnp.exp(a_last3 - a_cs3))], axis=0)
    expanded = jnp.dot(lhs3, expand_ref[...], preferred_element_type=F32)
    dt_x = expanded[0:l]
    decay_in = expanded[l:2 * l]
    decay_out = expanded[2 * l:3 * l]
    chunk_decay = decay_in[l - 1:l, :]
    x_dt = xs * dt_x
    x_dt_b = x_dt.astype(BF16)
    xd_b = (x_dt * decay_out).astype(BF16)

    lane = lax.broadcasted_iota(I32, (l, LANES), 1)
    lo_mask = lane < SSD_HEAD_DIM
    heads_per_group = SSD_HEADS // SSD_GROUPS
    gw = heads_per_group * SSD_HEAD_DIM

    for g in range(SSD_GROUPS):
        bg = bm[:, g * n:(g + 1) * n].astype(BF16)
        cg = cm[:, g * n:(g + 1) * n].astype(BF16)
        cb = lax.dot_general(cg, bg, (((1,), (1,)), ((), ())), preferred_element_type=F32)
        for pair in range(heads_per_group // 2):
            h0 = g * heads_per_group + 2 * pair
            lane0 = h0 * SSD_HEAD_DIM
            ms = []
            for h in (h0, h0 + 1):
                seg = a_cs2[:, h:h + 1] - a_cs2_t[h:h + 1, :]
                lmat = jnp.exp2(jnp.where(causal, seg, -jnp.inf))
                ms.append((cb * lmat).astype(BF16))
            lhs = jnp.concatenate(ms, axis=1)
            xp = x_dt_b[:, lane0:lane0 + LANES]
            zero = jnp.zeros_like(xp)
            rhs = jnp.concatenate([jnp.where(lo_mask, xp, zero), jnp.where(lo_mask, zero, xp)], axis=0)
            ydiag_ref[:, lane0:lane0 + LANES] = jnp.dot(lhs, rhs, preferred_element_type=F32)
        st = state_ref[:, g * gw:(g + 1) * gw]
        y_off = jnp.dot(cg, st.astype(BF16), preferred_element_type=F32)
        ydiag_ref[:, g * gw:(g + 1) * gw] += y_off * decay_in[:, g * gw:(g + 1) * gw]
        new = lax.dot_general(bg, xd_b[:, g * gw:(g + 1) * gw], (((0,), (0,)), ((), ())),
                              preferred_element_type=F32)
        state_ref[:, g * gw:(g + 1) * gw] = st * chunk_decay[:, g * gw:(g + 1) * gw] + new

    y = ydiag_ref[...] + dexp_ref[...] * xs
    y = y * _silu(z_ref[...].astype(F32))
    ms = jnp.mean(y * y, axis=-1, keepdims=True)
    y_ref[...] = (y * lax.rsqrt(ms + EPS) * nw_ref[...]).astype(y_ref.dtype)


def _ssd(proj, small, conv_w, conv_b, dt_bias, a_log, d_skip, norm_w, bsz, seq):
    l = SSD_CHUNK
    nc = seq // l
    t = bsz * seq
    h3 = SMALL_DT_COPIES * SSD_HEADS
    aneg = jnp.tile(-jnp.exp(a_log.astype(F32)), SMALL_DT_COPIES).reshape(1, h3)
    dtb = jnp.tile(dt_bias.astype(F32), SMALL_DT_COPIES).reshape(1, h3)
    dexp = jnp.repeat(d_skip.astype(F32), SSD_HEAD_DIM).reshape(1, SSD_DIM)
    expand = jnp.tile(jnp.repeat(jnp.eye(SSD_HEADS, dtype=BF16), SSD_HEAD_DIM, axis=1),
                      (SMALL_DT_COPIES, 1))
    out_row = jnp.arange((CONV_WIDTH - 1) * l)
    src_row = HALO + out_row % l - (CONV_WIDTH - 1) + out_row // l
    shift = (jnp.arange(HALO + l)[None, :] == src_row[:, None]).astype(BF16)
    xbc_blk = COL_XBC // CONV_DIM
    halo_per_chunk = l // HALO

    def row_map(b, c):
        return b * nc + c

    return pl.pallas_call(
        _ssd_kernel,
        name="ssd",
        grid=(bsz, nc),
        in_specs=[
            pl.BlockSpec((l, SSD_DIM), lambda b, c: (row_map(b, c), COL_Z // SSD_DIM)),
            pl.BlockSpec((l, CONV_DIM), lambda b, c: (row_map(b, c), xbc_blk)),
            pl.BlockSpec((HALO, CONV_DIM),
                         lambda b, c: (jnp.maximum(row_map(b, c) * halo_per_chunk - 1, 0), xbc_blk)),
            pl.BlockSpec((l, LANES), lambda b, c: (row_map(b, c), 0)),
            pl.BlockSpec((CONV_WIDTH, CONV_DIM), lambda b, c: (0, 0)),
            pl.BlockSpec((1, CONV_DIM), lambda b, c: (0, 0)),
            pl.BlockSpec((1, h3), lambda b, c: (0, 0)),
            pl.BlockSpec((1, h3), lambda b, c: (0, 0)),
            pl.BlockSpec((1, SSD_DIM), lambda b, c: (0, 0)),
            pl.BlockSpec((1, SSD_DIM), lambda b, c: (0, 0)),
            pl.BlockSpec((h3, SSD_DIM), lambda b, c: (0, 0)),
            pl.BlockSpec(((CONV_WIDTH - 1) * l, HALO + l), lambda b, c: (0, 0)),
        ],
        out_specs=pl.BlockSpec((l, SSD_DIM), lambda b, c: (row_map(b, c), 0)),
        out_shape=jax.ShapeDtypeStruct((t, SSD_DIM), BF16),
        scratch_shapes=[pltpu.VMEM((SSD_STATE, SSD_DIM), F32),
                        pltpu.VMEM((l, SSD_DIM), F32)],
        compiler_params=_cparams(("parallel", "arbitrary")),
    )(proj, proj, proj, small, conv_w.astype(F32), conv_b.astype(F32).reshape(1, CONV_DIM),
      dtb, aneg, dexp, norm_w.astype(F32).reshape(1, SSD_DIM), expand, shift)


CUM_ROWS = 256


def _cum_kernel(small_ref, fb_ref, cum_ref, carry_ref):
    j = pl.program_id(1)

    @pl.when(j == 0)
    def _():
        carry_ref[...] = jnp.zeros_like(carry_ref)

    lf = _log_sigmoid(small_ref[...] + fb_ref[...])
    row = lax.broadcasted_iota(I32, (CUM_ROWS, CUM_ROWS), 0)
    col = lax.broadcasted_iota(I32, (CUM_ROWS, CUM_ROWS), 1)
    tril = jnp.where(col <= row, 1.0, 0.0).astype(F32)
    cs = jnp.dot(tril, lf, preferred_element_type=F32, precision=lax.Precision.HIGHEST) + carry_ref[...]
    cum_ref[...] = cs * LOG2E
    carry_ref[...] = cs[CUM_ROWS - 1:CUM_ROWS, :]


def _fox_cum(small, fox_f_bias, bsz, seq):
    fb = jnp.zeros((1, LANES), F32).at[0, SMALL_F:SMALL_F + FOX_HEADS].set(fox_f_bias.astype(F32))
    nj = seq // CUM_ROWS
    return pl.pallas_call(
        _cum_kernel,
        name="fox_cum",
        grid=(bsz, nj),
        in_specs=[pl.BlockSpec((CUM_ROWS, LANES), lambda b, j: (b * nj + j, 0)),
                  pl.BlockSpec((1, LANES), lambda b, j: (0, 0))],
        out_specs=pl.BlockSpec((CUM_ROWS, LANES), lambda b, j: (b * nj + j, 0)),
        out_shape=jax.ShapeDtypeStruct((bsz * seq, LANES), F32),
        scratch_shapes=[pltpu.VMEM((1, LANES), F32)],
        compiler_params=_cparams(("parallel", "arbitrary")),
    )(small, fb)


FOX_HEADS_PER_STEP = 2


FOX_SLAB = 128


def _fox_kernel(q_ref, k_ref, v_ref, cq_ref, ck_ref, o_ref, sa_ref, sb_ref, p_ref, m_ref, l_ref, m_alt_ref,
                cqrep_ref, psum_ref, acc_ref, *, tq, tk):
    hp = pl.program_id(1)
    qi = pl.program_id(2)
    d = FOX_HEAD_DIM
    c2 = LOG2E / math.sqrt(d)
    lane = lax.broadcasted_iota(I32, (tq, LANES), 1)
    cq_all = cq_ref[...]

    qs, cqs = [], []
    for hh in range(FOX_HEADS_PER_STEP):
        qs.append((q_ref[:, hh * d:(hh + 1) * d].astype(F32) * c2).astype(BF16))
        head_lane = SMALL_F + hp * FOX_HEADS_PER_STEP + hh
        cq_col = jnp.sum(jnp.where(lane == head_lane, cq_all, 0.0), axis=-1, keepdims=True)
        cqs.append(jnp.broadcast_to(cq_col, (tq, LANES)))

    s_slots = (sa_ref, sb_ref)

    def scores(ki, slot):
        row0 = pl.multiple_of(ki * tk, tk)
        for hh in range(FOX_HEADS_PER_STEP):
            k = k_ref[pl.ds(row0, tk), hh * d:(hh + 1) * d]
            s = lax.dot_general(qs[hh], k, (((1,), (1,)), ((), ())), preferred_element_type=F32)
            s_slots[slot][hh] = s - ck_ref[hh, pl.ds(ki, 1), :]

    def update(ki, slot, masked):
        row0 = pl.multiple_of(ki * tk, tk)
        s_ref = s_slots[slot]
        m_bufs = (m_ref, m_alt_ref)
        n_ct = tk // LANES
        for hh in range(FOX_HEADS_PER_STEP):
            for rc in range(tq // FOX_SLAB):
                rows = slice(rc * FOX_SLAB, (rc + 1) * FOX_SLAB)
                n_vis = rc + 1 if masked else n_ct

                def slab(ct):
                    x = s_ref[hh, rows, ct * LANES:(ct + 1) * LANES]
                    if masked and ct == rc:
                        row = lax.broadcasted_iota(I32, (FOX_SLAB, LANES), 0)
                        col = lax.broadcasted_iota(I32, (FOX_SLAB, LANES), 1)
                        x = jnp.where(col <= row, x, -jnp.inf)
                    return x

                tmax = slab(0)
                for ct in range(1, n_vis):
                    tmax = jnp.maximum(tmax, slab(ct))
                row_max = jnp.max(tmax, axis=-1, keepdims=True)
                cq = cqrep_ref[hh, rows, :]
                m_new = jnp.maximum(m_bufs[slot][hh, rows, :], jnp.broadcast_to(row_max, (FOX_SLAB, LANES)) + cq)
                m_bufs[1 - slot][hh, rows, :] = m_new
                r = m_new - cq
                psum = None
                for ct in range(n_vis):
                    p = jnp.exp2(slab(ct) - r)
                    psum = p if psum is None else psum + p
                    p_ref[hh, rows, ct * LANES:(ct + 1) * LANES] = p.astype(BF16)
                for ct in range(n_vis, n_ct):
                    p_ref[hh, rows, ct * LANES:(ct + 1) * LANES] = jnp.zeros((FOX_SLAB, LANES), BF16)
                psum_ref[hh, rows, :] = psum
            m_old = m_bufs[slot][hh]
            m_cur = m_bufs[1 - slot][hh]
            l_ref[hh] = jnp.exp2(m_old - m_cur) * l_ref[hh] + psum_ref[hh]
            v = v_ref[pl.ds(row0, tk), hh * d:(hh + 1) * d]
            acc_ref[hh] = jnp.exp2(m_old - m_cur) * acc_ref[hh] + jnp.dot(p_ref[hh], v, preferred_element_type=F32)

    for hh in range(FOX_HEADS_PER_STEP):
        cqrep_ref[hh] = cqs[hh]

    m_ref[...] = jnp.full_like(m_ref, -jnp.inf)
    l_ref[...] = jnp.zeros_like(l_ref)
    acc_ref[...] = jnp.zeros_like(acc_ref)

    n_full = qi
    scores(0, 0)

    def pair(j, carry):
        b0 = 2 * j
        scores(b0 + 1, 1)
        update(b0, 0, False)
        scores(b0 + 2, 0)
        update(b0 + 1, 1, False)
        return carry

    lax.fori_loop(0, n_full // 2, pair, 0)

    @pl.when(n_full % 2 == 0)
    def _():
        update(n_full, 0, True)

    @pl.when(n_full % 2 == 1)
    def _():
        scores(n_full, 1)
        update(n_full - 1, 0, False)
        update(n_full, 1, True)

    for hh in range(FOX_HEADS_PER_STEP):
        l_fin = jnp.sum(l_ref[hh], axis=-1, keepdims=True)
        o_ref[:, hh * d:(hh + 1) * d] = (acc_ref[hh] / l_fin).astype(o_ref.dtype)


def _fox(proj, cum, bsz, seq, tq=1024):
    assert FOX_SLAB == LANES and tq % FOX_SLAB == 0 and seq % tq == 0
    tk = tq
    nq = seq // tq
    nk = seq // tk
    hps = FOX_HEADS_PER_STEP
    cum_row = cum[:, SMALL_F:SMALL_F + FOX_HEADS].reshape(bsz, seq, FOX_HEADS).transpose(0, 2, 1)
    cum_row = cum_row.reshape(bsz, FOX_HEADS, nk, tk)
    t = bsz * seq
    w = FOX_HEADS_PER_STEP * FOX_HEAD_DIM
    n_hp = FOX_HEADS // FOX_HEADS_PER_STEP
    kern = functools.partial(_fox_kernel, tq=tq, tk=tk)
    return pl.pallas_call(
        kern,
        name="fox",
        grid=(bsz, n_hp, nq),
        in_specs=[
            pl.BlockSpec((tq, w), lambda b, hp, qi: (b * nq + qi, COL_Q // w + hp)),
            pl.BlockSpec((seq, w), lambda b, hp, qi: (b, COL_K // w + hp)),
            pl.BlockSpec((seq, w), lambda b, hp, qi: (b, COL_V // w + hp)),
            pl.BlockSpec((tq, LANES), lambda b, hp, qi: (b * nq + qi, 0)),
            pl.BlockSpec((None, FOX_HEADS_PER_STEP, nk, tk), lambda b, hp, qi: (b, hp, 0, 0)),
        ],
        out_specs=pl.BlockSpec((tq, w), lambda b, hp, qi: (b * nq + qi, hp)),
        out_shape=jax.ShapeDtypeStruct((t, FOX_DIM), BF16),
        scratch_shapes=[pltpu.VMEM((hps, tq, tk), F32), pltpu.VMEM((hps, tq, tk), F32),
                        pltpu.VMEM((hps, tq, tk), BF16),
                        pltpu.VMEM((hps, tq, LANES), F32), pltpu.VMEM((hps, tq, LANES), F32),
                        pltpu.VMEM((hps, tq, LANES), F32), pltpu.VMEM((hps, tq, LANES), F32),
                        pltpu.VMEM((hps, tq, LANES), F32),
                        pltpu.VMEM((hps, tq, FOX_HEAD_DIM), F32)],
        compiler_params=_cparams(("parallel", "parallel", "arbitrary")),
    )(proj, proj, proj, cum, cum_row)


def _mix_kernel(ya_ref, yb_ref, ga_ref, gb_ref, wa_ref, wb_ref, o_ref):
    pa = jnp.dot(ya_ref[...], wa_ref[...], preferred_element_type=F32)
    pb = jnp.dot(yb_ref[...], wb_ref[...], preferred_element_type=F32)
    ga = 1.0 / (1.0 + jnp.exp(-ga_ref[...].astype(F32)))
    gb = 1.0 / (1.0 + jnp.exp(-gb_ref[...].astype(F32)))
    o_ref[...] = (ga * pa + gb * pb).astype(o_ref.dtype)


def _mix(y_a, y_b, proj, w_a, w_b, tm=512, tn=2048):
    t = y_a.shape[0]
    resident = pl.Buffered(1) if tn == D_MODEL else None
    return pl.pallas_call(
        _mix_kernel,
        name="mix",
        grid=(D_MODEL // tn, t // tm),
        in_specs=[
            pl.BlockSpec((tm, SSD_DIM), lambda j, i: (i, 0)),
            pl.BlockSpec((tm, FOX_DIM), lambda j, i: (i, 0)),
            pl.BlockSpec((tm, tn), lambda j, i: (i, COL_GA // tn + j)),
            pl.BlockSpec((tm, tn), lambda j, i: (i, COL_GB // tn + j)),
            pl.BlockSpec((SSD_DIM, tn), lambda j, i: (0, j), pipeline_mode=resident),
            pl.BlockSpec((FOX_DIM, tn), lambda j, i: (0, j), pipeline_mode=resident),
        ],
        out_specs=pl.BlockSpec((tm, tn), lambda j, i: (i, j)),
        out_shape=jax.ShapeDtypeStruct((t, D_MODEL), BF16),
        compiler_params=_cparams(("parallel", "parallel")),
    )(y_a, y_b, proj, proj, w_a, w_b)


def _outproj_kernel(m_ref, x_ref, wo_ref, nw_ref, wr_ref, br_ref, h_ref, u_ref, eidx_ref, wts_ref):
    h1 = x_ref[...] + jnp.dot(m_ref[...], wo_ref[...], preferred_element_type=F32)
    h_ref[...] = h1
    ms = jnp.mean(h1 * h1, axis=-1, keepdims=True)
    u2 = h1 * lax.rsqrt(ms + EPS) * nw_ref[...]
    u_ref[...] = _pack_bf16_pair(u2)

    u_hi = u2.astype(BF16)
    u_lo = (u2 - u_hi.astype(F32)).astype(BF16)
    hh_hl = jnp.dot(u_hi, wr_ref[...], preferred_element_type=F32)
    lh = jnp.dot(u_lo, wr_ref[:, :LANES], preferred_element_type=F32)
    logits = hh_hl[:, :LANES] + (hh_hl[:, LANES:] + lh) + br_ref[...]
    tm = logits.shape[0]
    lane = lax.broadcasted_iota(I32, (tm, LANES), 1)
    neg = -jnp.inf
    big = jnp.int32(2 * LANES)
    gl = jnp.where(lane < N_GROUPS, logits, neg)
    gmax = jnp.max(gl, axis=-1, keepdims=True)
    gsum = jnp.sum(jnp.exp(gl - gmax), axis=-1, keepdims=True)
    g_p = 1.0 / gsum
    g_idx = jnp.min(jnp.where(gl == gmax, lane, big), axis=-1, keepdims=True)
    e_of_lane = lane - N_GROUPS
    in_grp = (e_of_lane >= g_idx * EXPERTS_PER_GROUP) & (e_of_lane < (g_idx + 1) * EXPERTS_PER_GROUP)
    el = jnp.where(in_grp, logits, neg)
    m1 = jnp.max(el, axis=-1, keepdims=True)
    i1 = jnp.min(jnp.where(el == m1, lane, big), axis=-1, keepdims=True)
    el2 = jnp.where(lane == i1, neg, el)
    m2 = jnp.max(el2, axis=-1, keepdims=True)
    i2 = jnp.min(jnp.where(el2 == m2, lane, big), axis=-1, keepdims=True)
    esum = jnp.sum(jnp.exp(el - m1), axis=-1, keepdims=True)
    p1 = 1.0 / esum
    p2 = jnp.exp(m2 - m1) / esum
    w1 = g_p * (p1 / (p1 + p2))
    w2 = g_p * (p2 / (p1 + p2))
    eidx_ref[...] = jnp.where(lane == 0, i1 - N_GROUPS, jnp.where(lane == 1, i2 - N_GROUPS, 0))
    wts_ref[...] = jnp.where(lane == 0, w1, jnp.where(lane == 1, w2, 0.0))


def _outproj(mixed, x2, w_o, nw, w_router, b_router, tm=512):
    t = mixed.shape[0]
    resident = pl.Buffered(1)
    return pl.pallas_call(
        _outproj_kernel,
        name="outproj",
        grid=(t // tm,),
        in_specs=[
            pl.BlockSpec((tm, D_MODEL), lambda i: (i, 0)),
            pl.BlockSpec((tm, D_MODEL), lambda i: (i, 0)),
            pl.BlockSpec((D_MODEL, D_MODEL), lambda i: (0, 0), pipeline_mode=resident),
            pl.BlockSpec((1, D_MODEL), lambda i: (0, 0)),
            pl.BlockSpec((D_MODEL, 2 * LANES), lambda i: (0, 0), pipeline_mode=resident),
            pl.BlockSpec((1, LANES), lambda i: (0, 0)),
        ],
        out_specs=[
            pl.BlockSpec((tm, D_MODEL), lambda i: (i, 0)),
            pl.BlockSpec((tm, HALF), lambda i: (i, 0)),
            pl.BlockSpec((tm, LANES), lambda i: (i, 0)),
            pl.BlockSpec((tm, LANES), lambda i: (i, 0)),
        ],
        out_shape=[
            jax.ShapeDtypeStruct((t, D_MODEL), F32),
            jax.ShapeDtypeStruct((t, HALF), U32),
            jax.ShapeDtypeStruct((t, LANES), I32),
            jax.ShapeDtypeStruct((t, LANES), F32),
        ],
        compiler_params=_cparams(("parallel",)),
    )(mixed, x2, w_o, nw, w_router, b_router)


RANK_BLOCK = 512


def _rank_kernel(e_ref, rank_ref, cnt_ref, carry_ref):
    i = pl.program_id(0)
    r = RANK_BLOCK

    @pl.when(i == 0)
    def _():
        carry_ref[...] = jnp.zeros_like(carry_ref)

    e = e_ref[0]
    expert = lax.broadcasted_iota(I32, (N_EXPERTS, r), 0)
    onehot = jnp.where(expert == e, 1.0, 0.0).astype(F32)
    jrow = lax.broadcasted_iota(I32, (r, r), 0)
    jcol = lax.broadcasted_iota(I32, (r, r), 1)
    before = jnp.where(jrow < jcol, 1.0, 0.0).astype(BF16)
    cum = jnp.dot(onehot.astype(BF16), before, preferred_element_type=F32)
    carry = carry_ref[...]
    rank = jnp.sum(onehot * (cum + carry[:, 0:1]), axis=0, keepdims=True)
    rank_ref[0] = rank.astype(I32)
    carry = carry + jnp.sum(onehot, axis=1, keepdims=True)
    carry_ref[...] = carry
    cnt_ref[...] = carry


def _rank(e_blocks):
    nb = e_blocks.shape[0]
    return pl.pallas_call(
        _rank_kernel,
        name="rank",
        grid=(nb,),
        in_specs=[pl.BlockSpec((1, 1, RANK_BLOCK), lambda i: (i, 0, 0))],
        out_specs=[pl.BlockSpec((1, 1, RANK_BLOCK), lambda i: (i, 0, 0)),
                   pl.BlockSpec((N_EXPERTS, LANES), lambda i: (0, 0))],
        out_shape=[jax.ShapeDtypeStruct((nb, 1, RANK_BLOCK), I32),
                   jax.ShapeDtypeStruct((N_EXPERTS, LANES), F32)],
        scratch_shapes=[pltpu.VMEM((N_EXPERTS, LANES), F32)],
        compiler_params=_cparams(("arbitrary",)),
    )(e_blocks)


def _dest_kernel(e_ref, rank_ref, pstart_ref, dest_ref):
    e = e_ref[0]
    expert = lax.broadcasted_iota(I32, (N_EXPERTS, RANK_BLOCK), 0)
    start = jnp.sum(jnp.where(expert == e, pstart_ref[:, 0:1], 0), axis=0, keepdims=True)
    dest_ref[0] = rank_ref[0] + start


def _dest(e_blocks, rank, pstart):
    nb = e_blocks.shape[0]
    return pl.pallas_call(
        _dest_kernel,
        name="dest",
        grid=(nb,),
        in_specs=[pl.BlockSpec((1, 1, RANK_BLOCK), lambda i: (i, 0, 0)),
                  pl.BlockSpec((1, 1, RANK_BLOCK), lambda i: (i, 0, 0)),
                  pl.BlockSpec((N_EXPERTS, LANES), lambda i: (0, 0))],
        out_specs=pl.BlockSpec((1, 1, RANK_BLOCK), lambda i: (i, 0, 0)),
        out_shape=jax.ShapeDtypeStruct((nb, 1, RANK_BLOCK), I32),
        compiler_params=_cparams(("parallel",)),
    )(e_blocks, rank, pstart)


DMA_UNROLL = 16


DISPATCH_RING = 3


def _dispatch_kernel(last_blk_ref, nused_ref, dest_ref, u_hbm, xs_ref, ubuf_ref, zbuf_ref, lsem, rsem, zsem):
    i = pl.program_id(0)
    n_steps = pl.num_programs(0)
    n_total = xs_ref.shape[0] // ROW_BLOCK
    groups = ubuf_ref.shape[1]

    def load(blk, slot):
        return pltpu.make_async_copy(u_hbm.at[pl.ds(blk * groups, groups)], ubuf_ref.at[slot], lsem.at[slot])

    def rows_done(slot):
        return pltpu.make_async_copy(xs_ref.at[pl.ds(0, RANK_BLOCK)], xs_ref.at[pl.ds(0, RANK_BLOCK)], rsem.at[slot])

    @pl.when(i == 0)
    def _():
        load(0, 0).start()

    @pl.when(i == 0)
    def _():
        zbuf_ref[...] = jnp.zeros_like(zbuf_ref)

        def zero_block(blk):
            return pltpu.make_async_copy(zbuf_ref, xs_ref.at[pl.ds(blk * ROW_BLOCK, ROW_BLOCK)], zsem)

        def fill_expert(e, count):
            blk = last_blk_ref[e]

            @pl.when(blk >= 0)
            def _():
                zero_block(blk).start()

            return count + jnp.where(blk >= 0, 1, 0)

        n_fill = lax.fori_loop(0, N_EXPERTS, fill_expert, 0)

        def fill_tail(blk, carry):
            zero_block(blk).start()
            return carry

        lax.fori_loop(nused_ref[0], n_total, fill_tail, 0)

        def drain(k, carry):
            zero_block(0).wait()
            return carry

        lax.fori_loop(0, n_fill + n_total - nused_ref[0], drain, 0)

    slot = i % DISPATCH_RING
    load(i, slot).wait()

    @pl.when(i >= DISPATCH_RING - 1)
    def _():
        rows_done((i + 1) % DISPATCH_RING).wait()

    @pl.when(i + 1 < n_steps)
    def _():
        load(i + 1, (i + 1) % DISPATCH_RING).start()

    def issue(g, carry):
        c0 = g * DMA_UNROLL
        for j in range(DMA_UNROLL):
            dst_row = dest_ref[0, 0, c0 + j]
            pltpu.make_async_copy(ubuf_ref.at[slot, g, pl.ds(j // TOP_K, 1)], xs_ref.at[pl.ds(dst_row, 1)],
                                  rsem.at[slot]).start()
        return carry

    lax.fori_loop(0, RANK_BLOCK // DMA_UNROLL, issue, 0)

    @pl.when(i == n_steps - 1)
    def _():
        for back in range(DISPATCH_RING - 2, -1, -1):
            @pl.when(i >= back)
            def _(back=back):
                rows_done((i - back) % DISPATCH_RING).wait()


def _dispatch(dest, u2p, n_rows, last_blk, n_used):
    nb = dest.shape[0]
    groups = RANK_BLOCK // DMA_UNROLL
    grid_spec = pltpu.PrefetchScalarGridSpec(
        num_scalar_prefetch=2,
        grid=(nb,),
        in_specs=[pl.BlockSpec((1, 1, RANK_BLOCK), lambda i, lb, nu: (i, 0, 0), memory_space=pltpu.SMEM),
                  pl.BlockSpec(memory_space=pl.ANY)],
        out_specs=pl.BlockSpec(memory_space=pl.ANY),
        scratch_shapes=[pltpu.VMEM((DISPATCH_RING, groups, SUBLANES, HALF), U32),
                        pltpu.VMEM((ROW_BLOCK, HALF), U32),
                        pltpu.SemaphoreType.DMA((DISPATCH_RING,)),
                        pltpu.SemaphoreType.DMA((DISPATCH_RING,)),
                        pltpu.SemaphoreType.DMA(())],
    )
    assert DMA_UNROLL == TOP_K * SUBLANES
    return pl.pallas_call(
        _dispatch_kernel,
        name="dispatch",
        grid_spec=grid_spec,
        out_shape=jax.ShapeDtypeStruct((n_rows, HALF), U32),
        compiler_params=_cparams(("arbitrary",)),
    )(last_blk, n_used, dest, u2p.reshape(-1, SUBLANES, HALF))


ROW_DMA_PRIORITY = 1
ROW_RING = 8


def _experts_kernel(bstart_ref, nblk_ref, xs_ref, wg_ref, wu_ref, wd_ref, y_ref,
                    wgb_ref, wub_ref, wdb_ref, xbuf_ref, ybuf_ref, xsem, ysem):
    e = pl.program_id(0)
    n_e = pl.num_programs(0)
    g0 = bstart_ref[e]
    nb = nblk_ref[e]
    n_used = bstart_ref[n_e - 1] + nblk_ref[n_e - 1]
    n_total = y_ref.shape[0] // ROW_BLOCK

    def x_copy(g, slot):
        return pltpu.make_async_copy(xs_ref.at[pl.ds(g * ROW_BLOCK, ROW_BLOCK)], xbuf_ref.at[slot], xsem.at[slot])

    def y_copy(g, slot):
        return pltpu.make_async_copy(ybuf_ref.at[slot], y_ref.at[pl.ds(g * ROW_BLOCK, ROW_BLOCK)], ysem.at[slot])

    @pl.when(e == 0)
    def _():
        for g in range(ROW_RING - 1):
            @pl.when(g < n_used)
            def _(g=g):
                x_copy(g, g).start(priority=ROW_DMA_PRIORITY)

    @pl.when(nb > 0)
    def _():
        wgb_ref[...] = wg_ref[0].astype(BF16)
        wub_ref[...] = wu_ref[0].astype(BF16)
        wdb_ref[...] = wd_ref[0].astype(BF16)

    def block(j, carry):
        g = g0 + j
        slot = g % ROW_RING
        x_copy(g, slot).wait()

        @pl.when(g + ROW_RING - 1 < n_used)
        def _():
            x_copy(g + ROW_RING - 1, (g + ROW_RING - 1) % ROW_RING).start(priority=ROW_DMA_PRIORITY)

        @pl.when(g >= ROW_RING)
        def _():
            y_copy(g - ROW_RING, slot).wait()

        lo, hi = _unpack_bf16_pair(xbuf_ref[slot])
        lo = lo.astype(BF16)
        hi = hi.astype(BF16)
        gate = (jnp.dot(lo, wgb_ref[:HALF, :], preferred_element_type=F32)
                + jnp.dot(hi, wgb_ref[HALF:, :], preferred_element_type=F32))
        up = (jnp.dot(lo, wub_ref[:HALF, :], preferred_element_type=F32)
              + jnp.dot(hi, wub_ref[HALF:, :], preferred_element_type=F32))
        hdn = (_silu(gate) * up).astype(BF16)
        y = jnp.dot(hdn, wdb_ref[...], preferred_element_type=F32)
        ybuf_ref[slot] = _pack_bf16_pair(y)
        y_copy(g, slot).start(priority=ROW_DMA_PRIORITY)
        return carry

    lax.fori_loop(0, nb, block, 0)

    @pl.when(e == n_e - 1)
    def _():
        for back in range(ROW_RING, 0, -1):
            @pl.when(n_used >= back)
            def _(back=back):
                y_copy(n_used - back, (n_used - back) % ROW_RING).wait()

        ybuf_ref[0] = jnp.zeros((ROW_BLOCK, HALF), U32)

        def fill(g, carry):
            y_copy(g, 0).start()
            return carry

        def fill_done(g, carry):
            y_copy(g, 0).wait()
            return carry

        lax.fori_loop(n_used, n_total, fill, 0)
        lax.fori_loop(n_used, n_total, fill_done, 0)


def _experts(block_start, block_count, xs, w_gate, w_up, w_down):
    n_rows = xs.shape[0]

    def wmap(e, bs, bc):
        return (e, 0, 0)

    grid_spec = pltpu.PrefetchScalarGridSpec(
        num_scalar_prefetch=2,
        grid=(N_EXPERTS,),
        in_specs=[pl.BlockSpec(memory_space=pl.ANY),
                  pl.BlockSpec((1, D_MODEL, D_EXPERT), wmap),
                  pl.BlockSpec((1, D_MODEL, D_EXPERT), wmap),
                  pl.BlockSpec((1, D_EXPERT, D_MODEL), wmap)],
        out_specs=pl.BlockSpec(memory_space=pl.ANY),
        scratch_shapes=[pltpu.VMEM((D_MODEL, D_EXPERT), BF16),
                        pltpu.VMEM((D_MODEL, D_EXPERT), BF16),
                        pltpu.VMEM((D_EXPERT, D_MODEL), BF16),
                        pltpu.VMEM((ROW_RING, ROW_BLOCK, HALF), U32),
                        pltpu.VMEM((ROW_RING, ROW_BLOCK, HALF), U32),
                        pltpu.SemaphoreType.DMA((ROW_RING,)),
                        pltpu.SemaphoreType.DMA((ROW_RING,))],
    )
    return pl.pallas_call(
        _experts_kernel,
        name="experts",
        grid_spec=grid_spec,
        out_shape=jax.ShapeDtypeStruct((n_rows, HALF), U32),
        compiler_params=_cparams(("arbitrary",)),
    )(block_start, block_count, xs, w_gate, w_up, w_down)


COMBINE_TOKENS = RANK_BLOCK // TOP_K


def _combine_kernel(dest_ref, dest_next_ref, h_ref, wts_ref, nw_ref, y_ref, o_ref, buf_ref, sem):
    ts = COMBINE_TOKENS
    i = pl.program_id(0)
    slot = i % 2

    def gather(idx_ref, dst_slot):
        def issue(g, carry):
            c0 = g * DMA_UNROLL
            for j in range(DMA_UNROLL):
                src_row = idx_ref[0, 0, c0 + j]
                src_tile = lax.shift_right_logical(src_row, SUBLANES.bit_length() - 1)
                pltpu.make_async_copy(y_ref.at[src_tile, pl.ds(src_row & (SUBLANES - 1), 1)],
                                      buf_ref.at[dst_slot, j % TOP_K, g, pl.ds(j // TOP_K, 1)],
                                      sem.at[dst_slot]).start()
            return carry

        lax.fori_loop(0, RANK_BLOCK // DMA_UNROLL, issue, 0)

    @pl.when(i == 0)
    def _():
        gather(dest_ref, 0)

    @pl.when(i + 1 < pl.num_programs(0))
    def _():
        gather(dest_next_ref, 1 - slot)

    for k in range(TOP_K):
        pltpu.make_async_copy(y_ref.at[pl.ds(0, ts // SUBLANES)], buf_ref.at[slot, k], sem.at[slot]).wait()

    w = wts_ref[...]
    w0 = w[:, 0:1]
    w1 = w[:, 1:2]
    lo0, hi0 = _unpack_bf16_pair(buf_ref[slot, 0].reshape(ts, HALF))
    lo1, hi1 = _unpack_bf16_pair(buf_ref[slot, 1].reshape(ts, HALF))
    h = h_ref[...]
    out_lo = h[:, :HALF] + w0 * lo0 + w1 * lo1
    out_hi = h[:, HALF:] + w0 * hi0 + w1 * hi1
    ms = (jnp.sum(out_lo * out_lo, axis=-1, keepdims=True)
          + jnp.sum(out_hi * out_hi, axis=-1, keepdims=True)) * (1.0 / D_MODEL)
    inv = lax.rsqrt(ms + EPS)
    o_ref[:, :HALF] = out_lo * inv * nw_ref[:, :HALF]
    o_ref[:, HALF:] = out_hi * inv * nw_ref[:, HALF:]


def _combine(dest, h1, wts, nw, y):
    t = h1.shape[0]
    ts = COMBINE_TOKENS
    n_steps = t // ts
    return pl.pallas_call(
        _combine_kernel,
        name="combine",
        grid=(n_steps,),
        in_specs=[pl.BlockSpec((1, 1, RANK_BLOCK), lambda i: (i, 0, 0), memory_space=pltpu.SMEM),
                  pl.BlockSpec((1, 1, RANK_BLOCK), lambda i: (jnp.minimum(i + 1, n_steps - 1), 0, 0),
                               memory_space=pltpu.SMEM),
                  pl.BlockSpec((ts, D_MODEL), lambda i: (i, 0)),
                  pl.BlockSpec((ts, LANES), lambda i: (i, 0)),
                  pl.BlockSpec((1, D_MODEL), lambda i: (0, 0)),
                  pl.BlockSpec(memory_space=pl.ANY)],
        out_specs=pl.BlockSpec((ts, D_MODEL), lambda i: (i, 0)),
        out_shape=jax.ShapeDtypeStruct((t, D_MODEL), F32),
        scratch_shapes=[pltpu.VMEM((2, TOP_K, ts // SUBLANES, SUBLANES, HALF), U32),
                        pltpu.SemaphoreType.DMA((2,))],
        compiler_params=_cparams(("arbitrary",)),
    )(dest, dest, h1, wts, nw, y.reshape(-1, SUBLANES, HALF))


def _narrow_w_in(w_in_t):
    dt0 = SSD_DIM + CONV_DIM
    f0 = dt0 + SSD_HEADS + 3 * FOX_DIM
    dt = w_in_t[dt0:dt0 + SSD_HEADS]
    f = w_in_t[f0:f0 + FOX_HEADS]
    pad = jnp.zeros((LANES - SMALL_F - FOX_HEADS, w_in_t.shape[1]), w_in_t.dtype)
    return jnp.concatenate([dt] * SMALL_DT_COPIES + [f, pad], axis=0).T.astype(BF16)


def _layer(h, p, bsz, seq):
    t = bsz * seq
    w_in_t = jnp.swapaxes(p["w_in"].astype(F32), 0, 1)
    u, small = _norm_small(h, p["norm_mix_w"].astype(F32).reshape(1, D_MODEL), _narrow_w_in(w_in_t))
    proj = _in_proj(u, w_in_t)
    y_ssd = _ssd(proj, small, p["conv_w"], p["conv_b"], p["dt_bias"], p["a_log"], p["d_skip"],
                 p["ssd_norm_w"], bsz, seq)
    cum = _fox_cum(small, p["fox_f_bias"], bsz, seq)
    y_fox = _fox(proj, cum, bsz, seq)
    mixed = _mix(y_ssd, y_fox, proj, p["w_proj_ssd"].astype(BF16), p["w_proj_fox"].astype(BF16))

    w_router = jnp.concatenate(
        [p["w_router_group"], p["w_router_expert"],
         jnp.zeros((D_MODEL, LANES - N_GROUPS - N_EXPERTS), F32)], axis=1).astype(F32)
    w_router_hi = w_router.astype(BF16)
    w_router_lo = (w_router - w_router_hi.astype(F32)).astype(BF16)
    w_router = jnp.concatenate([w_router_hi, w_router_lo], axis=1)
    b_router = jnp.concatenate(
        [p["b_router_group"], p["b_router_expert"],
         jnp.zeros((LANES - N_GROUPS - N_EXPERTS,), F32)]).astype(F32).reshape(1, LANES)
    h1, u2p, eidx, wts = _outproj(mixed, h, p["w_out"].astype(BF16),
                                  p["norm_moe_w"].astype(F32).reshape(1, D_MODEL), w_router, b_router)

    tk = t * TOP_K
    e_blocks = eidx[:, :TOP_K].reshape(tk // RANK_BLOCK, 1, RANK_BLOCK)
    rank, counts = _rank(e_blocks)
    counts = counts[:, 0].astype(I32)
    padded = ((counts + ROW_BLOCK - 1) // ROW_BLOCK) * ROW_BLOCK
    pend = jnp.cumsum(padded)
    pstart = pend - padded
    n_blocks = tk // ROW_BLOCK + N_EXPERTS
    dest = _dest(e_blocks, rank, jnp.broadcast_to(pstart[:, None], (N_EXPERTS, LANES)).astype(I32))
    block_start = (pstart // ROW_BLOCK).astype(I32)
    block_count = (padded // ROW_BLOCK).astype(I32)
    last_blk = jnp.where(block_count > 0, block_start + block_count - 1, -1).astype(I32)
    n_used = (pend[-1:] // ROW_BLOCK).astype(I32)
    xs = _dispatch(dest, u2p, n_blocks * ROW_BLOCK, last_blk, n_used)
    y = _experts(block_start, block_count, xs, p["w_gate_exp"], p["w_up_exp"], p["w_down_exp"])
    return dest, h1, wts, y


def kernel(x, norm_mix_w, w_in, conv_w, conv_b, dt_bias, a_log, d_skip, ssd_norm_w, fox_f_bias, w_proj_ssd,
           w_proj_fox, w_out, norm_moe_w, w_router_group, b_router_group, w_router_expert, b_router_expert,
           w_gate_exp, w_up_exp, w_down_exp, norm_final_w):
    bsz, seq, _ = x.shape
    depth = w_in.shape[0]
    assert depth == 1, "the fused final norm assumes a single layer"
    stacked = dict(norm_mix_w=norm_mix_w, w_in=w_in, conv_w=conv_w, conv_b=conv_b, dt_bias=dt_bias, a_log=a_log,
                   d_skip=d_skip, ssd_norm_w=ssd_norm_w, fox_f_bias=fox_f_bias, w_proj_ssd=w_proj_ssd,
                   w_proj_fox=w_proj_fox, w_out=w_out, norm_moe_w=norm_moe_w, w_router_group=w_router_group,
                   b_router_group=b_router_group, w_router_expert=w_router_expert,
                   b_router_expert=b_router_expert, w_gate_exp=w_gate_exp, w_up_exp=w_up_exp,
                   w_down_exp=w_down_exp)
    p = {name: v[0] for name, v in stacked.items()}
    h = x.reshape(bsz * seq, D_MODEL)
    dest, h1, wts, y = _layer(h, p, bsz, seq)
    out = _combine(dest, h1, wts, norm_final_w.astype(F32).reshape(1, D_MODEL), y)
    return out.reshape(bsz, seq, D_MODEL)
```

```python
import functools
import math

import jax
import jax.numpy as jnp
from jax import lax
from jax.experimental import pallas as pl
from jax.experimental.pallas import tpu as pltpu

F32 = jnp.float32
BF16 = jnp.bfloat16
I32 = jnp.int32
U32 = jnp.uint32

D_MODEL = 2048
SSD_HEADS = 32
SSD_HEAD_DIM = 64
SSD_DIM = SSD_HEADS * SSD_HEAD_DIM
SSD_GROUPS = 4
SSD_STATE = 128
SSD_CHUNK = 128
CONV_WIDTH = 4
CONV_DIM = SSD_DIM + 2 * SSD_GROUPS * SSD_STATE
FOX_HEADS = 16
FOX_HEAD_DIM = 128
FOX_DIM = FOX_HEADS * FOX_HEAD_DIM
N_GROUPS = 8
EXPERTS_PER_GROUP = 8
N_EXPERTS = N_GROUPS * EXPERTS_PER_GROUP
TOP_K = 2
D_EXPERT = 512
EPS = 1e-6
LOG2E = 1.4426950408889634

LANES = 128
SUBLANES = 8
VMEM_LIMIT = 52 * 1024 * 1024

COL_Z = 0
COL_Q = COL_Z + SSD_DIM
COL_K = COL_Q + FOX_DIM
COL_V = COL_K + FOX_DIM
COL_GA = COL_V + FOX_DIM
COL_GB = COL_GA + D_MODEL
COL_XBC = COL_GB + D_MODEL
PROJ_COLS = COL_XBC + CONV_DIM
SMALL_DT = 0
SMALL_DT_COPIES = 3
SMALL_F = SMALL_DT_COPIES * SSD_HEADS

ROW_BLOCK = 128
HALF = D_MODEL // 2


def _cparams(sem, vmem=VMEM_LIMIT):
    return pltpu.CompilerParams(dimension_semantics=sem, vmem_limit_bytes=vmem)


def _silu(x):
    half = 0.5 * x
    return half + half * jnp.tanh(half)


def _softplus(x):
    return jnp.maximum(x, 0.0) + jnp.log(1.0 + jnp.exp(-jnp.abs(x)))


def _log_sigmoid(x):
    return -_softplus(-x)


def _split3(x):
    hi = x.astype(BF16)
    rest = x - hi.astype(F32)
    mid = rest.astype(BF16)
    lo = (rest - mid.astype(F32)).astype(BF16)
    return hi, mid, lo


def _pack_bf16_pair(x):
    n = x.shape[1] // 2
    lo = pltpu.bitcast(x[:, :n].astype(BF16).astype(F32), U32)
    hi = pltpu.bitcast(x[:, n:].astype(BF16).astype(F32), U32)
    return (hi & jnp.uint32(0xFFFF0000)) | (lo >> 16)


def _unpack_bf16_pair(p):
    lo = pltpu.bitcast(p << 16, F32)
    hi = pltpu.bitcast(p & jnp.uint32(0xFFFF0000), F32)
    return lo, hi


def _norm_small_kernel(x_ref, nw_ref, ws_ref, u_ref, s_ref):
    x = x_ref[...]
    ms = jnp.mean(x * x, axis=-1, keepdims=True)
    u = (x * lax.rsqrt(ms + EPS) * nw_ref[...]).astype(BF16)
    u_ref[...] = u
    s_ref[...] = jnp.dot(u, ws_ref[...], preferred_element_type=F32)


def _norm_small(x2, nw, w_small, tm=512):
    t = x2.shape[0]
    return pl.pallas_call(
        _norm_small_kernel,
        name="norm_small",
        grid=(t // tm,),
        in_specs=[pl.BlockSpec((tm, D_MODEL), lambda i: (i, 0)),
                  pl.BlockSpec((1, D_MODEL), lambda i: (0, 0)),
                  pl.BlockSpec((D_MODEL, LANES), lambda i: (0, 0))],
        out_specs=[pl.BlockSpec((tm, D_MODEL), lambda i: (i, 0)),
                   pl.BlockSpec((tm, LANES), lambda i: (i, 0))],
        out_shape=[jax.ShapeDtypeStruct((t, D_MODEL), BF16),
                   jax.ShapeDtypeStruct((t, LANES), F32)],
        compiler_params=_cparams(("parallel",)),
    )(x2, nw, w_small)


IN_TN = 1024
IN_XPOSE = 256
_IN_SEGMENTS = ((COL_Z, 0, SSD_DIM),
                (COL_Q, SSD_DIM + CONV_DIM + SSD_HEADS, 3 * FOX_DIM),
                (COL_GA, SSD_DIM + CONV_DIM + SSD_HEADS + 3 * FOX_DIM + FOX_HEADS, 2 * D_MODEL),
                (COL_XBC, SSD_DIM, CONV_DIM))


def _in_proj_source_rows():
    src = [0] * (PROJ_COLS // IN_TN)
    for out0, src0, width in _IN_SEGMENTS:
        for c in range(0, width, IN_TN):
            src[(out0 + c) // IN_TN] = src0 + c
    assert all(s % SUBLANES == 0 for s in src)
    return src


def _in_proj_kernel(row_ref, u_ref, wt_hbm, o_ref, st_ref, wbf_ref, sem):
    j = pl.program_id(0)
    i = pl.program_id(1)
    k = wbf_ref.shape[0]

    def window(jj, slot):
        row0 = pl.multiple_of(row_ref[jj], SUBLANES)
        return pltpu.make_async_copy(wt_hbm.at[pl.ds(row0, IN_TN), :], st_ref.at[slot], sem.at[slot])

    @pl.when((j == 0) & (i == 0))
    def _():
        window(0, 0).start()

    @pl.when(i == 0)
    def _():
        @pl.when(j + 1 < pl.num_programs(0))
        def _():
            window(j + 1, (j + 1) % 2).start()

        slot = j % 2
        window(j, slot).wait()
        for c in range(0, k, IN_XPOSE):
            wbf_ref[c:c + IN_XPOSE, :] = jnp.transpose(st_ref[slot, :, c:c + IN_XPOSE]).astype(BF16)

    o_ref[...] = jnp.dot(u_ref[...], wbf_ref[...], preferred_element_type=F32).astype(o_ref.dtype)


def _in_proj(u, w_in_t, tm=1024):
    m, k = u.shape
    grid_spec = pltpu.PrefetchScalarGridSpec(
        num_scalar_prefetch=1,
        grid=(PROJ_COLS // IN_TN, m // tm),
        in_specs=[pl.BlockSpec((tm, k), lambda j, i, rows: (i, 0)),
                  pl.BlockSpec(memory_space=pl.ANY)],
        out_specs=pl.BlockSpec((tm, IN_TN), lambda j, i, rows: (i, j)),
        scratch_shapes=[pltpu.VMEM((2, IN_TN, k), F32),
                        pltpu.VMEM((k, IN_TN), BF16),
                        pltpu.SemaphoreType.DMA((2,))],
    )
    return pl.pallas_call(
        _in_proj_kernel,
        name="in_proj",
        grid_spec=grid_spec,
        out_shape=jax.ShapeDtypeStruct((m, PROJ_COLS), BF16),
        compiler_params=_cparams(("arbitrary", "arbitrary")),
    )(jnp.asarray(_in_proj_source_rows(), I32), u, w_in_t)


HALO = 2 * SUBLANES


def _ssd_kernel(z_ref, xbc_ref, halo_ref, small_ref, cw_ref, cb_ref, dtb_ref, aneg_ref, dexp_ref,
                nw_ref, expand_ref, shift_ref, y_ref, state_ref, ydiag_ref):
    c = pl.program_id(1)
    l = SSD_CHUNK
    n = SSD_STATE

    @pl.when(c == 0)
    def _():
        state_ref[...] = jnp.zeros_like(state_ref)

    cur = xbc_ref[...]
    halo = halo_ref[...]
    halo = jnp.where(c == 0, jnp.zeros_like(halo), halo)
    ext = jnp.concatenate([halo, cur], axis=0)
    shifted = jnp.dot(shift_ref[...], ext, preferred_element_type=F32)
    conv = cb_ref[...] + cw_ref[CONV_WIDTH - 1:CONV_WIDTH, :] * cur.astype(F32)
    for j in range(CONV_WIDTH - 1):
        conv = conv + cw_ref[j:j + 1, :] * shifted[j * l:(j + 1) * l]
    xbc = _silu(conv)
    xs = xbc[:, :SSD_DIM]
    bm = xbc[:, SSD_DIM:SSD_DIM + SSD_GROUPS * n]
    cm = xbc[:, SSD_DIM + SSD_GROUPS * n:]

    h3 = SMALL_DT_COPIES * SSD_HEADS
    dt3 = _softplus(small_ref[:, SMALL_DT:SMALL_DT + h3] + dtb_ref[...])
    adt3 = dt3 * aneg_ref[...]
    row = lax.broadcasted_iota(I32, (l, l), 0)
    col = lax.broadcasted_iota(I32, (l, l), 1)
    causal = col <= row
    tril = jnp.where(causal, 1.0, 0.0).astype(BF16)
    a_cs3 = sum(jnp.dot(tril, piece, preferred_element_type=F32) for piece in _split3(adt3))
    a_cs2 = a_cs3[:, :SSD_HEADS] * LOG2E
    a_cs2_t = jnp.transpose(a_cs2)
    a_last3 = a_cs3[l - 1:l, :]

    lane3 = lax.broadcasted_iota(I32, (l, h3), 1)

    def pieces_by_lane_group(x3):
        hi, mid, lo = _split3(x3)
        return jnp.where(lane3 < SSD_HEADS, hi, jnp.where(lane3 < 2 * SSD_HEADS, mid, lo))

    lhs3 = jnp.concatenate([pieces_by_lane_group(dt3),
                            pieces_by_lane_group(jnp.exp(a_cs3)),
                            pieces_by_lane_group(jnp.exp(a_last3 - a_cs3))], axis=0)
    expanded = jnp.dot(lhs3, expand_ref[...], preferred_element_type=F32)
    dt_x = expanded[0:l]
    decay_in = expanded[l:2 * l]
    decay_out = expanded[2 * l:3 * l]
    chunk_decay = decay_in[l - 1:l, :]
    x_dt = xs * dt_x
    x_dt_b = x_dt.astype(BF16)
    xd_b = (x_dt * decay_out).astype(BF16)

    lane = lax.broadcasted_iota(I32, (l, LANES), 1)
    lo_mask = lane < SSD_HEAD_DIM
    heads_per_group = SSD_HEADS // SSD_GROUPS
    gw = heads_per_group * SSD_HEAD_DIM

    for g in range(SSD_GROUPS):
        bg = bm[:, g * n:(g + 1) * n].astype(BF16)
        cg = cm[:, g * n:(g + 1) * n].astype(BF16)
        cb = lax.dot_general(cg, bg, (((1,), (1,)), ((), ())), preferred_element_type=F32)
        for pair in range(heads_per_group // 2):
            h0 = g * heads_per_group + 2 * pair
            lane0 = h0 * SSD_HEAD_DIM
            ms = []
            for h in (h0, h0 + 1):
                seg = a_cs2[:, h:h + 1] - a_cs2_t[h:h + 1, :]
                lmat = jnp.exp2(jnp.where(causal, seg, -jnp.inf))
                ms.append((cb * lmat).astype(BF16))
            lhs = jnp.concatenate(ms, axis=1)
            xp = x_dt_b[:, lane0:lane0 + LANES]
            zero = jnp.zeros_like(xp)
            rhs = jnp.concatenate([jnp.where(lo_mask, xp, zero), jnp.where(lo_mask, zero, xp)], axis=0)
            ydiag_ref[:, lane0:lane0 + LANES] = jnp.dot(lhs, rhs, preferred_element_type=F32)
        st = state_ref[:, g * gw:(g + 1) * gw]
        y_off = jnp.dot(cg, st.astype(BF16), preferred_element_type=F32)
        ydiag_ref[:, g * gw:(g + 1) * gw] += y_off * decay_in[:, g * gw:(g + 1) * gw]
        new = lax.dot_general(bg, xd_b[:, g * gw:(g + 1) * gw], (((0,), (0,)), ((), ())),
                              preferred_element_type=F32)
        state_ref[:, g * gw:(g + 1) * gw] = st * chunk_decay[:, g * gw:(g + 1) * gw] + new

    y = ydiag_ref[...] + dexp_ref[...] * xs
    y = y * _silu(z_ref[...].astype(F32))
    ms = jnp.mean(y * y, axis=-1, keepdims=True)
    y_ref[...] = (y * lax.rsqrt(ms + EPS) * nw_ref[...]).astype(y_ref.dtype)


def _ssd(proj, small, conv_w, conv_b, dt_bias, a_log, d_skip, norm_w, bsz, seq):
    l = SSD_CHUNK
    nc = seq // l
    t = bsz * seq
    h3 = SMALL_DT_COPIES * SSD_HEADS
    aneg = jnp.tile(-jnp.exp(a_log.astype(F32)), SMALL_DT_COPIES).reshape(1, h3)
    dtb = jnp.tile(dt_bias.astype(F32), SMALL_DT_COPIES).reshape(1, h3)
    dexp = jnp.repeat(d_skip.astype(F32), SSD_HEAD_DIM).reshape(1, SSD_DIM)
    expand = jnp.tile(jnp.repeat(jnp.eye(SSD_HEADS, dtype=BF16), SSD_HEAD_DIM, axis=1),
                      (SMALL_DT_COPIES, 1))
    out_row = jnp.arange((CONV_WIDTH - 1) * l)
    src_row = HALO + out_row % l - (CONV_WIDTH - 1) + out_row // l
    shift = (jnp.arange(HALO + l)[None, :] == src_row[:, None]).astype(BF16)
    xbc_blk = COL_XBC // CONV_DIM
    halo_per_chunk = l // HALO

    def row_map(b, c):
        return b * nc + c

    return pl.pallas_call(
        _ssd_kernel,
        name="ssd",
        grid=(bsz, nc),
        in_specs=[
            pl.BlockSpec((l, SSD_DIM), lambda b, c: (row_map(b, c), COL_Z // SSD_DIM)),
            pl.BlockSpec((l, CONV_DIM), lambda b, c: (row_map(b, c), xbc_blk)),
            pl.BlockSpec((HALO, CONV_DIM),
                         lambda b, c: (jnp.maximum(row_map(b, c) * halo_per_chunk - 1, 0), xbc_blk)),
            pl.BlockSpec((l, LANES), lambda b, c: (row_map(b, c), 0)),
            pl.BlockSpec((CONV_WIDTH, CONV_DIM), lambda b, c: (0, 0)),
            pl.BlockSpec((1, CONV_DIM), lambda b, c: (0, 0)),
            pl.BlockSpec((1, h3), lambda b, c: (0, 0)),
            pl.BlockSpec((1, h3), lambda b, c: (0, 0)),
            pl.BlockSpec((1, SSD_DIM), lambda b, c: (0, 0)),
            pl.BlockSpec((1, SSD_DIM), lambda b, c: (0, 0)),
            pl.BlockSpec((h3, SSD_DIM), lambda b, c: (0, 0)),
            pl.BlockSpec(((CONV_WIDTH - 1) * l, HALO + l), lambda b, c: (0, 0)),
        ],
        out_specs=pl.BlockSpec((l, SSD_DIM), lambda b, c: (row_map(b, c), 0)),
        out_shape=jax.ShapeDtypeStruct((t, SSD_DIM), BF16),
        scratch_shapes=[pltpu.VMEM((SSD_STATE, SSD_DIM), F32),
                        pltpu.VMEM((l, SSD_DIM), F32)],
        compiler_params=_cparams(("parallel", "arbitrary")),
    )(proj, proj, proj, small, conv_w.astype(F32), conv_b.astype(F32).reshape(1, CONV_DIM),
      dtb, aneg, dexp, norm_w.astype(F32).reshape(1, SSD_DIM), expand, shift)


CUM_ROWS = 256


def _cum_kernel(small_ref, fb_ref, cum_ref, carry_ref):
    j = pl.program_id(1)

    @pl.when(j == 0)
    def _():
        carry_ref[...] = jnp.zeros_like(carry_ref)

    lf = _log_sigmoid(small_ref[...] + fb_ref[...])
    row = lax.broadcasted_iota(I32, (CUM_ROWS, CUM_ROWS), 0)
    col = lax.broadcasted_iota(I32, (CUM_ROWS, CUM_ROWS), 1)
    tril = jnp.where(col <= row, 1.0, 0.0).astype(F32)
    cs = jnp.dot(tril, lf, preferred_element_type=F32, precision=lax.Precision.HIGHEST) + carry_ref[...]
    cum_ref[...] = cs * LOG2E
    carry_ref[...] = cs[CUM_ROWS - 1:CUM_ROWS, :]


def _fox_cum(small, fox_f_bias, bsz, seq):
    fb = jnp.zeros((1, LANES), F32).at[0, SMALL_F:SMALL_F + FOX_HEADS].set(fox_f_bias.astype(F32))
    nj = seq // CUM_ROWS
    return pl.pallas_call(
        _cum_kernel,
        name="fox_cum",
        grid=(bsz, nj),
        in_specs=[pl.BlockSpec((CUM_ROWS, LANES), lambda b, j: (b * nj + j, 0)),
                  pl.BlockSpec((1, LANES), lambda b, j: (0, 0))],
        out_specs=pl.BlockSpec((CUM_ROWS, LANES), lambda b, j: (b * nj + j, 0)),
        out_shape=jax.ShapeDtypeStruct((bsz * seq, LANES), F32),
        scratch_shapes=[pltpu.VMEM((1, LANES), F32)],
        compiler_params=_cparams(("parallel", "arbitrary")),
    )(small, fb)


FOX_HEADS_PER_STEP = 2


FOX_SLAB = 128


def _fox_kernel(q_ref, k_ref, v_ref, cq_ref, ck_ref, o_ref, sa_ref, sb_ref, p_ref, m_ref, l_ref, m_alt_ref,
                cqrep_ref, psum_ref, acc_ref, *, tq, tk):
    hp = pl.program_id(1)
    qi = pl.program_id(2)
    d = FOX_HEAD_DIM
    c2 = LOG2E / math.sqrt(d)
    lane = lax.broadcasted_iota(I32, (tq, LANES), 1)
    cq_all = cq_ref[...]

    qs, cqs = [], []
    for hh in range(FOX_HEADS_PER_STEP):
        qs.append((q_ref[:, hh * d:(hh + 1) * d].astype(F32) * c2).astype(BF16))
        head_lane = SMALL_F + hp * FOX_HEADS_PER_STEP + hh
        cq_col = jnp.sum(jnp.where(lane == head_lane, cq_all, 0.0), axis=-1, keepdims=True)
        cqs.append(jnp.broadcast_to(cq_col, (tq, LANES)))

    s_slots = (sa_ref, sb_ref)

    def scores(ki, slot):
        row0 = pl.multiple_of(ki * tk, tk)
        for hh in range(FOX_HEADS_PER_STEP):
            k = k_ref[pl.ds(row0, tk), hh * d:(hh + 1) * d]
            s = lax.dot_general(qs[hh], k, (((1,), (1,)), ((), ())), preferred_element_type=F32)
            s_slots[slot][hh] = s - ck_ref[hh, pl.ds(ki, 1), :]

    def update(ki, slot, masked):
        row0 = pl.multiple_of(ki * tk, tk)
        s_ref = s_slots[slot]
        m_bufs = (m_ref, m_alt_ref)
        n_ct = tk // LANES
        for hh in range(FOX_HEADS_PER_STEP):
            for rc in range(tq // FOX_SLAB):
                rows = slice(rc * FOX_SLAB, (rc + 1) * FOX_SLAB)
                n_vis = rc + 1 if masked else n_ct

                def slab(ct):
                    x = s_ref[hh, rows, ct * LANES:(ct + 1) * LANES]
                    if masked and ct == rc:
                        row = lax.broadcasted_iota(I32, (FOX_SLAB, LANES), 0)
                        col = lax.broadcasted_iota(I32, (FOX_SLAB, LANES), 1)
                        x = jnp.where(col <= row, x, -jnp.inf)
                    return x

                tmax = slab(0)
                for ct in range(1, n_vis):
                    tmax = jnp.maximum(tmax, slab(ct))
                row_max = jnp.max(tmax, axis=-1, keepdims=True)
                cq = cqrep_ref[hh, rows, :]
                m_new = jnp.maximum(m_bufs[slot][hh, rows, :], jnp.broadcast_to(row_max, (FOX_SLAB, LANES)) + cq)
                m_bufs[1 - slot][hh, rows, :] = m_new
                r = m_new - cq
                psum = None
                for ct in range(n_vis):
                    p = jnp.exp2(slab(ct) - r)
                    psum = p if psum is None else psum + p
                    p_ref[hh, rows, ct * LANES:(ct + 1) * LANES] = p.astype(BF16)
                for ct in range(n_vis, n_ct):
                    p_ref[hh, rows, ct * LANES:(ct + 1) * LANES] = jnp.zeros((FOX_SLAB, LANES), BF16)
                psum_ref[hh, rows, :] = psum
            m_old = m_bufs[slot][hh]
            m_cur = m_bufs[1 - slot][hh]
            l_ref[hh] = jnp.exp2(m_old - m_cur) * l_ref[hh] + psum_ref[hh]
            v = v_ref[pl.ds(row0, tk), hh * d:(hh + 1) * d]
            acc_ref[hh] = jnp.exp2(m_old - m_cur) * acc_ref[hh] + jnp.dot(p_ref[hh], v, preferred_element_type=F32)

    for hh in range(FOX_HEADS_PER_STEP):
        cqrep_ref[hh] = cqs[hh]

    m_ref[...] = jnp.full_like(m_ref, -jnp.inf)
    l_ref[...] = jnp.zeros_like(l_ref)
    acc_ref[...] = jnp.zeros_like(acc_ref)

    n_full = qi
    scores(0, 0)

    def pair(j, carry):
        b0 = 2 * j
        scores(b0 + 1, 1)
        update(b0, 0, False)
        scores(b0 + 2, 0)
        update(b0 + 1, 1, False)
        return carry

    lax.fori_loop(0, n_full // 2, pair, 0)

    @pl.when(n_full % 2 == 0)
    def _():
        update(n_full, 0, True)

    @pl.when(n_full % 2 == 1)
    def _():
        scores(n_full, 1)
        update(n_full - 1, 0, False)
        update(n_full, 1, True)

    for hh in range(FOX_HEADS_PER_STEP):
        l_fin = jnp.sum(l_ref[hh], axis=-1, keepdims=True)
        o_ref[:, hh * d:(hh + 1) * d] = (acc_ref[hh] / l_fin).astype(o_ref.dtype)


def _fox(proj, cum, bsz, seq, tq=1024):
    assert FOX_SLAB == LANES and tq % FOX_SLAB == 0 and seq % tq == 0
    tk = tq
    nq = seq // tq
    nk = seq // tk
    hps = FOX_HEADS_PER_STEP
    cum_row = cum[:, SMALL_F:SMALL_F + FOX_HEADS].reshape(bsz, seq, FOX_HEADS).transpose(0, 2, 1)
    cum_row = cum_row.reshape(bsz, FOX_HEADS, nk, tk)
    t = bsz * seq
    w = FOX_HEADS_PER_STEP * FOX_HEAD_DIM
    n_hp = FOX_HEADS // FOX_HEADS_PER_STEP
    kern = functools.partial(_fox_kernel, tq=tq, tk=tk)
    return pl.pallas_call(
        kern,
        name="fox",
        grid=(bsz, n_hp, nq),
        in_specs=[
            pl.BlockSpec((tq, w), lambda b, hp, qi: (b * nq + qi, COL_Q // w + hp)),
            pl.BlockSpec((seq, w), lambda b, hp, qi: (b, COL_K // w + hp)),
            pl.BlockSpec((seq, w), lambda b, hp, qi: (b, COL_V // w + hp)),
            pl.BlockSpec((tq, LANES), lambda b, hp, qi: (b * nq + qi, 0)),
            pl.BlockSpec((None, FOX_HEADS_PER_STEP, nk, tk), lambda b, hp, qi: (b, hp, 0, 0)),
        ],
        out_specs=pl.BlockSpec((tq, w), lambda b, hp, qi: (b * nq + qi, hp)),
        out_shape=jax.ShapeDtypeStruct((t, FOX_DIM), BF16),
        scratch_shapes=[pltpu.VMEM((hps, tq, tk), F32), pltpu.VMEM((hps, tq, tk), F32),
                        pltpu.VMEM((hps, tq, tk), BF16),
                        pltpu.VMEM((hps, tq, LANES), F32), pltpu.VMEM((hps, tq, LANES), F32),
                        pltpu.VMEM((hps, tq, LANES), F32), pltpu.VMEM((hps, tq, LANES), F32),
                        pltpu.VMEM((hps, tq, LANES), F32),
                        pltpu.VMEM((hps, tq, FOX_HEAD_DIM), F32)],
        compiler_params=_cparams(("parallel", "parallel", "arbitrary")),
    )(proj, proj, proj, cum, cum_row)


def _mix_kernel(ya_ref, yb_ref, ga_ref, gb_ref, wa_ref, wb_ref, o_ref):
    pa = jnp.dot(ya_ref[...], wa_ref[...], preferred_element_type=F32)
    pb = jnp.dot(yb_ref[...], wb_ref[...], preferred_element_type=F32)
    ga = 1.0 / (1.0 + jnp.exp(-ga_ref[...].astype(F32)))
    gb = 1.0 / (1.0 + jnp.exp(-gb_ref[...].astype(F32)))
    o_ref[...] = (ga * pa + gb * pb).astype(o_ref.dtype)


def _mix(y_a, y_b, proj, w_a, w_b, tm=512, tn=2048):
    t = y_a.shape[0]
    resident = pl.Buffered(1) if tn == D_MODEL else None
    return pl.pallas_call(
        _mix_kernel,
        name="mix",
        grid=(D_MODEL // tn, t // tm),
        in_specs=[
            pl.BlockSpec((tm, SSD_DIM), lambda j, i: (i, 0)),
            pl.BlockSpec((tm, FOX_DIM), lambda j, i: (i, 0)),
            pl.BlockSpec((tm, tn), lambda j, i: (i, COL_GA // tn + j)),
            pl.BlockSpec((tm, tn), lambda j, i: (i, COL_GB // tn + j)),
            pl.BlockSpec((SSD_DIM, tn), lambda j, i: (0, j), pipeline_mode=resident),
            pl.BlockSpec((FOX_DIM, tn), lambda j, i: (0, j), pipeline_mode=resident),
        ],
        out_specs=pl.BlockSpec((tm, tn), lambda j, i: (i, j)),
        out_shape=jax.ShapeDtypeStruct((t, D_MODEL), BF16),
        compiler_params=_cparams(("parallel", "parallel")),
    )(y_a, y_b, proj, proj, w_a, w_b)


def _outproj_kernel(m_ref, x_ref, wo_ref, nw_ref, wr_ref, br_ref, h_ref, u_ref, eidx_ref, wts_ref):
    h1 = x_ref[...] + jnp.dot(m_ref[...], wo_ref[...], preferred_element_type=F32)
    h_ref[...] = h1
    ms = jnp.mean(h1 * h1, axis=-1, keepdims=True)
    u2 = h1 * lax.rsqrt(ms + EPS) * nw_ref[...]
    u_ref[...] = _pack_bf16_pair(u2)

    u_hi = u2.astype(BF16)
    u_lo = (u2 - u_hi.astype(F32)).astype(BF16)
    hh_hl = jnp.dot(u_hi, wr_ref[...], preferred_element_type=F32)
    lh = jnp.dot(u_lo, wr_ref[:, :LANES], preferred_element_type=F32)
    logits = hh_hl[:, :LANES] + (hh_hl[:, LANES:] + lh) + br_ref[...]
    tm = logits.shape[0]
    lane = lax.broadcasted_iota(I32, (tm, LANES), 1)
    neg = -jnp.inf
    big = jnp.int32(2 * LANES)
    gl = jnp.where(lane < N_GROUPS, logits, neg)
    gmax = jnp.max(gl, axis=-1, keepdims=True)
    gsum = jnp.sum(jnp.exp(gl - gmax), axis=-1, keepdims=True)
    g_p = 1.0 / gsum
    g_idx = jnp.min(jnp.where(gl == gmax, lane, big), axis=-1, keepdims=True)
    e_of_lane = lane - N_GROUPS
    in_grp = (e_of_lane >= g_idx * EXPERTS_PER_GROUP) & (e_of_lane < (g_idx + 1) * EXPERTS_PER_GROUP)
    el = jnp.where(in_grp, logits, neg)
    m1 = jnp.max(el, axis=-1, keepdims=True)
    i1 = jnp.min(jnp.where(el == m1, lane, big), axis=-1, keepdims=True)
    el2 = jnp.where(lane == i1, neg, el)
    m2 = jnp.max(el2, axis=-1, keepdims=True)
    i2 = jnp.min(jnp.where(el2 == m2, lane, big), axis=-1, keepdims=True)
    esum = jnp.sum(jnp.exp(el - m1), axis=-1, keepdims=True)
    p1 = 1.0 / esum
    p2 = jnp.exp(m2 - m1) / esum
    w1 = g_p * (p1 / (p1 + p2))
    w2 = g_p * (p2 / (p1 + p2))
    eidx_ref[...] = jnp.where(lane == 0, i1 - N_GROUPS, jnp.where(lane == 1, i2 - N_GROUPS, 0))
    wts_ref[...] = jnp.where(lane == 0, w1, jnp.where(lane == 1, w2, 0.0))


def _outproj(mixed, x2, w_o, nw, w_router, b_router, tm=512):
    t = mixed.shape[0]
    resident = pl.Buffered(1)
    return pl.pallas_call(
        _outproj_kernel,
        name="outproj",
        grid=(t // tm,),
        in_specs=[
            pl.BlockSpec((tm, D_MODEL), lambda i: (i, 0)),
            pl.BlockSpec((tm, D_MODEL), lambda i: (i, 0)),
            pl.BlockSpec((D_MODEL, D_MODEL), lambda i: (0, 0), pipeline_mode=resident),
            pl.BlockSpec((1, D_MODEL), lambda i: (0, 0)),
            pl.BlockSpec((D_MODEL, 2 * LANES), lambda i: (0, 0), pipeline_mode=resident),
            pl.BlockSpec((1, LANES), lambda i: (0, 0)),
        ],
        out_specs=[
            pl.BlockSpec((tm, D_MODEL), lambda i: (i, 0)),
            pl.BlockSpec((tm, HALF), lambda i: (i, 0)),
            pl.BlockSpec((tm, LANES), lambda i: (i, 0)),
            pl.BlockSpec((tm, LANES), lambda i: (i, 0)),
        ],
        out_shape=[
            jax.ShapeDtypeStruct((t, D_MODEL), F32),
            jax.ShapeDtypeStruct((t, HALF), U32),
            jax.ShapeDtypeStruct((t, LANES), I32),
            jax.ShapeDtypeStruct((t, LANES), F32),
        ],
        compiler_params=_cparams(("parallel",)),
    )(mixed, x2, w_o, nw, w_router, b_router)


RANK_BLOCK = 512


def _rank_kernel(e_ref, rank_ref, cnt_ref, carry_ref):
    i = pl.program_id(0)
    r = RANK_BLOCK

    @pl.when(i == 0)
    def _():
        carry_ref[...] = jnp.zeros_like(carry_ref)

    e = e_ref[0]
    expert = lax.broadcasted_iota(I32, (N_EXPERTS, r), 0)
    onehot = jnp.where(expert == e, 1.0, 0.0).astype(F32)
    jrow = lax.broadcasted_iota(I32, (r, r), 0)
    jcol = lax.broadcasted_iota(I32, (r, r), 1)
    before = jnp.where(jrow < jcol, 1.0, 0.0).astype(BF16)
    cum = jnp.dot(onehot.astype(BF16), before, preferred_element_type=F32)
    carry = carry_ref[...]
    rank = jnp.sum(onehot * (cum + carry[:, 0:1]), axis=0, keepdims=True)
    rank_ref[0] = rank.astype(I32)
    carry = carry + jnp.sum(onehot, axis=1, keepdims=True)
    carry_ref[...] = carry
    cnt_ref[...] = carry


def _rank(e_blocks):
    nb = e_blocks.shape[0]
    return pl.pallas_call(
        _rank_kernel,
        name="rank",
        grid=(nb,),
        in_specs=[pl.BlockSpec((1, 1, RANK_BLOCK), lambda i: (i, 0, 0))],
        out_specs=[pl.BlockSpec((1, 1, RANK_BLOCK), lambda i: (i, 0, 0)),
                   pl.BlockSpec((N_EXPERTS, LANES), lambda i: (0, 0))],
        out_shape=[jax.ShapeDtypeStruct((nb, 1, RANK_BLOCK), I32),
                   jax.ShapeDtypeStruct((N_EXPERTS, LANES), F32)],
        scratch_shapes=[pltpu.VMEM((N_EXPERTS, LANES), F32)],
        compiler_params=_cparams(("arbitrary",)),
    )(e_blocks)


def _dest_kernel(e_ref, rank_ref, pstart_ref, dest_ref):
    e = e_ref[0]
    expert = lax.broadcasted_iota(I32, (N_EXPERTS, RANK_BLOCK), 0)
    start = jnp.sum(jnp.where(expert == e, pstart_ref[:, 0:1], 0), axis=0, keepdims=True)
    dest_ref[0] = rank_ref[0] + start


def _dest(e_blocks, rank, pstart):
    nb = e_blocks.shape[0]
    return pl.pallas_call(
        _dest_kernel,
        name="dest",
        grid=(nb,),
        in_specs=[pl.BlockSpec((1, 1, RANK_BLOCK), lambda i: (i, 0, 0)),
                  pl.BlockSpec((1, 1, RANK_BLOCK), lambda i: (i, 0, 0)),
                  pl.BlockSpec((N_EXPERTS, LANES), lambda i: (0, 0))],
        out_specs=pl.BlockSpec((1, 1, RANK_BLOCK), lambda i: (i, 0, 0)),
        out_shape=jax.ShapeDtypeStruct((nb, 1, RANK_BLOCK), I32),
        compiler_params=_cparams(("parallel",)),
    )(e_blocks, rank, pstart)


DMA_UNROLL = 16


DISPATCH_RING = 3


def _dispatch_kernel(last_blk_ref, nused_ref, dest_ref, u_hbm, xs_ref, ubuf_ref, zbuf_ref, lsem, rsem, zsem):
    i = pl.program_id(0)
    n_steps = pl.num_programs(0)
    n_total = xs_ref.shape[0] // ROW_BLOCK
    groups = ubuf_ref.shape[1]

    def load(blk, slot):
        return pltpu.make_async_copy(u_hbm.at[pl.ds(blk * groups, groups)], ubuf_ref.at[slot], lsem.at[slot])

    def rows_done(slot):
        return pltpu.make_async_copy(xs_ref.at[pl.ds(0, RANK_BLOCK)], xs_ref.at[pl.ds(0, RANK_BLOCK)], rsem.at[slot])

    @pl.when(i == 0)
    def _():
        load(0, 0).start()

    @pl.when(i == 0)
    def _():
        zbuf_ref[...] = jnp.zeros_like(zbuf_ref)

        def zero_block(blk):
            return pltpu.make_async_copy(zbuf_ref, xs_ref.at[pl.ds(blk * ROW_BLOCK, ROW_BLOCK)], zsem)

        def fill_expert(e, count):
            blk = last_blk_ref[e]

            @pl.when(blk >= 0)
            def _():
                zero_block(blk).start()

            return count + jnp.where(blk >= 0, 1, 0)

        n_fill = lax.fori_loop(0, N_EXPERTS, fill_expert, 0)

        def fill_tail(blk, carry):
            zero_block(blk).start()
            return carry

        lax.fori_loop(nused_ref[0], n_total, fill_tail, 0)

        def drain(k, carry):
            zero_block(0).wait()
            return carry

        lax.fori_loop(0, n_fill + n_total - nused_ref[0], drain, 0)

    slot = i % DISPATCH_RING
    load(i, slot).wait()

    @pl.when(i >= DISPATCH_RING - 1)
    def _():
        rows_done((i + 1) % DISPATCH_RING).wait()

    @pl.when(i + 1 < n_steps)
    def _():
        load(i + 1, (i + 1) % DISPATCH_RING).start()

    def issue(g, carry):
        c0 = g * DMA_UNROLL
        for j in range(DMA_UNROLL):
            dst_row = dest_ref[0, 0, c0 + j]
            pltpu.make_async_copy(ubuf_ref.at[slot, g, pl.ds(j // TOP_K, 1)], xs_ref.at[pl.ds(dst_row, 1)],
                                  rsem.at[slot]).start()
        return carry

    lax.fori_loop(0, RANK_BLOCK // DMA_UNROLL, issue, 0)

    @pl.when(i == n_steps - 1)
    def _():
        for back in range(DISPATCH_RING - 2, -1, -1):
            @pl.when(i >= back)
            def _(back=back):
                rows_done((i - back) % DISPATCH_RING).wait()


def _dispatch(dest, u2p, n_rows, last_blk, n_used):
    nb = dest.shape[0]
    groups = RANK_BLOCK // DMA_UNROLL
    grid_spec = pltpu.PrefetchScalarGridSpec(
        num_scalar_prefetch=2,
        grid=(nb,),
        in_specs=[pl.BlockSpec((1, 1, RANK_BLOCK), lambda i, lb, nu: (i, 0, 0), memory_space=pltpu.SMEM),
                  pl.BlockSpec(memory_space=pl.ANY)],
        out_specs=pl.BlockSpec(memory_space=pl.ANY),
        scratch_shapes=[pltpu.VMEM((DISPATCH_RING, groups, SUBLANES, HALF), U32),
                        pltpu.VMEM((ROW_BLOCK, HALF), U32),
                        pltpu.SemaphoreType.DMA((DISPATCH_RING,)),
                        pltpu.SemaphoreType.DMA((DISPATCH_RING,)),
                        pltpu.SemaphoreType.DMA(())],
    )
    assert DMA_UNROLL == TOP_K * SUBLANES
    return pl.pallas_call(
        _dispatch_kernel,
        name="dispatch",
        grid_spec=grid_spec,
        out_shape=jax.ShapeDtypeStruct((n_rows, HALF), U32),
        compiler_params=_cparams(("arbitrary",)),
    )(last_blk, n_used, dest, u2p.reshape(-1, SUBLANES, HALF))


ROW_DMA_PRIORITY = 1
ROW_RING = 8


def _experts_kernel(bstart_ref, nblk_ref, xs_ref, wg_ref, wu_ref, wd_ref, y_ref,
                    wgb_ref, wub_ref, wdb_ref, xbuf_ref, ybuf_ref, xsem, ysem):
    e = pl.program_id(0)
    n_e = pl.num_programs(0)
    g0 = bstart_ref[e]
    nb = nblk_ref[e]
    n_used = bstart_ref[n_e - 1] + nblk_ref[n_e - 1]
    n_total = y_ref.shape[0] // ROW_BLOCK

    def x_copy(g, slot):
        return pltpu.make_async_copy(xs_ref.at[pl.ds(g * ROW_BLOCK, ROW_BLOCK)], xbuf_ref.at[slot], xsem.at[slot])

    def y_copy(g, slot):
        return pltpu.make_async_copy(ybuf_ref.at[slot], y_ref.at[pl.ds(g * ROW_BLOCK, ROW_BLOCK)], ysem.at[slot])

    @pl.when(e == 0)
    def _():
        for g in range(ROW_RING - 1):
            @pl.when(g < n_used)
            def _(g=g):
                x_copy(g, g).start(priority=ROW_DMA_PRIORITY)

    @pl.when(nb > 0)
    def _():
        wgb_ref[...] = wg_ref[0].astype(BF16)
        wub_ref[...] = wu_ref[0].astype(BF16)
        wdb_ref[...] = wd_ref[0].astype(BF16)

    def block(j, carry):
        g = g0 + j
        slot = g % ROW_RING
        x_copy(g, slot).wait()

        @pl.when(g + ROW_RING - 1 < n_used)
        def _():
            x_copy(g + ROW_RING - 1, (g + ROW_RING - 1) % ROW_RING).start(priority=ROW_DMA_PRIORITY)

        @pl.when(g >= ROW_RING)
        def _():
            y_copy(g - ROW_RING, slot).wait()

        lo, hi = _unpack_bf16_pair(xbuf_ref[slot])
        lo = lo.astype(BF16)
        hi = hi.astype(BF16)
        gate = (jnp.dot(lo, wgb_ref[:HALF, :], preferred_element_type=F32)
                + jnp.dot(hi, wgb_ref[HALF:, :], preferred_element_type=F32))
        up = (jnp.dot(lo, wub_ref[:HALF, :], preferred_element_type=F32)
              + jnp.dot(hi, wub_ref[HALF:, :], preferred_element_type=F32))
        hdn = (_silu(gate) * up).astype(BF16)
        y = jnp.dot(hdn, wdb_ref[...], preferred_element_type=F32)
        ybuf_ref[slot] = _pack_bf16_pair(y)
        y_copy(g, slot).start(priority=ROW_DMA_PRIORITY)
        return carry

    lax.fori_loop(0, nb, block, 0)

    @pl.when(e == n_e - 1)
    def _():
        for back in range(ROW_RING, 0, -1):
            @pl.when(n_used >= back)
            def _(back=back):
                y_copy(n_used - back, (n_used - back) % ROW_RING).wait()

        ybuf_ref[0] = jnp.zeros((ROW_BLOCK, HALF), U32)

        def fill(g, carry):
            y_copy(g, 0).start()
            return carry

        def fill_done(g, carry):
            y_copy(g, 0).wait()
            return carry

        lax.fori_loop(n_used, n_total, fill, 0)
        lax.fori_loop(n_used, n_total, fill_done, 0)


def _experts(block_start, block_count, xs, w_gate, w_up, w_down):
    n_rows = xs.shape[0]

    def wmap(e, bs, bc):
        return (e, 0, 0)

    grid_spec = pltpu.PrefetchScalarGridSpec(
        num_scalar_prefetch=2,
        grid=(N_EXPERTS,),
        in_specs=[pl.BlockSpec(memory_space=pl.ANY),
                  pl.BlockSpec((1, D_MODEL, D_EXPERT), wmap),
                  pl.BlockSpec((1, D_MODEL, D_EXPERT), wmap),
                  pl.BlockSpec((1, D_EXPERT, D_MODEL), wmap)],
        out_specs=pl.BlockSpec(memory_space=pl.ANY),
        scratch_shapes=[pltpu.VMEM((D_MODEL, D_EXPERT), BF16),
                        pltpu.VMEM((D_MODEL, D_EXPERT), BF16),
                        pltpu.VMEM((D_EXPERT, D_MODEL), BF16),
                        pltpu.VMEM((ROW_RING, ROW_BLOCK, HALF), U32),
                        pltpu.VMEM((ROW_RING, ROW_BLOCK, HALF), U32),
                        pltpu.SemaphoreType.DMA((ROW_RING,)),
                        pltpu.SemaphoreType.DMA((ROW_RING,))],
    )
    return pl.pallas_call(
        _experts_kernel,
        name="experts",
        grid_spec=grid_spec,
        out_shape=jax.ShapeDtypeStruct((n_rows, HALF), U32),
        compiler_params=_cparams(("arbitrary",)),
    )(block_start, block_count, xs, w_gate, w_up, w_down)


COMBINE_TOKENS = RANK_BLOCK // TOP_K
COMBINE_PARTS = 4


def _combine_kernel(dest_ref, dest_next_ref, h_ref, wts_ref, nw_ref, y_ref, o_ref, buf_ref, sem):
    ts = COMBINE_TOKENS
    i = pl.program_id(0)
    slot = i % 2

    def gather(idx_ref, dst_slot):
        def issue(g, carry):
            c0 = g * DMA_UNROLL
            for j in range(DMA_UNROLL):
                src_row = idx_ref[0, 0, c0 + j]
                src_tile = lax.shift_right_logical(src_row, SUBLANES.bit_length() - 1)
                pltpu.make_async_copy(y_ref.at[src_tile, pl.ds(src_row & (SUBLANES - 1), 1)],
                                      buf_ref.at[dst_slot, j % TOP_K, g, pl.ds(j // TOP_K, 1)],
                                      sem.at[dst_slot]).start()
            return carry

        lax.fori_loop(0, RANK_BLOCK // DMA_UNROLL, issue, 0)

    def slot_done(s):
        for k in range(TOP_K):
            pltpu.make_async_copy(y_ref.at[pl.ds(0, ts // SUBLANES)], buf_ref.at[s, k], sem.at[s]).wait()

    @pl.when(i == 0)
    def _():
        gather(dest_ref, 0)

    slot_done(slot)

    n_parts = COMBINE_PARTS
    rows_per_part = ts // n_parts
    tiles_per_part = rows_per_part // SUBLANES
    for part in range(n_parts):
        r0 = part * rows_per_part
        t0 = part * tiles_per_part
        w = wts_ref[r0:r0 + rows_per_part, :]
        w0 = w[:, 0:1]
        w1 = w[:, 1:2]
        lo0, hi0 = _unpack_bf16_pair(buf_ref[slot, 0, t0:t0 + tiles_per_part].reshape(rows_per_part, HALF))
        lo1, hi1 = _unpack_bf16_pair(buf_ref[slot, 1, t0:t0 + tiles_per_part].reshape(rows_per_part, HALF))
        h = h_ref[r0:r0 + rows_per_part, :]
        out_lo = h[:, :HALF] + w0 * lo0 + w1 * lo1
        out_hi = h[:, HALF:] + w0 * hi0 + w1 * hi1
        ms = (jnp.sum(out_lo * out_lo, axis=-1, keepdims=True)
              + jnp.sum(out_hi * out_hi, axis=-1, keepdims=True)) * (1.0 / D_MODEL)
        inv = lax.rsqrt(ms + EPS)
        o_ref[r0:r0 + rows_per_part, :HALF] = out_lo * inv * nw_ref[:, :HALF]
        o_ref[r0:r0 + rows_per_part, HALF:] = out_hi * inv * nw_ref[:, HALF:]
        for g in range(t0, t0 + tiles_per_part):
            for j in range(DMA_UNROLL):
                src_row = dest_next_ref[0, 0, g * DMA_UNROLL + j]
                src_tile = lax.shift_right_logical(src_row, SUBLANES.bit_length() - 1)
                pltpu.make_async_copy(y_ref.at[src_tile, pl.ds(src_row & (SUBLANES - 1), 1)],
                                      buf_ref.at[1 - slot, j % TOP_K, g, pl.ds(j // TOP_K, 1)],
                                      sem.at[1 - slot]).start()

    @pl.when(i == pl.num_programs(0) - 1)
    def _():
        slot_done(1 - slot)


def _combine(dest, h1, wts, nw, y):
    t = h1.shape[0]
    ts = COMBINE_TOKENS
    n_steps = t // ts
    return pl.pallas_call(
        _combine_kernel,
        name="combine",
        grid=(n_steps,),
        in_specs=[pl.BlockSpec((1, 1, RANK_BLOCK), lambda i: (i, 0, 0), memory_space=pltpu.SMEM),
                  pl.BlockSpec((1, 1, RANK_BLOCK), lambda i: (jnp.minimum(i + 1, n_steps - 1), 0, 0),
                               memory_space=pltpu.SMEM),
                  pl.BlockSpec((ts, D_MODEL), lambda i: (i, 0)),
                  pl.BlockSpec((ts, LANES), lambda i: (i, 0)),
                  pl.BlockSpec((1, D_MODEL), lambda i: (0, 0)),
                  pl.BlockSpec(memory_space=pl.ANY)],
        out_specs=pl.BlockSpec((ts, D_MODEL), lambda i: (i, 0)),
        out_shape=jax.ShapeDtypeStruct((t, D_MODEL), F32),
        scratch_shapes=[pltpu.VMEM((2, TOP_K, ts // SUBLANES, SUBLANES, HALF), U32),
                        pltpu.SemaphoreType.DMA((2,))],
        compiler_params=_cparams(("arbitrary",)),
    )(dest, dest, h1, wts, nw, y.reshape(-1, SUBLANES, HALF))


def _narrow_w_in(w_in_t):
    dt0 = SSD_DIM + CONV_DIM
    f0 = dt0 + SSD_HEADS + 3 * FOX_DIM
    dt = w_in_t[dt0:dt0 + SSD_HEADS]
    f = w_in_t[f0:f0 + FOX_HEADS]
    pad = jnp.zeros((LANES - SMALL_F - FOX_HEADS, w_in_t.shape[1]), w_in_t.dtype)
    return jnp.concatenate([dt] * SMALL_DT_COPIES + [f, pad], axis=0).T.astype(BF16)


def _layer(h, p, bsz, seq):
    t = bsz * seq
    w_in_t = jnp.swapaxes(p["w_in"].astype(F32), 0, 1)
    u, small = _norm_small(h, p["norm_mix_w"].astype(F32).reshape(1, D_MODEL), _narrow_w_in(w_in_t))
    proj = _in_proj(u, w_in_t)
    y_ssd = _ssd(proj, small, p["conv_w"], p["conv_b"], p["dt_bias"], p["a_log"], p["d_skip"],
                 p["ssd_norm_w"], bsz, seq)
    cum = _fox_cum(small, p["fox_f_bias"], bsz, seq)
    y_fox = _fox(proj, cum, bsz, seq)
    mixed = _mix(y_ssd, y_fox, proj, p["w_proj_ssd"].astype(BF16), p["w_proj_fox"].astype(BF16))

    w_router = jnp.concatenate(
        [p["w_router_group"], p["w_router_expert"],
         jnp.zeros((D_MODEL, LANES - N_GROUPS - N_EXPERTS), F32)], axis=1).astype(F32)
    w_router_hi = w_router.astype(BF16)
    w_router_lo = (w_router - w_router_hi.astype(F32)).astype(BF16)
    w_router = jnp.concatenate([w_router_hi, w_router_lo], axis=1)
    b_router = jnp.concatenate(
        [p["b_router_group"], p["b_router_expert"],
         jnp.zeros((LANES - N_GROUPS - N_EXPERTS,), F32)]).astype(F32).reshape(1, LANES)
    h1, u2p, eidx, wts = _outproj(mixed, h, p["w_out"].astype(BF16),
                                  p["norm_moe_w"].astype(F32).reshape(1, D_MODEL), w_router, b_router)

    tk = t * TOP_K
    e_blocks = eidx[:, :TOP_K].reshape(tk // RANK_BLOCK, 1, RANK_BLOCK)
    rank, counts = _rank(e_blocks)
    counts = counts[:, 0].astype(I32)
    padded = ((counts + ROW_BLOCK - 1) // ROW_BLOCK) * ROW_BLOCK
    pend = jnp.cumsum(padded)
    pstart = pend - padded
    n_blocks = tk // ROW_BLOCK + N_EXPERTS
    dest = _dest(e_blocks, rank, jnp.broadcast_to(pstart[:, None], (N_EXPERTS, LANES)).astype(I32))
    block_start = (pstart // ROW_BLOCK).astype(I32)
    block_count = (padded // ROW_BLOCK).astype(I32)
    last_blk = jnp.where(block_count > 0, block_start + block_count - 1, -1).astype(I32)
    n_used = (pend[-1:] // ROW_BLOCK).astype(I32)
    xs = _dispatch(dest, u2p, n_blocks * ROW_BLOCK, last_blk, n_used)
    y = _experts(block_start, block_count, xs, p["w_gate_exp"], p["w_up_exp"], p["w_down_exp"])
    return dest, h1, wts, y


def kernel(x, norm_mix_w, w_in, conv_w, conv_b, dt_bias, a_log, d_skip, ssd_norm_w, fox_f_bias, w_proj_ssd,
           w_proj_fox, w_out, norm_moe_w, w_router_group, b_router_group, w_router_expert, b_router_expert,
           w_gate_exp, w_up_exp, w_down_exp, norm_final_w):
    bsz, seq, _ = x.shape
    depth = w_in.shape[0]
    assert depth == 1, "the fused final norm assumes a single layer"
    stacked = dict(norm_mix_w=norm_mix_w, w_in=w_in, conv_w=conv_w, conv_b=conv_b, dt_bias=dt_bias, a_log=a_log,
                   d_skip=d_skip, ssd_norm_w=ssd_norm_w, fox_f_bias=fox_f_bias, w_proj_ssd=w_proj_ssd,
                   w_proj_fox=w_proj_fox, w_out=w_out, norm_moe_w=norm_moe_w, w_router_group=w_router_group,
                   b_router_group=b_router_group, w_router_expert=w_router_expert,
                   b_router_expert=b_router_expert, w_gate_exp=w_gate_exp, w_up_exp=w_up_exp,
                   w_down_exp=w_down_exp)
    p = {name: v[0] for name, v in stacked.items()}
    h = x.reshape(bsz * seq, D_MODEL)
    dest, h1, wts, y = _layer(h, p, bsz, seq)
    out = _combine(dest, h1, wts, norm_final_w.astype(F32).reshape(1, D_MODEL), y)
    return out.reshape(bsz, seq, D_MODEL)
```

```python
import functools
import math

import jax
import jax.numpy as jnp
from jax import lax
from jax.experimental import pallas as pl
from jax.experimental.pallas import tpu as pltpu

F32 = jnp.float32
BF16 = jnp.bfloat16
I32 = jnp.int32
U32 = jnp.uint32

D_MODEL = 2048
SSD_HEADS = 32
SSD_HEAD_DIM = 64
SSD_DIM = SSD_HEADS * SSD_HEAD_DIM
SSD_GROUPS = 4
SSD_STATE = 128
SSD_CHUNK = 128
CONV_WIDTH = 4
CONV_DIM = SSD_DIM + 2 * SSD_GROUPS * SSD_STATE
FOX_HEADS = 16
FOX_HEAD_DIM = 128
FOX_DIM = FOX_HEADS * FOX_HEAD_DIM
N_GROUPS = 8
EXPERTS_PER_GROUP = 8
N_EXPERTS = N_GROUPS * EXPERTS_PER_GROUP
TOP_K = 2
D_EXPERT = 512
EPS = 1e-6
LOG2E = 1.4426950408889634

LANES = 128
SUBLANES = 8
VMEM_LIMIT = 52 * 1024 * 1024

COL_Z = 0
COL_Q = COL_Z + SSD_DIM
COL_K = COL_Q + FOX_DIM
COL_V = COL_K + FOX_DIM
COL_GA = COL_V + FOX_DIM
COL_GB = COL_GA + D_MODEL
COL_XBC = COL_GB + D_MODEL
PROJ_COLS = COL_XBC + CONV_DIM
SMALL_DT = 0
SMALL_DT_COPIES = 3
SMALL_F = SMALL_DT_COPIES * SSD_HEADS

ROW_BLOCK = 128
HALF = D_MODEL // 2


def _cparams(sem, vmem=VMEM_LIMIT):
    return pltpu.CompilerParams(dimension_semantics=sem, vmem_limit_bytes=vmem)


def _silu(x):
    half = 0.5 * x
    return half + half * jnp.tanh(half)


def _softplus(x):
    return jnp.maximum(x, 0.0) + jnp.log(1.0 + jnp.exp(-jnp.abs(x)))


def _log_sigmoid(x):
    return -_softplus(-x)


def _split3(x):
    hi = x.astype(BF16)
    rest = x - hi.astype(F32)
    mid = rest.astype(BF16)
    lo = (rest - mid.astype(F32)).astype(BF16)
    return hi, mid, lo


def _pack_bf16_pair(x):
    n = x.shape[1] // 2
    lo = pltpu.bitcast(x[:, :n].astype(BF16).astype(F32), U32)
    hi = pltpu.bitcast(x[:, n:].astype(BF16).astype(F32), U32)
    return (hi & jnp.uint32(0xFFFF0000)) | (lo >> 16)


def _unpack_bf16_pair(p):
    lo = pltpu.bitcast(p << 16, F32)
    hi = pltpu.bitcast(p & jnp.uint32(0xFFFF0000), F32)
    return lo, hi


def _norm_small_kernel(x_ref, nw_ref, ws_ref, fb_ref, u_ref, s_ref, cum_ref, carry_ref, *, tiles_per_seq):
    i = pl.program_id(0)
    x = x_ref[...]
    ms = jnp.mean(x * x, axis=-1, keepdims=True)
    u = (x * lax.rsqrt(ms + EPS) * nw_ref[...]).astype(BF16)
    u_ref[...] = u
    small = jnp.dot(u, ws_ref[...], preferred_element_type=F32)
    s_ref[...] = small

    @pl.when(i % tiles_per_seq == 0)
    def _():
        carry_ref[...] = jnp.zeros_like(carry_ref)

    tm = x.shape[0]
    lf = _log_sigmoid(small + fb_ref[...])
    row = lax.broadcasted_iota(I32, (tm, tm), 0)
    col = lax.broadcasted_iota(I32, (tm, tm), 1)
    tril = jnp.where(col <= row, 1.0, 0.0).astype(BF16)
    cs = sum(jnp.dot(tril, piece, preferred_element_type=F32) for piece in _split3(lf)) + carry_ref[...]
    cum_ref[...] = cs * LOG2E
    carry_ref[...] = cs[tm - 1:tm, :]


def _norm_small(x2, nw, w_small, fox_f_bias, seq, tm=512):
    t = x2.shape[0]
    fb = jnp.zeros((1, LANES), F32).at[0, SMALL_F:SMALL_F + FOX_HEADS].set(fox_f_bias.astype(F32))
    kern = functools.partial(_norm_small_kernel, tiles_per_seq=seq // tm)
    return pl.pallas_call(
        kern,
        name="norm_small",
        grid=(t // tm,),
        in_specs=[pl.BlockSpec((tm, D_MODEL), lambda i: (i, 0)),
                  pl.BlockSpec((1, D_MODEL), lambda i: (0, 0)),
                  pl.BlockSpec((D_MODEL, LANES), lambda i: (0, 0)),
                  pl.BlockSpec((1, LANES), lambda i: (0, 0))],
        out_specs=[pl.BlockSpec((tm, D_MODEL), lambda i: (i, 0)),
                   pl.BlockSpec((tm, LANES), lambda i: (i, 0)),
                   pl.BlockSpec((tm, LANES), lambda i: (i, 0))],
        out_shape=[jax.ShapeDtypeStruct((t, D_MODEL), BF16),
                   jax.ShapeDtypeStruct((t, LANES), F32),
                   jax.ShapeDtypeStruct((t, LANES), F32)],
        scratch_shapes=[pltpu.VMEM((1, LANES), F32)],
        compiler_params=_cparams(("arbitrary",)),
    )(x2, nw, w_small, fb)


IN_TN = 1024
IN_XPOSE = 256
_IN_SEGMENTS = ((COL_Z, 0, SSD_DIM),
                (COL_Q, SSD_DIM + CONV_DIM + SSD_HEADS, 3 * FOX_DIM),
                (COL_GA, SSD_DIM + CONV_DIM + SSD_HEADS + 3 * FOX_DIM + FOX_HEADS, 2 * D_MODEL),
                (COL_XBC, SSD_DIM, CONV_DIM))


def _in_proj_source_rows():
    src = [0] * (PROJ_COLS // IN_TN)
    for out0, src0, width in _IN_SEGMENTS:
        for c in range(0, width, IN_TN):
            src[(out0 + c) // IN_TN] = src0 + c
    assert all(s % SUBLANES == 0 for s in src)
    return src


def _in_proj_kernel(row_ref, u_ref, wt_hbm, o_ref, st_ref, wbf_ref, sem):
    j = pl.program_id(0)
    i = pl.program_id(1)
    k = wbf_ref.shape[0]

    def window(jj, slot):
        row0 = pl.multiple_of(row_ref[jj], SUBLANES)
        return pltpu.make_async_copy(wt_hbm.at[pl.ds(row0, IN_TN), :], st_ref.at[slot], sem.at[slot])

    @pl.when((j == 0) & (i == 0))
    def _():
        window(0, 0).start()

    @pl.when(i == 0)
    def _():
        @pl.when(j + 1 < pl.num_programs(0))
        def _():
            window(j + 1, (j + 1) % 2).start()

        slot = j % 2
        window(j, slot).wait()
        for c in range(0, k, IN_XPOSE):
            wbf_ref[c:c + IN_XPOSE, :] = jnp.transpose(st_ref[slot, :, c:c + IN_XPOSE]).astype(BF16)

    o_ref[...] = jnp.dot(u_ref[...], wbf_ref[...], preferred_element_type=F32).astype(o_ref.dtype)


def _in_proj(u, w_in_t, tm=1024):
    m, k = u.shape
    grid_spec = pltpu.PrefetchScalarGridSpec(
        num_scalar_prefetch=1,
        grid=(PROJ_COLS // IN_TN, m // tm),
        in_specs=[pl.BlockSpec((tm, k), lambda j, i, rows: (i, 0)),
                  pl.BlockSpec(memory_space=pl.ANY)],
        out_specs=pl.BlockSpec((tm, IN_TN), lambda j, i, rows: (i, j)),
        scratch_shapes=[pltpu.VMEM((2, IN_TN, k), F32),
                        pltpu.VMEM((k, IN_TN), BF16),
                        pltpu.SemaphoreType.DMA((2,))],
    )
    return pl.pallas_call(
        _in_proj_kernel,
        name="in_proj",
        grid_spec=grid_spec,
        out_shape=jax.ShapeDtypeStruct((m, PROJ_COLS), BF16),
        compiler_params=_cparams(("arbitrary", "arbitrary")),
    )(jnp.asarray(_in_proj_source_rows(), I32), u, w_in_t)


HALO = 2 * SUBLANES


def _ssd_kernel(z_ref, xbc_ref, halo_ref, small_ref, cw_ref, cb_ref, dtb_ref, aneg_ref, dexp_ref,
                nw_ref, expand_ref, shift_ref, y_ref, state_ref, ydiag_ref):
    c = pl.program_id(1)
    l = SSD_CHUNK
    n = SSD_STATE

    @pl.when(c == 0)
    def _():
        state_ref[...] = jnp.zeros_like(state_ref)

    cur = xbc_ref[...]
    halo = halo_ref[...]
    halo = jnp.where(c == 0, jnp.zeros_like(halo), halo)
    ext = jnp.concatenate([halo, cur], axis=0)
    shifted = jnp.dot(shift_ref[...], ext, preferred_element_type=F32)
    conv = cb_ref[...] + cw_ref[CONV_WIDTH - 1:CONV_WIDTH, :] * cur.astype(F32)
    for j in range(CONV_WIDTH - 1):
        conv = conv + cw_ref[j:j + 1, :] * shifted[j * l:(j + 1) * l]
    xbc = _silu(conv)
    xs = xbc[:, :SSD_DIM]
    bm = xbc[:, SSD_DIM:SSD_DIM + SSD_GROUPS * n]
    cm = xbc[:, SSD_DIM + SSD_GROUPS * n:]

    h3 = SMALL_DT_COPIES * SSD_HEADS
    dt3 = _softplus(small_ref[:, SMALL_DT:SMALL_DT + h3] + dtb_ref[...])
    adt3 = dt3 * aneg_ref[...]
    row = lax.broadcasted_iota(I32, (l, l), 0)
    col = lax.broadcasted_iota(I32, (l, l), 1)
    causal = col <= row
    tril = jnp.where(causal, 1.0, 0.0).astype(BF16)
    a_cs3 = sum(jnp.dot(tril, piece, preferred_element_type=F32) for piece in _split3(adt3))
    a_cs2 = a_cs3[:, :SSD_HEADS] * LOG2E
    a_cs2_t = jnp.transpose(a_cs2)
    a_last3 = a_cs3[l - 1:l, :]

    lane3 = lax.broadcasted_iota(I32, (l, h3), 1)

    def pieces_by_lane_group(x3):
        hi, mid, lo = _split3(x3)
        return jnp.where(lane3 < SSD_HEADS, hi, jnp.where(lane3 < 2 * SSD_HEADS, mid, lo))

    lhs3 = jnp.concatenate([pieces_by_lane_group(dt3),
                            pieces_by_lane_group(jnp.exp(a_cs3)),
                            pieces_by_lane_group(jnp.exp(a_last3 - a_cs3))], axis=0)
    expanded = jnp.dot(lhs3, expand_ref[...], preferred_element_type=F32)
    dt_x = expanded[0:l]
    decay_in = expanded[l:2 * l]
    decay_out = expanded[2 * l:3 * l]
    chunk_decay = decay_in[l - 1:l, :]
    x_dt = xs * dt_x
    x_dt_b = x_dt.astype(BF16)
    xd_b = (x_dt * decay_out).astype(BF16)

    lane = lax.broadcasted_iota(I32, (l, LANES), 1)
    lo_mask = lane < SSD_HEAD_DIM
    heads_per_group = SSD_HEADS // SSD_GROUPS
    gw = heads_per_group * SSD_HEAD_DIM

    for g in range(SSD_GROUPS):
        bg = bm[:, g * n:(g + 1) * n].astype(BF16)
        cg = cm[:, g * n:(g + 1) * n].astype(BF16)
        cb = lax.dot_general(cg, bg, (((1,), (1,)), ((), ())), preferred_element_type=F32)
        for pair in range(heads_per_group // 2):
            h0 = g * heads_per_group + 2 * pair
            lane0 = h0 * SSD_HEAD_DIM
            ms = []
            for h in (h0, h0 + 1):
                seg = a_cs2[:, h:h + 1] - a_cs2_t[h:h + 1, :]
                lmat = jnp.exp2(jnp.where(causal, seg, -jnp.inf))
                ms.append((cb * lmat).astype(BF16))
            lhs = jnp.concatenate(ms, axis=1)
            xp = x_dt_b[:, lane0:lane0 + LANES]
            zero = jnp.zeros_like(xp)
            rhs = jnp.concatenate([jnp.where(lo_mask, xp, zero), jnp.where(lo_mask, zero, xp)], axis=0)
            ydiag_ref[:, lane0:lane0 + LANES] = jnp.dot(lhs, rhs, preferred_element_type=F32)
        st = state_ref[:, g * gw:(g + 1) * gw]
        y_off = jnp.dot(cg, st.astype(BF16), preferred_element_type=F32)
        ydiag_ref[:, g * gw:(g + 1) * gw] += y_off * decay_in[:, g * gw:(g + 1) * gw]
        new = lax.dot_general(bg, xd_b[:, g * gw:(g + 1) * gw], (((0,), (0,)), ((), ())),
                              preferred_element_type=F32)
        state_ref[:, g * gw:(g + 1) * gw] = st * chunk_decay[:, g * gw:(g + 1) * gw] + new

    y = ydiag_ref[...] + dexp_ref[...] * xs
    y = y * _silu(z_ref[...].astype(F32))
    ms = jnp.mean(y * y, axis=-1, keepdims=True)
    y_ref[...] = (y * lax.rsqrt(ms + EPS) * nw_ref[...]).astype(y_ref.dtype)


def _ssd(proj, small, conv_w, conv_b, dt_bias, a_log, d_skip, norm_w, bsz, seq):
    l = SSD_CHUNK
    nc = seq // l
    t = bsz * seq
    h3 = SMALL_DT_COPIES * SSD_HEADS
    aneg = jnp.tile(-jnp.exp(a_log.astype(F32)), SMALL_DT_COPIES).reshape(1, h3)
    dtb = jnp.tile(dt_bias.astype(F32), SMALL_DT_COPIES).reshape(1, h3)
    dexp = jnp.repeat(d_skip.astype(F32), SSD_HEAD_DIM).reshape(1, SSD_DIM)
    expand = jnp.tile(jnp.repeat(jnp.eye(SSD_HEADS, dtype=BF16), SSD_HEAD_DIM, axis=1),
                      (SMALL_DT_COPIES, 1))
    out_row = jnp.arange((CONV_WIDTH - 1) * l)
    src_row = HALO + out_row % l - (CONV_WIDTH - 1) + out_row // l
    shift = (jnp.arange(HALO + l)[None, :] == src_row[:, None]).astype(BF16)
    xbc_blk = COL_XBC // CONV_DIM
    halo_per_chunk = l // HALO

    def row_map(b, c):
        return b * nc + c

    return pl.pallas_call(
        _ssd_kernel,
        name="ssd",
        grid=(bsz, nc),
        in_specs=[
            pl.BlockSpec((l, SSD_DIM), lambda b, c: (row_map(b, c), COL_Z // SSD_DIM)),
            pl.BlockSpec((l, CONV_DIM), lambda b, c: (row_map(b, c), xbc_blk)),
            pl.BlockSpec((HALO, CONV_DIM),
                         lambda b, c: (jnp.maximum(row_map(b, c) * halo_per_chunk - 1, 0), xbc_blk)),
            pl.BlockSpec((l, LANES), lambda b, c: (row_map(b, c), 0)),
            pl.BlockSpec((CONV_WIDTH, CONV_DIM), lambda b, c: (0, 0)),
            pl.BlockSpec((1, CONV_DIM), lambda b, c: (0, 0)),
            pl.BlockSpec((1, h3), lambda b, c: (0, 0)),
            pl.BlockSpec((1, h3), lambda b, c: (0, 0)),
            pl.BlockSpec((1, SSD_DIM), lambda b, c: (0, 0)),
            pl.BlockSpec((1, SSD_DIM), lambda b, c: (0, 0)),
            pl.BlockSpec((h3, SSD_DIM), lambda b, c: (0, 0)),
            pl.BlockSpec(((CONV_WIDTH - 1) * l, HALO + l), lambda b, c: (0, 0)),
        ],
        out_specs=pl.BlockSpec((l, SSD_DIM), lambda b, c: (row_map(b, c), 0)),
        out_shape=jax.ShapeDtypeStruct((t, SSD_DIM), BF16),
        scratch_shapes=[pltpu.VMEM((SSD_STATE, SSD_DIM), F32),
                        pltpu.VMEM((l, SSD_DIM), F32)],
        compiler_params=_cparams(("parallel", "arbitrary")),
    )(proj, proj, proj, small, conv_w.astype(F32), conv_b.astype(F32).reshape(1, CONV_DIM),
      dtb, aneg, dexp, norm_w.astype(F32).reshape(1, SSD_DIM), expand, shift)


FOX_HEADS_PER_STEP = 2


FOX_SLAB = 128


def _fox_kernel(q_ref, k_ref, v_ref, cq_ref, ck_ref, o_ref, sa_ref, sb_ref, p_ref, m_ref, l_ref, m_alt_ref,
                cqrep_ref, psum_ref, acc_ref, *, tq, tk):
    hp = pl.program_id(1)
    qi = pl.program_id(2)
    d = FOX_HEAD_DIM
    c2 = LOG2E / math.sqrt(d)
    lane = lax.broadcasted_iota(I32, (tq, LANES), 1)
    cq_all = cq_ref[...]

    qs, cqs = [], []
    for hh in range(FOX_HEADS_PER_STEP):
        qs.append((q_ref[:, hh * d:(hh + 1) * d].astype(F32) * c2).astype(BF16))
        head_lane = SMALL_F + hp * FOX_HEADS_PER_STEP + hh
        cq_col = jnp.sum(jnp.where(lane == head_lane, cq_all, 0.0), axis=-1, keepdims=True)
        cqs.append(jnp.broadcast_to(cq_col, (tq, LANES)))

    s_slots = (sa_ref, sb_ref)

    def scores(ki, slot):
        row0 = pl.multiple_of(ki * tk, tk)
        for hh in range(FOX_HEADS_PER_STEP):
            k = k_ref[pl.ds(row0, tk), hh * d:(hh + 1) * d]
            s = lax.dot_general(qs[hh], k, (((1,), (1,)), ((), ())), preferred_element_type=F32)
            s_slots[slot][hh] = s - ck_ref[hh, pl.ds(ki, 1), :]

    def update(ki, slot, masked):
        row0 = pl.multiple_of(ki * tk, tk)
        s_ref = s_slots[slot]
        m_bufs = (m_ref, m_alt_ref)
        n_ct = tk // LANES
        for hh in range(FOX_HEADS_PER_STEP):
            for rc in range(tq // FOX_SLAB):
                rows = slice(rc * FOX_SLAB, (rc + 1) * FOX_SLAB)
                n_vis = rc + 1 if masked else n_ct

                def slab(ct):
                    x = s_ref[hh, rows, ct * LANES:(ct + 1) * LANES]
                    if masked and ct == rc:
                        row = lax.broadcasted_iota(I32, (FOX_SLAB, LANES), 0)
                        col = lax.broadcasted_iota(I32, (FOX_SLAB, LANES), 1)
                        x = jnp.where(col <= row, x, -jnp.inf)
                    return x

                tmax = slab(0)
                for ct in range(1, n_vis):
                    tmax = jnp.maximum(tmax, slab(ct))
                row_max = jnp.max(tmax, axis=-1, keepdims=True)
                cq = cqrep_ref[hh, rows, :]
                m_new = jnp.maximum(m_bufs[slot][hh, rows, :], jnp.broadcast_to(row_max, (FOX_SLAB, LANES)) + cq)
                m_bufs[1 - slot][hh, rows, :] = m_new
                r = m_new - cq
                psum = None
                for ct in range(n_vis):
                    p = jnp.exp2(slab(ct) - r)
                    psum = p if psum is None else psum + p
                    p_ref[hh, rows, ct * LANES:(ct + 1) * LANES] = p.astype(BF16)
                for ct in range(n_vis, n_ct):
                    p_ref[hh, rows, ct * LANES:(ct + 1) * LANES] = jnp.zeros((FOX_SLAB, LANES), BF16)
                psum_ref[hh, rows, :] = psum
            m_old = m_bufs[slot][hh]
            m_cur = m_bufs[1 - slot][hh]
            l_ref[hh] = jnp.exp2(m_old - m_cur) * l_ref[hh] + psum_ref[hh]
            v = v_ref[pl.ds(row0, tk), hh * d:(hh + 1) * d]
            acc_ref[hh] = jnp.exp2(m_old - m_cur) * acc_ref[hh] + jnp.dot(p_ref[hh], v, preferred_element_type=F32)

    for hh in range(FOX_HEADS_PER_STEP):
        cqrep_ref[hh] = cqs[hh]

    m_ref[...] = jnp.full_like(m_ref, -jnp.inf)
    l_ref[...] = jnp.zeros_like(l_ref)
    acc_ref[...] = jnp.zeros_like(acc_ref)

    n_full = qi
    scores(0, 0)

    def pair(j, carry):
        b0 = 2 * j
        scores(b0 + 1, 1)
        update(b0, 0, False)
        scores(b0 + 2, 0)
        update(b0 + 1, 1, False)
        return carry

    lax.fori_loop(0, n_full // 2, pair, 0)

    @pl.when(n_full % 2 == 0)
    def _():
        update(n_full, 0, True)

    @pl.when(n_full % 2 == 1)
    def _():
        scores(n_full, 1)
        update(n_full - 1, 0, False)
        update(n_full, 1, True)

    for hh in range(FOX_HEADS_PER_STEP):
        l_fin = jnp.sum(l_ref[hh], axis=-1, keepdims=True)
        o_ref[:, hh * d:(hh + 1) * d] = (acc_ref[hh] / l_fin).astype(o_ref.dtype)


def _fox(proj, cum, bsz, seq, tq=1024):
    assert FOX_SLAB == LANES and tq % FOX_SLAB == 0 and seq % tq == 0
    tk = tq
    nq = seq // tq
    nk = seq // tk
    hps = FOX_HEADS_PER_STEP
    cum_row = cum[:, SMALL_F:SMALL_F + FOX_HEADS].reshape(bsz, seq, FOX_HEADS).transpose(0, 2, 1)
    cum_row = cum_row.reshape(bsz, FOX_HEADS, nk, tk)
    t = bsz * seq
    w = FOX_HEADS_PER_STEP * FOX_HEAD_DIM
    n_hp = FOX_HEADS // FOX_HEADS_PER_STEP
    kern = functools.partial(_fox_kernel, tq=tq, tk=tk)
    return pl.pallas_call(
        kern,
        name="fox",
        grid=(bsz, n_hp, nq),
        in_specs=[
            pl.BlockSpec((tq, w), lambda b, hp, qi: (b * nq + qi, COL_Q // w + hp)),
            pl.BlockSpec((seq, w), lambda b, hp, qi: (b, COL_K // w + hp)),
            pl.BlockSpec((seq, w), lambda b, hp, qi: (b, COL_V // w + hp)),
            pl.BlockSpec((tq, LANES), lambda b, hp, qi: (b * nq + qi, 0)),
            pl.BlockSpec((None, FOX_HEADS_PER_STEP, nk, tk), lambda b, hp, qi: (b, hp, 0, 0)),
        ],
        out_specs=pl.BlockSpec((tq, w), lambda b, hp, qi: (b * nq + qi, hp)),
        out_shape=jax.ShapeDtypeStruct((t, FOX_DIM), BF16),
        scratch_shapes=[pltpu.VMEM((hps, tq, tk), F32), pltpu.VMEM((hps, tq, tk), F32),
                        pltpu.VMEM((hps, tq, tk), BF16),
                        pltpu.VMEM((hps, tq, LANES), F32), pltpu.VMEM((hps, tq, LANES), F32),
                        pltpu.VMEM((hps, tq, LANES), F32), pltpu.VMEM((hps, tq, LANES), F32),
                        pltpu.VMEM((hps, tq, LANES), F32),
                        pltpu.VMEM((hps, tq, FOX_HEAD_DIM), F32)],
        compiler_params=_cparams(("parallel", "parallel", "arbitrary")),
    )(proj, proj, proj, cum, cum_row)


def _mix_kernel(ya_ref, yb_ref, ga_ref, gb_ref, wa_ref, wb_ref, o_ref):
    pa = jnp.dot(ya_ref[...], wa_ref[...], preferred_element_type=F32)
    pb = jnp.dot(yb_ref[...], wb_ref[...], preferred_element_type=F32)
    ga = 1.0 / (1.0 + jnp.exp(-ga_ref[...].astype(F32)))
    gb = 1.0 / (1.0 + jnp.exp(-gb_ref[...].astype(F32)))
    o_ref[...] = (ga * pa + gb * pb).astype(o_ref.dtype)


def _mix(y_a, y_b, proj, w_a, w_b, tm=512, tn=2048):
    t = y_a.shape[0]
    resident = pl.Buffered(1) if tn == D_MODEL else None
    return pl.pallas_call(
        _mix_kernel,
        name="mix",
        grid=(D_MODEL // tn, t // tm),
        in_specs=[
            pl.BlockSpec((tm, SSD_DIM), lambda j, i: (i, 0)),
            pl.BlockSpec((tm, FOX_DIM), lambda j, i: (i, 0)),
            pl.BlockSpec((tm, tn), lambda j, i: (i, COL_GA // tn + j)),
            pl.BlockSpec((tm, tn), lambda j, i: (i, COL_GB // tn + j)),
            pl.BlockSpec((SSD_DIM, tn), lambda j, i: (0, j), pipeline_mode=resident),
            pl.BlockSpec((FOX_DIM, tn), lambda j, i: (0, j), pipeline_mode=resident),
        ],
        out_specs=pl.BlockSpec((tm, tn), lambda j, i: (i, j)),
        out_shape=jax.ShapeDtypeStruct((t, D_MODEL), BF16),
        compiler_params=_cparams(("parallel", "parallel")),
    )(y_a, y_b, proj, proj, w_a, w_b)


def _outproj_kernel(m_ref, x_ref, wo_ref, nw_ref, wr_ref, br_ref, h_ref, u_ref, eidx_ref, wts_ref):
    h1 = x_ref[...] + jnp.dot(m_ref[...], wo_ref[...], preferred_element_type=F32)
    h_ref[...] = h1
    ms = jnp.mean(h1 * h1, axis=-1, keepdims=True)
    u2 = h1 * lax.rsqrt(ms + EPS) * nw_ref[...]
    u_ref[...] = _pack_bf16_pair(u2)

    u_hi = u2.astype(BF16)
    u_lo = (u2 - u_hi.astype(F32)).astype(BF16)
    hh_hl = jnp.dot(u_hi, wr_ref[...], preferred_element_type=F32)
    lh = jnp.dot(u_lo, wr_ref[:, :LANES], preferred_element_type=F32)
    logits = hh_hl[:, :LANES] + (hh_hl[:, LANES:] + lh) + br_ref[...]
    tm = logits.shape[0]
    lane = lax.broadcasted_iota(I32, (tm, LANES), 1)
    neg = -jnp.inf
    big = jnp.int32(2 * LANES)
    gl = jnp.where(lane < N_GROUPS, logits, neg)
    gmax = jnp.max(gl, axis=-1, keepdims=True)
    gsum = jnp.sum(jnp.exp(gl - gmax), axis=-1, keepdims=True)
    g_p = 1.0 / gsum
    g_idx = jnp.min(jnp.where(gl == gmax, lane, big), axis=-1, keepdims=True)
    e_of_lane = lane - N_GROUPS
    in_grp = (e_of_lane >= g_idx * EXPERTS_PER_GROUP) & (e_of_lane < (g_idx + 1) * EXPERTS_PER_GROUP)
    el = jnp.where(in_grp, logits, neg)
    m1 = jnp.max(el, axis=-1, keepdims=True)
    i1 = jnp.min(jnp.where(el == m1, lane, big), axis=-1, keepdims=True)
    el2 = jnp.where(lane == i1, neg, el)
    m2 = jnp.max(el2, axis=-1, keepdims=True)
    i2 = jnp.min(jnp.where(el2 == m2, lane, big), axis=-1, keepdims=True)
    esum = jnp.sum(jnp.exp(el - m1), axis=-1, keepdims=True)
    p1 = 1.0 / esum
    p2 = jnp.exp(m2 - m1) / esum
    w1 = g_p * (p1 / (p1 + p2))
    w2 = g_p * (p2 / (p1 + p2))
    eidx_ref[...] = jnp.where(lane == 0, i1 - N_GROUPS, jnp.where(lane == 1, i2 - N_GROUPS, 0))
    wts_ref[...] = jnp.where(lane == 0, w1, jnp.where(lane == 1, w2, 0.0))


def _outproj(mixed, x2, w_o, nw, w_router, b_router, tm=512):
    t = mixed.shape[0]
    resident = pl.Buffered(1)
    return pl.pallas_call(
        _outproj_kernel,
        name="outproj",
        grid=(t // tm,),
        in_specs=[
            pl.BlockSpec((tm, D_MODEL), lambda i: (i, 0)),
            pl.BlockSpec((tm, D_MODEL), lambda i: (i, 0)),
            pl.BlockSpec((D_MODEL, D_MODEL), lambda i: (0, 0), pipeline_mode=resident),
            pl.BlockSpec((1, D_MODEL), lambda i: (0, 0)),
            pl.BlockSpec((D_MODEL, 2 * LANES), lambda i: (0, 0), pipeline_mode=resident),
            pl.BlockSpec((1, LANES), lambda i: (0, 0)),
        ],
        out_specs=[
            pl.BlockSpec((tm, D_MODEL), lambda i: (i, 0)),
            pl.BlockSpec((tm, HALF), lambda i: (i, 0)),
            pl.BlockSpec((tm, LANES), lambda i: (i, 0)),
            pl.BlockSpec((tm, LANES), lambda i: (i, 0)),
        ],
        out_shape=[
            jax.ShapeDtypeStruct((t, D_MODEL), F32),
            jax.ShapeDtypeStruct((t, HALF), U32),
            jax.ShapeDtypeStruct((t, LANES), I32),
            jax.ShapeDtypeStruct((t, LANES), F32),
        ],
        compiler_params=_cparams(("parallel",)),
    )(mixed, x2, w_o, nw, w_router, b_router)


RANK_BLOCK = 512


def _rank_kernel(e_ref, rank_ref, cnt_ref, carry_ref):
    i = pl.program_id(0)
    r = RANK_BLOCK

    @pl.when(i == 0)
    def _():
        carry_ref[...] = jnp.zeros_like(carry_ref)

    e = e_ref[0]
    expert = lax.broadcasted_iota(I32, (N_EXPERTS, r), 0)
    onehot = jnp.where(expert == e, 1.0, 0.0).astype(F32)
    jrow = lax.broadcasted_iota(I32, (r, r), 0)
    jcol = lax.broadcasted_iota(I32, (r, r), 1)
    before = jnp.where(jrow < jcol, 1.0, 0.0).astype(BF16)
    cum = jnp.dot(onehot.astype(BF16), before, preferred_element_type=F32)
    carry = carry_ref[...]
    rank = jnp.sum(onehot * (cum + carry[:, 0:1]), axis=0, keepdims=True)
    rank_ref[0] = rank.astype(I32)
    carry = carry + jnp.sum(onehot, axis=1, keepdims=True)
    carry_ref[...] = carry
    cnt_ref[...] = carry


def _rank(e_blocks):
    nb = e_blocks.shape[0]
    return pl.pallas_call(
        _rank_kernel,
        name="rank",
        grid=(nb,),
        in_specs=[pl.BlockSpec((1, 1, RANK_BLOCK), lambda i: (i, 0, 0))],
        out_specs=[pl.BlockSpec((1, 1, RANK_BLOCK), lambda i: (i, 0, 0)),
                   pl.BlockSpec((N_EXPERTS, LANES), lambda i: (0, 0))],
        out_shape=[jax.ShapeDtypeStruct((nb, 1, RANK_BLOCK), I32),
                   jax.ShapeDtypeStruct((N_EXPERTS, LANES), F32)],
        scratch_shapes=[pltpu.VMEM((N_EXPERTS, LANES), F32)],
        compiler_params=_cparams(("arbitrary",)),
    )(e_blocks)


def _dest_kernel(e_ref, rank_ref, pstart_ref, dest_ref):
    e = e_ref[0]
    expert = lax.broadcasted_iota(I32, (N_EXPERTS, RANK_BLOCK), 0)
    start = jnp.sum(jnp.where(expert == e, pstart_ref[:, 0:1], 0), axis=0, keepdims=True)
    dest_ref[0] = rank_ref[0] + start


def _dest(e_blocks, rank, pstart):
    nb = e_blocks.shape[0]
    return pl.pallas_call(
        _dest_kernel,
        name="dest",
        grid=(nb,),
        in_specs=[pl.BlockSpec((1, 1, RANK_BLOCK), lambda i: (i, 0, 0)),
                  pl.BlockSpec((1, 1, RANK_BLOCK), lambda i: (i, 0, 0)),
                  pl.BlockSpec((N_EXPERTS, LANES), lambda i: (0, 0))],
        out_specs=pl.BlockSpec((1, 1, RANK_BLOCK), lambda i: (i, 0, 0)),
        out_shape=jax.ShapeDtypeStruct((nb, 1, RANK_BLOCK), I32),
        compiler_params=_cparams(("parallel",)),
    )(e_blocks, rank, pstart)


DMA_UNROLL = 16


DISPATCH_RING = 3


def _dispatch_kernel(last_blk_ref, nused_ref, dest_ref, u_hbm, xs_ref, ubuf_ref, zbuf_ref, lsem, rsem, zsem):
    i = pl.program_id(0)
    n_steps = pl.num_programs(0)
    n_total = xs_ref.shape[0] // ROW_BLOCK
    groups = ubuf_ref.shape[1]

    def load(blk, slot):
        return pltpu.make_async_copy(u_hbm.at[pl.ds(blk * groups, groups)], ubuf_ref.at[slot], lsem.at[slot])

    def rows_done(slot):
        return pltpu.make_async_copy(xs_ref.at[pl.ds(0, RANK_BLOCK)], xs_ref.at[pl.ds(0, RANK_BLOCK)], rsem.at[slot])

    @pl.when(i == 0)
    def _():
        load(0, 0).start()

    @pl.when(i == 0)
    def _():
        zbuf_ref[...] = jnp.zeros_like(zbuf_ref)

        def zero_block(blk):
            return pltpu.make_async_copy(zbuf_ref, xs_ref.at[pl.ds(blk * ROW_BLOCK, ROW_BLOCK)], zsem)

        def fill_expert(e, count):
            blk = last_blk_ref[e]

            @pl.when(blk >= 0)
            def _():
                zero_block(blk).start()

            return count + jnp.where(blk >= 0, 1, 0)

        n_fill = lax.fori_loop(0, N_EXPERTS, fill_expert, 0)

        def fill_tail(blk, carry):
            zero_block(blk).start()
            return carry

        lax.fori_loop(nused_ref[0], n_total, fill_tail, 0)

        def drain(k, carry):
            zero_block(0).wait()
            return carry

        lax.fori_loop(0, n_fill + n_total - nused_ref[0], drain, 0)

    slot = i % DISPATCH_RING
    load(i, slot).wait()

    @pl.when(i >= DISPATCH_RING - 1)
    def _():
        rows_done((i + 1) % DISPATCH_RING).wait()

    @pl.when(i + 1 < n_steps)
    def _():
        load(i + 1, (i + 1) % DISPATCH_RING).start()

    def issue(g, carry):
        c0 = g * DMA_UNROLL
        for j in range(DMA_UNROLL):
            dst_row = dest_ref[0, 0, c0 + j]
            pltpu.make_async_copy(ubuf_ref.at[slot, g, pl.ds(j // TOP_K, 1)], xs_ref.at[pl.ds(dst_row, 1)],
                                  rsem.at[slot]).start()
        return carry

    lax.fori_loop(0, RANK_BLOCK // DMA_UNROLL, issue, 0)

    @pl.when(i == n_steps - 1)
    def _():
        for back in range(DISPATCH_RING - 2, -1, -1):
            @pl.when(i >= back)
            def _(back=back):
                rows_done((i - back) % DISPATCH_RING).wait()


def _dispatch(dest, u2p, n_rows, last_blk, n_used):
    nb = dest.shape[0]
    groups = RANK_BLOCK // DMA_UNROLL
    grid_spec = pltpu.PrefetchScalarGridSpec(
        num_scalar_prefetch=2,
        grid=(nb,),
        in_specs=[pl.BlockSpec((1, 1, RANK_BLOCK), lambda i, lb, nu: (i, 0, 0), memory_space=pltpu.SMEM),
                  pl.BlockSpec(memory_space=pl.ANY)],
        out_specs=pl.BlockSpec(memory_space=pl.ANY),
        scratch_shapes=[pltpu.VMEM((DISPATCH_RING, groups, SUBLANES, HALF), U32),
                        pltpu.VMEM((ROW_BLOCK, HALF), U32),
                        pltpu.SemaphoreType.DMA((DISPATCH_RING,)),
                        pltpu.SemaphoreType.DMA((DISPATCH_RING,)),
                        pltpu.SemaphoreType.DMA(())],
    )
    assert DMA_UNROLL == TOP_K * SUBLANES
    return pl.pallas_call(
        _dispatch_kernel,
        name="dispatch",
        grid_spec=grid_spec,
        out_shape=jax.ShapeDtypeStruct((n_rows, HALF), U32),
        compiler_params=_cparams(("arbitrary",)),
    )(last_blk, n_used, dest, u2p.reshape(-1, SUBLANES, HALF))


ROW_DMA_PRIORITY = 1
ROW_RING = 8


def _experts_kernel(bstart_ref, nblk_ref, xs_ref, wg_ref, wu_ref, wd_ref, y_ref,
                    wgb_ref, wub_ref, wdb_ref, xbuf_ref, ybuf_ref, xsem, ysem):
    e = pl.program_id(0)
    n_e = pl.num_programs(0)
    g0 = bstart_ref[e]
    nb = nblk_ref[e]
    n_used = bstart_ref[n_e - 1] + nblk_ref[n_e - 1]
    n_total = y_ref.shape[0] // ROW_BLOCK

    def x_copy(g, slot):
        return pltpu.make_async_copy(xs_ref.at[pl.ds(g * ROW_BLOCK, ROW_BLOCK)], xbuf_ref.at[slot], xsem.at[slot])

    def y_copy(g, slot):
        return pltpu.make_async_copy(ybuf_ref.at[slot], y_ref.at[pl.ds(g * ROW_BLOCK, ROW_BLOCK)], ysem.at[slot])

    @pl.when(e == 0)
    def _():
        for g in range(ROW_RING - 1):
            @pl.when(g < n_used)
            def _(g=g):
                x_copy(g, g).start(priority=ROW_DMA_PRIORITY)

    @pl.when(nb > 0)
    def _():
        wgb_ref[...] = wg_ref[0].astype(BF16)
        wub_ref[...] = wu_ref[0].astype(BF16)
        wdb_ref[...] = wd_ref[0].astype(BF16)

    def block(j, carry):
        g = g0 + j
        slot = g % ROW_RING
        x_copy(g, slot).wait()

        @pl.when(g + ROW_RING - 1 < n_used)
        def _():
            x_copy(g + ROW_RING - 1, (g + ROW_RING - 1) % ROW_RING).start(priority=ROW_DMA_PRIORITY)

        @pl.when(g >= ROW_RING)
        def _():
            y_copy(g - ROW_RING, slot).wait()

        lo, hi = _unpack_bf16_pair(xbuf_ref[slot])
        lo = lo.astype(BF16)
        hi = hi.astype(BF16)
        gate = (jnp.dot(lo, wgb_ref[:HALF, :], preferred_element_type=F32)
                + jnp.dot(hi, wgb_ref[HALF:, :], preferred_element_type=F32))
        up = (jnp.dot(lo, wub_ref[:HALF, :], preferred_element_type=F32)
              + jnp.dot(hi, wub_ref[HALF:, :], preferred_element_type=F32))
        hdn = (_silu(gate) * up).astype(BF16)
        y = jnp.dot(hdn, wdb_ref[...], preferred_element_type=F32)
        ybuf_ref[slot] = _pack_bf16_pair(y)
        y_copy(g, slot).start(priority=ROW_DMA_PRIORITY)
        return carry

    lax.fori_loop(0, nb, block, 0)

    @pl.when(e == n_e - 1)
    def _():
        for back in range(ROW_RING, 0, -1):
            @pl.when(n_used >= back)
            def _(back=back):
                y_copy(n_used - back, (n_used - back) % ROW_RING).wait()

        ybuf_ref[0] = jnp.zeros((ROW_BLOCK, HALF), U32)

        def fill(g, carry):
            y_copy(g, 0).start()
            return carry

        def fill_done(g, carry):
            y_copy(g, 0).wait()
            return carry

        lax.fori_loop(n_used, n_total, fill, 0)
        lax.fori_loop(n_used, n_total, fill_done, 0)


def _experts(block_start, block_count, xs, w_gate, w_up, w_down):
    n_rows = xs.shape[0]

    def wmap(e, bs, bc):
        return (e, 0, 0)

    grid_spec = pltpu.PrefetchScalarGridSpec(
        num_scalar_prefetch=2,
        grid=(N_EXPERTS,),
        in_specs=[pl.BlockSpec(memory_space=pl.ANY),
                  pl.BlockSpec((1, D_MODEL, D_EXPERT), wmap),
                  pl.BlockSpec((1, D_MODEL, D_EXPERT), wmap),
                  pl.BlockSpec((1, D_EXPERT, D_MODEL), wmap)],
        out_specs=pl.BlockSpec(memory_space=pl.ANY),
        scratch_shapes=[pltpu.VMEM((D_MODEL, D_EXPERT), BF16),
                        pltpu.VMEM((D_MODEL, D_EXPERT), BF16),
                        pltpu.VMEM((D_EXPERT, D_MODEL), BF16),
                        pltpu.VMEM((ROW_RING, ROW_BLOCK, HALF), U32),
                        pltpu.VMEM((ROW_RING, ROW_BLOCK, HALF), U32),
                        pltpu.SemaphoreType.DMA((ROW_RING,)),
                        pltpu.SemaphoreType.DMA((ROW_RING,))],
    )
    return pl.pallas_call(
        _experts_kernel,
        name="experts",
        grid_spec=grid_spec,
        out_shape=jax.ShapeDtypeStruct((n_rows, HALF), U32),
        compiler_params=_cparams(("arbitrary",)),
    )(block_start, block_count, xs, w_gate, w_up, w_down)


COMBINE_TOKENS = RANK_BLOCK // TOP_K


def _combine_kernel(dest_ref, dest_next_ref, h_ref, wts_ref, nw_ref, y_ref, o_ref, buf_ref, sem):
    ts = COMBINE_TOKENS
    i = pl.program_id(0)
    slot = i % 2

    def gather(idx_ref, dst_slot):
        def issue(g, carry):
            c0 = g * DMA_UNROLL
            for j in range(DMA_UNROLL):
                src_row = idx_ref[0, 0, c0 + j]
                src_tile = lax.shift_right_logical(src_row, SUBLANES.bit_length() - 1)
                pltpu.make_async_copy(y_ref.at[src_tile, pl.ds(src_row & (SUBLANES - 1), 1)],
                                      buf_ref.at[dst_slot, j % TOP_K, g, pl.ds(j // TOP_K, 1)],
                                      sem.at[dst_slot]).start()
            return carry

        lax.fori_loop(0, RANK_BLOCK // DMA_UNROLL, issue, 0)

    @pl.when(i == 0)
    def _():
        gather(dest_ref, 0)

    @pl.when(i + 1 < pl.num_programs(0))
    def _():
        gather(dest_next_ref, 1 - slot)

    for k in range(TOP_K):
        pltpu.make_async_copy(y_ref.at[pl.ds(0, ts // SUBLANES)], buf_ref.at[slot, k], sem.at[slot]).wait()

    w = wts_ref[...]
    w0 = w[:, 0:1]
    w1 = w[:, 1:2]
    lo0, hi0 = _unpack_bf16_pair(buf_ref[slot, 0].reshape(ts, HALF))
    lo1, hi1 = _unpack_bf16_pair(buf_ref[slot, 1].reshape(ts, HALF))
    h = h_ref[...]
    out_lo = h[:, :HALF] + w0 * lo0 + w1 * lo1
    out_hi = h[:, HALF:] + w0 * hi0 + w1 * hi1
    ms = (jnp.sum(out_lo * out_lo, axis=-1, keepdims=True)
          + jnp.sum(out_hi * out_hi, axis=-1, keepdims=True)) * (1.0 / D_MODEL)
    inv = lax.rsqrt(ms + EPS)
    o_ref[:, :HALF] = out_lo * inv * nw_ref[:, :HALF]
    o_ref[:, HALF:] = out_hi * inv * nw_ref[:, HALF:]


def _combine(dest, h1, wts, nw, y):
    t = h1.shape[0]
    ts = COMBINE_TOKENS
    n_steps = t // ts
    return pl.pallas_call(
        _combine_kernel,
        name="combine",
        grid=(n_steps,),
        in_specs=[pl.BlockSpec((1, 1, RANK_BLOCK), lambda i: (i, 0, 0), memory_space=pltpu.SMEM),
                  pl.BlockSpec((1, 1, RANK_BLOCK), lambda i: (jnp.minimum(i + 1, n_steps - 1), 0, 0),
                               memory_space=pltpu.SMEM),
                  pl.BlockSpec((ts, D_MODEL), lambda i: (i, 0)),
                  pl.BlockSpec((ts, LANES), lambda i: (i, 0)),
                  pl.BlockSpec((1, D_MODEL), lambda i: (0, 0)),
                  pl.BlockSpec(memory_space=pl.ANY)],
        out_specs=pl.BlockSpec((ts, D_MODEL), lambda i: (i, 0)),
        out_shape=jax.ShapeDtypeStruct((t, D_MODEL), F32),
        scratch_shapes=[pltpu.VMEM((2, TOP_K, ts // SUBLANES, SUBLANES, HALF), U32),
                        pltpu.SemaphoreType.DMA((2,))],
        compiler_params=_cparams(("arbitrary",)),
    )(dest, dest, h1, wts, nw, y.reshape(-1, SUBLANES, HALF))


def _narrow_w_in(w_in_t):
    dt0 = SSD_DIM + CONV_DIM
    f0 = dt0 + SSD_HEADS + 3 * FOX_DIM
    dt = w_in_t[dt0:dt0 + SSD_HEADS]
    f = w_in_t[f0:f0 + FOX_HEADS]
    pad = jnp.zeros((LANES - SMALL_F - FOX_HEADS, w_in_t.shape[1]), w_in_t.dtype)
    return jnp.concatenate([dt] * SMALL_DT_COPIES + [f, pad], axis=0).T.astype(BF16)


def _layer(h, p, bsz, seq):
    t = bsz * seq
    w_in_t = jnp.swapaxes(p["w_in"].astype(F32), 0, 1)
    u, small, cum = _norm_small(h, p["norm_mix_w"].astype(F32).reshape(1, D_MODEL), _narrow_w_in(w_in_t),
                                p["fox_f_bias"], seq)
    proj = _in_proj(u, w_in_t)
    y_ssd = _ssd(proj, small, p["conv_w"], p["conv_b"], p["dt_bias"], p["a_log"], p["d_skip"],
                 p["ssd_norm_w"], bsz, seq)
    y_fox = _fox(proj, cum, bsz, seq)
    mixed = _mix(y_ssd, y_fox, proj, p["w_proj_ssd"].astype(BF16), p["w_proj_fox"].astype(BF16))

    w_router = jnp.concatenate(
        [p["w_router_group"], p["w_router_expert"],
         jnp.zeros((D_MODEL, LANES - N_GROUPS - N_EXPERTS), F32)], axis=1).astype(F32)
    w_router_hi = w_router.astype(BF16)
    w_router_lo = (w_router - w_router_hi.astype(F32)).astype(BF16)
    w_router = jnp.concatenate([w_router_hi, w_router_lo], axis=1)
    b_router = jnp.concatenate(
        [p["b_router_group"], p["b_router_expert"],
         jnp.zeros((LANES - N_GROUPS - N_EXPERTS,), F32)]).astype(F32).reshape(1, LANES)
    h1, u2p, eidx, wts = _outproj(mixed, h, p["w_out"].astype(BF16),
                                  p["norm_moe_w"].astype(F32).reshape(1, D_MODEL), w_router, b_router)

    tk = t * TOP_K
    e_blocks = eidx[:, :TOP_K].reshape(tk // RANK_BLOCK, 1, RANK_BLOCK)
    rank, counts = _rank(e_blocks)
    counts = counts[:, 0].astype(I32)
    padded = ((counts + ROW_BLOCK - 1) // ROW_BLOCK) * ROW_BLOCK
    pend = jnp.cumsum(padded)
    pstart = pend - padded
    n_blocks = tk // ROW_BLOCK + N_EXPERTS
    dest = _dest(e_blocks, rank, jnp.broadcast_to(pstart[:, None], (N_EXPERTS, LANES)).astype(I32))
    block_start = (pstart // ROW_BLOCK).astype(I32)
    block_count = (padded // ROW_BLOCK).astype(I32)
    last_blk = jnp.where(block_count > 0, block_start + block_count - 1, -1).astype(I32)
    n_used = (pend[-1:] // ROW_BLOCK).astype(I32)
    xs = _dispatch(dest, u2p, n_blocks * ROW_BLOCK, last_blk, n_used)
    y = _experts(block_start, block_count, xs, p["w_gate_exp"], p["w_up_exp"], p["w_down_exp"])
    return dest, h1, wts, y


def kernel(x, norm_mix_w, w_in, conv_w, conv_b, dt_bias, a_log, d_skip, ssd_norm_w, fox_f_bias, w_proj_ssd,
           w_proj_fox, w_out, norm_moe_w, w_router_group, b_router_group, w_router_expert, b_router_expert,
           w_gate_exp, w_up_exp, w_down_exp, norm_final_w):
    bsz, seq, _ = x.shape
    depth = w_in.shape[0]
    assert depth == 1, "the fused final norm assumes a single layer"
    stacked = dict(norm_mix_w=norm_mix_w, w_in=w_in, conv_w=conv_w, conv_b=conv_b, dt_bias=dt_bias, a_log=a_log,
                   d_skip=d_skip, ssd_norm_w=ssd_norm_w, fox_f_bias=fox_f_bias, w_proj_ssd=w_proj_ssd,
                   w_proj_fox=w_proj_fox, w_out=w_out, norm_moe_w=norm_moe_w, w_router_group=w_router_group,
                   b_router_group=b_router_group, w_router_expert=w_router_expert,
                   b_router_expert=b_router_expert, w_gate_exp=w_gate_exp, w_up_exp=w_up_exp,
                   w_down_exp=w_down_exp)
    p = {name: v[0] for name, v in stacked.items()}
    h = x.reshape(bsz * seq, D_MODEL)
    dest, h1, wts, y = _layer(h, p, bsz, seq)
    out = _combine(dest, h1, wts, norm_final_w.astype(F32).reshape(1, D_MODEL), y)
    return out.reshape(bsz, seq, D_MODEL)
```

```python
import functools
import math

import jax
import jax.numpy as jnp
from jax import lax
from jax.experimental import pallas as pl
from jax.experimental.pallas import tpu as pltpu

F32 = jnp.float32
BF16 = jnp.bfloat16
I32 = jnp.int32
U32 = jnp.uint32

D_MODEL = 2048
SSD_HEADS = 32
SSD_HEAD_DIM = 64
SSD_DIM = SSD_HEADS * SSD_HEAD_DIM
SSD_GROUPS = 4
SSD_STATE = 128
SSD_CHUNK = 128
CONV_WIDTH = 4
CONV_DIM = SSD_DIM + 2 * SSD_GROUPS * SSD_STATE
FOX_HEADS = 16
FOX_HEAD_DIM = 128
FOX_DIM = FOX_HEADS * FOX_HEAD_DIM
N_GROUPS = 8
EXPERTS_PER_GROUP = 8
N_EXPERTS = N_GROUPS * EXPERTS_PER_GROUP
TOP_K = 2
D_EXPERT = 512
EPS = 1e-6
LOG2E = 1.4426950408889634

LANES = 128
SUBLANES = 8
VMEM_LIMIT = 52 * 1024 * 1024

COL_Z = 0
COL_Q = COL_Z + SSD_DIM
COL_K = COL_Q + FOX_DIM
COL_V = COL_K + FOX_DIM
COL_GA = COL_V + FOX_DIM
COL_GB = COL_GA + D_MODEL
COL_XBC = COL_GB + D_MODEL
PROJ_COLS = COL_XBC + CONV_DIM
SMALL_DT = 0
SMALL_DT_COPIES = 3
SMALL_F = SMALL_DT_COPIES * SSD_HEADS

ROW_BLOCK = 128
HALF = D_MODEL // 2


def _cparams(sem, vmem=VMEM_LIMIT):
    return pltpu.CompilerParams(dimension_semantics=sem, vmem_limit_bytes=vmem)


def _silu(x):
    half = 0.5 * x
    return half + half * jnp.tanh(half)


def _softplus(x):
    return jnp.maximum(x, 0.0) + jnp.log(1.0 + jnp.exp(-jnp.abs(x)))


def _log_sigmoid(x):
    return -_softplus(-x)


def _split3(x):
    hi = x.astype(BF16)
    rest = x - hi.astype(F32)
    mid = rest.astype(BF16)
    lo = (rest - mid.astype(F32)).astype(BF16)
    return hi, mid, lo


def _pack_bf16_pair(x):
    n = x.shape[1] // 2
    lo = pltpu.bitcast(x[:, :n].astype(BF16).astype(F32), U32)
    hi = pltpu.bitcast(x[:, n:].astype(BF16).astype(F32), U32)
    return (hi & jnp.uint32(0xFFFF0000)) | (lo >> 16)


def _unpack_bf16_pair(p):
    lo = pltpu.bitcast(p << 16, F32)
    hi = pltpu.bitcast(p & jnp.uint32(0xFFFF0000), F32)
    return lo, hi


def _norm_small_kernel(x_ref, nw_ref, ws_ref, fb_ref, u_ref, s_ref, cum_ref, carry_ref, *, tiles_per_seq):
    i = pl.program_id(0)
    x = x_ref[...]
    ms = jnp.mean(x * x, axis=-1, keepdims=True)
    u = (x * lax.rsqrt(ms + EPS) * nw_ref[...]).astype(BF16)
    u_ref[...] = u
    small = jnp.dot(u, ws_ref[...], preferred_element_type=F32)
    s_ref[...] = small

    @pl.when(i % tiles_per_seq == 0)
    def _():
        carry_ref[...] = jnp.zeros_like(carry_ref)

    tm = x.shape[0]
    lf = _log_sigmoid(small + fb_ref[...])
    row = lax.broadcasted_iota(I32, (tm, tm), 0)
    col = lax.broadcasted_iota(I32, (tm, tm), 1)
    tril = jnp.where(col <= row, 1.0, 0.0).astype(BF16)
    cs = sum(jnp.dot(tril, piece, preferred_element_type=F32) for piece in _split3(lf)) + carry_ref[...]
    cum_ref[...] = cs * LOG2E
    carry_ref[...] = cs[tm - 1:tm, :]


def _norm_small(x2, nw, w_small, fox_f_bias, seq, tm=512):
    t = x2.shape[0]
    fb = jnp.zeros((1, LANES), F32).at[0, SMALL_F:SMALL_F + FOX_HEADS].set(fox_f_bias.astype(F32))
    kern = functools.partial(_norm_small_kernel, tiles_per_seq=seq // tm)
    return pl.pallas_call(
        kern,
        name="norm_small",
        grid=(t // tm,),
        in_specs=[pl.BlockSpec((tm, D_MODEL), lambda i: (i, 0)),
                  pl.BlockSpec((1, D_MODEL), lambda i: (0, 0)),
                  pl.BlockSpec((D_MODEL, LANES), lambda i: (0, 0)),
                  pl.BlockSpec((1, LANES), lambda i: (0, 0))],
        out_specs=[pl.BlockSpec((tm, D_MODEL), lambda i: (i, 0)),
                   pl.BlockSpec((tm, LANES), lambda i: (i, 0)),
                   pl.BlockSpec((tm, LANES), lambda i: (i, 0))],
        out_shape=[jax.ShapeDtypeStruct((t, D_MODEL), BF16),
                   jax.ShapeDtypeStruct((t, LANES), F32),
                   jax.ShapeDtypeStruct((t, LANES), F32)],
        scratch_shapes=[pltpu.VMEM((1, LANES), F32)],
        compiler_params=_cparams(("arbitrary",)),
    )(x2, nw, w_small, fb)


IN_TN = 1024
IN_XPOSE = 256
_IN_SEGMENTS = ((COL_Z, 0, SSD_DIM),
                (COL_Q, SSD_DIM + CONV_DIM + SSD_HEADS, 3 * FOX_DIM),
                (COL_GA, SSD_DIM + CONV_DIM + SSD_HEADS + 3 * FOX_DIM + FOX_HEADS, 2 * D_MODEL),
                (COL_XBC, SSD_DIM, CONV_DIM))


def _in_proj_source_rows():
    src = [0] * (PROJ_COLS // IN_TN)
    for out0, src0, width in _IN_SEGMENTS:
        for c in range(0, width, IN_TN):
            src[(out0 + c) // IN_TN] = src0 + c
    assert all(s % SUBLANES == 0 for s in src)
    return src


def _in_proj_kernel(row_ref, u_ref, wt_hbm, o_ref, st_ref, wbf_ref, sem):
    j = pl.program_id(0)
    i = pl.program_id(1)
    k = wbf_ref.shape[0]

    def window(jj, slot):
        row0 = pl.multiple_of(row_ref[jj], SUBLANES)
        return pltpu.make_async_copy(wt_hbm.at[pl.ds(row0, IN_TN), :], st_ref.at[slot], sem.at[slot])

    @pl.when((j == 0) & (i == 0))
    def _():
        window(0, 0).start()

    @pl.when(i == 0)
    def _():
        @pl.when(j + 1 < pl.num_programs(0))
        def _():
            window(j + 1, (j + 1) % 2).start()

        slot = j % 2
        window(j, slot).wait()
        for c in range(0, k, IN_XPOSE):
            wbf_ref[c:c + IN_XPOSE, :] = jnp.transpose(st_ref[slot, :, c:c + IN_XPOSE]).astype(BF16)

    o_ref[...] = jnp.dot(u_ref[...], wbf_ref[...], preferred_element_type=F32).astype(o_ref.dtype)


def _in_proj(u, w_in_t, tm=1024):
    m, k = u.shape
    grid_spec = pltpu.PrefetchScalarGridSpec(
        num_scalar_prefetch=1,
        grid=(PROJ_COLS // IN_TN, m // tm),
        in_specs=[pl.BlockSpec((tm, k), lambda j, i, rows: (i, 0)),
                  pl.BlockSpec(memory_space=pl.ANY)],
        out_specs=pl.BlockSpec((tm, IN_TN), lambda j, i, rows: (i, j)),
        scratch_shapes=[pltpu.VMEM((2, IN_TN, k), F32),
                        pltpu.VMEM((k, IN_TN), BF16),
                        pltpu.SemaphoreType.DMA((2,))],
    )
    return pl.pallas_call(
        _in_proj_kernel,
        name="in_proj",
        grid_spec=grid_spec,
        out_shape=jax.ShapeDtypeStruct((m, PROJ_COLS), BF16),
        compiler_params=_cparams(("arbitrary", "arbitrary")),
    )(jnp.asarray(_in_proj_source_rows(), I32), u, w_in_t)


HALO = 2 * SUBLANES


SSD_CHUNKS_PER_STEP = 2


def _ssd_kernel(z_ref, xbc_ref, halo_ref, small_ref, cw_ref, cb_ref, dtb_ref, aneg_ref, dexp_ref,
                nw_ref, expand_ref, shift_ref, y_ref, state_ref, ydiag_ref):
    c = pl.program_id(1)
    l = SSD_CHUNK

    @pl.when(c == 0)
    def _():
        state_ref[...] = jnp.zeros_like(state_ref)

    for sub in range(SSD_CHUNKS_PER_STEP):
        rows = slice(sub * l, (sub + 1) * l)
        if sub == 0:
            halo = halo_ref[...]
            halo = jnp.where(c == 0, jnp.zeros_like(halo), halo)
        else:
            halo = xbc_ref[sub * l - HALO:sub * l, :]
        _ssd_chunk(rows, halo, z_ref, xbc_ref, small_ref, cw_ref, cb_ref, dtb_ref, aneg_ref, dexp_ref,
                   nw_ref, expand_ref, shift_ref, y_ref, state_ref, ydiag_ref)


def _ssd_chunk(rows, halo, z_ref, xbc_ref, small_ref, cw_ref, cb_ref, dtb_ref, aneg_ref, dexp_ref,
               nw_ref, expand_ref, shift_ref, y_ref, state_ref, ydiag_ref):
    l = SSD_CHUNK
    n = SSD_STATE
    cur = xbc_ref[rows, :]
    ext = jnp.concatenate([halo, cur], axis=0)
    shifted = jnp.dot(shift_ref[...], ext, preferred_element_type=F32)
    conv = cb_ref[...] + cw_ref[CONV_WIDTH - 1:CONV_WIDTH, :] * cur.astype(F32)
    for j in range(CONV_WIDTH - 1):
        conv = conv + cw_ref[j:j + 1, :] * shifted[j * l:(j + 1) * l]
    xbc = _silu(conv)
    xs = xbc[:, :SSD_DIM]
    bm = xbc[:, SSD_DIM:SSD_DIM + SSD_GROUPS * n]
    cm = xbc[:, SSD_DIM + SSD_GROUPS * n:]

    h3 = SMALL_DT_COPIES * SSD_HEADS
    dt3 = _softplus(small_ref[rows, SMALL_DT:SMALL_DT + h3] + dtb_ref[...])
    adt3 = dt3 * aneg_ref[...]
    row = lax.broadcasted_iota(I32, (l, l), 0)
    col = lax.broadcasted_iota(I32, (l, l), 1)
    causal = col <= row
    tril = jnp.where(causal, 1.0, 0.0).astype(BF16)
    a_cs3 = sum(jnp.dot(tril, piece, preferred_element_type=F32) for piece in _split3(adt3))
    a_cs2 = a_cs3[:, :SSD_HEADS] * LOG2E
    a_cs2_t = jnp.transpose(a_cs2)
    a_last3 = a_cs3[l - 1:l, :]

    lane3 = lax.broadcasted_iota(I32, (l, h3), 1)

    def pieces_by_lane_group(x3):
        hi, mid, lo = _split3(x3)
        return jnp.where(lane3 < SSD_HEADS, hi, jnp.where(lane3 < 2 * SSD_HEADS, mid, lo))

    lhs3 = jnp.concatenate([pieces_by_lane_group(dt3),
                            pieces_by_lane_group(jnp.exp(a_cs3)),
                            pieces_by_lane_group(jnp.exp(a_last3 - a_cs3))], axis=0)
    expanded = jnp.dot(lhs3, expand_ref[...], preferred_element_type=F32)
    dt_x = expanded[0:l]
    decay_in = expanded[l:2 * l]
    decay_out = expanded[2 * l:3 * l]
    chunk_decay = decay_in[l - 1:l, :]
    x_dt = xs * dt_x
    x_dt_b = x_dt.astype(BF16)
    xd_b = (x_dt * decay_out).astype(BF16)

    lane = lax.broadcasted_iota(I32, (l, LANES), 1)
    lo_mask = lane < SSD_HEAD_DIM
    heads_per_group = SSD_HEADS // SSD_GROUPS
    gw = heads_per_group * SSD_HEAD_DIM

    for g in range(SSD_GROUPS):
        bg = bm[:, g * n:(g + 1) * n].astype(BF16)
        cg = cm[:, g * n:(g + 1) * n].astype(BF16)
        cb = lax.dot_general(cg, bg, (((1,), (1,)), ((), ())), preferred_element_type=F32)
        for pair in range(heads_per_group // 2):
            h0 = g * heads_per_group + 2 * pair
            lane0 = h0 * SSD_HEAD_DIM
            ms = []
            for h in (h0, h0 + 1):
                seg = a_cs2[:, h:h + 1] - a_cs2_t[h:h + 1, :]
                lmat = jnp.exp2(jnp.where(causal, seg, -jnp.inf))
                ms.append((cb * lmat).astype(BF16))
            lhs = jnp.concatenate(ms, axis=1)
            xp = x_dt_b[:, lane0:lane0 + LANES]
            zero = jnp.zeros_like(xp)
            rhs = jnp.concatenate([jnp.where(lo_mask, xp, zero), jnp.where(lo_mask, zero, xp)], axis=0)
            ydiag_ref[:, lane0:lane0 + LANES] = jnp.dot(lhs, rhs, preferred_element_type=F32)
        st = state_ref[:, g * gw:(g + 1) * gw]
        y_off = jnp.dot(cg, st.astype(BF16), preferred_element_type=F32)
        ydiag_ref[:, g * gw:(g + 1) * gw] += y_off * decay_in[:, g * gw:(g + 1) * gw]
        new = lax.dot_general(bg, xd_b[:, g * gw:(g + 1) * gw], (((0,), (0,)), ((), ())),
                              preferred_element_type=F32)
        state_ref[:, g * gw:(g + 1) * gw] = st * chunk_decay[:, g * gw:(g + 1) * gw] + new

    y = ydiag_ref[...] + dexp_ref[...] * xs
    y = y * _silu(z_ref[rows, :].astype(F32))
    ms = jnp.mean(y * y, axis=-1, keepdims=True)
    y_ref[rows, :] = (y * lax.rsqrt(ms + EPS) * nw_ref[...]).astype(y_ref.dtype)


def _ssd(proj, small, conv_w, conv_b, dt_bias, a_log, d_skip, norm_w, bsz, seq):
    l = SSD_CHUNK
    lb = SSD_CHUNKS_PER_STEP * l
    nc = seq // lb
    t = bsz * seq
    h3 = SMALL_DT_COPIES * SSD_HEADS
    aneg = jnp.tile(-jnp.exp(a_log.astype(F32)), SMALL_DT_COPIES).reshape(1, h3)
    dtb = jnp.tile(dt_bias.astype(F32), SMALL_DT_COPIES).reshape(1, h3)
    dexp = jnp.repeat(d_skip.astype(F32), SSD_HEAD_DIM).reshape(1, SSD_DIM)
    expand = jnp.tile(jnp.repeat(jnp.eye(SSD_HEADS, dtype=BF16), SSD_HEAD_DIM, axis=1),
                      (SMALL_DT_COPIES, 1))
    out_row = jnp.arange((CONV_WIDTH - 1) * l)
    src_row = HALO + out_row % l - (CONV_WIDTH - 1) + out_row // l
    shift = (jnp.arange(HALO + l)[None, :] == src_row[:, None]).astype(BF16)
    xbc_blk = COL_XBC // CONV_DIM
    halo_per_chunk = lb // HALO

    def row_map(b, c):
        return b * nc + c

    return pl.pallas_call(
        _ssd_kernel,
        name="ssd",
        grid=(bsz, nc),
        in_specs=[
            pl.BlockSpec((lb, SSD_DIM), lambda b, c: (row_map(b, c), COL_Z // SSD_DIM)),
            pl.BlockSpec((lb, CONV_DIM), lambda b, c: (row_map(b, c), xbc_blk)),
            pl.BlockSpec((HALO, CONV_DIM),
                         lambda b, c: (jnp.maximum(row_map(b, c) * halo_per_chunk - 1, 0), xbc_blk)),
            pl.BlockSpec((lb, LANES), lambda b, c: (row_map(b, c), 0)),
            pl.BlockSpec((CONV_WIDTH, CONV_DIM), lambda b, c: (0, 0)),
            pl.BlockSpec((1, CONV_DIM), lambda b, c: (0, 0)),
            pl.BlockSpec((1, h3), lambda b, c: (0, 0)),
            pl.BlockSpec((1, h3), lambda b, c: (0, 0)),
            pl.BlockSpec((1, SSD_DIM), lambda b, c: (0, 0)),
            pl.BlockSpec((1, SSD_DIM), lambda b, c: (0, 0)),
            pl.BlockSpec((h3, SSD_DIM), lambda b, c: (0, 0)),
            pl.BlockSpec(((CONV_WIDTH - 1) * l, HALO + l), lambda b, c: (0, 0)),
        ],
        out_specs=pl.BlockSpec((lb, SSD_DIM), lambda b, c: (row_map(b, c), 0)),
        out_shape=jax.ShapeDtypeStruct((t, SSD_DIM), BF16),
        scratch_shapes=[pltpu.VMEM((SSD_STATE, SSD_DIM), F32),
                        pltpu.VMEM((l, SSD_DIM), F32)],
        compiler_params=_cparams(("parallel", "arbitrary")),
    )(proj, proj, proj, small, conv_w.astype(F32), conv_b.astype(F32).reshape(1, CONV_DIM),
      dtb, aneg, dexp, norm_w.astype(F32).reshape(1, SSD_DIM), expand, shift)


FOX_HEADS_PER_STEP = 2


FOX_SLAB = 128


def _fox_kernel(q_ref, k_ref, v_ref, cq_ref, ck_ref, o_ref, sa_ref, sb_ref, p_ref, m_ref, l_ref, m_alt_ref,
                cqrep_ref, psum_ref, acc_ref, *, tq, tk):
    hp = pl.program_id(1)
    qi = pl.program_id(2)
    d = FOX_HEAD_DIM
    c2 = LOG2E / math.sqrt(d)
    lane = lax.broadcasted_iota(I32, (tq, LANES), 1)
    cq_all = cq_ref[...]

    qs, cqs = [], []
    for hh in range(FOX_HEADS_PER_STEP):
        qs.append((q_ref[:, hh * d:(hh + 1) * d].astype(F32) * c2).astype(BF16))
        head_lane = SMALL_F + hp * FOX_HEADS_PER_STEP + hh
        cq_col = jnp.sum(jnp.where(lane == head_lane, cq_all, 0.0), axis=-1, keepdims=True)
        cqs.append(jnp.broadcast_to(cq_col, (tq, LANES)))

    s_slots = (sa_ref, sb_ref)

    def scores(ki, slot):
        row0 = pl.multiple_of(ki * tk, tk)
        for hh in range(FOX_HEADS_PER_STEP):
            k = k_ref[pl.ds(row0, tk), hh * d:(hh + 1) * d]
            s = lax.dot_general(qs[hh], k, (((1,), (1,)), ((), ())), preferred_element_type=F32)
            s_slots[slot][hh] = s - ck_ref[hh, pl.ds(ki, 1), :]

    def update(ki, slot, masked):
        row0 = pl.multiple_of(ki * tk, tk)
        s_ref = s_slots[slot]
        m_bufs = (m_ref, m_alt_ref)
        n_ct = tk // LANES
        for hh in range(FOX_HEADS_PER_STEP):
            for rc in range(tq // FOX_SLAB):
                rows = slice(rc * FOX_SLAB, (rc + 1) * FOX_SLAB)
                n_vis = rc + 1 if masked else n_ct

                def slab(ct):
                    x = s_ref[hh, rows, ct * LANES:(ct + 1) * LANES]
                    if masked and ct == rc:
                        row = lax.broadcasted_iota(I32, (FOX_SLAB, LANES), 0)
                        col = lax.broadcasted_iota(I32, (FOX_SLAB, LANES), 1)
                        x = jnp.where(col <= row, x, -jnp.inf)
                    return x

                tmax = slab(0)
                for ct in range(1, n_vis):
                    tmax = jnp.maximum(tmax, slab(ct))
                row_max = jnp.max(tmax, axis=-1, keepdims=True)
                cq = cqrep_ref[hh, rows, :]
                m_new = jnp.maximum(m_bufs[slot][hh, rows, :], jnp.broadcast_to(row_max, (FOX_SLAB, LANES)) + cq)
                m_bufs[1 - slot][hh, rows, :] = m_new
                r = m_new - cq
                psum = None
                for ct in range(n_vis):
                    p = jnp.exp2(slab(ct) - r)
                    psum = p if psum is None else psum + p
                    p_ref[hh, rows, ct * LANES:(ct + 1) * LANES] = p.astype(BF16)
                for ct in range(n_vis, n_ct):
                    p_ref[hh, rows, ct * LANES:(ct + 1) * LANES] = jnp.zeros((FOX_SLAB, LANES), BF16)
                psum_ref[hh, rows, :] = psum
            m_old = m_bufs[slot][hh]
            m_cur = m_bufs[1 - slot][hh]
            l_ref[hh] = jnp.exp2(m_old - m_cur) * l_ref[hh] + psum_ref[hh]
            v = v_ref[pl.ds(row0, tk), hh * d:(hh + 1) * d]
            acc_ref[hh] = jnp.exp2(m_old - m_cur) * acc_ref[hh] + jnp.dot(p_ref[hh], v, preferred_element_type=F32)

    for hh in range(FOX_HEADS_PER_STEP):
        cqrep_ref[hh] = cqs[hh]

    m_ref[...] = jnp.full_like(m_ref, -jnp.inf)
    l_ref[...] = jnp.zeros_like(l_ref)
    acc_ref[...] = jnp.zeros_like(acc_ref)

    n_full = qi
    scores(0, 0)

    def pair(j, carry):
        b0 = 2 * j
        scores(b0 + 1, 1)
        update(b0, 0, False)
        scores(b0 + 2, 0)
        update(b0 + 1, 1, False)
        return carry

    lax.fori_loop(0, n_full // 2, pair, 0)

    @pl.when(n_full % 2 == 0)
    def _():
        update(n_full, 0, True)

    @pl.when(n_full % 2 == 1)
    def _():
        scores(n_full, 1)
        update(n_full - 1, 0, False)
        update(n_full, 1, True)

    for hh in range(FOX_HEADS_PER_STEP):
        l_fin = jnp.sum(l_ref[hh], axis=-1, keepdims=True)
        o_ref[:, hh * d:(hh + 1) * d] = (acc_ref[hh] / l_fin).astype(o_ref.dtype)


def _fox(proj, cum, bsz, seq, tq=1024):
    assert FOX_SLAB == LANES and tq % FOX_SLAB == 0 and seq % tq == 0
    tk = tq
    nq = seq // tq
    nk = seq // tk
    hps = FOX_HEADS_PER_STEP
    cum_row = cum[:, SMALL_F:SMALL_F + FOX_HEADS].reshape(bsz, seq, FOX_HEADS).transpose(0, 2, 1)
    cum_row = cum_row.reshape(bsz, FOX_HEADS, nk, tk)
    t = bsz * seq
    w = FOX_HEADS_PER_STEP * FOX_HEAD_DIM
    n_hp = FOX_HEADS // FOX_HEADS_PER_STEP
    kern = functools.partial(_fox_kernel, tq=tq, tk=tk)
    return pl.pallas_call(
        kern,
        name="fox",
        grid=(bsz, n_hp, nq),
        in_specs=[
            pl.BlockSpec((tq, w), lambda b, hp, qi: (b * nq + qi, COL_Q // w + hp)),
            pl.BlockSpec((seq, w), lambda b, hp, qi: (b, COL_K // w + hp)),
            pl.BlockSpec((seq, w), lambda b, hp, qi: (b, COL_V // w + hp)),
            pl.BlockSpec((tq, LANES), lambda b, hp, qi: (b * nq + qi, 0)),
            pl.BlockSpec((None, FOX_HEADS_PER_STEP, nk, tk), lambda b, hp, qi: (b, hp, 0, 0)),
        ],
        out_specs=pl.BlockSpec((tq, w), lambda b, hp, qi: (b * nq + qi, hp)),
        out_shape=jax.ShapeDtypeStruct((t, FOX_DIM), BF16),
        scratch_shapes=[pltpu.VMEM((hps, tq, tk), F32), pltpu.VMEM((hps, tq, tk), F32),
                        pltpu.VMEM((hps, tq, tk), BF16),
                        pltpu.VMEM((hps, tq, LANES), F32), pltpu.VMEM((hps, tq, LANES), F32),
                        pltpu.VMEM((hps, tq, LANES), F32), pltpu.VMEM((hps, tq, LANES), F32),
                        pltpu.VMEM((hps, tq, LANES), F32),
                        pltpu.VMEM((hps, tq, FOX_HEAD_DIM), F32)],
        compiler_params=_cparams(("parallel", "parallel", "arbitrary")),
    )(proj, proj, proj, cum, cum_row)


def _mix_kernel(ya_ref, yb_ref, ga_ref, gb_ref, wa_ref, wb_ref, o_ref):
    pa = jnp.dot(ya_ref[...], wa_ref[...], preferred_element_type=F32)
    pb = jnp.dot(yb_ref[...], wb_ref[...], preferred_element_type=F32)
    ga = 1.0 / (1.0 + jnp.exp(-ga_ref[...].astype(F32)))
    gb = 1.0 / (1.0 + jnp.exp(-gb_ref[...].astype(F32)))
    o_ref[...] = (ga * pa + gb * pb).astype(o_ref.dtype)


def _mix(y_a, y_b, proj, w_a, w_b, tm=512, tn=2048):
    t = y_a.shape[0]
    resident = pl.Buffered(1) if tn == D_MODEL else None
    return pl.pallas_call(
        _mix_kernel,
        name="mix",
        grid=(D_MODEL // tn, t // tm),
        in_specs=[
            pl.BlockSpec((tm, SSD_DIM), lambda j, i: (i, 0)),
            pl.BlockSpec((tm, FOX_DIM), lambda j, i: (i, 0)),
            pl.BlockSpec((tm, tn), lambda j, i: (i, COL_GA // tn + j)),
            pl.BlockSpec((tm, tn), lambda j, i: (i, COL_GB // tn + j)),
            pl.BlockSpec((SSD_DIM, tn), lambda j, i: (0, j), pipeline_mode=resident),
            pl.BlockSpec((FOX_DIM, tn), lambda j, i: (0, j), pipeline_mode=resident),
        ],
        out_specs=pl.BlockSpec((tm, tn), lambda j, i: (i, j)),
        out_shape=jax.ShapeDtypeStruct((t, D_MODEL), BF16),
        compiler_params=_cparams(("parallel", "parallel")),
    )(y_a, y_b, proj, proj, w_a, w_b)


def _outproj_kernel(m_ref, x_ref, wo_ref, nw_ref, wr_ref, br_ref, h_ref, u_ref, eidx_ref, wts_ref):
    h1 = x_ref[...] + jnp.dot(m_ref[...], wo_ref[...], preferred_element_type=F32)
    h_ref[...] = h1
    ms = jnp.mean(h1 * h1, axis=-1, keepdims=True)
    u2 = h1 * lax.rsqrt(ms + EPS) * nw_ref[...]
    u_ref[...] = _pack_bf16_pair(u2)

    u_hi = u2.astype(BF16)
    u_lo = (u2 - u_hi.astype(F32)).astype(BF16)
    hh_hl = jnp.dot(u_hi, wr_ref[...], preferred_element_type=F32)
    lh = jnp.dot(u_lo, wr_ref[:, :LANES], preferred_element_type=F32)
    logits = hh_hl[:, :LANES] + (hh_hl[:, LANES:] + lh) + br_ref[...]
    tm = logits.shape[0]
    lane = lax.broadcasted_iota(I32, (tm, LANES), 1)
    neg = -jnp.inf
    big = jnp.int32(2 * LANES)
    gl = jnp.where(lane < N_GROUPS, logits, neg)
    gmax = jnp.max(gl, axis=-1, keepdims=True)
    gsum = jnp.sum(jnp.exp(gl - gmax), axis=-1, keepdims=True)
    g_p = 1.0 / gsum
    g_idx = jnp.min(jnp.where(gl == gmax, lane, big), axis=-1, keepdims=True)
    e_of_lane = lane - N_GROUPS
    in_grp = (e_of_lane >= g_idx * EXPERTS_PER_GROUP) & (e_of_lane < (g_idx + 1) * EXPERTS_PER_GROUP)
    el = jnp.where(in_grp, logits, neg)
    m1 = jnp.max(el, axis=-1, keepdims=True)
    i1 = jnp.min(jnp.where(el == m1, lane, big), axis=-1, keepdims=True)
    el2 = jnp.where(lane == i1, neg, el)
    m2 = jnp.max(el2, axis=-1, keepdims=True)
    i2 = jnp.min(jnp.where(el2 == m2, lane, big), axis=-1, keepdims=True)
    esum = jnp.sum(jnp.exp(el - m1), axis=-1, keepdims=True)
    p1 = 1.0 / esum
    p2 = jnp.exp(m2 - m1) / esum
    w1 = g_p * (p1 / (p1 + p2))
    w2 = g_p * (p2 / (p1 + p2))
    eidx_ref[...] = jnp.where(lane == 0, i1 - N_GROUPS, jnp.where(lane == 1, i2 - N_GROUPS, 0))
    wts_ref[...] = jnp.where(lane == 0, w1, jnp.where(lane == 1, w2, 0.0))


def _outproj(mixed, x2, w_o, nw, w_router, b_router, tm=512):
    t = mixed.shape[0]
    resident = pl.Buffered(1)
    return pl.pallas_call(
        _outproj_kernel,
        name="outproj",
        grid=(t // tm,),
        in_specs=[
            pl.BlockSpec((tm, D_MODEL), lambda i: (i, 0)),
            pl.BlockSpec((tm, D_MODEL), lambda i: (i, 0)),
            pl.BlockSpec((D_MODEL, D_MODEL), lambda i: (0, 0), pipeline_mode=resident),
            pl.BlockSpec((1, D_MODEL), lambda i: (0, 0)),
            pl.BlockSpec((D_MODEL, 2 * LANES), lambda i: (0, 0), pipeline_mode=resident),
            pl.BlockSpec((1, LANES), lambda i: (0, 0)),
        ],
        out_specs=[
            pl.BlockSpec((tm, D_MODEL), lambda i: (i, 0)),
            pl.BlockSpec((tm, HALF), lambda i: (i, 0)),
            pl.BlockSpec((tm, LANES), lambda i: (i, 0)),
            pl.BlockSpec((tm, LANES), lambda i: (i, 0)),
        ],
        out_shape=[
            jax.ShapeDtypeStruct((t, D_MODEL), F32),
            jax.ShapeDtypeStruct((t, HALF), U32),
            jax.ShapeDtypeStruct((t, LANES), I32),
            jax.ShapeDtypeStruct((t, LANES), F32),
        ],
        compiler_params=_cparams(("parallel",)),
    )(mixed, x2, w_o, nw, w_router, b_router)


RANK_BLOCK = 512


def _rank_kernel(e_ref, rank_ref, cnt_ref, carry_ref):
    i = pl.program_id(0)
    r = RANK_BLOCK

    @pl.when(i == 0)
    def _():
        carry_ref[...] = jnp.zeros_like(carry_ref)

    e = e_ref[0]
    expert = lax.broadcasted_iota(I32, (N_EXPERTS, r), 0)
    onehot = jnp.where(expert == e, 1.0, 0.0).astype(F32)
    jrow = lax.broadcasted_iota(I32, (r, r), 0)
    jcol = lax.broadcasted_iota(I32, (r, r), 1)
    before = jnp.where(jrow < jcol, 1.0, 0.0).astype(BF16)
    cum = jnp.dot(onehot.astype(BF16), before, preferred_element_type=F32)
    carry = carry_ref[...]
    rank = jnp.sum(onehot * (cum + carry[:, 0:1]), axis=0, keepdims=True)
    rank_ref[0] = rank.astype(I32)
    carry = carry + jnp.sum(onehot, axis=1, keepdims=True)
    carry_ref[...] = carry
    cnt_ref[...] = carry


def _rank(e_blocks):
    nb = e_blocks.shape[0]
    return pl.pallas_call(
        _rank_kernel,
        name="rank",
        grid=(nb,),
        in_specs=[pl.BlockSpec((1, 1, RANK_BLOCK), lambda i: (i, 0, 0))],
        out_specs=[pl.BlockSpec((1, 1, RANK_BLOCK), lambda i: (i, 0, 0)),
                   pl.BlockSpec((N_EXPERTS, LANES), lambda i: (0, 0))],
        out_shape=[jax.ShapeDtypeStruct((nb, 1, RANK_BLOCK), I32),
                   jax.ShapeDtypeStruct((N_EXPERTS, LANES), F32)],
        scratch_shapes=[pltpu.VMEM((N_EXPERTS, LANES), F32)],
        compiler_params=_cparams(("arbitrary",)),
    )(e_blocks)


def _dest_kernel(e_ref, rank_ref, pstart_ref, dest_ref):
    e = e_ref[0]
    expert = lax.broadcasted_iota(I32, (N_EXPERTS, RANK_BLOCK), 0)
    start = jnp.sum(jnp.where(expert == e, pstart_ref[:, 0:1], 0), axis=0, keepdims=True)
    dest_ref[0] = rank_ref[0] + start


def _dest(e_blocks, rank, pstart):
    nb = e_blocks.shape[0]
    return pl.pallas_call(
        _dest_kernel,
        name="dest",
        grid=(nb,),
        in_specs=[pl.BlockSpec((1, 1, RANK_BLOCK), lambda i: (i, 0, 0)),
                  pl.BlockSpec((1, 1, RANK_BLOCK), lambda i: (i, 0, 0)),
                  pl.BlockSpec((N_EXPERTS, LANES), lambda i: (0, 0))],
        out_specs=pl.BlockSpec((1, 1, RANK_BLOCK), lambda i: (i, 0, 0)),
        out_shape=jax.ShapeDtypeStruct((nb, 1, RANK_BLOCK), I32),
        compiler_params=_cparams(("parallel",)),
    )(e_blocks, rank, pstart)


DMA_UNROLL = 16


DISPATCH_RING = 3


def _dispatch_kernel(last_blk_ref, nused_ref, dest_ref, u_hbm, xs_ref, ubuf_ref, zbuf_ref, lsem, rsem, zsem):
    i = pl.program_id(0)
    n_steps = pl.num_programs(0)
    n_total = xs_ref.shape[0] // ROW_BLOCK
    groups = ubuf_ref.shape[1]

    def load(blk, slot):
        return pltpu.make_async_copy(u_hbm.at[pl.ds(blk * groups, groups)], ubuf_ref.at[slot], lsem.at[slot])

    def rows_done(slot):
        return pltpu.make_async_copy(xs_ref.at[pl.ds(0, RANK_BLOCK)], xs_ref.at[pl.ds(0, RANK_BLOCK)], rsem.at[slot])

    @pl.when(i == 0)
    def _():
        load(0, 0).start()

    @pl.when(i == 0)
    def _():
        zbuf_ref[...] = jnp.zeros_like(zbuf_ref)

        def zero_block(blk):
            return pltpu.make_async_copy(zbuf_ref, xs_ref.at[pl.ds(blk * ROW_BLOCK, ROW_BLOCK)], zsem)

        def fill_expert(e, count):
            blk = last_blk_ref[e]

            @pl.when(blk >= 0)
            def _():
                zero_block(blk).start()

            return count + jnp.where(blk >= 0, 1, 0)

        n_fill = lax.fori_loop(0, N_EXPERTS, fill_expert, 0)

        def fill_tail(blk, carry):
            zero_block(blk).start()
            return carry

        lax.fori_loop(nused_ref[0], n_total, fill_tail, 0)

        def drain(k, carry):
            zero_block(0).wait()
            return carry

        lax.fori_loop(0, n_fill + n_total - nused_ref[0], drain, 0)

    slot = i % DISPATCH_RING
    load(i, slot).wait()

    @pl.when(i >= DISPATCH_RING - 1)
    def _():
        rows_done((i + 1) % DISPATCH_RING).wait()

    @pl.when(i + 1 < n_steps)
    def _():
        load(i + 1, (i + 1) % DISPATCH_RING).start()

    def issue(g, carry):
        c0 = g * DMA_UNROLL
        for j in range(DMA_UNROLL):
            dst_row = dest_ref[0, 0, c0 + j]
            pltpu.make_async_copy(ubuf_ref.at[slot, g, pl.ds(j // TOP_K, 1)], xs_ref.at[pl.ds(dst_row, 1)],
                                  rsem.at[slot]).start()
        return carry

    lax.fori_loop(0, RANK_BLOCK // DMA_UNROLL, issue, 0)

    @pl.when(i == n_steps - 1)
    def _():
        for back in range(DISPATCH_RING - 2, -1, -1):
            @pl.when(i >= back)
            def _(back=back):
                rows_done((i - back) % DISPATCH_RING).wait()


def _dispatch(dest, u2p, n_rows, last_blk, n_used):
    nb = dest.shape[0]
    groups = RANK_BLOCK // DMA_UNROLL
    grid_spec = pltpu.PrefetchScalarGridSpec(
        num_scalar_prefetch=2,
        grid=(nb,),
        in_specs=[pl.BlockSpec((1, 1, RANK_BLOCK), lambda i, lb, nu: (i, 0, 0), memory_space=pltpu.SMEM),
                  pl.BlockSpec(memory_space=pl.ANY)],
        out_specs=pl.BlockSpec(memory_space=pl.ANY),
        scratch_shapes=[pltpu.VMEM((DISPATCH_RING, groups, SUBLANES, HALF), U32),
                        pltpu.VMEM((ROW_BLOCK, HALF), U32),
                        pltpu.SemaphoreType.DMA((DISPATCH_RING,)),
                        pltpu.SemaphoreType.DMA((DISPATCH_RING,)),
                        pltpu.SemaphoreType.DMA(())],
    )
    assert DMA_UNROLL == TOP_K * SUBLANES
    return pl.pallas_call(
        _dispatch_kernel,
        name="dispatch",
        grid_spec=grid_spec,
        out_shape=jax.ShapeDtypeStruct((n_rows, HALF), U32),
        compiler_params=_cparams(("arbitrary",)),
    )(last_blk, n_used, dest, u2p.reshape(-1, SUBLANES, HALF))


ROW_DMA_PRIORITY = 1
ROW_RING = 8


def _experts_kernel(bstart_ref, nblk_ref, xs_ref, wg_ref, wu_ref, wd_ref, y_ref,
                    wgb_ref, wub_ref, wdb_ref, xbuf_ref, ybuf_ref, xsem, ysem):
    e = pl.program_id(0)
    n_e = pl.num_programs(0)
    g0 = bstart_ref[e]
    nb = nblk_ref[e]
    n_used = bstart_ref[n_e - 1] + nblk_ref[n_e - 1]
    n_total = y_ref.shape[0] // ROW_BLOCK

    def x_copy(g, slot):
        return pltpu.make_async_copy(xs_ref.at[pl.ds(g * ROW_BLOCK, ROW_BLOCK)], xbuf_ref.at[slot], xsem.at[slot])

    def y_copy(g, slot):
        return pltpu.make_async_copy(ybuf_ref.at[slot], y_ref.at[pl.ds(g * ROW_BLOCK, ROW_BLOCK)], ysem.at[slot])

    @pl.when(e == 0)
    def _():
        for g in range(ROW_RING - 1):
            @pl.when(g < n_used)
            def _(g=g):
                x_copy(g, g).start(priority=ROW_DMA_PRIORITY)

    @pl.when(nb > 0)
    def _():
        wgb_ref[...] = wg_ref[0].astype(BF16)
        wub_ref[...] = wu_ref[0].astype(BF16)
        wdb_ref[...] = wd_ref[0].astype(BF16)

    def block(j, carry):
        g = g0 + j
        slot = g % ROW_RING
        x_copy(g, slot).wait()

        @pl.when(g + ROW_RING - 1 < n_used)
        def _():
            x_copy(g + ROW_RING - 1, (g + ROW_RING - 1) % ROW_RING).start(priority=ROW_DMA_PRIORITY)

        @pl.when(g >= ROW_RING)
        def _():
            y_copy(g - ROW_RING, slot).wait()

        lo, hi = _unpack_bf16_pair(xbuf_ref[slot])
        lo = lo.astype(BF16)
        hi = hi.astype(BF16)
        gate = (jnp.dot(lo, wgb_ref[:HALF, :], preferred_element_type=F32)
                + jnp.dot(hi, wgb_ref[HALF:, :], preferred_element_type=F32))
        up = (jnp.dot(lo, wub_ref[:HALF, :], preferred_element_type=F32)
              + jnp.dot(hi, wub_ref[HALF:, :], preferred_element_type=F32))
        hdn = (_silu(gate) * up).astype(BF16)
        y = jnp.dot(hdn, wdb_ref[...], preferred_element_type=F32)
        ybuf_ref[slot] = _pack_bf16_pair(y)
        y_copy(g, slot).start(priority=ROW_DMA_PRIORITY)
        return carry

    lax.fori_loop(0, nb, block, 0)

    @pl.when(e == n_e - 1)
    def _():
        for back in range(ROW_RING, 0, -1):
            @pl.when(n_used >= back)
            def _(back=back):
                y_copy(n_used - back, (n_used - back) % ROW_RING).wait()

        ybuf_ref[0] = jnp.zeros((ROW_BLOCK, HALF), U32)

        def fill(g, carry):
            y_copy(g, 0).start()
            return carry

        def fill_done(g, carry):
            y_copy(g, 0).wait()
            return carry

        lax.fori_loop(n_used, n_total, fill, 0)
        lax.fori_loop(n_used, n_total, fill_done, 0)


def _experts(block_start, block_count, xs, w_gate, w_up, w_down):
    n_rows = xs.shape[0]

    def wmap(e, bs, bc):
        return (e, 0, 0)

    grid_spec = pltpu.PrefetchScalarGridSpec(
        num_scalar_prefetch=2,
        grid=(N_EXPERTS,),
        in_specs=[pl.BlockSpec(memory_space=pl.ANY),
                  pl.BlockSpec((1, D_MODEL, D_EXPERT), wmap),
                  pl.BlockSpec((1, D_MODEL, D_EXPERT), wmap),
                  pl.BlockSpec((1, D_EXPERT, D_MODEL), wmap)],
        out_specs=pl.BlockSpec(memory_space=pl.ANY),
        scratch_shapes=[pltpu.VMEM((D_MODEL, D_EXPERT), BF16),
                        pltpu.VMEM((D_MODEL, D_EXPERT), BF16),
                        pltpu.VMEM((D_EXPERT, D_MODEL), BF16),
                        pltpu.VMEM((ROW_RING, ROW_BLOCK, HALF), U32),
                        pltpu.VMEM((ROW_RING, ROW_BLOCK, HALF), U32),
                        pltpu.SemaphoreType.DMA((ROW_RING,)),
                        pltpu.SemaphoreType.DMA((ROW_RING,))],
    )
    return pl.pallas_call(
        _experts_kernel,
        name="experts",
        grid_spec=grid_spec,
        out_shape=jax.ShapeDtypeStruct((n_rows, HALF), U32),
        compiler_params=_cparams(("arbitrary",)),
    )(block_start, block_count, xs, w_gate, w_up, w_down)


COMBINE_TOKENS = RANK_BLOCK // TOP_K


def _combine_kernel(dest_ref, dest_next_ref, h_ref, wts_ref, nw_ref, y_ref, o_ref, buf_ref, sem):
    ts = COMBINE_TOKENS
    i = pl.program_id(0)
    slot = i % 2

    def gather(idx_ref, dst_slot):
        def issue(g, carry):
            c0 = g * DMA_UNROLL
            for j in range(DMA_UNROLL):
                src_row = idx_ref[0, 0, c0 + j]
                src_tile = lax.shift_right_logical(src_row, SUBLANES.bit_length() - 1)
                pltpu.make_async_copy(y_ref.at[src_tile, pl.ds(src_row & (SUBLANES - 1), 1)],
                                      buf_ref.at[dst_slot, j % TOP_K, g, pl.ds(j // TOP_K, 1)],
                                      sem.at[dst_slot]).start()
            return carry

        lax.fori_loop(0, RANK_BLOCK // DMA_UNROLL, issue, 0)

    @pl.when(i == 0)
    def _():
        gather(dest_ref, 0)

    @pl.when(i + 1 < pl.num_programs(0))
    def _():
        gather(dest_next_ref, 1 - slot)

    for k in range(TOP_K):
        pltpu.make_async_copy(y_ref.at[pl.ds(0, ts // SUBLANES)], buf_ref.at[slot, k], sem.at[slot]).wait()

    w = wts_ref[...]
    w0 = w[:, 0:1]
    w1 = w[:, 1:2]
    lo0, hi0 = _unpack_bf16_pair(buf_ref[slot, 0].reshape(ts, HALF))
    lo1, hi1 = _unpack_bf16_pair(buf_ref[slot, 1].reshape(ts, HALF))
    h = h_ref[...]
    out_lo = h[:, :HALF] + w0 * lo0 + w1 * lo1
    out_hi = h[:, HALF:] + w0 * hi0 + w1 * hi1
    ms = (jnp.sum(out_lo * out_lo, axis=-1, keepdims=True)
          + jnp.sum(out_hi * out_hi, axis=-1, keepdims=True)) * (1.0 / D_MODEL)
    inv = lax.rsqrt(ms + EPS)
    o_ref[:, :HALF] = out_lo * inv * nw_ref[:, :HALF]
    o_ref[:, HALF:] = out_hi * inv * nw_ref[:, HALF:]


def _combine(dest, h1, wts, nw, y):
    t = h1.shape[0]
    ts = COMBINE_TOKENS
    n_steps = t // ts
    return pl.pallas_call(
        _combine_kernel,
        name="combine",
        grid=(n_steps,),
        in_specs=[pl.BlockSpec((1, 1, RANK_BLOCK), lambda i: (i, 0, 0), memory_space=pltpu.SMEM),
                  pl.BlockSpec((1, 1, RANK_BLOCK), lambda i: (jnp.minimum(i + 1, n_steps - 1), 0, 0),
                               memory_space=pltpu.SMEM),
                  pl.BlockSpec((ts, D_MODEL), lambda i: (i, 0)),
                  pl.BlockSpec((ts, LANES), lambda i: (i, 0)),
                  pl.BlockSpec((1, D_MODEL), lambda i: (0, 0)),
                  pl.BlockSpec(memory_space=pl.ANY)],
        out_specs=pl.BlockSpec((ts, D_MODEL), lambda i: (i, 0)),
        out_shape=jax.ShapeDtypeStruct((t, D_MODEL), F32),
        scratch_shapes=[pltpu.VMEM((2, TOP_K, ts // SUBLANES, SUBLANES, HALF), U32),
                        pltpu.SemaphoreType.DMA((2,))],
        compiler_params=_cparams(("arbitrary",)),
    )(dest, dest, h1, wts, nw, y.reshape(-1, SUBLANES, HALF))


def _narrow_w_in(w_in_t):
    dt0 = SSD_DIM + CONV_DIM
    f0 = dt0 + SSD_HEADS + 3 * FOX_DIM
    dt = w_in_t[dt0:dt0 + SSD_HEADS]
    f = w_in_t[f0:f0 + FOX_HEADS]
    pad = jnp.zeros((LANES - SMALL_F - FOX_HEADS, w_in_t.shape[1]), w_in_t.dtype)
    return jnp.concatenate([dt] * SMALL_DT_COPIES + [f, pad], axis=0).T.astype(BF16)


def _layer(h, p, bsz, seq):
    t = bsz * seq
    w_in_t = jnp.swapaxes(p["w_in"].astype(F32), 0, 1)
    u, small, cum = _norm_small(h, p["norm_mix_w"].astype(F32).reshape(1, D_MODEL), _narrow_w_in(w_in_t),
                                p["fox_f_bias"], seq)
    proj = _in_proj(u, w_in_t)
    y_ssd = _ssd(proj, small, p["conv_w"], p["conv_b"], p["dt_bias"], p["a_log"], p["d_skip"],
                 p["ssd_norm_w"], bsz, seq)
    y_fox = _fox(proj, cum, bsz, seq)
    mixed = _mix(y_ssd, y_fox, proj, p["w_proj_ssd"].astype(BF16), p["w_proj_fox"].astype(BF16))

    w_router = jnp.concatenate(
        [p["w_router_group"], p["w_router_expert"],
         jnp.zeros((D_MODEL, LANES - N_GROUPS - N_EXPERTS), F32)], axis=1).astype(F32)
    w_router_hi = w_router.astype(BF16)
    w_router_lo = (w_router - w_router_hi.astype(F32)).astype(BF16)
    w_router = jnp.concatenate([w_router_hi, w_router_lo], axis=1)
    b_router = jnp.concatenate(
        [p["b_router_group"], p["b_router_expert"],
         jnp.zeros((LANES - N_GROUPS - N_EXPERTS,), F32)]).astype(F32).reshape(1, LANES)
    h1, u2p, eidx, wts = _outproj(mixed, h, p["w_out"].astype(BF16),
                                  p["norm_moe_w"].astype(F32).reshape(1, D_MODEL), w_router, b_router)

    tk = t * TOP_K
    e_blocks = eidx[:, :TOP_K].reshape(tk // RANK_BLOCK, 1, RANK_BLOCK)
    rank, counts = _rank(e_blocks)
    counts = counts[:, 0].astype(I32)
    padded = ((counts + ROW_BLOCK - 1) // ROW_BLOCK) * ROW_BLOCK
    pend = jnp.cumsum(padded)
    pstart = pend - padded
    n_blocks = tk // ROW_BLOCK + N_EXPERTS
    dest = _dest(e_blocks, rank, jnp.broadcast_to(pstart[:, None], (N_EXPERTS, LANES)).astype(I32))
    block_start = (pstart // ROW_BLOCK).astype(I32)
    block_count = (padded // ROW_BLOCK).astype(I32)
    last_blk = jnp.where(block_count > 0, block_start + block_count - 1, -1).astype(I32)
    n_used = (pend[-1:] // ROW_BLOCK).astype(I32)
    xs = _dispatch(dest, u2p, n_blocks * ROW_BLOCK, last_blk, n_used)
    y = _experts(block_start, block_count, xs, p["w_gate_exp"], p["w_up_exp"], p["w_down_exp"])
    return dest, h1, wts, y


def kernel(x, norm_mix_w, w_in, conv_w, conv_b, dt_bias, a_log, d_skip, ssd_norm_w, fox_f_bias, w_proj_ssd,
           w_proj_fox, w_out, norm_moe_w, w_router_group, b_router_group, w_router_expert, b_router_expert,
           w_gate_exp, w_up_exp, w_down_exp, norm_final_w):
    bsz, seq, _ = x.shape
    depth = w_in.shape[0]
    assert depth == 1, "the fused final norm assumes a single layer"
    stacked = dict(norm_mix_w=norm_mix_w, w_in=w_in, conv_w=conv_w, conv_b=conv_b, dt_bias=dt_bias, a_log=a_log,
                   d_skip=d_skip, ssd_norm_w=ssd_norm_w, fox_f_bias=fox_f_bias, w_proj_ssd=w_proj_ssd,
                   w_proj_fox=w_proj_fox, w_out=w_out, norm_moe_w=norm_moe_w, w_router_group=w_router_group,
                   b_router_group=b_router_group, w_router_expert=w_router_expert,
                   b_router_expert=b_router_expert, w_gate_exp=w_gate_exp, w_up_exp=w_up_exp,
                   w_down_exp=w_down_exp)
    p = {name: v[0] for name, v in stacked.items()}
    h = x.reshape(bsz * seq, D_MODEL)
    dest, h1, wts, y = _layer(h, p, bsz, seq)
    out = _combine(dest, h1, wts, norm_final_w.astype(F32).reshape(1, D_MODEL), y)
    return out.reshape(bsz, seq, D_MODEL)
```
